```python
import jax, jax.numpy as jnp
from jax import lax
import numpy as np

D_MODEL = 1024
BATCH = 8
SEQ = 2048
DEPTH = 1
DEC_BATCH = 128
DEC_SEQ = 1
PAST_LEN = 16384
PAGE_SIZE = 128

A_WIDTH = D_MODEL // 2
A_HEADS = 8
A_HEAD_DIM = A_WIDTH // A_HEADS
B_WIDTH = D_MODEL - A_WIDTH
B_GROUPS = 8
MIX_WIDTH = A_WIDTH + B_WIDTH
IN_WIDTH = 2 * A_WIDTH + 3 * B_WIDTH
CHUNK = 128
CONV_W = 3
N_MEM = 256
X_HEADS = 4
X_HEAD_DIM = D_MODEL // X_HEADS
N_EXPERTS = 32
TOP_K = 4
D_EXPERT = D_MODEL
SWIGLU_LIMIT = 7.0
SWIGLU_ALPHA = 1.702
MOE_BLOCK = 128
EPS = 1e-5

kernel_name = "hybrid_spatialgate_shortconv_memxattn_moe_step"


def rmsnorm(x, g):
    xf = x.astype(jnp.float32)
    r = lax.rsqrt(jnp.mean(xf * xf, axis=-1, keepdims=True) + EPS)
    return (xf * r).astype(x.dtype) * g


def spatial_gate(v, w_s, b_s):
    Bt, L, H, hd = v.shape
    Lp = -(-L // CHUNK) * CHUNK
    vp = jnp.pad(v, ((0, 0), (0, Lp - L), (0, 0), (0, 0)))
    vc = vp.reshape(Bt, Lp // CHUNK, CHUNK, H, hd)
    mask = jnp.tril(jnp.ones((CHUNK, CHUNK), dtype=bool))
    w = jnp.where(mask[None], w_s, jnp.zeros((), w_s.dtype))
    out = jnp.einsum('hts,bcshd->bcthd', w, vc) + b_s.T[None, None, :, :, None]
    return out.reshape(Bt, Lp, H, hd)[:, :L]


def mixer(h, conv_prev, w_in, g_v, w_s, b_s, w_conv, w_out):
    Bt, L, _ = h.shape
    proj = h @ w_in
    u, v, hb, bg, cg = jnp.split(
        proj, [A_WIDTH, 2 * A_WIDTH, 2 * A_WIDTH + B_WIDTH, 2 * A_WIDTH + 2 * B_WIDTH], axis=-1)
    u = jax.nn.gelu(u, approximate=False)
    v = jax.nn.gelu(v, approximate=False).reshape(Bt, L, A_HEADS, A_HEAD_DIM)
    v = rmsnorm(v, g_v)
    a_out = u * spatial_gate(v, w_s, b_s).reshape(Bt, L, A_WIDTH)
    z = cg * hb
    zcat = jnp.concatenate([conv_prev, z], axis=1)
    conv = zcat[:, 0:L] * w_conv[0]
    for k in range(1, CONV_W):
        conv = conv + zcat[:, k:k + L] * w_conv[k]
    b_out = bg * conv
    y = jnp.concatenate([a_out, b_out], axis=-1) @ w_out
    return y, v, zcat[:, L:]


def mem_kv(mem, g_mem, w_k, w_v):
    Bt = mem.shape[0]
    m = rmsnorm(mem, g_mem)
    k = (m @ w_k).reshape(Bt, N_MEM, X_HEADS, X_HEAD_DIM)
    v = (m @ w_v).reshape(Bt, N_MEM, X_HEADS, X_HEAD_DIM)
    return k, v


def cross_attn(h, k, v, w_q, w_xo):
    Bt, L, _ = h.shape
    q = (h @ w_q).reshape(Bt, L, X_HEADS, X_HEAD_DIM)
    s = jnp.einsum('bqhd,bkhd->bhqk', q, k).astype(jnp.float32) * (X_HEAD_DIM ** -0.5)
    p = jax.nn.softmax(s, axis=-1).astype(v.dtype)
    o = jnp.einsum('bhqk,bkhd->bqhd', p, v).reshape(Bt, L, D_MODEL)
    return o @ w_xo


def expert_ffn(xb, w1, b1, w2, b2):
    gu = xb @ w1 + b1
    gate, up = gu[:, :D_EXPERT], gu[:, D_EXPERT:]
    gate = jnp.minimum(gate, SWIGLU_LIMIT)
    up = jnp.clip(up, -SWIGLU_LIMIT, SWIGLU_LIMIT)
    glu = gate * jax.nn.sigmoid(gate * SWIGLU_ALPHA)
    return ((up + 1) * glu) @ w2 + b2


def moe(h, w_router, b_router, w1, b1, w2, b2):
    Bt, L, D = h.shape
    T = Bt * L
    xt = h.reshape(T, D)
    logits = (xt @ w_router).astype(jnp.float32) + b_router.astype(jnp.float32)
    top_logit, top_e = lax.top_k(logits, TOP_K)
    gates = jax.nn.softmax(top_logit, axis=-1)
    A = T * TOP_K
    flat_e = top_e.reshape(A).astype(jnp.int32)
    flat_tok = jnp.repeat(jnp.arange(T, dtype=jnp.int32), TOP_K)
    order = jnp.argsort(flat_e)
    sorted_e = flat_e[order]
    sorted_tok = flat_tok[order]
    sorted_gate = gates.reshape(A)[order]
    counts = jnp.bincount(flat_e, length=N_EXPERTS).astype(jnp.int32)
    starts = jnp.cumsum(counts) - counts
    padded = (counts + MOE_BLOCK - 1) // MOE_BLOCK * MOE_BLOCK
    pad_ends = jnp.cumsum(padded)
    pad_starts = pad_ends - padded
    dest = pad_starts[sorted_e] + (jnp.arange(A, dtype=jnp.int32) - starts[sorted_e])
    n_blocks = -(-A // MOE_BLOCK) + N_EXPERTS
    P = n_blocks * MOE_BLOCK
    row_tok = jnp.full((P,), T, dtype=jnp.int32).at[dest].set(sorted_tok)
    x_ext = jnp.concatenate([xt, jnp.zeros((1, D), xt.dtype)], axis=0)
    xb = x_ext[row_tok].reshape(n_blocks, MOE_BLOCK, D)
    block_e = jnp.minimum(
        jnp.searchsorted(pad_ends, jnp.arange(n_blocks, dtype=jnp.int32) * MOE_BLOCK, side='right'),
        N_EXPERTS - 1).astype(jnp.int32)

    def run_block(args):
        xblk, e = args
        return expert_ffn(xblk, w1[e], b1[e], w2[e], b2[e])

    yb = lax.map(run_block, (xb, block_e)).reshape(P, D)
    contrib = yb[dest] * sorted_gate[:, None].astype(yb.dtype)
    out = jax.ops.segment_sum(contrib, sorted_tok, num_segments=T)
    return out.reshape(Bt, L, D)


def layer(x, conv_prev, mk, mv, g_mix, w_in, g_v, w_s, b_s, w_conv, w_out,
          g_xattn, w_q, w_xo, g_moe, w_router, b_router, w1, b1, w2, b2):
    y, vrows, conv_new = mixer(rmsnorm(x, g_mix), conv_prev, w_in, g_v, w_s, b_s, w_conv, w_out)
    x = x + y
    x = x + cross_attn(rmsnorm(x, g_xattn), mk, mv, w_q, w_xo)
    x = x + moe(rmsnorm(x, g_moe), w_router, b_router, w1, b1, w2, b2)
    return x, vrows, conv_new


def setup_inputs(seed: int = 0) -> dict:
    key = jax.random.key(seed)
    ks = jax.random.split(key, 32)
    f32 = jnp.float32
    nrm = lambda k, shape, s: jax.random.normal(k, shape, f32) * s
    gain = lambda k, shape: 1.0 + 0.05 * jax.random.normal(k, shape, f32)
    return {
        "x_prompt": nrm(ks[0], (BATCH, SEQ, D_MODEL), 1.0),
        "x_sample": nrm(ks[1], (DEC_BATCH, DEC_SEQ, D_MODEL), 1.0),
        "mem_prompt": nrm(ks[2], (BATCH, N_MEM, D_MODEL), 1.0),
        "cache_mem_k": nrm(ks[3], (DEPTH, DEC_BATCH, N_MEM, X_HEADS, X_HEAD_DIM), 1.0),
        "cache_mem_v": nrm(ks[4], (DEPTH, DEC_BATCH, N_MEM, X_HEADS, X_HEAD_DIM), 1.0),
        "state_conv": nrm(ks[5], (DEPTH, DEC_BATCH, CONV_W - 1, B_WIDTH), 1.0),
        "g_mix": gain(ks[6], (DEPTH, D_MODEL)),
        "w_in": nrm(ks[7], (DEPTH, D_MODEL, IN_WIDTH), D_MODEL ** -0.5),
        "g_v": gain(ks[8], (DEPTH, A_HEADS, A_HEAD_DIM)),
        "w_spatial": nrm(ks[9], (DEPTH, A_HEADS, CHUNK, CHUNK), CHUNK ** -0.5),
        "b_spatial": gain(ks[10], (DEPTH, A_HEADS, CHUNK)),
        "w_conv": nrm(ks[11], (DEPTH, CONV_W, B_WIDTH), CONV_W ** -0.5),
        "w_out": nrm(ks[12], (DEPTH, MIX_WIDTH, D_MODEL), MIX_WIDTH ** -0.5),
        "g_xattn": gain(ks[13], (DEPTH, D_MODEL)),
        "g_mem": gain(ks[14], (DEPTH, D_MODEL)),
        "w_q": nrm(ks[15], (DEPTH, D_MODEL, D_MODEL), D_MODEL ** -0.5),
        "w_k": nrm(ks[16], (DEPTH, D_MODEL, D_MODEL), D_MODEL ** -0.5),
        "w_v": nrm(ks[17], (DEPTH, D_MODEL, D_MODEL), D_MODEL ** -0.5),
        "w_xo": nrm(ks[18], (DEPTH, D_MODEL, D_MODEL), D_MODEL ** -0.5),
        "g_moe": gain(ks[19], (DEPTH, D_MODEL)),
        "w_router": nrm(ks[20], (DEPTH, D_MODEL, N_EXPERTS), D_MODEL ** -0.5),
        "b_router": nrm(ks[21], (DEPTH, N_EXPERTS), 0.01),
        "w_gate_up": nrm(ks[22], (DEPTH, N_EXPERTS, D_MODEL, 2 * D_EXPERT), D_MODEL ** -0.5),
        "b_gate_up": nrm(ks[23], (DEPTH, N_EXPERTS, 2 * D_EXPERT), 0.02),
        "w_down": nrm(ks[24], (DEPTH, N_EXPERTS, D_EXPERT, D_MODEL), D_EXPERT ** -0.5),
        "b_down": nrm(ks[25], (DEPTH, N_EXPERTS, D_MODEL), 0.02),
        "g_final": gain(ks[26], (D_MODEL,)),
    }


def reference(x_prompt, x_sample, mem_prompt, cache_mem_k, cache_mem_v, state_conv,
              g_mix, w_in, g_v, w_spatial, b_spatial, w_conv, w_out,
              g_xattn, g_mem, w_q, w_k, w_v, w_xo,
              g_moe, w_router, b_router, w_gate_up, b_gate_up, w_down, b_down, g_final):
    xp = x_prompt
    xs = x_sample
    mk_p_list, mv_p_list, conv_p_list, conv_s_list, chunk_v_list = [], [], [], [], []
    for l in range(DEPTH):
        lw = (g_mix[l], w_in[l], g_v[l], w_spatial[l], b_spatial[l], w_conv[l], w_out[l],
              g_xattn[l], w_q[l], w_xo[l], g_moe[l], w_router[l], b_router[l],
              w_gate_up[l], b_gate_up[l], w_down[l], b_down[l])
        mk_p, mv_p = mem_kv(mem_prompt, g_mem[l], w_k[l], w_v[l])
        conv0 = jnp.zeros((xp.shape[0], CONV_W - 1, B_WIDTH), xp.dtype)
        xp, _, conv_p = layer(xp, conv0, mk_p, mv_p, *lw)
        xs, v_s, conv_s = layer(xs, state_conv[l], cache_mem_k[l], cache_mem_v[l], *lw)
        mk_p_list.append(mk_p)
        mv_p_list.append(mv_p)
        conv_p_list.append(conv_p)
        conv_s_list.append(conv_s)
        chunk_v_list.append(v_s)
    y_prompt = rmsnorm(xp, g_final)
    y_sample = rmsnorm(xs, g_final)
    mem_k_prompt = jnp.stack(mk_p_list)
    mem_v_prompt = jnp.stack(mv_p_list)
    conv_state_prompt = jnp.stack(conv_p_list)
    conv_state_sample = jnp.stack(conv_s_list)
    chunk_v_sample = jnp.stack(chunk_v_list)
    return (y_prompt, y_sample, mem_k_prompt, mem_v_prompt, conv_state_prompt, conv_state_sample, chunk_v_sample)
```

```python
import functools

import jax
import jax.numpy as jnp
from jax import lax
from jax.experimental import pallas as pl
from jax.experimental.pallas import tpu as pltpu

F32 = jnp.float32
BF16 = jnp.bfloat16
I32 = jnp.int32
U32 = jnp.uint32

A_HEADS = 8
CHUNK = 128
CONV_W = 3
X_HEADS = 4
N_EXPERTS = 32
TOP_K = 4
SWIGLU_LIMIT = 7.0
SWIGLU_ALPHA = 1.702
EPS = 1e-5

MIX_TILE = 512
ATT_TILE = 512
KV_TILE = 512
ROUTE_TILE = 512
MOE_BLOCK = 256
COMBINE_TILE = 256
SAMPLE_ATT_GROUP = 4
VMEM_LIMIT = 56 * 1024 * 1024


def _cparams(sem=None):
    return pltpu.CompilerParams(dimension_semantics=sem, vmem_limit_bytes=VMEM_LIMIT)


def _rms(x, g):
    r = lax.rsqrt(jnp.mean(x * x, axis=-1, keepdims=True) + EPS)
    return (x * r) * g


def _gelu(x):
    return 0.5 * x * (1.0 + lax.erf(x * 0.7071067811865476))


def _dot(a, b):
    return jnp.dot(a, b, preferred_element_type=F32)


def _dot_nt(a, b):
    return lax.dot_general(a, b, (((1,), (1,)), ((), ())), preferred_element_type=F32)


def _memkv_kernel(m_ref, g_ref, wk_ref, wv_ref, k_ref, v_ref):
    h = _rms(m_ref[...], g_ref[...]).astype(BF16)
    k_ref[...] = _dot(h, wk_ref[...])
    v_ref[...] = _dot(h, wv_ref[...])


def _mem_kv(mem2d, g_mem, wk, wv):
    n, d = mem2d.shape
    row = pl.BlockSpec((KV_TILE, d), lambda i: (i, 0))
    full = lambda shape: pl.BlockSpec(shape, lambda i: (0,) * len(shape))
    return pl.pallas_call(
        _memkv_kernel,
        grid=(n // KV_TILE,),
        in_specs=[row, full((1, d)), full((d, d)), full((d, d))],
        out_specs=[row, row],
        out_shape=[jax.ShapeDtypeStruct((n, d), F32)] * 2,
        compiler_params=_cparams(("arbitrary",)),
        name="mem_kv",
    )(mem2d, g_mem, wk, wv)


def _head_rms(v, gv, a_width):
    hd = a_width // A_HEADS
    r_i = lax.broadcasted_iota(I32, (a_width, a_width), 0) // hd
    c_i = lax.broadcasted_iota(I32, (a_width, a_width), 1) // hd
    ones_bd = jnp.where(r_i == c_i, 1.0, 0.0).astype(BF16)
    sq = v * v
    sq_hi = sq.astype(BF16)
    sq_lo = (sq - sq_hi.astype(F32)).astype(BF16)
    gs = _dot(sq_hi, ones_bd) + _dot(sq_lo, ones_bd)
    return (v * lax.rsqrt(gs * (1.0 / hd) + EPS)) * gv


def _mixer_prompt_kernel(x_ref, gmix_ref, win_ref, gv_ref, ws_ref, bs_ref, wc_ref, wout_ref,
                         x1_ref, cs_ref, zbuf, *, tile, a_width, b_width):
    j = pl.program_id(1)
    nj = pl.num_programs(1)
    x = x_ref[...]
    h = _rms(x, gmix_ref[...]).astype(BF16)
    proj = _dot(h, win_ref[...])
    u = _gelu(proj[:, :a_width])
    v = _head_rms(_gelu(proj[:, a_width:2 * a_width]), gv_ref[...], a_width)
    o = 2 * a_width
    hb = proj[:, o:o + b_width]
    bg = proj[:, o + b_width:o + 2 * b_width]
    cg = proj[:, o + 2 * b_width:o + 3 * b_width]

    lane = lax.broadcasted_iota(I32, (CHUNK, 2 * (a_width // A_HEADS)), 1)
    first = lane < (a_width // A_HEADS)
    t_i = lax.broadcasted_iota(I32, (CHUNK, 2 * CHUNK), 0)
    s_i = lax.broadcasted_iota(I32, (CHUNK, 2 * CHUNK), 1) % CHUNK
    causal = s_i <= t_i
    pair_cols = []
    for p in range(A_HEADS // 2):
        w_pair = jnp.where(causal, ws_ref[p], 0.0).astype(BF16)
        vp = v[:, p * CHUNK:(p + 1) * CHUNK]
        rows = []
        for c in range(tile // CHUNK):
            vc = vp[c * CHUNK:(c + 1) * CHUNK]
            rhs = jnp.concatenate([jnp.where(first, vc, 0.0), jnp.where(first, 0.0, vc)],
                                  axis=0).astype(BF16)
            rows.append(_dot(w_pair, rhs))
        pair_cols.append(jnp.concatenate(rows, axis=0))
    gate = jnp.concatenate(pair_cols, axis=1)
    bias = jnp.concatenate([bs_ref[...]] * (tile // CHUNK), axis=0)
    a_out = u * (gate + bias)

    @pl.when(j == 0)
    def _():
        zbuf[0:8, :] = jnp.zeros((8, b_width), F32)

    z = cg * hb
    zbuf[8:tile + 8, :] = z
    z1 = zbuf[7:tile + 7, :]
    z2 = zbuf[6:tile + 6, :]
    wc = wc_ref[...]
    conv = z2 * wc[0:1] + z1 * wc[1:2] + z * wc[2:3]
    b_out = bg * conv
    tail = zbuf[tile:tile + 8, :]
    zbuf[0:8, :] = tail

    @pl.when(j == nj - 1)
    def _():
        cs_ref[...] = tail[8 - (CONV_W - 1):, :]

    y = _dot(a_out.astype(BF16), wout_ref[0:a_width, :]) + _dot(b_out.astype(BF16), wout_ref[a_width:, :])
    x1_ref[...] = x + y


def _mixer_prompt(x, g_mix, w_in, g_v, ws_pairs, bs_full, w_conv, w_out):
    b, s, d = x.shape
    a_width = g_v.shape[1]
    b_width = w_conv.shape[1]
    in_width = w_in.shape[1]
    tile = MIX_TILE
    full = lambda shape: pl.BlockSpec(shape, lambda i, j: (0,) * len(shape))
    kern = functools.partial(_mixer_prompt_kernel, tile=tile, a_width=a_width, b_width=b_width)
    return pl.pallas_call(
        kern,
        grid=(b, s // tile),
        in_specs=[
            pl.BlockSpec((None, tile, d), lambda i, j: (i, j, 0)),
            full((1, d)), full((d, in_width)), full((1, a_width)),
            full(ws_pairs.shape), full(bs_full.shape), full(w_conv.shape), full(w_out.shape),
        ],
        out_specs=[
            pl.BlockSpec((None, tile, d), lambda i, j: (i, j, 0)),
            pl.BlockSpec((None, CONV_W - 1, b_width), lambda i, j: (i, 0, 0)),
        ],
        out_shape=[jax.ShapeDtypeStruct((b, s, d), F32),
                   jax.ShapeDtypeStruct((b, CONV_W - 1, b_width), F32)],
        scratch_shapes=[pltpu.VMEM((tile + 8, b_width), F32)],
        compiler_params=_cparams(("arbitrary", "arbitrary")),
        name="mixer_prompt",
    )(x, g_mix, w_in, g_v, ws_pairs, bs_full, w_conv, w_out)


def _mixer_sample_kernel(x_ref, s0_ref, s1_ref, gmix_ref, win_ref, gv_ref, w00_ref, b0_ref, wc_ref,
                         wout_ref, gx_ref, wq_ref, x1_ref, v_ref, z_ref, q_ref, *, a_width, b_width):
    x = x_ref[...]
    h = _rms(x, gmix_ref[...]).astype(BF16)
    proj = _dot(h, win_ref[...])
    u = _gelu(proj[:, :a_width])
    v = _head_rms(_gelu(proj[:, a_width:2 * a_width]), gv_ref[...], a_width)
    v_ref[...] = v
    o = 2 * a_width
    hb = proj[:, o:o + b_width]
    bg = proj[:, o + b_width:o + 2 * b_width]
    cg = proj[:, o + 2 * b_width:o + 3 * b_width]
    a_out = u * (v * w00_ref[...] + b0_ref[...])
    z = cg * hb
    z_ref[...] = z
    wc = wc_ref[...]
    conv = s0_ref[...] * wc[0:1] + s1_ref[...] * wc[1:2] + z * wc[2:3]
    b_out = bg * conv
    y = _dot(a_out.astype(BF16), wout_ref[0:a_width, :]) + _dot(b_out.astype(BF16), wout_ref[a_width:, :])
    x1 = x + y
    x1_ref[...] = x1
    q_ref[...] = _dot(_rms(x1, gx_ref[...]).astype(BF16), wq_ref[...])


def _mixer_sample(x, s0, s1, g_mix, w_in, g_v, w00, b0, w_conv, w_out, g_x, w_q):
    n, d = x.shape
    a_width = g_v.shape[1]
    b_width = w_conv.shape[1]
    kern = functools.partial(_mixer_sample_kernel, a_width=a_width, b_width=b_width)
    return pl.pallas_call(
        kern,
        out_shape=[jax.ShapeDtypeStruct((n, d), F32), jax.ShapeDtypeStruct((n, a_width), F32),
                   jax.ShapeDtypeStruct((n, b_width), F32), jax.ShapeDtypeStruct((n, d), F32)],
        compiler_params=_cparams(),
        name="mixer_sample",
    )(x, s0, s1, g_mix, w_in, g_v, w00, b0, w_conv, w_out, g_x, w_q)


def _router_tail(x2, gmoe, wrt, br, h3_ref, tope_ref, gates_ref):
    d = x2.shape[1]
    h3 = _rms(x2, gmoe).astype(BF16)
    h3f = h3.astype(F32)
    lo = lax.shift_right_logical(pltpu.bitcast(h3f[:, :d // 2], U32), jnp.uint32(16))
    hi = lax.bitwise_and(pltpu.bitcast(h3f[:, d // 2:], U32), jnp.uint32(0xFFFF0000))
    h3_ref[...] = lax.bitwise_or(lo, hi)

    logits = _dot_nt(wrt, h3) + br
    n_e, n_t = logits.shape
    e_idx = lax.broadcasted_iota(I32, (n_e, n_t), 0).astype(F32)
    tops, idxs = [], []
    for _ in range(TOP_K):
        m = jnp.max(logits, axis=0, keepdims=True)
        idx = jnp.min(jnp.where(logits == m, e_idx, float(n_e)), axis=0, keepdims=True)
        tops.append(m)
        idxs.append(idx)
        logits = jnp.where(e_idx == idx, -jnp.inf, logits)
    top = jnp.concatenate(tops, axis=0)
    ex = jnp.exp(top - top[0:1])
    gates_ref[...] = ex / jnp.sum(ex, axis=0, keepdims=True)
    tope_ref[...] = jnp.concatenate(idxs, axis=0).astype(I32)


def _attn_prompt_kernel(x1_ref, k_ref, v_ref, gx_ref, wq_ref, wxo_ref, gmoe_ref, wrt_ref, br_ref,
                        x2_ref, h3_ref, tope_ref, gates_ref):
    x = x1_ref[...]
    d = x.shape[1]
    hd = d // X_HEADS
    q = _dot(_rms(x, gx_ref[...]).astype(BF16), wq_ref[...]).astype(BF16)
    kb = k_ref[...].astype(BF16)
    vb = v_ref[...].astype(BF16)
    outs = []
    for hh in range(X_HEADS):
        sl = slice(hh * hd, (hh + 1) * hd)
        s = _dot_nt(q[:, sl], kb[:, sl]) * (hd ** -0.5)
        e = jnp.exp(s - jnp.max(s, axis=-1, keepdims=True))
        p = e / jnp.sum(e, axis=-1, keepdims=True)
        outs.append(_dot(p.astype(BF16), vb[:, sl]))
    o = jnp.concatenate(outs, axis=1).astype(BF16)
    x2 = x + _dot(o, wxo_ref[...])
    x2_ref[...] = x2
    _router_tail(x2, gmoe_ref[...], wrt_ref[...], br_ref[...], h3_ref, tope_ref, gates_ref)


def _attn_prompt(x1, mk, mv, g_x, w_q, w_xo, g_moe, wrt, br):
    b, s, d = x1.shape
    n_mem = mk.shape[1]
    tile = ATT_TILE
    nq = s // tile
    full = lambda shape: pl.BlockSpec(shape, lambda i, j: (0,) * len(shape))
    tok = pl.BlockSpec((None, tile, d), lambda i, j: (i, j, 0))
    mem = pl.BlockSpec((None, n_mem, d), lambda i, j: (i, 0, 0))
    lanes = pl.BlockSpec((TOP_K, tile), lambda i, j: (0, i * nq + j))
    return pl.pallas_call(
        _attn_prompt_kernel,
        grid=(b, nq),
        in_specs=[tok, mem, mem, full((1, d)), full((d, d)), full((d, d)), full((1, d)),
                  full(wrt.shape), full(br.shape)],
        out_specs=[tok, pl.BlockSpec((None, tile, d // 2), lambda i, j: (i, j, 0)), lanes, lanes],
        out_shape=[jax.ShapeDtypeStruct((b, s, d), F32), jax.ShapeDtypeStruct((b, s, d // 2), U32),
                   jax.ShapeDtypeStruct((TOP_K, b * s), I32), jax.ShapeDtypeStruct((TOP_K, b * s), F32)],
        compiler_params=_cparams(("arbitrary", "arbitrary")),
        name="attn_prompt",
    )(x1, mk, mv, g_x, w_q, w_xo, g_moe, wrt, br)


def _attn_sample_kernel(q_ref, k_ref, v_ref, o_ref, *, group):
    i = pl.program_id(0)
    d = q_ref.shape[1]
    hd = d // X_HEADS
    head = lax.broadcasted_iota(I32, (X_HEADS, d), 0)
    lane_head = lax.broadcasted_iota(I32, (X_HEADS, d), 1) // hd
    own = head == lane_head
    for g in range(group):
        row = i * group + g
        qm = jnp.where(own, q_ref[pl.ds(row, 1), :], 0.0).astype(BF16)
        s = _dot_nt(qm, k_ref[g].astype(BF16)) * (hd ** -0.5)
        e = jnp.exp(s - jnp.max(s, axis=-1, keepdims=True))
        p = e / jnp.sum(e, axis=-1, keepdims=True)
        o4 = _dot(p.astype(BF16), v_ref[g].astype(BF16))
        o_ref[pl.ds(row, 1), :] = jnp.sum(jnp.where(own, o4, 0.0), axis=0, keepdims=True)


def _attn_sample(q, ck, cv):
    n, d = q.shape
    n_mem = ck.shape[1]
    group = SAMPLE_ATT_GROUP
    whole = pl.BlockSpec((n, d), lambda i: (0, 0))
    kv = pl.BlockSpec((group, n_mem, d), lambda i: (i, 0, 0))
    return pl.pallas_call(
        functools.partial(_attn_sample_kernel, group=group),
        grid=(n // group,),
        in_specs=[whole, kv, kv],
        out_specs=whole,
        out_shape=jax.ShapeDtypeStruct((n, d), F32),
        compiler_params=_cparams(("arbitrary",)),
        name="attn_sample",
    )(q, ck, cv)


def _tail_sample_kernel(x1_ref, o_ref, wxo_ref, gmoe_ref, wrt_ref, br_ref,
                        x2_ref, h3_ref, tope_ref, gates_ref):
    x2 = x1_ref[...] + _dot(o_ref[...].astype(BF16), wxo_ref[...])
    x2_ref[...] = x2
    _router_tail(x2, gmoe_ref[...], wrt_ref[...], br_ref[...], h3_ref, tope_ref, gates_ref)


def _tail_sample(x1, o, w_xo, g_moe, wrt, br):
    n, d = x1.shape
    return pl.pallas_call(
        _tail_sample_kernel,
        out_shape=[jax.ShapeDtypeStruct((n, d), F32), jax.ShapeDtypeStruct((n, d // 2), U32),
                   jax.ShapeDtypeStruct((TOP_K, n), I32), jax.ShapeDtypeStruct((TOP_K, n), F32)],
        compiler_params=_cparams(),
        name="tail_sample",
    )(x1, o, w_xo, g_moe, wrt, br)


def _route_kernel(tope_ref, dest_ref, cnt_ref, *, n_blocks, tile, moe_block):
    e_idx = lax.broadcasted_iota(I32, (N_EXPERTS, tile), 0)
    earlier = jnp.where(lax.broadcasted_iota(I32, (tile, tile), 0)
                        < lax.broadcasted_iota(I32, (tile, tile), 1), 1.0, 0.0).astype(BF16)

    def onehot(k, off):
        return e_idx == tope_ref[pl.ds(k, 1), pl.ds(off, tile)]

    def count_body(j, cnt):
        off = pl.multiple_of(j * tile, tile)
        for k in range(TOP_K):
            cnt = cnt + jnp.sum(jnp.where(onehot(k, off), 1.0, 0.0), axis=1, keepdims=True)
        return cnt

    counts = lax.fori_loop(0, n_blocks, count_body, jnp.zeros((N_EXPERTS, 1), F32))
    cnt_ref[...] = jnp.broadcast_to(counts, cnt_ref.shape).astype(I32)

    n_blk = jnp.floor((counts + (moe_block - 1)) * (1.0 / moe_block))
    n_hi = jnp.floor(n_blk * (1.0 / 16.0))
    n_lo = n_blk - 16.0 * n_hi
    below = jnp.where(lax.broadcasted_iota(I32, (N_EXPERTS, N_EXPERTS), 1)
                      < lax.broadcasted_iota(I32, (N_EXPERTS, N_EXPERTS), 0), 1.0, 0.0).astype(BF16)
    wide = lambda c: jnp.broadcast_to(c, (N_EXPERTS, 128)).astype(BF16)
    start = (16.0 * _dot(below, wide(n_hi)) + _dot(below, wide(n_lo)))[:, 0:1] * float(moe_block)

    def dest_body(j, run):
        off = pl.multiple_of(j * tile, tile)
        for k in range(TOP_K):
            oh = onehot(k, off)
            ohf = jnp.where(oh, 1.0, 0.0)
            before = _dot(ohf.astype(BF16), earlier)
            dest = jnp.sum(jnp.where(oh, before + run, 0.0), axis=0, keepdims=True)
            dest_ref[pl.ds(k, 1), pl.ds(off, tile)] = dest.astype(I32)
            run = run + jnp.sum(ohf, axis=1, keepdims=True)
        return run

    lax.fori_loop(0, n_blocks, dest_body, start)


def _route(tope_pad):
    t_pad = tope_pad.shape[1]
    kern = functools.partial(_route_kernel, n_blocks=t_pad // ROUTE_TILE, tile=ROUTE_TILE,
                             moe_block=MOE_BLOCK)
    return pl.pallas_call(
        kern,
        out_shape=[jax.ShapeDtypeStruct((TOP_K, t_pad), I32),
                   jax.ShapeDtypeStruct((N_EXPERTS, 128), I32)],
        compiler_params=_cparams(),
        name="route",
    )(tope_pad)


def _dispatch_kernel(fill_ref, dest_ref, h_hbm, xs_hbm, zblk, sem, *, n_tok, tile, blk, n_blocks):
    i = pl.program_id(0)

    def row_copy(t, d):
        return pltpu.make_async_copy(h_hbm.at[pl.ds(t, 1)], xs_hbm.at[pl.ds(d, 1)], sem.at[0])

    def zero_row(r):
        return pltpu.make_async_copy(zblk.at[pl.ds(0, 1)], xs_hbm.at[pl.ds(r, 1)], sem.at[1])

    def zero_block(n):
        return pltpu.make_async_copy(zblk, xs_hbm.at[pl.ds(pl.multiple_of(n * blk, blk), blk)], sem.at[2])

    @pl.when(i == 0)
    def _():
        zblk[...] = jnp.zeros(zblk.shape, zblk.dtype)
        n_used = fill_ref[2, 0]
        for e in range(N_EXPERTS):
            lo, hi = fill_ref[0, e], fill_ref[1, e]
            lax.fori_loop(lo, hi, lambda r, c: (zero_row(r).start(), c)[1], 0)
        lax.fori_loop(n_used, n_blocks, lambda n, c: (zero_block(n).start(), c)[1], 0)
        for e in range(N_EXPERTS):
            lo, hi = fill_ref[0, e], fill_ref[1, e]
            lax.fori_loop(lo, hi, lambda r, c: (zero_row(r).wait(), c)[1], 0)
        lax.fori_loop(n_used, n_blocks, lambda n, c: (zero_block(n).wait(), c)[1], 0)

    base = i * tile
    n_here = jnp.minimum(tile, n_tok - base)

    def issue(t, c):
        for k in range(TOP_K):
            row_copy(base + t, dest_ref[k, t]).start()
        return c

    def drain(t, c):
        for k in range(TOP_K):
            row_copy(0, 0).wait()
        return c

    lax.fori_loop(0, n_here, issue, 0)
    lax.fori_loop(0, n_here, drain, 0)


def _dispatch(fill, dest, h_all, n_rows):
    n_tok, w = h_all.shape
    tile = ROUTE_TILE
    grid_spec = pltpu.PrefetchScalarGridSpec(
        num_scalar_prefetch=1,
        grid=(dest.shape[1] // tile,),
        in_specs=[pl.BlockSpec((TOP_K, tile), lambda i, f: (0, i), memory_space=pltpu.SMEM),
                  pl.BlockSpec(memory_space=pl.ANY)],
        out_specs=pl.BlockSpec(memory_space=pl.ANY),
        scratch_shapes=[pltpu.VMEM((MOE_BLOCK, w), U32), pltpu.SemaphoreType.DMA((3,))],
    )
    return pl.pallas_call(
        functools.partial(_dispatch_kernel, n_tok=n_tok, tile=tile, blk=MOE_BLOCK,
                          n_blocks=n_rows // MOE_BLOCK),
        grid_spec=grid_spec,
        out_shape=jax.ShapeDtypeStruct((n_rows, w), U32),
        compiler_params=_cparams(("arbitrary",)),
        name="dispatch",
    )(fill, dest, h_all)


def _ffn_kernel(be_ref, nu_ref, xs_ref, w1_ref, b1_ref, w2_ref, b2_ref, y_ref, w1b, w2b):
    i = pl.program_id(0)

    @pl.when(i >= nu_ref[0])
    def _():
        y_ref[...] = jnp.zeros(y_ref.shape, y_ref.dtype)

    @pl.when(i < nu_ref[0])
    def _():
        e = be_ref[i]
        prev = be_ref[jnp.maximum(i - 1, 0)]

        @pl.when((i == 0) | (e != prev))
        def _():
            w1b[...] = w1_ref[...].astype(BF16)
            w2b[...] = w2_ref[...].astype(BF16)

        w = xs_ref[...]
        half = w.shape[1]
        d_e = w2b.shape[0]
        lo = pltpu.bitcast(lax.shift_left(w, jnp.uint32(16)), F32).astype(BF16)
        hi = pltpu.bitcast(lax.bitwise_and(w, jnp.uint32(0xFFFF0000)), F32).astype(BF16)
        gu = _dot(lo, w1b[0:half, :]) + _dot(hi, w1b[half:, :]) + b1_ref[...]
        gate = jnp.minimum(gu[:, :d_e], SWIGLU_LIMIT)
        up = jnp.clip(gu[:, d_e:], -SWIGLU_LIMIT, SWIGLU_LIMIT)
        glu = gate * jax.nn.sigmoid(gate * SWIGLU_ALPHA)
        act = ((up + 1.0) * glu).astype(BF16)
        y_ref[...] = _dot(act, w2b[...]) + b2_ref[...]


def _expert_ffn(block_e, n_used, xs, w1, b1, w2, b2):
    n_rows, half = xs.shape
    n_e, d, d2 = w1.shape
    d_e = w2.shape[1]
    blk = MOE_BLOCK
    last = lambda i, be, nu: jnp.minimum(i, nu[0] - 1)
    grid_spec = pltpu.PrefetchScalarGridSpec(
        num_scalar_prefetch=2,
        grid=(n_rows // blk,),
        in_specs=[
            pl.BlockSpec((blk, half), lambda i, be, nu: (last(i, be, nu), 0)),
            pl.BlockSpec((None, d, d2), lambda i, be, nu: (be[last(i, be, nu)], 0, 0)),
            pl.BlockSpec((None, 1, d2), lambda i, be, nu: (be[last(i, be, nu)], 0, 0)),
            pl.BlockSpec((None, d_e, d), lambda i, be, nu: (be[last(i, be, nu)], 0, 0)),
            pl.BlockSpec((None, 1, d), lambda i, be, nu: (be[last(i, be, nu)], 0, 0)),
        ],
        out_specs=pl.BlockSpec((blk, d), lambda i, be, nu: (i, 0)),
        scratch_shapes=[pltpu.VMEM((d, d2), BF16), pltpu.VMEM((d_e, d), BF16)],
    )
    return pl.pallas_call(
        _ffn_kernel,
        grid_spec=grid_spec,
        out_shape=jax.ShapeDtypeStruct((n_rows, d), F32),
        compiler_params=_cparams(("arbitrary",)),
        name="expert_ffn",
    )(block_e, n_used, xs, w1, b1.reshape(n_e, 1, d2), w2, b2.reshape(n_e, 1, d))


def _combine_kernel(dest_ref, x2_ref, gates_ref, gf_ref, y_hbm, out_ref, buf, sem, *, tile):
    def row_copy(d, k, t):
        return pltpu.make_async_copy(y_hbm.at[pl.ds(d, 1)], buf.at[k, pl.ds(t, 1)], sem.at[0])

    def issue(t, c):
        for k in range(TOP_K):
            row_copy(dest_ref[k, t], k, t).start()
        return c

    def drain(t, c):
        for k in range(TOP_K):
            row_copy(0, k, t).wait()
        return c

    lax.fori_loop(0, tile, issue, 0)
    lax.fori_loop(0, tile, drain, 0)

    g = jnp.concatenate([gates_ref[...], jnp.zeros((128 - TOP_K, tile), F32)], axis=0)
    gt = g.T
    moe = gt[:, 0:1] * buf[0]
    for k in range(1, TOP_K):
        moe = moe + gt[:, k:k + 1] * buf[k]
    out_ref[...] = _rms(x2_ref[...] + moe, gf_ref[...])


def _combine(dest, x2, gates, g_final, y, tile):
    n, d = x2.shape
    grid_spec = pltpu.PrefetchScalarGridSpec(
        num_scalar_prefetch=0,
        grid=(n // tile,),
        in_specs=[
            pl.BlockSpec((TOP_K, tile), lambda i: (0, i), memory_space=pltpu.SMEM),
            pl.BlockSpec((tile, d), lambda i: (i, 0)),
            pl.BlockSpec((TOP_K, tile), lambda i: (0, i)),
            pl.BlockSpec((1, d), lambda i: (0, 0)),
            pl.BlockSpec(memory_space=pl.ANY),
        ],
        out_specs=pl.BlockSpec((tile, d), lambda i: (i, 0)),
        scratch_shapes=[pltpu.VMEM((TOP_K, tile, d), F32), pltpu.SemaphoreType.DMA((1,))],
    )
    return pl.pallas_call(
        functools.partial(_combine_kernel, tile=tile),
        grid_spec=grid_spec,
        out_shape=jax.ShapeDtypeStruct((n, d), F32),
        compiler_params=_cparams(("arbitrary",)),
        name="combine",
    )(dest, x2, gates, g_final, y)


def kernel(x_prompt, x_sample, mem_prompt, cache_mem_k, cache_mem_v, state_conv, g_mix, w_in, g_v,
           w_spatial, b_spatial, w_conv, w_out, g_xattn, g_mem, w_q, w_k, w_v, w_xo, g_moe, w_router,
           b_router, w_gate_up, b_gate_up, w_down, b_down, g_final):
    depth = g_mix.shape[0]
    assert depth == 1, "one layer supported"
    b, s, d = x_prompt.shape
    nb, ns, _ = x_sample.shape
    assert ns == 1
    n_mem = mem_prompt.shape[1]
    a_heads, a_hd = g_v.shape[1], g_v.shape[2]
    a_width = a_heads * a_hd
    b_width = w_conv.shape[2]
    assert a_heads == A_HEADS and 2 * a_hd == CHUNK and w_spatial.shape[2] == CHUNK
    assert s % MIX_TILE == 0 and s % ATT_TILE == 0 and (b * n_mem) % KV_TILE == 0
    assert nb % SAMPLE_ATT_GROUP == 0
    l = 0
    row = lambda a: a.reshape(1, -1)

    w_in_b = w_in[l].astype(BF16)
    w_out_b = w_out[l].astype(BF16)
    w_q_b = w_q[l].astype(BF16)
    w_k_b = w_k[l].astype(BF16)
    w_v_b = w_v[l].astype(BF16)
    w_xo_b = w_xo[l].astype(BF16)
    wrt_b = w_router[l].T.astype(BF16)
    br_col = b_router[l].reshape(N_EXPERTS, 1).astype(F32)
    gv_row = row(g_v[l])
    ws = w_spatial[l]
    ws_pairs = jnp.concatenate([ws[0::2], ws[1::2]], axis=2)
    bs_full = jnp.repeat(b_spatial[l].T, a_hd, axis=1)
    w00 = row(jnp.repeat(ws[:, 0, 0], a_hd))
    b0 = row(jnp.repeat(b_spatial[l][:, 0], a_hd))

    mk2, mv2 = _mem_kv(mem_prompt.reshape(b * n_mem, d), row(g_mem[l]), w_k_b, w_v_b)
    x1_p, conv_p = _mixer_prompt(x_prompt, row(g_mix[l]), w_in_b, gv_row, ws_pairs, bs_full,
                                 w_conv[l], w_out_b)
    x2_p, h3_p, tope_p, gates_p = _attn_prompt(
        x1_p, mk2.reshape(b, n_mem, d), mv2.reshape(b, n_mem, d), row(g_xattn[l]), w_q_b, w_xo_b,
        row(g_moe[l]), wrt_b, br_col)

    xs2 = x_sample.reshape(nb, d)
    st = state_conv[l]
    x1_s, v_s, z_s, q_s = _mixer_sample(xs2, st[:, 0], st[:, 1], row(g_mix[l]), w_in_b, gv_row, w00, b0,
                                        w_conv[l], w_out_b, row(g_xattn[l]), w_q_b)
    o_s = _attn_sample(q_s, cache_mem_k[l].reshape(nb, n_mem, d), cache_mem_v[l].reshape(nb, n_mem, d))
    x2_s, h3_s, tope_s, gates_s = _tail_sample(x1_s, o_s, w_xo_b, row(g_moe[l]), wrt_b, br_col)

    t_p = b * s
    t_all = t_p + nb
    t_pad = -(-t_all // ROUTE_TILE) * ROUTE_TILE
    h_all = jnp.concatenate([h3_p.reshape(t_p, d // 2), h3_s], axis=0)
    tope_all = jnp.concatenate([tope_p, tope_s, jnp.full((TOP_K, t_pad - t_all), -1, I32)], axis=1)
    dest, cnt = _route(tope_all)
    counts = cnt[:, 0]
    padded = (counts + MOE_BLOCK - 1) // MOE_BLOCK * MOE_BLOCK
    pad_ends = jnp.cumsum(padded)
    pad_starts = pad_ends - padded
    n_blocks = -(-(t_all * TOP_K) // MOE_BLOCK) + N_EXPERTS
    block_e = jnp.minimum(
        jnp.searchsorted(pad_ends, jnp.arange(n_blocks, dtype=I32) * MOE_BLOCK, side="right"),
        N_EXPERTS - 1).astype(I32)
    n_used = (pad_ends[-1] // MOE_BLOCK).astype(I32).reshape(1)
    fill = jnp.stack([pad_starts + counts, pad_ends, jnp.broadcast_to(n_used, (N_EXPERTS,))]).astype(I32)
    xs_rows = _dispatch(fill, dest, h_all, n_blocks * MOE_BLOCK)
    y_rows = _expert_ffn(block_e, n_used, xs_rows, w_gate_up[l], b_gate_up[l], w_down[l], b_down[l])

    gf = row(g_final)
    y_p = _combine(dest[:, :t_p], x2_p.reshape(t_p, d), gates_p, gf, y_rows, COMBINE_TILE)
    y_s = _combine(dest[:, t_p:t_all], x2_s, gates_s, gf, y_rows, nb)

    x_heads = cache_mem_k.shape[3]
    return (y_p.reshape(b, s, d),
            y_s.reshape(nb, 1, d),
            mk2.reshape(1, b, n_mem, x_heads, d // x_heads),
            mv2.reshape(1, b, n_mem, x_heads, d // x_heads),
            conv_p.reshape(1, b, CONV_W - 1, b_width),
            jnp.stack([st[:, 1], z_s], axis=1).reshape(1, nb, CONV_W - 1, b_width),
            v_s.reshape(1, nb, 1, a_heads, a_hd))
```

```python
import functools

import jax
import jax.numpy as jnp
from jax import lax
from jax.experimental import pallas as pl
from jax.experimental.pallas import tpu as pltpu
from jax.experimental.pallas import tpu_sc as plsc

F32 = jnp.float32
BF16 = jnp.bfloat16
I32 = jnp.int32
U32 = jnp.uint32

A_HEADS = 8
CHUNK = 128
CONV_W = 3
X_HEADS = 4
N_EXPERTS = 32
TOP_K = 4
SWIGLU_LIMIT = 7.0
SWIGLU_ALPHA = 1.702
EPS = 1e-5

MIX_TILE = 512
ATT_TILE = 512
KV_TILE = 512
ROUTE_TILE = 512
MOE_BLOCK = 256
COMBINE_TILE = 256
SAMPLE_ATT_GROUP = 4
SC_CHUNK = 128
VMEM_LIMIT = 56 * 1024 * 1024


def _cparams(sem=None):
    return pltpu.CompilerParams(dimension_semantics=sem, vmem_limit_bytes=VMEM_LIMIT)


def _rms(x, g):
    r = lax.rsqrt(jnp.mean(x * x, axis=-1, keepdims=True) + EPS)
    return (x * r) * g


def _gelu(x):
    return 0.5 * x * (1.0 + lax.erf(x * 0.7071067811865476))


def _dot(a, b):
    return jnp.dot(a, b, preferred_element_type=F32)


def _dot_nt(a, b):
    return lax.dot_general(a, b, (((1,), (1,)), ((), ())), preferred_element_type=F32)


def _memkv_kernel(m_ref, g_ref, wk_ref, wv_ref, k_ref, v_ref):
    h = _rms(m_ref[...], g_ref[...]).astype(BF16)
    k_ref[...] = _dot(h, wk_ref[...])
    v_ref[...] = _dot(h, wv_ref[...])


def _mem_kv(mem2d, g_mem, wk, wv):
    n, d = mem2d.shape
    row = pl.BlockSpec((KV_TILE, d), lambda i: (i, 0))
    full = lambda shape: pl.BlockSpec(shape, lambda i: (0,) * len(shape))
    return pl.pallas_call(
        _memkv_kernel,
        grid=(n // KV_TILE,),
        in_specs=[row, full((1, d)), full((d, d)), full((d, d))],
        out_specs=[row, row],
        out_shape=[jax.ShapeDtypeStruct((n, d), F32)] * 2,
        compiler_params=_cparams(("arbitrary",)),
        name="mem_kv",
    )(mem2d, g_mem, wk, wv)


def _head_rms(v, gv, a_width):
    hd = a_width // A_HEADS
    r_i = lax.broadcasted_iota(I32, (a_width, a_width), 0) // hd
    c_i = lax.broadcasted_iota(I32, (a_width, a_width), 1) // hd
    ones_bd = jnp.where(r_i == c_i, 1.0, 0.0).astype(BF16)
    sq = v * v
    sq_hi = sq.astype(BF16)
    sq_lo = (sq - sq_hi.astype(F32)).astype(BF16)
    gs = _dot(sq_hi, ones_bd) + _dot(sq_lo, ones_bd)
    return (v * lax.rsqrt(gs * (1.0 / hd) + EPS)) * gv


def _mixer_prompt_kernel(x_ref, gmix_ref, win_ref, gv_ref, ws_ref, bs_ref, wc_ref, wout_ref,
                         x1_ref, cs_ref, zbuf, *, tile, a_width, b_width):
    j = pl.program_id(1)
    nj = pl.num_programs(1)
    x = x_ref[...]
    h = _rms(x, gmix_ref[...]).astype(BF16)
    proj = _dot(h, win_ref[...])
    u = _gelu(proj[:, :a_width])
    v = _head_rms(_gelu(proj[:, a_width:2 * a_width]), gv_ref[...], a_width)
    o = 2 * a_width
    hb = proj[:, o:o + b_width]
    bg = proj[:, o + b_width:o + 2 * b_width]
    cg = proj[:, o + 2 * b_width:o + 3 * b_width]

    lane = lax.broadcasted_iota(I32, (CHUNK, 2 * (a_width // A_HEADS)), 1)
    first = lane < (a_width // A_HEADS)
    t_i = lax.broadcasted_iota(I32, (CHUNK, 2 * CHUNK), 0)
    s_i = lax.broadcasted_iota(I32, (CHUNK, 2 * CHUNK), 1) % CHUNK
    causal = s_i <= t_i
    pair_cols = []
    for p in range(A_HEADS // 2):
        w_pair = jnp.where(causal, ws_ref[p], 0.0).astype(BF16)
        vp = v[:, p * CHUNK:(p + 1) * CHUNK]
        rows = []
        for c in range(tile // CHUNK):
            vc = vp[c * CHUNK:(c + 1) * CHUNK]
            rhs = jnp.concatenate([jnp.where(first, vc, 0.0), jnp.where(first, 0.0, vc)],
                                  axis=0).astype(BF16)
            rows.append(_dot(w_pair, rhs))
        pair_cols.append(jnp.concatenate(rows, axis=0))
    gate = jnp.concatenate(pair_cols, axis=1)
    bias = jnp.concatenate([bs_ref[...]] * (tile // CHUNK), axis=0)
    a_out = u * (gate + bias)

    @pl.when(j == 0)
    def _():
        zbuf[0:8, :] = jnp.zeros((8, b_width), F32)

    z = cg * hb
    zbuf[8:tile + 8, :] = z
    z1 = zbuf[7:tile + 7, :]
    z2 = zbuf[6:tile + 6, :]
    wc = wc_ref[...]
    conv = z2 * wc[0:1] + z1 * wc[1:2] + z * wc[2:3]
    b_out = bg * conv
    tail = zbuf[tile:tile + 8, :]
    zbuf[0:8, :] = tail

    @pl.when(j == nj - 1)
    def _():
        cs_ref[...] = tail[8 - (CONV_W - 1):, :]

    y = _dot(a_out.astype(BF16), wout_ref[0:a_width, :]) + _dot(b_out.astype(BF16), wout_ref[a_width:, :])
    x1_ref[...] = x + y


def _mixer_prompt(x, g_mix, w_in, g_v, ws_pairs, bs_full, w_conv, w_out):
    b, s, d = x.shape
    a_width = g_v.shape[1]
    b_width = w_conv.shape[1]
    in_width = w_in.shape[1]
    tile = MIX_TILE
    full = lambda shape: pl.BlockSpec(shape, lambda i, j: (0,) * len(shape))
    kern = functools.partial(_mixer_prompt_kernel, tile=tile, a_width=a_width, b_width=b_width)
    return pl.pallas_call(
        kern,
        grid=(b, s // tile),
        in_specs=[
            pl.BlockSpec((None, tile, d), lambda i, j: (i, j, 0)),
            full((1, d)), full((d, in_width)), full((1, a_width)),
            full(ws_pairs.shape), full(bs_full.shape), full(w_conv.shape), full(w_out.shape),
        ],
        out_specs=[
            pl.BlockSpec((None, tile, d), lambda i, j: (i, j, 0)),
            pl.BlockSpec((None, CONV_W - 1, b_width), lambda i, j: (i, 0, 0)),
        ],
        out_shape=[jax.ShapeDtypeStruct((b, s, d), F32),
                   jax.ShapeDtypeStruct((b, CONV_W - 1, b_width), F32)],
        scratch_shapes=[pltpu.VMEM((tile + 8, b_width), F32)],
        compiler_params=_cparams(("arbitrary", "arbitrary")),
        name="mixer_prompt",
    )(x, g_mix, w_in, g_v, ws_pairs, bs_full, w_conv, w_out)


def _mixer_sample_kernel(x_ref, s0_ref, s1_ref, gmix_ref, win_ref, gv_ref, w00_ref, b0_ref, wc_ref,
                         wout_ref, gx_ref, wq_ref, x1_ref, v_ref, z_ref, q_ref, *, a_width, b_width):
    x = x_ref[...]
    h = _rms(x, gmix_ref[...]).astype(BF16)
    proj = _dot(h, win_ref[...])
    u = _gelu(proj[:, :a_width])
    v = _head_rms(_gelu(proj[:, a_width:2 * a_width]), gv_ref[...], a_width)
    v_ref[...] = v
    o = 2 * a_width
    hb = proj[:, o:o + b_width]
    bg = proj[:, o + b_width:o + 2 * b_width]
    cg = proj[:, o + 2 * b_width:o + 3 * b_width]
    a_out = u * (v * w00_ref[...] + b0_ref[...])
    z = cg * hb
    z_ref[...] = z
    wc = wc_ref[...]
    conv = s0_ref[...] * wc[0:1] + s1_ref[...] * wc[1:2] + z * wc[2:3]
    b_out = bg * conv
    y = _dot(a_out.astype(BF16), wout_ref[0:a_width, :]) + _dot(b_out.astype(BF16), wout_ref[a_width:, :])
    x1 = x + y
    x1_ref[...] = x1
    q_ref[...] = _dot(_rms(x1, gx_ref[...]).astype(BF16), wq_ref[...])


def _mixer_sample(x, s0, s1, g_mix, w_in, g_v, w00, b0, w_conv, w_out, g_x, w_q):
    n, d = x.shape
    a_width = g_v.shape[1]
    b_width = w_conv.shape[1]
    kern = functools.partial(_mixer_sample_kernel, a_width=a_width, b_width=b_width)
    return pl.pallas_call(
        kern,
        out_shape=[jax.ShapeDtypeStruct((n, d), F32), jax.ShapeDtypeStruct((n, a_width), F32),
                   jax.ShapeDtypeStruct((n, b_width), F32), jax.ShapeDtypeStruct((n, d), F32)],
        compiler_params=_cparams(),
        name="mixer_sample",
    )(x, s0, s1, g_mix, w_in, g_v, w00, b0, w_conv, w_out, g_x, w_q)


def _router_tail(x2, gmoe, wrt, br, h3_ref, tope_ref, gates_ref):
    d = x2.shape[1]
    h3 = _rms(x2, gmoe).astype(BF16)
    h3f = h3.astype(F32)
    lo = lax.shift_right_logical(pltpu.bitcast(h3f[:, :d // 2], U32), jnp.uint32(16))
    hi = lax.bitwise_and(pltpu.bitcast(h3f[:, d // 2:], U32), jnp.uint32(0xFFFF0000))
    h3_ref[...] = lax.bitwise_or(lo, hi)

    logits = _dot_nt(wrt, h3) + br
    n_e, n_t = logits.shape
    e_idx = lax.broadcasted_iota(I32, (n_e, n_t), 0).astype(F32)
    tops, idxs = [], []
    for _ in range(TOP_K):
        m = jnp.max(logits, axis=0, keepdims=True)
        idx = jnp.min(jnp.where(logits == m, e_idx, float(n_e)), axis=0, keepdims=True)
        tops.append(m)
        idxs.append(idx)
        logits = jnp.where(e_idx == idx, -jnp.inf, logits)
    top = jnp.concatenate(tops, axis=0)
    ex = jnp.exp(top - top[0:1])
    gates_ref[...] = ex / jnp.sum(ex, axis=0, keepdims=True)
    tope_ref[...] = jnp.concatenate(idxs, axis=0).astype(I32)


def _attn_prompt_kernel(x1_ref, k_ref, v_ref, gx_ref, wq_ref, wxo_ref, gmoe_ref, wrt_ref, br_ref,
                        x2_ref, h3_ref, tope_ref, gates_ref):
    x = x1_ref[...]
    d = x.shape[1]
    hd = d // X_HEADS
    q = _dot(_rms(x, gx_ref[...]).astype(BF16), wq_ref[...]).astype(BF16)
    kb = k_ref[...].astype(BF16)
    vb = v_ref[...].astype(BF16)
    outs = []
    for hh in range(X_HEADS):
        sl = slice(hh * hd, (hh + 1) * hd)
        s = _dot_nt(q[:, sl], kb[:, sl]) * (hd ** -0.5)
        e = jnp.exp(s - jnp.max(s, axis=-1, keepdims=True))
        p = e / jnp.sum(e, axis=-1, keepdims=True)
        outs.append(_dot(p.astype(BF16), vb[:, sl]))
    o = jnp.concatenate(outs, axis=1).astype(BF16)
    x2 = x + _dot(o, wxo_ref[...])
    x2_ref[...] = x2
    _router_tail(x2, gmoe_ref[...], wrt_ref[...], br_ref[...], h3_ref, tope_ref, gates_ref)


def _attn_prompt(x1, mk, mv, g_x, w_q, w_xo, g_moe, wrt, br):
    b, s, d = x1.shape
    n_mem = mk.shape[1]
    tile = ATT_TILE
    nq = s // tile
    full = lambda shape: pl.BlockSpec(shape, lambda i, j: (0,) * len(shape))
    tok = pl.BlockSpec((None, tile, d), lambda i, j: (i, j, 0))
    mem = pl.BlockSpec((None, n_mem, d), lambda i, j: (i, 0, 0))
    lanes = pl.BlockSpec((TOP_K, tile), lambda i, j: (0, i * nq + j))
    return pl.pallas_call(
        _attn_prompt_kernel,
        grid=(b, nq),
        in_specs=[tok, mem, mem, full((1, d)), full((d, d)), full((d, d)), full((1, d)),
                  full(wrt.shape), full(br.shape)],
        out_specs=[tok, pl.BlockSpec((None, tile, d // 2), lambda i, j: (i, j, 0)), lanes, lanes],
        out_shape=[jax.ShapeDtypeStruct((b, s, d), F32), jax.ShapeDtypeStruct((b, s, d // 2), U32),
                   jax.ShapeDtypeStruct((TOP_K, b * s), I32), jax.ShapeDtypeStruct((TOP_K, b * s), F32)],
        compiler_params=_cparams(("arbitrary", "arbitrary")),
        name="attn_prompt",
    )(x1, mk, mv, g_x, w_q, w_xo, g_moe, wrt, br)


def _attn_sample_kernel(q_ref, k_ref, v_ref, o_ref, *, group):
    i = pl.program_id(0)
    d = q_ref.shape[1]
    hd = d // X_HEADS
    head = lax.broadcasted_iota(I32, (X_HEADS, d), 0)
    lane_head = lax.broadcasted_iota(I32, (X_HEADS, d), 1) // hd
    own = head == lane_head
    for g in range(group):
        row = i * group + g
        qm = jnp.where(own, q_ref[pl.ds(row, 1), :], 0.0).astype(BF16)
        s = _dot_nt(qm, k_ref[g].astype(BF16)) * (hd ** -0.5)
        e = jnp.exp(s - jnp.max(s, axis=-1, keepdims=True))
        p = e / jnp.sum(e, axis=-1, keepdims=True)
        o4 = _dot(p.astype(BF16), v_ref[g].astype(BF16))
        o_ref[pl.ds(row, 1), :] = jnp.sum(jnp.where(own, o4, 0.0), axis=0, keepdims=True)


def _attn_sample(q, ck, cv):
    n, d = q.shape
    n_mem = ck.shape[1]
    group = SAMPLE_ATT_GROUP
    whole = pl.BlockSpec((n, d), lambda i: (0, 0))
    kv = pl.BlockSpec((group, n_mem, d), lambda i: (i, 0, 0))
    return pl.pallas_call(
        functools.partial(_attn_sample_kernel, group=group),
        grid=(n // group,),
        in_specs=[whole, kv, kv],
        out_specs=whole,
        out_shape=jax.ShapeDtypeStruct((n, d), F32),
        compiler_params=_cparams(("arbitrary",)),
        name="attn_sample",
    )(q, ck, cv)


def _tail_sample_kernel(x1_ref, o_ref, wxo_ref, gmoe_ref, wrt_ref, br_ref,
                        x2_ref, h3_ref, tope_ref, gates_ref):
    x2 = x1_ref[...] + _dot(o_ref[...].astype(BF16), wxo_ref[...])
    x2_ref[...] = x2
    _router_tail(x2, gmoe_ref[...], wrt_ref[...], br_ref[...], h3_ref, tope_ref, gates_ref)


def _tail_sample(x1, o, w_xo, g_moe, wrt, br):
    n, d = x1.shape
    return pl.pallas_call(
        _tail_sample_kernel,
        out_shape=[jax.ShapeDtypeStruct((n, d), F32), jax.ShapeDtypeStruct((n, d // 2), U32),
                   jax.ShapeDtypeStruct((TOP_K, n), I32), jax.ShapeDtypeStruct((TOP_K, n), F32)],
        compiler_params=_cparams(),
        name="tail_sample",
    )(x1, o, w_xo, g_moe, wrt, br)


def _route_kernel(tope_ref, dest_ref, cnt_ref, *, n_blocks, tile, moe_block):
    e_idx = lax.broadcasted_iota(I32, (N_EXPERTS, tile), 0)
    earlier = jnp.where(lax.broadcasted_iota(I32, (tile, tile), 0)
                        < lax.broadcasted_iota(I32, (tile, tile), 1), 1.0, 0.0).astype(BF16)

    def onehot(k, off):
        return e_idx == tope_ref[pl.ds(k, 1), pl.ds(off, tile)]

    def count_body(j, cnt):
        off = pl.multiple_of(j * tile, tile)
        for k in range(TOP_K):
            cnt = cnt + jnp.sum(jnp.where(onehot(k, off), 1.0, 0.0), axis=1, keepdims=True)
        return cnt

    counts = lax.fori_loop(0, n_blocks, count_body, jnp.zeros((N_EXPERTS, 1), F32))
    cnt_ref[...] = jnp.broadcast_to(counts, cnt_ref.shape).astype(I32)

    n_blk = jnp.floor((counts + (moe_block - 1)) * (1.0 / moe_block))
    n_hi = jnp.floor(n_blk * (1.0 / 16.0))
    n_lo = n_blk - 16.0 * n_hi
    below = jnp.where(lax.broadcasted_iota(I32, (N_EXPERTS, N_EXPERTS), 1)
                      < lax.broadcasted_iota(I32, (N_EXPERTS, N_EXPERTS), 0), 1.0, 0.0).astype(BF16)
    wide = lambda c: jnp.broadcast_to(c, (N_EXPERTS, 128)).astype(BF16)
    start = (16.0 * _dot(below, wide(n_hi)) + _dot(below, wide(n_lo)))[:, 0:1] * float(moe_block)

    def dest_body(j, run):
        off = pl.multiple_of(j * tile, tile)
        for k in range(TOP_K):
            oh = onehot(k, off)
            ohf = jnp.where(oh, 1.0, 0.0)
            before = _dot(ohf.astype(BF16), earlier)
            dest = jnp.sum(jnp.where(oh, before + run, 0.0), axis=0, keepdims=True)
            dest_ref[pl.ds(k, 1), pl.ds(off, tile)] = dest.astype(I32)
            run = run + jnp.sum(ohf, axis=1, keepdims=True)
        return run

    lax.fori_loop(0, n_blocks, dest_body, start)


def _route(tope_pad):
    t_pad = tope_pad.shape[1]
    kern = functools.partial(_route_kernel, n_blocks=t_pad // ROUTE_TILE, tile=ROUTE_TILE,
                             moe_block=MOE_BLOCK)
    return pl.pallas_call(
        kern,
        out_shape=[jax.ShapeDtypeStruct((TOP_K, t_pad), I32),
                   jax.ShapeDtypeStruct((N_EXPERTS, 128), I32)],
        compiler_params=_cparams(),
        name="route",
    )(tope_pad)


def _dispatch(dest3, pad3, zero_rows, h_all, n_rows):
    n_tok, w = h_all.shape
    n_chunks = dest3.shape[0]
    n_pad_chunks = pad3.shape[0]
    mesh = plsc.VectorSubcoreMesh(core_axis_name="c", subcore_axis_name="s")
    n_workers = mesh.num_cores * mesh.num_subcores

    def body(dest_hbm, pad_hbm, zero_hbm, h_hbm, xs_hbm, idx_v, rows_v):
        wid = lax.axis_index("s") * mesh.num_cores + lax.axis_index("c")
        pltpu.sync_copy(zero_hbm, rows_v)
        for j in range(-(-n_pad_chunks // n_workers)):
            c = wid + j * n_workers

            @pl.when(c < n_pad_chunks)
            def _():
                pltpu.sync_copy(pad_hbm.at[c], idx_v.at[pl.ds(0, 1)])
                pltpu.sync_copy(rows_v, xs_hbm.at[idx_v.at[0]])

        for j in range(-(-n_chunks // n_workers)):
            c = wid + j * n_workers

            @pl.when(c < n_chunks)
            def _():
                pltpu.sync_copy(dest_hbm.at[c], idx_v)
                pltpu.sync_copy(h_hbm.at[pl.ds(c * SC_CHUNK, SC_CHUNK)], rows_v)
                for k in range(TOP_K):
                    pltpu.sync_copy(rows_v, xs_hbm.at[idx_v.at[k]])

    return pl.kernel(
        body,
        out_type=jax.ShapeDtypeStruct((n_rows, w), U32),
        mesh=mesh,
        scratch_types=[pltpu.VMEM((TOP_K, SC_CHUNK), I32), pltpu.VMEM((SC_CHUNK, w), U32)],
        name="dispatch_sc",
    )(dest3, pad3, zero_rows, h_all)


def _ffn_kernel(be_ref, nu_ref, xs_ref, w1_ref, b1_ref, w2_ref, b2_ref, y_ref, w1b, w2b):
    i = pl.program_id(0)

    @pl.when(i >= nu_ref[0])
    def _():
        y_ref[...] = jnp.zeros(y_ref.shape, y_ref.dtype)

    @pl.when(i < nu_ref[0])
    def _():
        e = be_ref[i]
        prev = be_ref[jnp.maximum(i - 1, 0)]

        @pl.when((i == 0) | (e != prev))
        def _():
            w1b[...] = w1_ref[...].astype(BF16)
            w2b[...] = w2_ref[...].astype(BF16)

        w = xs_ref[...]
        half = w.shape[1]
        d_e = w2b.shape[0]
        lo = pltpu.bitcast(lax.shift_left(w, jnp.uint32(16)), F32).astype(BF16)
        hi = pltpu.bitcast(lax.bitwise_and(w, jnp.uint32(0xFFFF0000)), F32).astype(BF16)
        gu = _dot(lo, w1b[0:half, :]) + _dot(hi, w1b[half:, :]) + b1_ref[...]
        gate = jnp.minimum(gu[:, :d_e], SWIGLU_LIMIT)
        up = jnp.clip(gu[:, d_e:], -SWIGLU_LIMIT, SWIGLU_LIMIT)
        glu = gate * jax.nn.sigmoid(gate * SWIGLU_ALPHA)
        act = ((up + 1.0) * glu).astype(BF16)
        y_ref[...] = _dot(act, w2b[...]) + b2_ref[...]


def _expert_ffn(block_e, n_used, xs, w1, b1, w2, b2):
    n_rows, half = xs.shape
    n_e, d, d2 = w1.shape
    d_e = w2.shape[1]
    blk = MOE_BLOCK
    last = lambda i, be, nu: jnp.minimum(i, nu[0] - 1)
    grid_spec = pltpu.PrefetchScalarGridSpec(
        num_scalar_prefetch=2,
        grid=(n_rows // blk,),
        in_specs=[
            pl.BlockSpec((blk, half), lambda i, be, nu: (last(i, be, nu), 0)),
            pl.BlockSpec((None, d, d2), lambda i, be, nu: (be[last(i, be, nu)], 0, 0)),
            pl.BlockSpec((None, 1, d2), lambda i, be, nu: (be[last(i, be, nu)], 0, 0)),
            pl.BlockSpec((None, d_e, d), lambda i, be, nu: (be[last(i, be, nu)], 0, 0)),
            pl.BlockSpec((None, 1, d), lambda i, be, nu: (be[last(i, be, nu)], 0, 0)),
        ],
        out_specs=pl.BlockSpec((blk, d), lambda i, be, nu: (i, 0)),
        scratch_shapes=[pltpu.VMEM((d, d2), BF16), pltpu.VMEM((d_e, d), BF16)],
    )
    return pl.pallas_call(
        _ffn_kernel,
        grid_spec=grid_spec,
        out_shape=jax.ShapeDtypeStruct((n_rows, d), F32),
        compiler_params=_cparams(("arbitrary",)),
        name="expert_ffn",
    )(block_e, n_used, xs, w1, b1.reshape(n_e, 1, d2), w2, b2.reshape(n_e, 1, d))


def _combine_kernel(dest_ref, x2_ref, gates_ref, gf_ref, y_hbm, out_ref, buf, sem, *, tile):
    def row_copy(d, k, t):
        return pltpu.make_async_copy(y_hbm.at[pl.ds(d, 1)], buf.at[k, pl.ds(t, 1)], sem.at[0])

    def issue(t, c):
        for k in range(TOP_K):
            row_copy(dest_ref[k, t], k, t).start()
        return c

    def drain(t, c):
        for k in range(TOP_K):
            row_copy(0, k, t).wait()
        return c

    lax.fori_loop(0, tile, issue, 0)
    lax.fori_loop(0, tile, drain, 0)

    g = jnp.concatenate([gates_ref[...], jnp.zeros((128 - TOP_K, tile), F32)], axis=0)
    gt = g.T
    moe = gt[:, 0:1] * buf[0]
    for k in range(1, TOP_K):
        moe = moe + gt[:, k:k + 1] * buf[k]
    out_ref[...] = _rms(x2_ref[...] + moe, gf_ref[...])


def _combine(dest, x2, gates, g_final, y, tile):
    n, d = x2.shape
    grid_spec = pltpu.PrefetchScalarGridSpec(
        num_scalar_prefetch=0,
        grid=(n // tile,),
        in_specs=[
            pl.BlockSpec((TOP_K, tile), lambda i: (0, i), memory_space=pltpu.SMEM),
            pl.BlockSpec((tile, d), lambda i: (i, 0)),
            pl.BlockSpec((TOP_K, tile), lambda i: (0, i)),
            pl.BlockSpec((1, d), lambda i: (0, 0)),
            pl.BlockSpec(memory_space=pl.ANY),
        ],
        out_specs=pl.BlockSpec((tile, d), lambda i: (i, 0)),
        scratch_shapes=[pltpu.VMEM((TOP_K, tile, d), F32), pltpu.SemaphoreType.DMA((1,))],
    )
    return pl.pallas_call(
        functools.partial(_combine_kernel, tile=tile),
        grid_spec=grid_spec,
        out_shape=jax.ShapeDtypeStruct((n, d), F32),
        compiler_params=_cparams(("arbitrary",)),
        name="combine",
    )(dest, x2, gates, g_final, y)


def kernel(x_prompt, x_sample, mem_prompt, cache_mem_k, cache_mem_v, state_conv, g_mix, w_in, g_v,
           w_spatial, b_spatial, w_conv, w_out, g_xattn, g_mem, w_q, w_k, w_v, w_xo, g_moe, w_router,
           b_router, w_gate_up, b_gate_up, w_down, b_down, g_final):
    depth = g_mix.shape[0]
    assert depth == 1, "one layer supported"
    b, s, d = x_prompt.shape
    nb, ns, _ = x_sample.shape
    assert ns == 1
    n_mem = mem_prompt.shape[1]
    a_heads, a_hd = g_v.shape[1], g_v.shape[2]
    a_width = a_heads * a_hd
    b_width = w_conv.shape[2]
    assert a_heads == A_HEADS and 2 * a_hd == CHUNK and w_spatial.shape[2] == CHUNK
    assert s % MIX_TILE == 0 and s % ATT_TILE == 0 and (b * n_mem) % KV_TILE == 0
    assert nb % SAMPLE_ATT_GROUP == 0
    l = 0
    row = lambda a: a.reshape(1, -1)

    w_in_b = w_in[l].astype(BF16)
    w_out_b = w_out[l].astype(BF16)
    w_q_b = w_q[l].astype(BF16)
    w_k_b = w_k[l].astype(BF16)
    w_v_b = w_v[l].astype(BF16)
    w_xo_b = w_xo[l].astype(BF16)
    wrt_b = w_router[l].T.astype(BF16)
    br_col = b_router[l].reshape(N_EXPERTS, 1).astype(F32)
    gv_row = row(g_v[l])
    ws = w_spatial[l]
    ws_pairs = jnp.concatenate([ws[0::2], ws[1::2]], axis=2)
    bs_full = jnp.repeat(b_spatial[l].T, a_hd, axis=1)
    w00 = row(jnp.repeat(ws[:, 0, 0], a_hd))
    b0 = row(jnp.repeat(b_spatial[l][:, 0], a_hd))

    mk2, mv2 = _mem_kv(mem_prompt.reshape(b * n_mem, d), row(g_mem[l]), w_k_b, w_v_b)
    x1_p, conv_p = _mixer_prompt(x_prompt, row(g_mix[l]), w_in_b, gv_row, ws_pairs, bs_full,
                                 w_conv[l], w_out_b)
    x2_p, h3_p, tope_p, gates_p = _attn_prompt(
        x1_p, mk2.reshape(b, n_mem, d), mv2.reshape(b, n_mem, d), row(g_xattn[l]), w_q_b, w_xo_b,
        row(g_moe[l]), wrt_b, br_col)

    xs2 = x_sample.reshape(nb, d)
    st = state_conv[l]
    x1_s, v_s, z_s, q_s = _mixer_sample(xs2, st[:, 0], st[:, 1], row(g_mix[l]), w_in_b, gv_row, w00, b0,
                                        w_conv[l], w_out_b, row(g_xattn[l]), w_q_b)
    o_s = _attn_sample(q_s, cache_mem_k[l].reshape(nb, n_mem, d), cache_mem_v[l].reshape(nb, n_mem, d))
    x2_s, h3_s, tope_s, gates_s = _tail_sample(x1_s, o_s, w_xo_b, row(g_moe[l]), wrt_b, br_col)

    t_p = b * s
    t_all = t_p + nb
    t_pad = -(-t_all // ROUTE_TILE) * ROUTE_TILE
    h_all = jnp.concatenate([h3_p.reshape(t_p, d // 2), h3_s], axis=0)
    tope_all = jnp.concatenate([tope_p, tope_s, jnp.full((TOP_K, t_pad - t_all), -1, I32)], axis=1)
    dest, cnt = _route(tope_all)
    counts = cnt[:, 0]
    padded = (counts + MOE_BLOCK - 1) // MOE_BLOCK * MOE_BLOCK
    pad_ends = jnp.cumsum(padded)
    pad_starts = pad_ends - padded
    n_blocks = -(-(t_all * TOP_K) // MOE_BLOCK) + N_EXPERTS
    block_first = jnp.arange(n_blocks, dtype=I32) * MOE_BLOCK
    block_e = jnp.minimum(jnp.sum(pad_ends[None, :] <= block_first[:, None], axis=1),
                          N_EXPERTS - 1).astype(I32)
    n_used = (pad_ends[-1] // MOE_BLOCK).astype(I32).reshape(1)
    n_rows = n_blocks * MOE_BLOCK
    assert t_all % SC_CHUNK == 0
    dest3 = dest[:, :t_all].reshape(TOP_K, t_all // SC_CHUNK, SC_CHUNK).transpose(1, 0, 2)
    pad_rows = (pad_starts + counts)[:, None] + jnp.arange(MOE_BLOCK, dtype=I32)[None, :]
    pad3 = jnp.where(pad_rows < pad_ends[:, None], pad_rows, n_rows - 1).astype(I32)
    pad3 = pad3.reshape(-1, 1, SC_CHUNK)
    xs_rows = _dispatch(dest3, pad3, jnp.zeros((SC_CHUNK, d // 2), U32), h_all, n_rows)
    y_rows = _expert_ffn(block_e, n_used, xs_rows, w_gate_up[l], b_gate_up[l], w_down[l], b_down[l])

    gf = row(g_final)
    y_p = _combine(dest[:, :t_p], x2_p.reshape(t_p, d), gates_p, gf, y_rows, COMBINE_TILE)
    y_s = _combine(dest[:, t_p:t_all], x2_s, gates_s, gf, y_rows, nb)

    x_heads = cache_mem_k.shape[3]
    return (y_p.reshape(b, s, d),
            y_s.reshape(nb, 1, d),
            mk2.reshape(1, b, n_mem, x_heads, d // x_heads),
            mv2.reshape(1, b, n_mem, x_heads, d // x_heads),
            conv_p.reshape(1, b, CONV_W - 1, b_width),
            jnp.stack([st[:, 1], z_s], axis=1).reshape(1, nb, CONV_W - 1, b_width),
            v_s.reshape(1, nb, 1, a_heads, a_hd))
```

```python
import functools

import jax
import jax.numpy as jnp
from jax import lax
from jax.experimental import pallas as pl
from jax.experimental.pallas import tpu as pltpu
from jax.experimental.pallas import tpu_sc as plsc

F32 = jnp.float32
BF16 = jnp.bfloat16
I32 = jnp.int32
U32 = jnp.uint32

A_HEADS = 8
CHUNK = 128
CONV_W = 3
X_HEADS = 4
N_EXPERTS = 32
TOP_K = 4
SWIGLU_LIMIT = 7.0
SWIGLU_ALPHA = 1.702
EPS = 1e-5

MIX_TILE = 512
ATT_TILE = 512
KV_TILE = 512
ROUTE_TILE = 512
MOE_BLOCK = 256
COMBINE_TILE = 256
SAMPLE_ATT_GROUP = 4
SC_CHUNK = 128
VMEM_LIMIT = 56 * 1024 * 1024


def _cparams(sem=None):
    return pltpu.CompilerParams(dimension_semantics=sem, vmem_limit_bytes=VMEM_LIMIT)


def _rms(x, g):
    r = lax.rsqrt(jnp.mean(x * x, axis=-1, keepdims=True) + EPS)
    return (x * r) * g


def _gelu(x):
    return 0.5 * x * (1.0 + lax.erf(x * 0.7071067811865476))


def _dot(a, b):
    return jnp.dot(a, b, preferred_element_type=F32)


def _dot_nt(a, b):
    return lax.dot_general(a, b, (((1,), (1,)), ((), ())), preferred_element_type=F32)


def _memkv_kernel(m_ref, g_ref, wk_ref, wv_ref, k_ref, v_ref):
    h = _rms(m_ref[...], g_ref[...]).astype(BF16)
    k_ref[...] = _dot(h, wk_ref[...])
    v_ref[...] = _dot(h, wv_ref[...])


def _mem_kv(mem2d, g_mem, wk, wv):
    n, d = mem2d.shape
    row = pl.BlockSpec((KV_TILE, d), lambda i: (i, 0))
    full = lambda shape: pl.BlockSpec(shape, lambda i: (0,) * len(shape))
    return pl.pallas_call(
        _memkv_kernel,
        grid=(n // KV_TILE,),
        in_specs=[row, full((1, d)), full((d, d)), full((d, d))],
        out_specs=[row, row],
        out_shape=[jax.ShapeDtypeStruct((n, d), F32)] * 2,
        compiler_params=_cparams(("arbitrary",)),
        name="mem_kv",
    )(mem2d, g_mem, wk, wv)


def _head_rms(v, gv, a_width):
    hd = a_width // A_HEADS
    r_i = lax.broadcasted_iota(I32, (a_width, a_width), 0) // hd
    c_i = lax.broadcasted_iota(I32, (a_width, a_width), 1) // hd
    ones_bd = jnp.where(r_i == c_i, 1.0, 0.0).astype(BF16)
    sq = v * v
    sq_hi = sq.astype(BF16)
    sq_lo = (sq - sq_hi.astype(F32)).astype(BF16)
    gs = _dot(sq_hi, ones_bd) + _dot(sq_lo, ones_bd)
    return (v * lax.rsqrt(gs * (1.0 / hd) + EPS)) * gv


def _mixer_prompt_kernel(x_ref, gmix_ref, win_ref, gv_ref, ws_ref, bs_ref, wc_ref, wout_ref,
                         x1_ref, cs_ref, zbuf, *, tile, a_width, b_width):
    j = pl.program_id(1)
    nj = pl.num_programs(1)
    x = x_ref[...]
    h = _rms(x, gmix_ref[...]).astype(BF16)
    proj = _dot(h, win_ref[...])
    u = _gelu(proj[:, :a_width])
    v = _head_rms(_gelu(proj[:, a_width:2 * a_width]), gv_ref[...], a_width)
    o = 2 * a_width
    hb = proj[:, o:o + b_width]
    bg = proj[:, o + b_width:o + 2 * b_width]
    cg = proj[:, o + 2 * b_width:o + 3 * b_width]

    lane = lax.broadcasted_iota(I32, (CHUNK, 2 * (a_width // A_HEADS)), 1)
    first = lane < (a_width // A_HEADS)
    t_i = lax.broadcasted_iota(I32, (CHUNK, 2 * CHUNK), 0)
    s_i = lax.broadcasted_iota(I32, (CHUNK, 2 * CHUNK), 1) % CHUNK
    causal = s_i <= t_i
    pair_cols = []
    for p in range(A_HEADS // 2):
        w_pair = jnp.where(causal, ws_ref[p], 0.0).astype(BF16)
        vp = v[:, p * CHUNK:(p + 1) * CHUNK]
        rows = []
        for c in range(tile // CHUNK):
            vc = vp[c * CHUNK:(c + 1) * CHUNK]
            rhs = jnp.concatenate([jnp.where(first, vc, 0.0), jnp.where(first, 0.0, vc)],
                                  axis=0).astype(BF16)
            rows.append(_dot(w_pair, rhs))
        pair_cols.append(jnp.concatenate(rows, axis=0))
    gate = jnp.concatenate(pair_cols, axis=1)
    bias = jnp.concatenate([bs_ref[...]] * (tile // CHUNK), axis=0)
    a_out = u * (gate + bias)

    @pl.when(j == 0)
    def _():
        zbuf[0:8, :] = jnp.zeros((8, b_width), F32)

    z = cg * hb
    zbuf[8:tile + 8, :] = z
    z1 = zbuf[7:tile + 7, :]
    z2 = zbuf[6:tile + 6, :]
    wc = wc_ref[...]
    conv = z2 * wc[0:1] + z1 * wc[1:2] + z * wc[2:3]
    b_out = bg * conv
    tail = zbuf[tile:tile + 8, :]
    zbuf[0:8, :] = tail

    @pl.when(j == nj - 1)
    def _():
        cs_ref[...] = tail[8 - (CONV_W - 1):, :]

    y = _dot(a_out.astype(BF16), wout_ref[0:a_width, :]) + _dot(b_out.astype(BF16), wout_ref[a_width:, :])
    x1_ref[...] = x + y


def _mixer_prompt(x, g_mix, w_in, g_v, ws_pairs, bs_full, w_conv, w_out):
    b, s, d = x.shape
    a_width = g_v.shape[1]
    b_width = w_conv.shape[1]
    in_width = w_in.shape[1]
    tile = MIX_TILE
    full = lambda shape: pl.BlockSpec(shape, lambda i, j: (0,) * len(shape))
    kern = functools.partial(_mixer_prompt_kernel, tile=tile, a_width=a_width, b_width=b_width)
    return pl.pallas_call(
        kern,
        grid=(b, s // tile),
        in_specs=[
            pl.BlockSpec((None, tile, d), lambda i, j: (i, j, 0)),
            full((1, d)), full((d, in_width)), full((1, a_width)),
            full(ws_pairs.shape), full(bs_full.shape), full(w_conv.shape), full(w_out.shape),
        ],
        out_specs=[
            pl.BlockSpec((None, tile, d), lambda i, j: (i, j, 0)),
            pl.BlockSpec((None, CONV_W - 1, b_width), lambda i, j: (i, 0, 0)),
        ],
        out_shape=[jax.ShapeDtypeStruct((b, s, d), F32),
                   jax.ShapeDtypeStruct((b, CONV_W - 1, b_width), F32)],
        scratch_shapes=[pltpu.VMEM((tile + 8, b_width), F32)],
        compiler_params=_cparams(("arbitrary", "arbitrary")),
        name="mixer_prompt",
    )(x, g_mix, w_in, g_v, ws_pairs, bs_full, w_conv, w_out)


def _mixer_sample_kernel(x_ref, s0_ref, s1_ref, gmix_ref, win_ref, gv_ref, w00_ref, b0_ref, wc_ref,
                         wout_ref, gx_ref, wq_ref, x1_ref, v_ref, z_ref, q_ref, *, a_width, b_width):
    x = x_ref[...]
    h = _rms(x, gmix_ref[...]).astype(BF16)
    proj = _dot(h, win_ref[...])
    u = _gelu(proj[:, :a_width])
    v = _head_rms(_gelu(proj[:, a_width:2 * a_width]), gv_ref[...], a_width)
    v_ref[...] = v
    o = 2 * a_width
    hb = proj[:, o:o + b_width]
    bg = proj[:, o + b_width:o + 2 * b_width]
    cg = proj[:, o + 2 * b_width:o + 3 * b_width]
    a_out = u * (v * w00_ref[...] + b0_ref[...])
    z = cg * hb
    z_ref[...] = z
    wc = wc_ref[...]
    conv = s0_ref[...] * wc[0:1] + s1_ref[...] * wc[1:2] + z * wc[2:3]
    b_out = bg * conv
    y = _dot(a_out.astype(BF16), wout_ref[0:a_width, :]) + _dot(b_out.astype(BF16), wout_ref[a_width:, :])
    x1 = x + y
    x1_ref[...] = x1
    q_ref[...] = _dot(_rms(x1, gx_ref[...]).astype(BF16), wq_ref[...])


def _mixer_sample(x, s0, s1, g_mix, w_in, g_v, w00, b0, w_conv, w_out, g_x, w_q):
    n, d = x.shape
    a_width = g_v.shape[1]
    b_width = w_conv.shape[1]
    kern = functools.partial(_mixer_sample_kernel, a_width=a_width, b_width=b_width)
    return pl.pallas_call(
        kern,
        out_shape=[jax.ShapeDtypeStruct((n, d), F32), jax.ShapeDtypeStruct((n, a_width), F32),
                   jax.ShapeDtypeStruct((n, b_width), F32), jax.ShapeDtypeStruct((n, d), F32)],
        compiler_params=_cparams(),
        name="mixer_sample",
    )(x, s0, s1, g_mix, w_in, g_v, w00, b0, w_conv, w_out, g_x, w_q)


def _router_tail(x2, gmoe, wrt, br, h3_ref, tope_ref, gates_ref):
    d = x2.shape[1]
    h3 = _rms(x2, gmoe).astype(BF16)
    h3f = h3.astype(F32)
    lo = lax.shift_right_logical(pltpu.bitcast(h3f[:, :d // 2], U32), jnp.uint32(16))
    hi = lax.bitwise_and(pltpu.bitcast(h3f[:, d // 2:], U32), jnp.uint32(0xFFFF0000))
    h3_ref[...] = lax.bitwise_or(lo, hi)

    logits = _dot_nt(wrt, h3) + br
    n_e, n_t = logits.shape
    e_idx = lax.broadcasted_iota(I32, (n_e, n_t), 0).astype(F32)
    tops, idxs = [], []
    for _ in range(TOP_K):
        m = jnp.max(logits, axis=0, keepdims=True)
        idx = jnp.min(jnp.where(logits == m, e_idx, float(n_e)), axis=0, keepdims=True)
        tops.append(m)
        idxs.append(idx)
        logits = jnp.where(e_idx == idx, -jnp.inf, logits)
    top = jnp.concatenate(tops, axis=0)
    ex = jnp.exp(top - top[0:1])
    gates_ref[...] = ex / jnp.sum(ex, axis=0, keepdims=True)
    tope_ref[...] = jnp.concatenate(idxs, axis=0).astype(I32)


def _attn_prompt_kernel(x1_ref, k_ref, v_ref, gx_ref, wq_ref, wxo_ref, gmoe_ref, wrt_ref, br_ref,
                        x2_ref, h3_ref, tope_ref, gates_ref):
    x = x1_ref[...]
    d = x.shape[1]
    hd = d // X_HEADS
    q = _dot(_rms(x, gx_ref[...]).astype(BF16), wq_ref[...]).astype(BF16)
    kb = k_ref[...].astype(BF16)
    vb = v_ref[...].astype(BF16)
    outs = []
    for hh in range(X_HEADS):
        sl = slice(hh * hd, (hh + 1) * hd)
        s = _dot_nt(q[:, sl], kb[:, sl]) * (hd ** -0.5)
        e = jnp.exp(s - jnp.max(s, axis=-1, keepdims=True))
        p = e / jnp.sum(e, axis=-1, keepdims=True)
        outs.append(_dot(p.astype(BF16), vb[:, sl]))
    o = jnp.concatenate(outs, axis=1).astype(BF16)
    x2 = x + _dot(o, wxo_ref[...])
    x2_ref[...] = x2
    _router_tail(x2, gmoe_ref[...], wrt_ref[...], br_ref[...], h3_ref, tope_ref, gates_ref)


def _attn_prompt(x1, mk, mv, g_x, w_q, w_xo, g_moe, wrt, br):
    b, s, d = x1.shape
    n_mem = mk.shape[1]
    tile = ATT_TILE
    nq = s // tile
    full = lambda shape: pl.BlockSpec(shape, lambda i, j: (0,) * len(shape))
    tok = pl.BlockSpec((None, tile, d), lambda i, j: (i, j, 0))
    mem = pl.BlockSpec((None, n_mem, d), lambda i, j: (i, 0, 0))
    lanes = pl.BlockSpec((TOP_K, tile), lambda i, j: (0, i * nq + j))
    return pl.pallas_call(
        _attn_prompt_kernel,
        grid=(b, nq),
        in_specs=[tok, mem, mem, full((1, d)), full((d, d)), full((d, d)), full((1, d)),
                  full(wrt.shape), full(br.shape)],
        out_specs=[tok, pl.BlockSpec((None, tile, d // 2), lambda i, j: (i, j, 0)), lanes, lanes],
        out_shape=[jax.ShapeDtypeStruct((b, s, d), F32), jax.ShapeDtypeStruct((b, s, d // 2), U32),
                   jax.ShapeDtypeStruct((TOP_K, b * s), I32), jax.ShapeDtypeStruct((TOP_K, b * s), F32)],
        compiler_params=_cparams(("arbitrary", "arbitrary")),
        name="attn_prompt",
    )(x1, mk, mv, g_x, w_q, w_xo, g_moe, wrt, br)


def _attn_sample_kernel(q_ref, k_ref, v_ref, o_ref, *, group):
    hd = q_ref.shape[2]
    for g in range(group):
        q = q_ref[g]
        s = jnp.sum(k_ref[g] * q[None], axis=-1, keepdims=True) * (hd ** -0.5)
        e = jnp.exp(s - jnp.max(s, axis=0, keepdims=True))
        p = e / jnp.sum(e, axis=0, keepdims=True)
        o_ref[g] = jnp.sum(p * v_ref[g], axis=0)


def _attn_sample(q, ck, cv):
    n, heads, hd = q.shape
    n_mem = ck.shape[1]
    group = SAMPLE_ATT_GROUP
    qo = pl.BlockSpec((group, heads, hd), lambda i: (i, 0, 0))
    kv = pl.BlockSpec((group, n_mem, heads, hd), lambda i: (i, 0, 0, 0))
    return pl.pallas_call(
        functools.partial(_attn_sample_kernel, group=group),
        grid=(n // group,),
        in_specs=[qo, kv, kv],
        out_specs=qo,
        out_shape=jax.ShapeDtypeStruct((n, heads, hd), F32),
        compiler_params=_cparams(("arbitrary",)),
        name="attn_sample",
    )(q, ck, cv)


def _tail_sample_kernel(x1_ref, o_ref, wxo_ref, gmoe_ref, wrt_ref, br_ref,
                        x2_ref, h3_ref, tope_ref, gates_ref):
    x2 = x1_ref[...] + _dot(o_ref[...].astype(BF16), wxo_ref[...])
    x2_ref[...] = x2
    _router_tail(x2, gmoe_ref[...], wrt_ref[...], br_ref[...], h3_ref, tope_ref, gates_ref)


def _tail_sample(x1, o, w_xo, g_moe, wrt, br):
    n, d = x1.shape
    return pl.pallas_call(
        _tail_sample_kernel,
        out_shape=[jax.ShapeDtypeStruct((n, d), F32), jax.ShapeDtypeStruct((n, d // 2), U32),
                   jax.ShapeDtypeStruct((TOP_K, n), I32), jax.ShapeDtypeStruct((TOP_K, n), F32)],
        compiler_params=_cparams(),
        name="tail_sample",
    )(x1, o, w_xo, g_moe, wrt, br)


def _route_kernel(tope_ref, dest_ref, cnt_ref, *, n_blocks, tile, moe_block):
    e_idx = lax.broadcasted_iota(I32, (N_EXPERTS, tile), 0)
    earlier = jnp.where(lax.broadcasted_iota(I32, (tile, tile), 0)
                        < lax.broadcasted_iota(I32, (tile, tile), 1), 1.0, 0.0).astype(BF16)

    def onehot(k, off):
        return e_idx == tope_ref[pl.ds(k, 1), pl.ds(off, tile)]

    def count_body(j, cnt):
        off = pl.multiple_of(j * tile, tile)
        for k in range(TOP_K):
            cnt = cnt + jnp.sum(jnp.where(onehot(k, off), 1.0, 0.0), axis=1, keepdims=True)
        return cnt

    counts = lax.fori_loop(0, n_blocks, count_body, jnp.zeros((N_EXPERTS, 1), F32))
    cnt_ref[...] = jnp.broadcast_to(counts, cnt_ref.shape).astype(I32)

    n_blk = jnp.floor((counts + (moe_block - 1)) * (1.0 / moe_block))
    n_hi = jnp.floor(n_blk * (1.0 / 16.0))
    n_lo = n_blk - 16.0 * n_hi
    below = jnp.where(lax.broadcasted_iota(I32, (N_EXPERTS, N_EXPERTS), 1)
                      < lax.broadcasted_iota(I32, (N_EXPERTS, N_EXPERTS), 0), 1.0, 0.0).astype(BF16)
    wide = lambda c: jnp.broadcast_to(c, (N_EXPERTS, 128)).astype(BF16)
    start = (16.0 * _dot(below, wide(n_hi)) + _dot(below, wide(n_lo)))[:, 0:1] * float(moe_block)

    def dest_body(j, run):
        off = pl.multiple_of(j * tile, tile)
        for k in range(TOP_K):
            oh = onehot(k, off)
            ohf = jnp.where(oh, 1.0, 0.0)
            before = _dot(ohf.astype(BF16), earlier)
            dest = jnp.sum(jnp.where(oh, before + run, 0.0), axis=0, keepdims=True)
            dest_ref[pl.ds(k, 1), pl.ds(off, tile)] = dest.astype(I32)
            run = run + jnp.sum(ohf, axis=1, keepdims=True)
        return run

    lax.fori_loop(0, n_blocks, dest_body, start)


def _route(tope_pad):
    t_pad = tope_pad.shape[1]
    kern = functools.partial(_route_kernel, n_blocks=t_pad // ROUTE_TILE, tile=ROUTE_TILE,
                             moe_block=MOE_BLOCK)
    return pl.pallas_call(
        kern,
        out_shape=[jax.ShapeDtypeStruct((TOP_K, t_pad), I32),
                   jax.ShapeDtypeStruct((N_EXPERTS, 128), I32)],
        compiler_params=_cparams(),
        name="route",
    )(tope_pad)


def _dispatch(dest3, pad3, zero_rows, h_all, n_rows):
    n_tok, w = h_all.shape
    n_chunks = dest3.shape[0]
    n_pad_chunks = pad3.shape[0]
    mesh = plsc.VectorSubcoreMesh(core_axis_name="c", subcore_axis_name="s")
    n_workers = mesh.num_cores * mesh.num_subcores

    def body(dest_hbm, pad_hbm, zero_hbm, h_hbm, xs_hbm, idx_v, rows_v):
        wid = lax.axis_index("s") * mesh.num_cores + lax.axis_index("c")
        pltpu.sync_copy(zero_hbm, rows_v)
        for j in range(-(-n_pad_chunks // n_workers)):
            c = wid + j * n_workers

            @pl.when(c < n_pad_chunks)
            def _():
                pltpu.sync_copy(pad_hbm.at[c], idx_v.at[pl.ds(0, 1)])
                pltpu.sync_copy(rows_v, xs_hbm.at[idx_v.at[0]])

        for j in range(-(-n_chunks // n_workers)):
            c = wid + j * n_workers

            @pl.when(c < n_chunks)
            def _():
                pltpu.sync_copy(dest_hbm.at[c], idx_v)
                pltpu.sync_copy(h_hbm.at[pl.ds(c * SC_CHUNK, SC_CHUNK)], rows_v)
                for k in range(TOP_K):
                    pltpu.sync_copy(rows_v, xs_hbm.at[idx_v.at[k]])

    return pl.kernel(
        body,
        out_type=jax.ShapeDtypeStruct((n_rows, w), U32),
        mesh=mesh,
        scratch_types=[pltpu.VMEM((TOP_K, SC_CHUNK), I32), pltpu.VMEM((SC_CHUNK, w), U32)],
        name="dispatch_sc",
    )(dest3, pad3, zero_rows, h_all)


def _ffn_kernel(be_ref, nu_ref, xs_ref, w1_ref, b1_ref, w2_ref, b2_ref, y_ref, w1b, w2b):
    i = pl.program_id(0)

    @pl.when(i >= nu_ref[0])
    def _():
        y_ref[...] = jnp.zeros(y_ref.shape, y_ref.dtype)

    @pl.when(i < nu_ref[0])
    def _():
        e = be_ref[i]
        prev = be_ref[jnp.maximum(i - 1, 0)]

        @pl.when((i == 0) | (e != prev))
        def _():
            w1b[...] = w1_ref[...].astype(BF16)
            w2b[...] = w2_ref[...].astype(BF16)

        w = xs_ref[...]
        half = w.shape[1]
        d_e = w2b.shape[0]
        lo = pltpu.bitcast(lax.shift_left(w, jnp.uint32(16)), F32).astype(BF16)
        hi = pltpu.bitcast(lax.bitwise_and(w, jnp.uint32(0xFFFF0000)), F32).astype(BF16)
        gu = _dot(lo, w1b[0:half, :]) + _dot(hi, w1b[half:, :]) + b1_ref[...]
        gate = jnp.minimum(gu[:, :d_e], SWIGLU_LIMIT)
        up = jnp.clip(gu[:, d_e:], -SWIGLU_LIMIT, SWIGLU_LIMIT)
        glu = gate * jax.nn.sigmoid(gate * SWIGLU_ALPHA)
        act = ((up + 1.0) * glu).astype(BF16)
        y_ref[...] = _dot(act, w2b[...]) + b2_ref[...]


def _expert_ffn(block_e, n_used, xs, w1, b1, w2, b2):
    n_rows, half = xs.shape
    n_e, d, d2 = w1.shape
    d_e = w2.shape[1]
    blk = MOE_BLOCK
    last = lambda i, be, nu: jnp.minimum(i, nu[0] - 1)
    grid_spec = pltpu.PrefetchScalarGridSpec(
        num_scalar_prefetch=2,
        grid=(n_rows // blk,),
        in_specs=[
            pl.BlockSpec((blk, half), lambda i, be, nu: (last(i, be, nu), 0)),
            pl.BlockSpec((None, d, d2), lambda i, be, nu: (be[last(i, be, nu)], 0, 0)),
            pl.BlockSpec((None, 1, d2), lambda i, be, nu: (be[last(i, be, nu)], 0, 0)),
            pl.BlockSpec((None, d_e, d), lambda i, be, nu: (be[last(i, be, nu)], 0, 0)),
            pl.BlockSpec((None, 1, d), lambda i, be, nu: (be[last(i, be, nu)], 0, 0)),
        ],
        out_specs=pl.BlockSpec((blk, d), lambda i, be, nu: (i, 0)),
        scratch_shapes=[pltpu.VMEM((d, d2), BF16), pltpu.VMEM((d_e, d), BF16)],
    )
    return pl.pallas_call(
        _ffn_kernel,
        grid_spec=grid_spec,
        out_shape=jax.ShapeDtypeStruct((n_rows, d), F32),
        compiler_params=_cparams(("arbitrary",)),
        name="expert_ffn",
    )(block_e, n_used, xs, w1, b1.reshape(n_e, 1, d2), w2, b2.reshape(n_e, 1, d))


def _combine_kernel(dest_ref, x2_ref, gates_ref, gf_ref, y_hbm, out_ref, buf, sem, *, tile):
    def row_copy(d, k, t):
        return pltpu.make_async_copy(y_hbm.at[pl.ds(d, 1)], buf.at[k, pl.ds(t, 1)], sem.at[0])

    for t in range(tile):
        for k in range(TOP_K):
            row_copy(dest_ref[k, t], k, t).start()
    for t in range(tile):
        for k in range(TOP_K):
            row_copy(0, k, t).wait()

    g = jnp.concatenate([gates_ref[...], jnp.zeros((128 - TOP_K, tile), F32)], axis=0)
    gt = g.T
    moe = gt[:, 0:1] * buf[0]
    for k in range(1, TOP_K):
        moe = moe + gt[:, k:k + 1] * buf[k]
    out_ref[...] = _rms(x2_ref[...] + moe, gf_ref[...])


def _combine(dest, x2, gates, g_final, y, tile):
    n, d = x2.shape
    grid_spec = pltpu.PrefetchScalarGridSpec(
        num_scalar_prefetch=0,
        grid=(n // tile,),
        in_specs=[
            pl.BlockSpec((TOP_K, tile), lambda i: (0, i), memory_space=pltpu.SMEM),
            pl.BlockSpec((tile, d), lambda i: (i, 0)),
            pl.BlockSpec((TOP_K, tile), lambda i: (0, i)),
            pl.BlockSpec((1, d), lambda i: (0, 0)),
            pl.BlockSpec(memory_space=pl.ANY),
        ],
        out_specs=pl.BlockSpec((tile, d), lambda i: (i, 0)),
        scratch_shapes=[pltpu.VMEM((TOP_K, tile, d), F32), pltpu.SemaphoreType.DMA((1,))],
    )
    return pl.pallas_call(
        functools.partial(_combine_kernel, tile=tile),
        grid_spec=grid_spec,
        out_shape=jax.ShapeDtypeStruct((n, d), F32),
        compiler_params=_cparams(("arbitrary",)),
        name="combine",
    )(dest, x2, gates, g_final, y)


def kernel(x_prompt, x_sample, mem_prompt, cache_mem_k, cache_mem_v, state_conv, g_mix, w_in, g_v,
           w_spatial, b_spatial, w_conv, w_out, g_xattn, g_mem, w_q, w_k, w_v, w_xo, g_moe, w_router,
           b_router, w_gate_up, b_gate_up, w_down, b_down, g_final):
    depth = g_mix.shape[0]
    assert depth == 1, "one layer supported"
    b, s, d = x_prompt.shape
    nb, ns, _ = x_sample.shape
    assert ns == 1
    n_mem = mem_prompt.shape[1]
    a_heads, a_hd = g_v.shape[1], g_v.shape[2]
    a_width = a_heads * a_hd
    b_width = w_conv.shape[2]
    assert a_heads == A_HEADS and 2 * a_hd == CHUNK and w_spatial.shape[2] == CHUNK
    assert s % MIX_TILE == 0 and s % ATT_TILE == 0 and (b * n_mem) % KV_TILE == 0
    assert nb % SAMPLE_ATT_GROUP == 0
    l = 0
    row = lambda a: a.reshape(1, -1)

    w_in_b = w_in[l].astype(BF16)
    w_out_b = w_out[l].astype(BF16)
    w_q_b = w_q[l].astype(BF16)
    w_k_b = w_k[l].astype(BF16)
    w_v_b = w_v[l].astype(BF16)
    w_xo_b = w_xo[l].astype(BF16)
    wrt_b = w_router[l].T.astype(BF16)
    br_col = b_router[l].reshape(N_EXPERTS, 1).astype(F32)
    gv_row = row(g_v[l])
    ws = w_spatial[l]
    ws_pairs = jnp.concatenate([ws[0::2], ws[1::2]], axis=2)
    bs_full = jnp.repeat(b_spatial[l].T, a_hd, axis=1)
    w00 = row(jnp.repeat(ws[:, 0, 0], a_hd))
    b0 = row(jnp.repeat(b_spatial[l][:, 0], a_hd))

    mk2, mv2 = _mem_kv(mem_prompt.reshape(b * n_mem, d), row(g_mem[l]), w_k_b, w_v_b)
    x1_p, conv_p = _mixer_prompt(x_prompt, row(g_mix[l]), w_in_b, gv_row, ws_pairs, bs_full,
                                 w_conv[l], w_out_b)
    x2_p, h3_p, tope_p, gates_p = _attn_prompt(
        x1_p, mk2.reshape(b, n_mem, d), mv2.reshape(b, n_mem, d), row(g_xattn[l]), w_q_b, w_xo_b,
        row(g_moe[l]), wrt_b, br_col)

    xs2 = x_sample.reshape(nb, d)
    st = state_conv[l]
    x1_s, v_s, z_s, q_s = _mixer_sample(xs2, st[:, 0], st[:, 1], row(g_mix[l]), w_in_b, gv_row, w00, b0,
                                        w_conv[l], w_out_b, row(g_xattn[l]), w_q_b)
    x_heads, x_hd = cache_mem_k.shape[3], cache_mem_k.shape[4]
    assert depth == 1 and x_heads == X_HEADS
    o_s = _attn_sample(q_s.reshape(nb, x_heads, x_hd), cache_mem_k.reshape(nb, n_mem, x_heads, x_hd),
                       cache_mem_v.reshape(nb, n_mem, x_heads, x_hd)).reshape(nb, d)
    x2_s, h3_s, tope_s, gates_s = _tail_sample(x1_s, o_s, w_xo_b, row(g_moe[l]), wrt_b, br_col)

    t_p = b * s
    t_all = t_p + nb
    t_pad = -(-t_all // ROUTE_TILE) * ROUTE_TILE
    h_all = jnp.concatenate([h3_p.reshape(t_p, d // 2), h3_s], axis=0)
    tope_all = jnp.concatenate([tope_p, tope_s, jnp.full((TOP_K, t_pad - t_all), -1, I32)], axis=1)
    dest, cnt = _route(tope_all)
    counts = cnt[:, 0]
    padded = (counts + MOE_BLOCK - 1) // MOE_BLOCK * MOE_BLOCK
    pad_ends = jnp.cumsum(padded)
    pad_starts = pad_ends - padded
    n_blocks = -(-(t_all * TOP_K) // MOE_BLOCK) + N_EXPERTS
    block_first = jnp.arange(n_blocks, dtype=I32) * MOE_BLOCK
    block_e = jnp.minimum(jnp.sum(pad_ends[None, :] <= block_first[:, None], axis=1),
                          N_EXPERTS - 1).astype(I32)
    n_used = (pad_ends[-1] // MOE_BLOCK).astype(I32).reshape(1)
    n_rows = n_blocks * MOE_BLOCK
    assert t_all % SC_CHUNK == 0
    dest3 = dest[:, :t_all].reshape(TOP_K, t_all // SC_CHUNK, SC_CHUNK).transpose(1, 0, 2)
    pad_rows = (pad_starts + counts)[:, None] + jnp.arange(MOE_BLOCK, dtype=I32)[None, :]
    pad3 = jnp.where(pad_rows < pad_ends[:, None], pad_rows, n_rows - 1).astype(I32)
    pad3 = pad3.reshape(-1, 1, SC_CHUNK)
    xs_rows = _dispatch(dest3, pad3, jnp.zeros((SC_CHUNK, d // 2), U32), h_all, n_rows)
    y_rows = _expert_ffn(block_e, n_used, xs_rows, w_gate_up[l], b_gate_up[l], w_down[l], b_down[l])

    gf = row(g_final)
    y_p = _combine(dest[:, :t_p], x2_p.reshape(t_p, d), gates_p, gf, y_rows, COMBINE_TILE)
    y_s = _combine(dest[:, t_p:t_all], x2_s, gates_s, gf, y_rows, nb)

    x_heads = cache_mem_k.shape[3]
    return (y_p.reshape(b, s, d),
            y_s.reshape(nb, 1, d),
            mk2.reshape(1, b, n_mem, x_heads, d // x_heads),
            mv2.reshape(1, b, n_mem, x_heads, d // x_heads),
            conv_p.reshape(1, b, CONV_W - 1, b_width),
            jnp.stack([st[:, 1], z_s], axis=1).reshape(1, nb, CONV_W - 1, b_width),
            v_s.reshape(1, nb, 1, a_heads, a_hd))
```

```python
import functools

import jax
import jax.numpy as jnp
from jax import lax
from jax.experimental import pallas as pl
from jax.experimental.pallas import tpu as pltpu
from jax.experimental.pallas import tpu_sc as plsc

F32 = jnp.float32
BF16 = jnp.bfloat16
I32 = jnp.int32
U32 = jnp.uint32

A_HEADS = 8
CHUNK = 128
CONV_W = 3
X_HEADS = 4
N_EXPERTS = 32
TOP_K = 4
SWIGLU_LIMIT = 7.0
SWIGLU_ALPHA = 1.702
EPS = 1e-5

MIX_TILE = 512
ATT_TILE = 512
KV_TILE = 512
ROUTE_TILE = 512
MOE_BLOCK = 256
COMBINE_TILE = 256
SAMPLE_ATT_GROUP = 4
SC_CHUNK = 128
VMEM_LIMIT = 56 * 1024 * 1024


def _cparams(sem=None):
    return pltpu.CompilerParams(dimension_semantics=sem, vmem_limit_bytes=VMEM_LIMIT)


def _rms(x, g):
    r = lax.rsqrt(jnp.mean(x * x, axis=-1, keepdims=True) + EPS)
    return (x * r) * g


def _gelu(x):
    return 0.5 * x * (1.0 + lax.erf(x * 0.7071067811865476))


def _dot(a, b):
    return jnp.dot(a, b, preferred_element_type=F32)


def _dot_nt(a, b):
    return lax.dot_general(a, b, (((1,), (1,)), ((), ())), preferred_element_type=F32)


def _memkv_kernel(m_ref, g_ref, wk_ref, wv_ref, k_ref, v_ref):
    h = _rms(m_ref[...], g_ref[...]).astype(BF16)
    k_ref[...] = _dot(h, wk_ref[...])
    v_ref[...] = _dot(h, wv_ref[...])


def _mem_kv(mem2d, g_mem, wk, wv):
    n, d = mem2d.shape
    row = pl.BlockSpec((KV_TILE, d), lambda i: (i, 0))
    full = lambda shape: pl.BlockSpec(shape, lambda i: (0,) * len(shape))
    return pl.pallas_call(
        _memkv_kernel,
        grid=(n // KV_TILE,),
        in_specs=[row, full((1, d)), full((d, d)), full((d, d))],
        out_specs=[row, row],
        out_shape=[jax.ShapeDtypeStruct((n, d), F32)] * 2,
        compiler_params=_cparams(("arbitrary",)),
        name="mem_kv",
    )(mem2d, g_mem, wk, wv)


def _head_rms(v, gv, a_width):
    hd = a_width // A_HEADS
    r_i = lax.broadcasted_iota(I32, (a_width, a_width), 0) // hd
    c_i = lax.broadcasted_iota(I32, (a_width, a_width), 1) // hd
    ones_bd = jnp.where(r_i == c_i, 1.0, 0.0).astype(BF16)
    sq = v * v
    sq_hi = sq.astype(BF16)
    sq_lo = (sq - sq_hi.astype(F32)).astype(BF16)
    gs = _dot(sq_hi, ones_bd) + _dot(sq_lo, ones_bd)
    return (v * lax.rsqrt(gs * (1.0 / hd) + EPS)) * gv


def _mixer_prompt_kernel(x_ref, gmix_ref, win_ref, gv_ref, ws_ref, bs_ref, wc_ref, wout_ref,
                         x1_ref, cs_ref, zbuf, *, tile, a_width, b_width):
    j = pl.program_id(1)
    nj = pl.num_programs(1)
    x = x_ref[...]
    h = _rms(x, gmix_ref[...]).astype(BF16)
    proj = _dot(h, win_ref[...])
    u = _gelu(proj[:, :a_width])
    v = _head_rms(_gelu(proj[:, a_width:2 * a_width]), gv_ref[...], a_width)
    o = 2 * a_width
    hb = proj[:, o:o + b_width]
    bg = proj[:, o + b_width:o + 2 * b_width]
    cg = proj[:, o + 2 * b_width:o + 3 * b_width]

    lane = lax.broadcasted_iota(I32, (CHUNK, 2 * (a_width // A_HEADS)), 1)
    first = lane < (a_width // A_HEADS)
    t_i = lax.broadcasted_iota(I32, (CHUNK, 2 * CHUNK), 0)
    s_i = lax.broadcasted_iota(I32, (CHUNK, 2 * CHUNK), 1) % CHUNK
    causal = s_i <= t_i
    pair_cols = []
    for p in range(A_HEADS // 2):
        w_pair = jnp.where(causal, ws_ref[p], 0.0).astype(BF16)
        vp = v[:, p * CHUNK:(p + 1) * CHUNK]
        rows = []
        for c in range(tile // CHUNK):
            vc = vp[c * CHUNK:(c + 1) * CHUNK]
            rhs = jnp.concatenate([jnp.where(first, vc, 0.0), jnp.where(first, 0.0, vc)],
                                  axis=0).astype(BF16)
            rows.append(_dot(w_pair, rhs))
        pair_cols.append(jnp.concatenate(rows, axis=0))
    gate = jnp.concatenate(pair_cols, axis=1)
    bias = jnp.concatenate([bs_ref[...]] * (tile // CHUNK), axis=0)
    a_out = u * (gate + bias)

    @pl.when(j == 0)
    def _():
        zbuf[0:8, :] = jnp.zeros((8, b_width), F32)

    z = cg * hb
    zbuf[8:tile + 8, :] = z
    z1 = zbuf[7:tile + 7, :]
    z2 = zbuf[6:tile + 6, :]
    wc = wc_ref[...]
    conv = z2 * wc[0:1] + z1 * wc[1:2] + z * wc[2:3]
    b_out = bg * conv
    tail = zbuf[tile:tile + 8, :]
    zbuf[0:8, :] = tail

    @pl.when(j == nj - 1)
    def _():
        cs_ref[...] = tail[8 - (CONV_W - 1):, :]

    y = _dot(a_out.astype(BF16), wout_ref[0:a_width, :]) + _dot(b_out.astype(BF16), wout_ref[a_width:, :])
    x1_ref[...] = x + y


def _mixer_prompt(x, g_mix, w_in, g_v, ws_pairs, bs_full, w_conv, w_out):
    b, s, d = x.shape
    a_width = g_v.shape[1]
    b_width = w_conv.shape[1]
    in_width = w_in.shape[1]
    tile = MIX_TILE
    full = lambda shape: pl.BlockSpec(shape, lambda i, j: (0,) * len(shape))
    kern = functools.partial(_mixer_prompt_kernel, tile=tile, a_width=a_width, b_width=b_width)
    return pl.pallas_call(
        kern,
        grid=(b, s // tile),
        in_specs=[
            pl.BlockSpec((None, tile, d), lambda i, j: (i, j, 0)),
            full((1, d)), full((d, in_width)), full((1, a_width)),
            full(ws_pairs.shape), full(bs_full.shape), full(w_conv.shape), full(w_out.shape),
        ],
        out_specs=[
            pl.BlockSpec((None, tile, d), lambda i, j: (i, j, 0)),
            pl.BlockSpec((None, CONV_W - 1, b_width), lambda i, j: (i, 0, 0)),
        ],
        out_shape=[jax.ShapeDtypeStruct((b, s, d), F32),
                   jax.ShapeDtypeStruct((b, CONV_W - 1, b_width), F32)],
        scratch_shapes=[pltpu.VMEM((tile + 8, b_width), F32)],
        compiler_params=_cparams(("arbitrary", "arbitrary")),
        name="mixer_prompt",
    )(x, g_mix, w_in, g_v, ws_pairs, bs_full, w_conv, w_out)


def _mixer_sample_kernel(x_ref, s0_ref, s1_ref, gmix_ref, win_ref, gv_ref, w00_ref, b0_ref, wc_ref,
                         wout_ref, gx_ref, wq_ref, x1_ref, v_ref, z_ref, q_ref, *, a_width, b_width):
    x = x_ref[...]
    h = _rms(x, gmix_ref[...]).astype(BF16)
    proj = _dot(h, win_ref[...])
    u = _gelu(proj[:, :a_width])
    v = _head_rms(_gelu(proj[:, a_width:2 * a_width]), gv_ref[...], a_width)
    v_ref[...] = v
    o = 2 * a_width
    hb = proj[:, o:o + b_width]
    bg = proj[:, o + b_width:o + 2 * b_width]
    cg = proj[:, o + 2 * b_width:o + 3 * b_width]
    a_out = u * (v * w00_ref[...] + b0_ref[...])
    z = cg * hb
    z_ref[...] = z
    wc = wc_ref[...]
    conv = s0_ref[...] * wc[0:1] + s1_ref[...] * wc[1:2] + z * wc[2:3]
    b_out = bg * conv
    y = _dot(a_out.astype(BF16), wout_ref[0:a_width, :]) + _dot(b_out.astype(BF16), wout_ref[a_width:, :])
    x1 = x + y
    x1_ref[...] = x1
    q_ref[...] = _dot(_rms(x1, gx_ref[...]).astype(BF16), wq_ref[...])


def _mixer_sample(x, s0, s1, g_mix, w_in, g_v, w00, b0, w_conv, w_out, g_x, w_q):
    n, d = x.shape
    a_width = g_v.shape[1]
    b_width = w_conv.shape[1]
    kern = functools.partial(_mixer_sample_kernel, a_width=a_width, b_width=b_width)
    return pl.pallas_call(
        kern,
        out_shape=[jax.ShapeDtypeStruct((n, d), F32), jax.ShapeDtypeStruct((n, a_width), F32),
                   jax.ShapeDtypeStruct((n, b_width), F32), jax.ShapeDtypeStruct((n, d), F32)],
        compiler_params=_cparams(),
        name="mixer_sample",
    )(x, s0, s1, g_mix, w_in, g_v, w00, b0, w_conv, w_out, g_x, w_q)


def _router_tail(x2, gmoe, wrt, br, h3_ref, tope_ref, gates_ref):
    d = x2.shape[1]
    h3 = _rms(x2, gmoe).astype(BF16)
    h3f = h3.astype(F32)
    lo = lax.shift_right_logical(pltpu.bitcast(h3f[:, :d // 2], U32), jnp.uint32(16))
    hi = lax.bitwise_and(pltpu.bitcast(h3f[:, d // 2:], U32), jnp.uint32(0xFFFF0000))
    h3_ref[...] = lax.bitwise_or(lo, hi)

    logits = _dot_nt(wrt, h3) + br
    n_e, n_t = logits.shape
    e_idx = lax.broadcasted_iota(I32, (n_e, n_t), 0).astype(F32)
    tops, idxs = [], []
    for _ in range(TOP_K):
        m = jnp.max(logits, axis=0, keepdims=True)
        idx = jnp.min(jnp.where(logits == m, e_idx, float(n_e)), axis=0, keepdims=True)
        tops.append(m)
        idxs.append(idx)
        logits = jnp.where(e_idx == idx, -jnp.inf, logits)
    top = jnp.concatenate(tops, axis=0)
    ex = jnp.exp(top - top[0:1])
    gates_ref[...] = ex / jnp.sum(ex, axis=0, keepdims=True)
    tope_ref[...] = jnp.concatenate(idxs, axis=0).astype(I32)


def _attn_prompt_kernel(x1_ref, k_ref, v_ref, gx_ref, wq_ref, wxo_ref, gmoe_ref, wrt_ref, br_ref,
                        x2_ref, h3_ref, tope_ref, gates_ref):
    x = x1_ref[...]
    d = x.shape[1]
    hd = d // X_HEADS
    q = _dot(_rms(x, gx_ref[...]).astype(BF16), wq_ref[...]).astype(BF16)
    kb = k_ref[...].astype(BF16)
    vb = v_ref[...].astype(BF16)
    outs = []
    for hh in range(X_HEADS):
        sl = slice(hh * hd, (hh + 1) * hd)
        s = _dot_nt(q[:, sl], kb[:, sl]) * (hd ** -0.5)
        e = jnp.exp(s - jnp.max(s, axis=-1, keepdims=True))
        p = e / jnp.sum(e, axis=-1, keepdims=True)
        outs.append(_dot(p.astype(BF16), vb[:, sl]))
    o = jnp.concatenate(outs, axis=1).astype(BF16)
    x2 = x + _dot(o, wxo_ref[...])
    x2_ref[...] = x2
    _router_tail(x2, gmoe_ref[...], wrt_ref[...], br_ref[...], h3_ref, tope_ref, gates_ref)


def _attn_prompt(x1, mk, mv, g_x, w_q, w_xo, g_moe, wrt, br):
    b, s, d = x1.shape
    n_mem = mk.shape[1]
    tile = ATT_TILE
    nq = s // tile
    full = lambda shape: pl.BlockSpec(shape, lambda i, j: (0,) * len(shape))
    tok = pl.BlockSpec((None, tile, d), lambda i, j: (i, j, 0))
    mem = pl.BlockSpec((None, n_mem, d), lambda i, j: (i, 0, 0))
    lanes = pl.BlockSpec((TOP_K, tile), lambda i, j: (0, i * nq + j))
    return pl.pallas_call(
        _attn_prompt_kernel,
        grid=(b, nq),
        in_specs=[tok, mem, mem, full((1, d)), full((d, d)), full((d, d)), full((1, d)),
                  full(wrt.shape), full(br.shape)],
        out_specs=[tok, pl.BlockSpec((None, tile, d // 2), lambda i, j: (i, j, 0)), lanes, lanes],
        out_shape=[jax.ShapeDtypeStruct((b, s, d), F32), jax.ShapeDtypeStruct((b, s, d // 2), U32),
                   jax.ShapeDtypeStruct((TOP_K, b * s), I32), jax.ShapeDtypeStruct((TOP_K, b * s), F32)],
        compiler_params=_cparams(("arbitrary", "arbitrary")),
        name="attn_prompt",
    )(x1, mk, mv, g_x, w_q, w_xo, g_moe, wrt, br)


def _attn_sample_kernel(q_ref, k_ref, v_ref, o_ref, *, group):
    hd = q_ref.shape[2]
    for g in range(group):
        q = q_ref[g]
        s = jnp.sum(k_ref[g] * q[None], axis=-1, keepdims=True) * (hd ** -0.5)
        e = jnp.exp(s - jnp.max(s, axis=0, keepdims=True))
        p = e / jnp.sum(e, axis=0, keepdims=True)
        o_ref[g] = jnp.sum(p * v_ref[g], axis=0)


def _attn_sample(q, ck, cv):
    n, heads, hd = q.shape
    n_mem = ck.shape[1]
    group = SAMPLE_ATT_GROUP
    qo = pl.BlockSpec((group, heads, hd), lambda i: (i, 0, 0))
    kv = pl.BlockSpec((group, n_mem, heads, hd), lambda i: (i, 0, 0, 0))
    return pl.pallas_call(
        functools.partial(_attn_sample_kernel, group=group),
        grid=(n // group,),
        in_specs=[qo, kv, kv],
        out_specs=qo,
        out_shape=jax.ShapeDtypeStruct((n, heads, hd), F32),
        compiler_params=_cparams(("arbitrary",)),
        name="attn_sample",
    )(q, ck, cv)


def _tail_sample_kernel(x1_ref, o_ref, wxo_ref, gmoe_ref, wrt_ref, br_ref,
                        x2_ref, h3_ref, tope_ref, gates_ref):
    x2 = x1_ref[...] + _dot(o_ref[...].astype(BF16), wxo_ref[...])
    x2_ref[...] = x2
    _router_tail(x2, gmoe_ref[...], wrt_ref[...], br_ref[...], h3_ref, tope_ref, gates_ref)


def _tail_sample(x1, o, w_xo, g_moe, wrt, br):
    n, d = x1.shape
    return pl.pallas_call(
        _tail_sample_kernel,
        out_shape=[jax.ShapeDtypeStruct((n, d), F32), jax.ShapeDtypeStruct((n, d // 2), U32),
                   jax.ShapeDtypeStruct((TOP_K, n), I32), jax.ShapeDtypeStruct((TOP_K, n), F32)],
        compiler_params=_cparams(),
        name="tail_sample",
    )(x1, o, w_xo, g_moe, wrt, br)


def _route_kernel(tope_ref, dest_ref, cnt_ref, *, n_blocks, tile, moe_block):
    e_idx = lax.broadcasted_iota(I32, (N_EXPERTS, tile), 0)
    earlier = jnp.where(lax.broadcasted_iota(I32, (tile, tile), 0)
                        < lax.broadcasted_iota(I32, (tile, tile), 1), 1.0, 0.0).astype(BF16)

    def onehot(k, off):
        return e_idx == tope_ref[pl.ds(k, 1), pl.ds(off, tile)]

    def count_body(j, cnt):
        off = pl.multiple_of(j * tile, tile)
        for k in range(TOP_K):
            cnt = cnt + jnp.sum(jnp.where(onehot(k, off), 1.0, 0.0), axis=1, keepdims=True)
        return cnt

    counts = lax.fori_loop(0, n_blocks, count_body, jnp.zeros((N_EXPERTS, 1), F32))
    cnt_ref[...] = jnp.broadcast_to(counts, cnt_ref.shape).astype(I32)

    n_blk = jnp.floor((counts + (moe_block - 1)) * (1.0 / moe_block))
    n_hi = jnp.floor(n_blk * (1.0 / 16.0))
    n_lo = n_blk - 16.0 * n_hi
    below = jnp.where(lax.broadcasted_iota(I32, (N_EXPERTS, N_EXPERTS), 1)
                      < lax.broadcasted_iota(I32, (N_EXPERTS, N_EXPERTS), 0), 1.0, 0.0).astype(BF16)
    wide = lambda c: jnp.broadcast_to(c, (N_EXPERTS, 128)).astype(BF16)
    start = (16.0 * _dot(below, wide(n_hi)) + _dot(below, wide(n_lo)))[:, 0:1] * float(moe_block)

    def dest_body(j, run):
        off = pl.multiple_of(j * tile, tile)
        for k in range(TOP_K):
            oh = onehot(k, off)
            ohf = jnp.where(oh, 1.0, 0.0)
            before = _dot(ohf.astype(BF16), earlier)
            dest = jnp.sum(jnp.where(oh, before + run, 0.0), axis=0, keepdims=True)
            dest_ref[pl.ds(k, 1), pl.ds(off, tile)] = dest.astype(I32)
            run = run + jnp.sum(ohf, axis=1, keepdims=True)
        return run

    lax.fori_loop(0, n_blocks, dest_body, start)


def _route(tope_pad):
    t_pad = tope_pad.shape[1]
    kern = functools.partial(_route_kernel, n_blocks=t_pad // ROUTE_TILE, tile=ROUTE_TILE,
                             moe_block=MOE_BLOCK)
    return pl.pallas_call(
        kern,
        out_shape=[jax.ShapeDtypeStruct((TOP_K, t_pad), I32),
                   jax.ShapeDtypeStruct((N_EXPERTS, 128), I32)],
        compiler_params=_cparams(),
        name="route",
    )(tope_pad)


def _dispatch(dest3, pad3, zero_rows, h_prompt, h_sample, n_rows):
    w = h_prompt.shape[1]
    n_prompt_chunks = h_prompt.shape[0] // SC_CHUNK
    n_chunks = dest3.shape[0]
    n_pad_chunks = pad3.shape[0]
    mesh = plsc.VectorSubcoreMesh(core_axis_name="c", subcore_axis_name="s")
    n_workers = mesh.num_cores * mesh.num_subcores

    def body(dest_hbm, pad_hbm, zero_hbm, hp_hbm, hs_hbm, xs_hbm, idx_v, rows_v):
        wid = lax.axis_index("s") * mesh.num_cores + lax.axis_index("c")
        pltpu.sync_copy(zero_hbm, rows_v)
        for j in range(-(-n_pad_chunks // n_workers)):
            c = wid + j * n_workers

            @pl.when(c < n_pad_chunks)
            def _():
                pltpu.sync_copy(pad_hbm.at[c], idx_v.at[pl.ds(0, 1)])
                pltpu.sync_copy(rows_v, xs_hbm.at[idx_v.at[0]])

        def scatter_chunk(c, src_hbm, src_chunk):
            pltpu.sync_copy(dest_hbm.at[c], idx_v)
            pltpu.sync_copy(src_hbm.at[pl.ds(src_chunk * SC_CHUNK, SC_CHUNK)], rows_v)
            for k in range(TOP_K):
                pltpu.sync_copy(rows_v, xs_hbm.at[idx_v.at[k]])

        for j in range(-(-n_chunks // n_workers)):
            c = wid + j * n_workers
            pl.when(c < n_prompt_chunks)(lambda: scatter_chunk(c, hp_hbm, c))
            pl.when((c >= n_prompt_chunks) & (c < n_chunks))(
                lambda: scatter_chunk(c, hs_hbm, c - n_prompt_chunks))

    return pl.kernel(
        body,
        out_type=jax.ShapeDtypeStruct((n_rows, w), U32),
        mesh=mesh,
        scratch_types=[pltpu.VMEM((TOP_K, SC_CHUNK), I32), pltpu.VMEM((SC_CHUNK, w), U32)],
        name="dispatch_sc",
    )(dest3, pad3, zero_rows, h_prompt, h_sample)


def _ffn_kernel(be_ref, nu_ref, xs_ref, w1_ref, b1_ref, w2_ref, b2_ref, y_ref):
    i = pl.program_id(0)

    @pl.when(i >= nu_ref[0])
    def _():
        y_ref[...] = jnp.zeros(y_ref.shape, y_ref.dtype)

    @pl.when(i < nu_ref[0])
    def _():
        w = xs_ref[...]
        half = w.shape[1]
        d_e = w2_ref.shape[0]
        lo = pltpu.bitcast(lax.shift_left(w, jnp.uint32(16)), F32).astype(BF16)
        hi = pltpu.bitcast(lax.bitwise_and(w, jnp.uint32(0xFFFF0000)), F32).astype(BF16)
        gu = _dot(lo, w1_ref[0:half, :]) + _dot(hi, w1_ref[half:, :]) + b1_ref[...]
        gate = jnp.minimum(gu[:, :d_e], SWIGLU_LIMIT)
        up = jnp.clip(gu[:, d_e:], -SWIGLU_LIMIT, SWIGLU_LIMIT)
        glu = gate * jax.nn.sigmoid(gate * SWIGLU_ALPHA)
        act = ((up + 1.0) * glu).astype(BF16)
        y_ref[...] = _dot(act, w2_ref[...]) + b2_ref[...]


def _expert_ffn(block_e, n_used, xs, w1, b1, w2, b2):
    n_rows, half = xs.shape
    n_e, d, d2 = w1.shape
    d_e = w2.shape[1]
    blk = MOE_BLOCK
    last = lambda i, be, nu: jnp.minimum(i, nu[0] - 1)
    grid_spec = pltpu.PrefetchScalarGridSpec(
        num_scalar_prefetch=2,
        grid=(n_rows // blk,),
        in_specs=[
            pl.BlockSpec((blk, half), lambda i, be, nu: (last(i, be, nu), 0)),
            pl.BlockSpec((None, d, d2), lambda i, be, nu: (be[last(i, be, nu)], 0, 0)),
            pl.BlockSpec((None, 1, d2), lambda i, be, nu: (be[last(i, be, nu)], 0, 0)),
            pl.BlockSpec((None, d_e, d), lambda i, be, nu: (be[last(i, be, nu)], 0, 0)),
            pl.BlockSpec((None, 1, d), lambda i, be, nu: (be[last(i, be, nu)], 0, 0)),
        ],
        out_specs=pl.BlockSpec((blk, d), lambda i, be, nu: (i, 0)),
    )
    return pl.pallas_call(
        _ffn_kernel,
        grid_spec=grid_spec,
        out_shape=jax.ShapeDtypeStruct((n_rows, d), F32),
        compiler_params=_cparams(("arbitrary",)),
        name="expert_ffn",
    )(block_e, n_used, xs, w1, b1.reshape(n_e, 1, d2), w2, b2.reshape(n_e, 1, d))


def _combine_kernel(dest_ref, x2_ref, gates_ref, gf_ref, y_hbm, out_ref, buf, sem, *, tile):
    def row_copy(d, k, t):
        return pltpu.make_async_copy(y_hbm.at[pl.ds(d, 1)], buf.at[k, pl.ds(t, 1)], sem.at[0])

    for t in range(tile):
        for k in range(TOP_K):
            row_copy(dest_ref[k, t], k, t).start()
    for t in range(tile):
        for k in range(TOP_K):
            row_copy(0, k, t).wait()

    g = jnp.concatenate([gates_ref[...], jnp.zeros((128 - TOP_K, tile), F32)], axis=0)
    gt = g.T
    moe = gt[:, 0:1] * buf[0]
    for k in range(1, TOP_K):
        moe = moe + gt[:, k:k + 1] * buf[k]
    out_ref[...] = _rms(x2_ref[...] + moe, gf_ref[...])


def _combine(dest, x2, gates, g_final, y, tile):
    n, d = x2.shape
    grid_spec = pltpu.PrefetchScalarGridSpec(
        num_scalar_prefetch=0,
        grid=(n // tile,),
        in_specs=[
            pl.BlockSpec((TOP_K, tile), lambda i: (0, i), memory_space=pltpu.SMEM),
            pl.BlockSpec((tile, d), lambda i: (i, 0)),
            pl.BlockSpec((TOP_K, tile), lambda i: (0, i)),
            pl.BlockSpec((1, d), lambda i: (0, 0)),
            pl.BlockSpec(memory_space=pl.ANY),
        ],
        out_specs=pl.BlockSpec((tile, d), lambda i: (i, 0)),
        scratch_shapes=[pltpu.VMEM((TOP_K, tile, d), F32), pltpu.SemaphoreType.DMA((1,))],
    )
    return pl.pallas_call(
        functools.partial(_combine_kernel, tile=tile),
        grid_spec=grid_spec,
        out_shape=jax.ShapeDtypeStruct((n, d), F32),
        compiler_params=_cparams(("arbitrary",)),
        name="combine",
    )(dest, x2, gates, g_final, y)


def kernel(x_prompt, x_sample, mem_prompt, cache_mem_k, cache_mem_v, state_conv, g_mix, w_in, g_v,
           w_spatial, b_spatial, w_conv, w_out, g_xattn, g_mem, w_q, w_k, w_v, w_xo, g_moe, w_router,
           b_router, w_gate_up, b_gate_up, w_down, b_down, g_final):
    depth = g_mix.shape[0]
    assert depth == 1, "one layer supported"
    b, s, d = x_prompt.shape
    nb, ns, _ = x_sample.shape
    assert ns == 1
    n_mem = mem_prompt.shape[1]
    a_heads, a_hd = g_v.shape[1], g_v.shape[2]
    a_width = a_heads * a_hd
    b_width = w_conv.shape[2]
    assert a_heads == A_HEADS and 2 * a_hd == CHUNK and w_spatial.shape[2] == CHUNK
    assert s % MIX_TILE == 0 and s % ATT_TILE == 0 and (b * n_mem) % KV_TILE == 0
    assert nb % SAMPLE_ATT_GROUP == 0
    l = 0
    row = lambda a: a.reshape(1, -1)

    w_in_b = w_in[l].astype(BF16)
    w_out_b = w_out[l].astype(BF16)
    w_q_b = w_q[l].astype(BF16)
    w_k_b = w_k[l].astype(BF16)
    w_v_b = w_v[l].astype(BF16)
    w_xo_b = w_xo[l].astype(BF16)
    wrt_b = w_router[l].T.astype(BF16)
    br_col = b_router[l].reshape(N_EXPERTS, 1).astype(F32)
    gv_row = row(g_v[l])
    ws = w_spatial[l]
    ws_pairs = jnp.concatenate([ws[0::2], ws[1::2]], axis=2)
    bs_full = jnp.repeat(b_spatial[l].T, a_hd, axis=1)
    w00 = row(jnp.repeat(ws[:, 0, 0], a_hd))
    b0 = row(jnp.repeat(b_spatial[l][:, 0], a_hd))

    mk2, mv2 = _mem_kv(mem_prompt.reshape(b * n_mem, d), row(g_mem[l]), w_k_b, w_v_b)
    x1_p, conv_p = _mixer_prompt(x_prompt, row(g_mix[l]), w_in_b, gv_row, ws_pairs, bs_full,
                                 w_conv[l], w_out_b)
    x2_p, h3_p, tope_p, gates_p = _attn_prompt(
        x1_p, mk2.reshape(b, n_mem, d), mv2.reshape(b, n_mem, d), row(g_xattn[l]), w_q_b, w_xo_b,
        row(g_moe[l]), wrt_b, br_col)

    xs2 = x_sample.reshape(nb, d)
    st = state_conv[l]
    x1_s, v_s, z_s, q_s = _mixer_sample(xs2, st[:, 0], st[:, 1], row(g_mix[l]), w_in_b, gv_row, w00, b0,
                                        w_conv[l], w_out_b, row(g_xattn[l]), w_q_b)
    x_heads, x_hd = cache_mem_k.shape[3], cache_mem_k.shape[4]
    assert depth == 1 and x_heads == X_HEADS
    o_s = _attn_sample(q_s.reshape(nb, x_heads, x_hd), cache_mem_k.reshape(nb, n_mem, x_heads, x_hd),
                       cache_mem_v.reshape(nb, n_mem, x_heads, x_hd)).reshape(nb, d)
    x2_s, h3_s, tope_s, gates_s = _tail_sample(x1_s, o_s, w_xo_b, row(g_moe[l]), wrt_b, br_col)

    t_p = b * s
    t_all = t_p + nb
    t_pad = -(-t_all // ROUTE_TILE) * ROUTE_TILE
    tope_all = jnp.concatenate([tope_p, tope_s, jnp.full((TOP_K, t_pad - t_all), -1, I32)], axis=1)
    dest, cnt = _route(tope_all)
    counts = cnt[:, 0]
    padded = (counts + MOE_BLOCK - 1) // MOE_BLOCK * MOE_BLOCK
    pad_ends = jnp.cumsum(padded)
    pad_starts = pad_ends - padded
    n_blocks = -(-(t_all * TOP_K) // MOE_BLOCK) + N_EXPERTS
    block_first = jnp.arange(n_blocks, dtype=I32) * MOE_BLOCK
    block_e = jnp.minimum(jnp.sum(pad_ends[None, :] <= block_first[:, None], axis=1),
                          N_EXPERTS - 1).astype(I32)
    n_used = (pad_ends[-1] // MOE_BLOCK).astype(I32).reshape(1)
    n_rows = n_blocks * MOE_BLOCK
    assert t_all % SC_CHUNK == 0
    dest3 = dest[:, :t_all].reshape(TOP_K, t_all // SC_CHUNK, SC_CHUNK).transpose(1, 0, 2)
    pad_rows = (pad_starts + counts)[:, None] + jnp.arange(MOE_BLOCK, dtype=I32)[None, :]
    pad3 = jnp.where(pad_rows < pad_ends[:, None], pad_rows, n_rows - 1).astype(I32)
    pad3 = pad3.reshape(-1, 1, SC_CHUNK)
    assert t_p % SC_CHUNK == 0
    xs_rows = _dispatch(dest3, pad3, jnp.zeros((SC_CHUNK, d // 2), U32), h3_p.reshape(t_p, d // 2), h3_s,
                        n_rows)
    w1_f32, w2_f32, _ = lax.optimization_barrier((w_gate_up[l], w_down[l], dest3))
    y_rows = _expert_ffn(block_e, n_used, xs_rows, w1_f32.astype(BF16), b_gate_up[l],
                         w2_f32.astype(BF16), b_down[l])

    gf = row(g_final)
    y_p = _combine(dest[:, :t_p], x2_p.reshape(t_p, d), gates_p, gf, y_rows, COMBINE_TILE)
    y_s = _combine(dest[:, t_p:t_all], x2_s, gates_s, gf, y_rows, nb)

    x_heads = cache_mem_k.shape[3]
    return (y_p.reshape(b, s, d),
            y_s.reshape(nb, 1, d),
            mk2.reshape(1, b, n_mem, x_heads, d // x_heads),
            mv2.reshape(1, b, n_mem, x_heads, d // x_heads),
            conv_p.reshape(1, b, CONV_W - 1, b_width),
            jnp.stack([st[:, 1], z_s], axis=1).reshape(1, nb, CONV_W - 1, b_width),
            v_s.reshape(1, nb, 1, a_heads, a_hd))
```

```python
import functools

import jax
import jax.numpy as jnp
from jax import lax
from jax.experimental import pallas as pl
from jax.experimental.pallas import tpu as pltpu
from jax.experimental.pallas import tpu_sc as plsc

F32 = jnp.float32
BF16 = jnp.bfloat16
I32 = jnp.int32
U32 = jnp.uint32

A_HEADS = 8
CHUNK = 128
CONV_W = 3
X_HEADS = 4
N_EXPERTS = 32
TOP_K = 4
SWIGLU_LIMIT = 7.0
SWIGLU_ALPHA = 1.702
EPS = 1e-5

MIX_TILE = 512
ATT_TILE = 512
KV_TILE = 512
ROUTE_TILE = 512
MOE_BLOCK = 256
COMBINE_TILE = 256
SAMPLE_ATT_GROUP = 4
SC_CHUNK = 64
VMEM_LIMIT = 56 * 1024 * 1024


def _cparams(sem=None):
    return pltpu.CompilerParams(dimension_semantics=sem, vmem_limit_bytes=VMEM_LIMIT)


def _rms(x, g):
    r = lax.rsqrt(jnp.mean(x * x, axis=-1, keepdims=True) + EPS)
    return (x * r) * g


def _gelu(x):
    return 0.5 * x * (1.0 + lax.erf(x * 0.7071067811865476))


def _dot(a, b):
    return jnp.dot(a, b, preferred_element_type=F32)


def _dot_nt(a, b):
    return lax.dot_general(a, b, (((1,), (1,)), ((), ())), preferred_element_type=F32)


def _memkv_kernel(m_ref, g_ref, wk_ref, wv_ref, k_ref, v_ref):
    h = _rms(m_ref[...], g_ref[...]).astype(BF16)
    k_ref[...] = _dot(h, wk_ref[...])
    v_ref[...] = _dot(h, wv_ref[...])


def _mem_kv(mem2d, g_mem, wk, wv):
    n, d = mem2d.shape
    row = pl.BlockSpec((KV_TILE, d), lambda i: (i, 0))
    full = lambda shape: pl.BlockSpec(shape, lambda i: (0,) * len(shape))
    return pl.pallas_call(
        _memkv_kernel,
        grid=(n // KV_TILE,),
        in_specs=[row, full((1, d)), full((d, d)), full((d, d))],
        out_specs=[row, row],
        out_shape=[jax.ShapeDtypeStruct((n, d), F32)] * 2,
        compiler_params=_cparams(("arbitrary",)),
        name="mem_kv",
    )(mem2d, g_mem, wk, wv)


def _head_rms(v, gv, a_width):
    hd = a_width // A_HEADS
    r_i = lax.broadcasted_iota(I32, (a_width, a_width), 0) // hd
    c_i = lax.broadcasted_iota(I32, (a_width, a_width), 1) // hd
    ones_bd = jnp.where(r_i == c_i, 1.0, 0.0).astype(BF16)
    sq = v * v
    sq_hi = sq.astype(BF16)
    sq_lo = (sq - sq_hi.astype(F32)).astype(BF16)
    gs = _dot(sq_hi, ones_bd) + _dot(sq_lo, ones_bd)
    return (v * lax.rsqrt(gs * (1.0 / hd) + EPS)) * gv


def _mixer_prompt_kernel(x_ref, gmix_ref, win_ref, gv_ref, ws_ref, bs_ref, wc_ref, wout_ref,
                         x1_ref, cs_ref, zbuf, *, tile, a_width, b_width):
    j = pl.program_id(1)
    nj = pl.num_programs(1)
    x = x_ref[...]
    h = _rms(x, gmix_ref[...]).astype(BF16)
    proj = _dot(h, win_ref[...])
    u = _gelu(proj[:, :a_width])
    v = _head_rms(_gelu(proj[:, a_width:2 * a_width]), gv_ref[...], a_width)
    o = 2 * a_width
    hb = proj[:, o:o + b_width]
    bg = proj[:, o + b_width:o + 2 * b_width]
    cg = proj[:, o + 2 * b_width:o + 3 * b_width]

    lane = lax.broadcasted_iota(I32, (CHUNK, 2 * (a_width // A_HEADS)), 1)
    first = lane < (a_width // A_HEADS)
    t_i = lax.broadcasted_iota(I32, (CHUNK, 2 * CHUNK), 0)
    s_i = lax.broadcasted_iota(I32, (CHUNK, 2 * CHUNK), 1) % CHUNK
    causal = s_i <= t_i
    pair_cols = []
    for p in range(A_HEADS // 2):
        w_pair = jnp.where(causal, ws_ref[p], 0.0).astype(BF16)
        vp = v[:, p * CHUNK:(p + 1) * CHUNK]
        rows = []
        for c in range(tile // CHUNK):
            vc = vp[c * CHUNK:(c + 1) * CHUNK]
            rhs = jnp.concatenate([jnp.where(first, vc, 0.0), jnp.where(first, 0.0, vc)],
                                  axis=0).astype(BF16)
            rows.append(_dot(w_pair, rhs))
        pair_cols.append(jnp.concatenate(rows, axis=0))
    gate = jnp.concatenate(pair_cols, axis=1)
    bias = jnp.concatenate([bs_ref[...]] * (tile // CHUNK), axis=0)
    a_out = u * (gate + bias)

    @pl.when(j == 0)
    def _():
        zbuf[0:8, :] = jnp.zeros((8, b_width), F32)

    z = cg * hb
    zbuf[8:tile + 8, :] = z
    z1 = zbuf[7:tile + 7, :]
    z2 = zbuf[6:tile + 6, :]
    wc = wc_ref[...]
    conv = z2 * wc[0:1] + z1 * wc[1:2] + z * wc[2:3]
    b_out = bg * conv
    tail = zbuf[tile:tile + 8, :]
    zbuf[0:8, :] = tail

    @pl.when(j == nj - 1)
    def _():
        cs_ref[...] = tail[8 - (CONV_W - 1):, :]

    y = _dot(a_out.astype(BF16), wout_ref[0:a_width, :]) + _dot(b_out.astype(BF16), wout_ref[a_width:, :])
    x1_ref[...] = x + y


def _mixer_prompt(x, g_mix, w_in, g_v, ws_pairs, bs_full, w_conv, w_out):
    b, s, d = x.shape
    a_width = g_v.shape[1]
    b_width = w_conv.shape[1]
    in_width = w_in.shape[1]
    tile = MIX_TILE
    full = lambda shape: pl.BlockSpec(shape, lambda i, j: (0,) * len(shape))
    kern = functools.partial(_mixer_prompt_kernel, tile=tile, a_width=a_width, b_width=b_width)
    return pl.pallas_call(
        kern,
        grid=(b, s // tile),
        in_specs=[
            pl.BlockSpec((None, tile, d), lambda i, j: (i, j, 0)),
            full((1, d)), full((d, in_width)), full((1, a_width)),
            full(ws_pairs.shape), full(bs_full.shape), full(w_conv.shape), full(w_out.shape),
        ],
        out_specs=[
            pl.BlockSpec((None, tile, d), lambda i, j: (i, j, 0)),
            pl.BlockSpec((None, CONV_W - 1, b_width), lambda i, j: (i, 0, 0)),
        ],
        out_shape=[jax.ShapeDtypeStruct((b, s, d), F32),
                   jax.ShapeDtypeStruct((b, CONV_W - 1, b_width), F32)],
        scratch_shapes=[pltpu.VMEM((tile + 8, b_width), F32)],
        compiler_params=_cparams(("arbitrary", "arbitrary")),
        name="mixer_prompt",
    )(x, g_mix, w_in, g_v, ws_pairs, bs_full, w_conv, w_out)


def _mixer_sample_kernel(x_ref, s0_ref, s1_ref, gmix_ref, win_ref, gv_ref, w00_ref, b0_ref, wc_ref,
                         wout_ref, gx_ref, wq_ref, x1_ref, v_ref, z_ref, q_ref, *, a_width, b_width):
    x = x_ref[...]
    h = _rms(x, gmix_ref[...]).astype(BF16)
    proj = _dot(h, win_ref[...])
    u = _gelu(proj[:, :a_width])
    v = _head_rms(_gelu(proj[:, a_width:2 * a_width]), gv_ref[...], a_width)
    v_ref[...] = v
    o = 2 * a_width
    hb = proj[:, o:o + b_width]
    bg = proj[:, o + b_width:o + 2 * b_width]
    cg = proj[:, o + 2 * b_width:o + 3 * b_width]
    a_out = u * (v * w00_ref[...] + b0_ref[...])
    z = cg * hb
    z_ref[...] = z
    wc = wc_ref[...]
    conv = s0_ref[...] * wc[0:1] + s1_ref[...] * wc[1:2] + z * wc[2:3]
    b_out = bg * conv
    y = _dot(a_out.astype(BF16), wout_ref[0:a_width, :]) + _dot(b_out.astype(BF16), wout_ref[a_width:, :])
    x1 = x + y
    x1_ref[...] = x1
    q_ref[...] = _dot(_rms(x1, gx_ref[...]).astype(BF16), wq_ref[...])


def _mixer_sample(x, s0, s1, g_mix, w_in, g_v, w00, b0, w_conv, w_out, g_x, w_q):
    n, d = x.shape
    a_width = g_v.shape[1]
    b_width = w_conv.shape[1]
    kern = functools.partial(_mixer_sample_kernel, a_width=a_width, b_width=b_width)
    return pl.pallas_call(
        kern,
        out_shape=[jax.ShapeDtypeStruct((n, d), F32), jax.ShapeDtypeStruct((n, a_width), F32),
                   jax.ShapeDtypeStruct((n, b_width), F32), jax.ShapeDtypeStruct((n, d), F32)],
        compiler_params=_cparams(),
        name="mixer_sample",
    )(x, s0, s1, g_mix, w_in, g_v, w00, b0, w_conv, w_out, g_x, w_q)


def _router_tail(x2, gmoe, wrt, br, h3_ref, tope_ref, gates_ref):
    d = x2.shape[1]
    h3 = _rms(x2, gmoe).astype(BF16)
    h3f = h3.astype(F32)
    lo = lax.shift_right_logical(pltpu.bitcast(h3f[:, :d // 2], U32), jnp.uint32(16))
    hi = lax.bitwise_and(pltpu.bitcast(h3f[:, d // 2:], U32), jnp.uint32(0xFFFF0000))
    h3_ref[...] = lax.bitwise_or(lo, hi)

    logits = _dot_nt(wrt, h3) + br
    n_e, n_t = logits.shape
    e_idx = lax.broadcasted_iota(I32, (n_e, n_t), 0).astype(F32)
    tops, idxs = [], []
    for _ in range(TOP_K):
        m = jnp.max(logits, axis=0, keepdims=True)
        idx = jnp.min(jnp.where(logits == m, e_idx, float(n_e)), axis=0, keepdims=True)
        tops.append(m)
        idxs.append(idx)
        logits = jnp.where(e_idx == idx, -jnp.inf, logits)
    top = jnp.concatenate(tops, axis=0)
    ex = jnp.exp(top - top[0:1])
    gates_ref[...] = ex / jnp.sum(ex, axis=0, keepdims=True)
    tope_ref[...] = jnp.concatenate(idxs, axis=0).astype(I32)


def _attn_prompt_kernel(x1_ref, k_ref, v_ref, gx_ref, wq_ref, wxo_ref, gmoe_ref, wrt_ref, br_ref,
                        x2_ref, h3_ref, tope_ref, gates_ref):
    x = x1_ref[...]
    d = x.shape[1]
    hd = d // X_HEADS
    q = _dot(_rms(x, gx_ref[...]).astype(BF16), wq_ref[...]).astype(BF16)
    kb = k_ref[...].astype(BF16)
    vb = v_ref[...].astype(BF16)
    outs = []
    for hh in range(X_HEADS):
        sl = slice(hh * hd, (hh + 1) * hd)
        s = _dot_nt(q[:, sl], kb[:, sl]) * (hd ** -0.5)
        e = jnp.exp(s - jnp.max(s, axis=-1, keepdims=True))
        p = e / jnp.sum(e, axis=-1, keepdims=True)
        outs.append(_dot(p.astype(BF16), vb[:, sl]))
    o = jnp.concatenate(outs, axis=1).astype(BF16)
    x2 = x + _dot(o, wxo_ref[...])
    x2_ref[...] = x2
    _router_tail(x2, gmoe_ref[...], wrt_ref[...], br_ref[...], h3_ref, tope_ref, gates_ref)


def _attn_prompt(x1, mk, mv, g_x, w_q, w_xo, g_moe, wrt, br):
    b, s, d = x1.shape
    n_mem = mk.shape[1]
    tile = ATT_TILE
    nq = s // tile
    full = lambda shape: pl.BlockSpec(shape, lambda i, j: (0,) * len(shape))
    tok = pl.BlockSpec((None, tile, d), lambda i, j: (i, j, 0))
    mem = pl.BlockSpec((None, n_mem, d), lambda i, j: (i, 0, 0))
    lanes = pl.BlockSpec((TOP_K, tile), lambda i, j: (0, i * nq + j))
    return pl.pallas_call(
        _attn_prompt_kernel,
        grid=(b, nq),
        in_specs=[tok, mem, mem, full((1, d)), full((d, d)), full((d, d)), full((1, d)),
                  full(wrt.shape), full(br.shape)],
        out_specs=[tok, pl.BlockSpec((None, tile, d // 2), lambda i, j: (i, j, 0)), lanes, lanes],
        out_shape=[jax.ShapeDtypeStruct((b, s, d), F32), jax.ShapeDtypeStruct((b, s, d // 2), U32),
                   jax.ShapeDtypeStruct((TOP_K, b * s), I32), jax.ShapeDtypeStruct((TOP_K, b * s), F32)],
        compiler_params=_cparams(("arbitrary", "arbitrary")),
        name="attn_prompt",
    )(x1, mk, mv, g_x, w_q, w_xo, g_moe, wrt, br)


def _attn_sample_kernel(q_ref, k_ref, v_ref, o_ref, *, group):
    hd = q_ref.shape[2]
    for g in range(group):
        q = q_ref[g]
        s = jnp.sum(k_ref[g] * q[None], axis=-1, keepdims=True) * (hd ** -0.5)
        e = jnp.exp(s - jnp.max(s, axis=0, keepdims=True))
        p = e / jnp.sum(e, axis=0, keepdims=True)
        o_ref[g] = jnp.sum(p * v_ref[g], axis=0)


def _attn_sample(q, ck, cv):
    n, heads, hd = q.shape
    n_mem = ck.shape[1]
    group = SAMPLE_ATT_GROUP
    qo = pl.BlockSpec((group, heads, hd), lambda i: (i, 0, 0))
    kv = pl.BlockSpec((group, n_mem, heads, hd), lambda i: (i, 0, 0, 0))
    return pl.pallas_call(
        functools.partial(_attn_sample_kernel, group=group),
        grid=(n // group,),
        in_specs=[qo, kv, kv],
        out_specs=qo,
        out_shape=jax.ShapeDtypeStruct((n, heads, hd), F32),
        compiler_params=_cparams(("arbitrary",)),
        name="attn_sample",
    )(q, ck, cv)


def _tail_sample_kernel(x1_ref, o_ref, wxo_ref, gmoe_ref, wrt_ref, br_ref,
                        x2_ref, h3_ref, tope_ref, gates_ref):
    x2 = x1_ref[...] + _dot(o_ref[...].astype(BF16), wxo_ref[...])
    x2_ref[...] = x2
    _router_tail(x2, gmoe_ref[...], wrt_ref[...], br_ref[...], h3_ref, tope_ref, gates_ref)


def _tail_sample(x1, o, w_xo, g_moe, wrt, br):
    n, d = x1.shape
    return pl.pallas_call(
        _tail_sample_kernel,
        out_shape=[jax.ShapeDtypeStruct((n, d), F32), jax.ShapeDtypeStruct((n, d // 2), U32),
                   jax.ShapeDtypeStruct((TOP_K, n), I32), jax.ShapeDtypeStruct((TOP_K, n), F32)],
        compiler_params=_cparams(),
        name="tail_sample",
    )(x1, o, w_xo, g_moe, wrt, br)


def _route_kernel(tope_ref, dest_ref, cnt_ref, *, n_blocks, tile, moe_block):
    e_idx = lax.broadcasted_iota(I32, (N_EXPERTS, tile), 0)
    earlier = jnp.where(lax.broadcasted_iota(I32, (tile, tile), 0)
                        < lax.broadcasted_iota(I32, (tile, tile), 1), 1.0, 0.0).astype(BF16)

    def onehot(k, off):
        return e_idx == tope_ref[pl.ds(k, 1), pl.ds(off, tile)]

    def count_body(j, cnt):
        off = pl.multiple_of(j * tile, tile)
        for k in range(TOP_K):
            cnt = cnt + jnp.sum(jnp.where(onehot(k, off), 1.0, 0.0), axis=1, keepdims=True)
        return cnt

    counts = lax.fori_loop(0, n_blocks, count_body, jnp.zeros((N_EXPERTS, 1), F32))
    cnt_ref[...] = jnp.broadcast_to(counts, cnt_ref.shape).astype(I32)

    n_blk = jnp.floor((counts + (moe_block - 1)) * (1.0 / moe_block))
    n_hi = jnp.floor(n_blk * (1.0 / 16.0))
    n_lo = n_blk - 16.0 * n_hi
    below = jnp.where(lax.broadcasted_iota(I32, (N_EXPERTS, N_EXPERTS), 1)
                      < lax.broadcasted_iota(I32, (N_EXPERTS, N_EXPERTS), 0), 1.0, 0.0).astype(BF16)
    wide = lambda c: jnp.broadcast_to(c, (N_EXPERTS, 128)).astype(BF16)
    start = (16.0 * _dot(below, wide(n_hi)) + _dot(below, wide(n_lo)))[:, 0:1] * float(moe_block)

    def dest_body(j, run):
        off = pl.multiple_of(j * tile, tile)
        for k in range(TOP_K):
            oh = onehot(k, off)
            ohf = jnp.where(oh, 1.0, 0.0)
            before = _dot(ohf.astype(BF16), earlier)
            dest = jnp.sum(jnp.where(oh, before + run, 0.0), axis=0, keepdims=True)
            dest_ref[pl.ds(k, 1), pl.ds(off, tile)] = dest.astype(I32)
            run = run + jnp.sum(ohf, axis=1, keepdims=True)
        return run

    lax.fori_loop(0, n_blocks, dest_body, start)


def _route(tope_pad):
    t_pad = tope_pad.shape[1]
    kern = functools.partial(_route_kernel, n_blocks=t_pad // ROUTE_TILE, tile=ROUTE_TILE,
                             moe_block=MOE_BLOCK)
    return pl.pallas_call(
        kern,
        out_shape=[jax.ShapeDtypeStruct((TOP_K, t_pad), I32),
                   jax.ShapeDtypeStruct((N_EXPERTS, 128), I32)],
        compiler_params=_cparams(),
        name="route",
    )(tope_pad)


def _dispatch(dest3, pad3, zero_rows, h_prompt, h_sample, n_rows):
    w = h_prompt.shape[1]
    n_prompt_chunks = h_prompt.shape[0] // SC_CHUNK
    n_chunks = dest3.shape[0]
    n_pad_chunks = pad3.shape[0]
    mesh = plsc.VectorSubcoreMesh(core_axis_name="c", subcore_axis_name="s")
    n_workers = mesh.num_cores * mesh.num_subcores

    n_steps = -(-n_chunks // n_workers)

    def body(dest_hbm, pad_hbm, zero_hbm, hp_hbm, hs_hbm, xs_hbm, idx0, idx1, rows0, rows1, sem):
        wid = lax.axis_index("s") * mesh.num_cores + lax.axis_index("c")
        idx_v, rows_v = (idx0, idx1), (rows0, rows1)

        pltpu.sync_copy(zero_hbm, rows0)
        for j in range(-(-n_pad_chunks // n_workers)):
            c = wid + j * n_workers

            @pl.when(c < n_pad_chunks)
            def _():
                pltpu.sync_copy(pad_hbm.at[c], idx0.at[pl.ds(0, 1)])
                pltpu.sync_copy(rows0, xs_hbm.at[idx0.at[0]])

        def chunk(j):
            return wid + j * n_workers

        def loads(j, slot):
            c = chunk(j)
            prompt = (hp_hbm, c)
            sample = (hs_hbm, c - n_prompt_chunks)
            return c, [
                (pltpu.make_async_copy(dest_hbm.at[c], idx_v[slot], sem.at[slot]), None),
                (pltpu.make_async_copy(prompt[0].at[pl.ds(prompt[1] * SC_CHUNK, SC_CHUNK)], rows_v[slot],
                                       sem.at[slot]), c < n_prompt_chunks),
                (pltpu.make_async_copy(sample[0].at[pl.ds(sample[1] * SC_CHUNK, SC_CHUNK)], rows_v[slot],
                                       sem.at[slot]), c >= n_prompt_chunks),
            ]

        def start_loads(j, slot):
            c, copies = loads(j, slot)
            for cp, cond in copies:
                pl.when((c < n_chunks) if cond is None else ((c < n_chunks) & cond))(cp.start)

        def wait_loads(j, slot):
            c, copies = loads(j, slot)
            for cp, cond in copies:
                pl.when((c < n_chunks) if cond is None else ((c < n_chunks) & cond))(cp.wait)

        def scatters(slot):
            return [pltpu.make_async_copy(rows_v[slot], xs_hbm.at[idx_v[slot].at[k]], sem.at[2 + slot])
                    for k in range(TOP_K)]

        start_loads(0, 0)
        for j in range(n_steps):
            slot = j % 2
            wait_loads(j, slot)
            if j >= 1:
                @pl.when(chunk(j - 1) < n_chunks)
                def _():
                    for cp in scatters(1 - slot):
                        cp.wait()
            if j + 1 < n_steps:
                start_loads(j + 1, 1 - slot)

            @pl.when(chunk(j) < n_chunks)
            def _():
                for cp in scatters(slot):
                    cp.start()

        @pl.when(chunk(n_steps - 1) < n_chunks)
        def _():
            for cp in scatters((n_steps - 1) % 2):
                cp.wait()

    return pl.kernel(
        body,
        out_type=jax.ShapeDtypeStruct((n_rows, w), U32),
        mesh=mesh,
        scratch_types=[pltpu.VMEM((TOP_K, SC_CHUNK), I32), pltpu.VMEM((TOP_K, SC_CHUNK), I32),
                       pltpu.VMEM((SC_CHUNK, w), U32), pltpu.VMEM((SC_CHUNK, w), U32),
                       pltpu.SemaphoreType.DMA((4,))],
        name="dispatch_sc",
    )(dest3, pad3, zero_rows, h_prompt, h_sample)


def _ffn_kernel(be_ref, nu_ref, xs_ref, w1_ref, b1_ref, w2_ref, b2_ref, y_ref, w1b, w2b):
    i = pl.program_id(0)

    @pl.when(i >= nu_ref[0])
    def _():
        y_ref[...] = jnp.zeros(y_ref.shape, y_ref.dtype)

    @pl.when(i < nu_ref[0])
    def _():
        e = be_ref[i]
        prev = be_ref[jnp.maximum(i - 1, 0)]

        @pl.when((i == 0) | (e != prev))
        def _():
            w1b[...] = w1_ref[...].astype(BF16)
            w2b[...] = w2_ref[...].astype(BF16)

        w = xs_ref[...]
        half = w.shape[1]
        d_e = w2b.shape[0]
        lo = pltpu.bitcast(lax.shift_left(w, jnp.uint32(16)), F32).astype(BF16)
        hi = pltpu.bitcast(lax.bitwise_and(w, jnp.uint32(0xFFFF0000)), F32).astype(BF16)
        gu = _dot(lo, w1b[0:half, :]) + _dot(hi, w1b[half:, :]) + b1_ref[...]
        gate = jnp.minimum(gu[:, :d_e], SWIGLU_LIMIT)
        up = jnp.clip(gu[:, d_e:], -SWIGLU_LIMIT, SWIGLU_LIMIT)
        glu = gate * jax.nn.sigmoid(gate * SWIGLU_ALPHA)
        act = ((up + 1.0) * glu).astype(BF16)
        y_ref[...] = _dot(act, w2b[...]) + b2_ref[...]


def _expert_ffn(block_e, n_used, xs, w1, b1, w2, b2):
    n_rows, half = xs.shape
    n_e, d, d2 = w1.shape
    d_e = w2.shape[1]
    blk = MOE_BLOCK
    last = lambda i, be, nu: jnp.minimum(i, nu[0] - 1)
    grid_spec = pltpu.PrefetchScalarGridSpec(
        num_scalar_prefetch=2,
        grid=(n_rows // blk,),
        in_specs=[
            pl.BlockSpec((blk, half), lambda i, be, nu: (last(i, be, nu), 0)),
            pl.BlockSpec((None, d, d2), lambda i, be, nu: (be[last(i, be, nu)], 0, 0)),
            pl.BlockSpec((None, 1, d2), lambda i, be, nu: (be[last(i, be, nu)], 0, 0)),
            pl.BlockSpec((None, d_e, d), lambda i, be, nu: (be[last(i, be, nu)], 0, 0)),
            pl.BlockSpec((None, 1, d), lambda i, be, nu: (be[last(i, be, nu)], 0, 0)),
        ],
        out_specs=pl.BlockSpec((blk, d), lambda i, be, nu: (i, 0)),
        scratch_shapes=[pltpu.VMEM((d, d2), BF16), pltpu.VMEM((d_e, d), BF16)],
    )
    return pl.pallas_call(
        _ffn_kernel,
        grid_spec=grid_spec,
        out_shape=jax.ShapeDtypeStruct((n_rows, d), F32),
        compiler_params=_cparams(("arbitrary",)),
        name="expert_ffn",
    )(block_e, n_used, xs, w1, b1.reshape(n_e, 1, d2), w2, b2.reshape(n_e, 1, d))


def _combine_kernel(dest_ref, x2_ref, gates_ref, gf_ref, y_hbm, out_ref, buf, sem, *, tile):
    def row_copy(d, k, t):
        return pltpu.make_async_copy(y_hbm.at[pl.ds(d, 1)], buf.at[k, pl.ds(t, 1)], sem.at[0])

    for t in range(tile):
        for k in range(TOP_K):
            row_copy(dest_ref[k, t], k, t).start()
    for t in range(tile):
        for k in range(TOP_K):
            row_copy(0, k, t).wait()

    g = jnp.concatenate([gates_ref[...], jnp.zeros((128 - TOP_K, tile), F32)], axis=0)
    gt = g.T
    moe = gt[:, 0:1] * buf[0]
    for k in range(1, TOP_K):
        moe = moe + gt[:, k:k + 1] * buf[k]
    out_ref[...] = _rms(x2_ref[...] + moe, gf_ref[...])


def _combine(dest, x2, gates, g_final, y, tile):
    n, d = x2.shape
    grid_spec = pltpu.PrefetchScalarGridSpec(
        num_scalar_prefetch=0,
        grid=(n // tile,),
        in_specs=[
            pl.BlockSpec((TOP_K, tile), lambda i: (0, i), memory_space=pltpu.SMEM),
            pl.BlockSpec((tile, d), lambda i: (i, 0)),
            pl.BlockSpec((TOP_K, tile), lambda i: (0, i)),
            pl.BlockSpec((1, d), lambda i: (0, 0)),
            pl.BlockSpec(memory_space=pl.ANY),
        ],
        out_specs=pl.BlockSpec((tile, d), lambda i: (i, 0)),
        scratch_shapes=[pltpu.VMEM((TOP_K, tile, d), F32), pltpu.SemaphoreType.DMA((1,))],
    )
    return pl.pallas_call(
        functools.partial(_combine_kernel, tile=tile),
        grid_spec=grid_spec,
        out_shape=jax.ShapeDtypeStruct((n, d), F32),
        compiler_params=_cparams(("arbitrary",)),
        name="combine",
    )(dest, x2, gates, g_final, y)


def kernel(x_prompt, x_sample, mem_prompt, cache_mem_k, cache_mem_v, state_conv, g_mix, w_in, g_v,
           w_spatial, b_spatial, w_conv, w_out, g_xattn, g_mem, w_q, w_k, w_v, w_xo, g_moe, w_router,
           b_router, w_gate_up, b_gate_up, w_down, b_down, g_final):
    depth = g_mix.shape[0]
    assert depth == 1, "one layer supported"
    b, s, d = x_prompt.shape
    nb, ns, _ = x_sample.shape
    assert ns == 1
    n_mem = mem_prompt.shape[1]
    a_heads, a_hd = g_v.shape[1], g_v.shape[2]
    a_width = a_heads * a_hd
    b_width = w_conv.shape[2]
    assert a_heads == A_HEADS and 2 * a_hd == CHUNK and w_spatial.shape[2] == CHUNK
    assert s % MIX_TILE == 0 and s % ATT_TILE == 0 and (b * n_mem) % KV_TILE == 0
    assert nb % SAMPLE_ATT_GROUP == 0
    l = 0
    row = lambda a: a.reshape(1, -1)

    w_in_b = w_in[l].astype(BF16)
    w_out_b = w_out[l].astype(BF16)
    w_q_b = w_q[l].astype(BF16)
    w_k_b = w_k[l].astype(BF16)
    w_v_b = w_v[l].astype(BF16)
    w_xo_b = w_xo[l].astype(BF16)
    wrt_b = w_router[l].T.astype(BF16)
    br_col = b_router[l].reshape(N_EXPERTS, 1).astype(F32)
    gv_row = row(g_v[l])
    ws = w_spatial[l]
    ws_pairs = jnp.concatenate([ws[0::2], ws[1::2]], axis=2)
    bs_full = jnp.repeat(b_spatial[l].T, a_hd, axis=1)
    w00 = row(jnp.repeat(ws[:, 0, 0], a_hd))
    b0 = row(jnp.repeat(b_spatial[l][:, 0], a_hd))

    mk2, mv2 = _mem_kv(mem_prompt.reshape(b * n_mem, d), row(g_mem[l]), w_k_b, w_v_b)
    x1_p, conv_p = _mixer_prompt(x_prompt, row(g_mix[l]), w_in_b, gv_row, ws_pairs, bs_full,
                                 w_conv[l], w_out_b)
    x2_p, h3_p, tope_p, gates_p = _attn_prompt(
        x1_p, mk2.reshape(b, n_mem, d), mv2.reshape(b, n_mem, d), row(g_xattn[l]), w_q_b, w_xo_b,
        row(g_moe[l]), wrt_b, br_col)

    xs2 = x_sample.reshape(nb, d)
    st = state_conv[l]
    x1_s, v_s, z_s, q_s = _mixer_sample(xs2, st[:, 0], st[:, 1], row(g_mix[l]), w_in_b, gv_row, w00, b0,
                                        w_conv[l], w_out_b, row(g_xattn[l]), w_q_b)
    x_heads, x_hd = cache_mem_k.shape[3], cache_mem_k.shape[4]
    assert depth == 1 and x_heads == X_HEADS
    o_s = _attn_sample(q_s.reshape(nb, x_heads, x_hd), cache_mem_k.reshape(nb, n_mem, x_heads, x_hd),
                       cache_mem_v.reshape(nb, n_mem, x_heads, x_hd)).reshape(nb, d)
    x2_s, h3_s, tope_s, gates_s = _tail_sample(x1_s, o_s, w_xo_b, row(g_moe[l]), wrt_b, br_col)

    t_p = b * s
    t_all = t_p + nb
    t_pad = -(-t_all // ROUTE_TILE) * ROUTE_TILE
    tope_all = jnp.concatenate([tope_p, tope_s, jnp.full((TOP_K, t_pad - t_all), -1, I32)], axis=1)
    dest, cnt = _route(tope_all)
    counts = cnt[:, 0]
    padded = (counts + MOE_BLOCK - 1) // MOE_BLOCK * MOE_BLOCK
    pad_ends = jnp.cumsum(padded)
    pad_starts = pad_ends - padded
    n_blocks = -(-(t_all * TOP_K) // MOE_BLOCK) + N_EXPERTS
    block_first = jnp.arange(n_blocks, dtype=I32) * MOE_BLOCK
    block_e = jnp.minimum(jnp.sum(pad_ends[None, :] <= block_first[:, None], axis=1),
                          N_EXPERTS - 1).astype(I32)
    n_used = (pad_ends[-1] // MOE_BLOCK).astype(I32).reshape(1)
    n_rows = n_blocks * MOE_BLOCK
    assert t_all % SC_CHUNK == 0
    dest3 = dest[:, :t_all].reshape(TOP_K, t_all // SC_CHUNK, SC_CHUNK).transpose(1, 0, 2)
    pad_rows = (pad_starts + counts)[:, None] + jnp.arange(MOE_BLOCK, dtype=I32)[None, :]
    pad3 = jnp.where(pad_rows < pad_ends[:, None], pad_rows, n_rows - 1).astype(I32)
    pad3 = pad3.reshape(-1, 1, SC_CHUNK)
    assert t_p % SC_CHUNK == 0
    xs_rows = _dispatch(dest3, pad3, jnp.zeros((SC_CHUNK, d // 2), U32), h3_p.reshape(t_p, d // 2), h3_s,
                        n_rows)
    y_rows = _expert_ffn(block_e, n_used, xs_rows, w_gate_up[l], b_gate_up[l], w_down[l], b_down[l])

    gf = row(g_final)
    y_p = _combine(dest[:, :t_p], x2_p.reshape(t_p, d), gates_p, gf, y_rows, COMBINE_TILE)
    y_s = _combine(dest[:, t_p:t_all], x2_s, gates_s, gf, y_rows, nb)

    x_heads = cache_mem_k.shape[3]
    return (y_p.reshape(b, s, d),
            y_s.reshape(nb, 1, d),
            mk2.reshape(1, b, n_mem, x_heads, d // x_heads),
            mv2.reshape(1, b, n_mem, x_heads, d // x_heads),
            conv_p.reshape(1, b, CONV_W - 1, b_width),
            jnp.stack([st[:, 1], z_s], axis=1).reshape(1, nb, CONV_W - 1, b_width),
            v_s.reshape(1, nb, 1, a_heads, a_hd))
```

```python
import functools

import jax
import jax.numpy as jnp
from jax import lax
from jax.experimental import pallas as pl
from jax.experimental.pallas import tpu as pltpu
from jax.experimental.pallas import tpu_sc as plsc

F32 = jnp.float32
BF16 = jnp.bfloat16
I32 = jnp.int32
U32 = jnp.uint32

A_HEADS = 8
CHUNK = 128
CONV_W = 3
X_HEADS = 4
N_EXPERTS = 32
TOP_K = 4
SWIGLU_LIMIT = 7.0
SWIGLU_ALPHA = 1.702
EPS = 1e-5

MIX_TILE = 512
ATT_TILE = 512
KV_TILE = 512
ROUTE_TILE = 512
MOE_BLOCK = 256
COMBINE_TILE = 256
SAMPLE_ATT_GROUP = 4
SC_CHUNK = 32
META_WIDTH = 128
VMEM_LIMIT = 56 * 1024 * 1024


def _cparams(sem=None):
    return pltpu.CompilerParams(dimension_semantics=sem, vmem_limit_bytes=VMEM_LIMIT)


def _rms(x, g):
    r = lax.rsqrt(jnp.mean(x * x, axis=-1, keepdims=True) + EPS)
    return (x * r) * g


def _gelu(x):
    return 0.5 * x * (1.0 + lax.erf(x * 0.7071067811865476))


def _dot(a, b):
    return jnp.dot(a, b, preferred_element_type=F32)


def _dot_nt(a, b):
    return lax.dot_general(a, b, (((1,), (1,)), ((), ())), preferred_element_type=F32)


def _memkv_kernel(m_ref, g_ref, wk_ref, wv_ref, k_ref, v_ref):
    h = _rms(m_ref[...], g_ref[...]).astype(BF16)
    k_ref[...] = _dot(h, wk_ref[...])
    v_ref[...] = _dot(h, wv_ref[...])


def _mem_kv(mem2d, g_mem, wk, wv):
    n, d = mem2d.shape
    row = pl.BlockSpec((KV_TILE, d), lambda i: (i, 0))
    full = lambda shape: pl.BlockSpec(shape, lambda i: (0,) * len(shape))
    return pl.pallas_call(
        _memkv_kernel,
        grid=(n // KV_TILE,),
        in_specs=[row, full((1, d)), full((d, d)), full((d, d))],
        out_specs=[row, row],
        out_shape=[jax.ShapeDtypeStruct((n, d), F32)] * 2,
        compiler_params=_cparams(("arbitrary",)),
        name="mem_kv",
    )(mem2d, g_mem, wk, wv)


def _head_rms(v, gv, a_width):
    hd = a_width // A_HEADS
    r_i = lax.broadcasted_iota(I32, (a_width, a_width), 0) // hd
    c_i = lax.broadcasted_iota(I32, (a_width, a_width), 1) // hd
    ones_bd = jnp.where(r_i == c_i, 1.0, 0.0).astype(BF16)
    sq = v * v
    sq_hi = sq.astype(BF16)
    sq_lo = (sq - sq_hi.astype(F32)).astype(BF16)
    gs = _dot(sq_hi, ones_bd) + _dot(sq_lo, ones_bd)
    return (v * lax.rsqrt(gs * (1.0 / hd) + EPS)) * gv


def _mixer_prompt_kernel(x_ref, gmix_ref, win_ref, gv_ref, ws_ref, bs_ref, wc_ref, wout_ref,
                         x1_ref, cs_ref, zbuf, *, tile, a_width, b_width):
    j = pl.program_id(1)
    nj = pl.num_programs(1)
    x = x_ref[...]
    h = _rms(x, gmix_ref[...]).astype(BF16)
    proj = _dot(h, win_ref[...])
    u = _gelu(proj[:, :a_width])
    v = _head_rms(_gelu(proj[:, a_width:2 * a_width]), gv_ref[...], a_width)
    o = 2 * a_width
    hb = proj[:, o:o + b_width]
    bg = proj[:, o + b_width:o + 2 * b_width]
    cg = proj[:, o + 2 * b_width:o + 3 * b_width]

    lane = lax.broadcasted_iota(I32, (CHUNK, 2 * (a_width // A_HEADS)), 1)
    first = lane < (a_width // A_HEADS)
    t_i = lax.broadcasted_iota(I32, (CHUNK, 2 * CHUNK), 0)
    s_i = lax.broadcasted_iota(I32, (CHUNK, 2 * CHUNK), 1) % CHUNK
    causal = s_i <= t_i
    pair_cols = []
    for p in range(A_HEADS // 2):
        w_pair = jnp.where(causal, ws_ref[p], 0.0).astype(BF16)
        vp = v[:, p * CHUNK:(p + 1) * CHUNK]
        rows = []
        for c in range(tile // CHUNK):
            vc = vp[c * CHUNK:(c + 1) * CHUNK]
            rhs = jnp.concatenate([jnp.where(first, vc, 0.0), jnp.where(first, 0.0, vc)],
                                  axis=0).astype(BF16)
            rows.append(_dot(w_pair, rhs))
        pair_cols.append(jnp.concatenate(rows, axis=0))
    gate = jnp.concatenate(pair_cols, axis=1)
    bias = jnp.concatenate([bs_ref[...]] * (tile // CHUNK), axis=0)
    a_out = u * (gate + bias)

    @pl.when(j == 0)
    def _():
        zbuf[0:8, :] = jnp.zeros((8, b_width), F32)

    z = cg * hb
    zbuf[8:tile + 8, :] = z
    z1 = zbuf[7:tile + 7, :]
    z2 = zbuf[6:tile + 6, :]
    wc = wc_ref[...]
    conv = z2 * wc[0:1] + z1 * wc[1:2] + z * wc[2:3]
    b_out = bg * conv
    tail = zbuf[tile:tile + 8, :]
    zbuf[0:8, :] = tail

    @pl.when(j == nj - 1)
    def _():
        cs_ref[...] = tail[8 - (CONV_W - 1):, :]

    y = _dot(a_out.astype(BF16), wout_ref[0:a_width, :]) + _dot(b_out.astype(BF16), wout_ref[a_width:, :])
    x1_ref[...] = x + y


def _mixer_prompt(x, g_mix, w_in, g_v, ws_pairs, bs_full, w_conv, w_out):
    b, s, d = x.shape
    a_width = g_v.shape[1]
    b_width = w_conv.shape[1]
    in_width = w_in.shape[1]
    tile = MIX_TILE
    full = lambda shape: pl.BlockSpec(shape, lambda i, j: (0,) * len(shape))
    kern = functools.partial(_mixer_prompt_kernel, tile=tile, a_width=a_width, b_width=b_width)
    return pl.pallas_call(
        kern,
        grid=(b, s // tile),
        in_specs=[
            pl.BlockSpec((None, tile, d), lambda i, j: (i, j, 0)),
            full((1, d)), full((d, in_width)), full((1, a_width)),
            full(ws_pairs.shape), full(bs_full.shape), full(w_conv.shape), full(w_out.shape),
        ],
        out_specs=[
            pl.BlockSpec((None, tile, d), lambda i, j: (i, j, 0)),
            pl.BlockSpec((None, CONV_W - 1, b_width), lambda i, j: (i, 0, 0)),
        ],
        out_shape=[jax.ShapeDtypeStruct((b, s, d), F32),
                   jax.ShapeDtypeStruct((b, CONV_W - 1, b_width), F32)],
        scratch_shapes=[pltpu.VMEM((tile + 8, b_width), F32)],
        compiler_params=_cparams(("arbitrary", "arbitrary")),
        name="mixer_prompt",
    )(x, g_mix, w_in, g_v, ws_pairs, bs_full, w_conv, w_out)


def _mixer_sample_kernel(x_ref, s0_ref, s1_ref, gmix_ref, win_ref, gv_ref, w00_ref, b0_ref, wc_ref,
                         wout_ref, gx_ref, wq_ref, x1_ref, v_ref, z_ref, q_ref, *, a_width, b_width):
    x = x_ref[...]
    h = _rms(x, gmix_ref[...]).astype(BF16)
    proj = _dot(h, win_ref[...])
    u = _gelu(proj[:, :a_width])
    v = _head_rms(_gelu(proj[:, a_width:2 * a_width]), gv_ref[...], a_width)
    v_ref[...] = v
    o = 2 * a_width
    hb = proj[:, o:o + b_width]
    bg = proj[:, o + b_width:o + 2 * b_width]
    cg = proj[:, o + 2 * b_width:o + 3 * b_width]
    a_out = u * (v * w00_ref[...] + b0_ref[...])
    z = cg * hb
    z_ref[...] = z
    wc = wc_ref[...]
    conv = s0_ref[...] * wc[0:1] + s1_ref[...] * wc[1:2] + z * wc[2:3]
    b_out = bg * conv
    y = _dot(a_out.astype(BF16), wout_ref[0:a_width, :]) + _dot(b_out.astype(BF16), wout_ref[a_width:, :])
    x1 = x + y
    x1_ref[...] = x1
    q_ref[...] = _dot(_rms(x1, gx_ref[...]).astype(BF16), wq_ref[...])


def _mixer_sample(x, s0, s1, g_mix, w_in, g_v, w00, b0, w_conv, w_out, g_x, w_q):
    n, d = x.shape
    a_width = g_v.shape[1]
    b_width = w_conv.shape[1]
    kern = functools.partial(_mixer_sample_kernel, a_width=a_width, b_width=b_width)
    return pl.pallas_call(
        kern,
        out_shape=[jax.ShapeDtypeStruct((n, d), F32), jax.ShapeDtypeStruct((n, a_width), F32),
                   jax.ShapeDtypeStruct((n, b_width), F32), jax.ShapeDtypeStruct((n, d), F32)],
        compiler_params=_cparams(),
        name="mixer_sample",
    )(x, s0, s1, g_mix, w_in, g_v, w00, b0, w_conv, w_out, g_x, w_q)


def _router_tail(x2, gmoe, wrt, br, h3_ref, tope_ref, gates_ref):
    d = x2.shape[1]
    h3 = _rms(x2, gmoe).astype(BF16)
    h3f = h3.astype(F32)
    lo = lax.shift_right_logical(pltpu.bitcast(h3f[:, :d // 2], U32), jnp.uint32(16))
    hi = lax.bitwise_and(pltpu.bitcast(h3f[:, d // 2:], U32), jnp.uint32(0xFFFF0000))
    h3_ref[...] = lax.bitwise_or(lo, hi)

    logits = _dot_nt(wrt, h3) + br
    n_e, n_t = logits.shape
    e_idx = lax.broadcasted_iota(I32, (n_e, n_t), 0).astype(F32)
    tops, idxs = [], []
    for _ in range(TOP_K):
        m = jnp.max(logits, axis=0, keepdims=True)
        idx = jnp.min(jnp.where(logits == m, e_idx, float(n_e)), axis=0, keepdims=True)
        tops.append(m)
        idxs.append(idx)
        logits = jnp.where(e_idx == idx, -jnp.inf, logits)
    top = jnp.concatenate(tops, axis=0)
    ex = jnp.exp(top - top[0:1])
    gates_ref[...] = ex / jnp.sum(ex, axis=0, keepdims=True)
    tope_ref[...] = jnp.concatenate(idxs, axis=0).astype(I32)


def _attn_prompt_kernel(x1_ref, k_ref, v_ref, gx_ref, wq_ref, wxo_ref, gmoe_ref, wrt_ref, br_ref,
                        x2_ref, h3_ref, tope_ref, gates_ref):
    x = x1_ref[...]
    d = x.shape[1]
    hd = d // X_HEADS
    q = _dot(_rms(x, gx_ref[...]).astype(BF16), wq_ref[...]).astype(BF16)
    kb = k_ref[...].astype(BF16)
    vb = v_ref[...].astype(BF16)
    outs = []
    for hh in range(X_HEADS):
        sl = slice(hh * hd, (hh + 1) * hd)
        s = _dot_nt(q[:, sl], kb[:, sl]) * (hd ** -0.5)
        e = jnp.exp(s - jnp.max(s, axis=-1, keepdims=True))
        p = e / jnp.sum(e, axis=-1, keepdims=True)
        outs.append(_dot(p.astype(BF16), vb[:, sl]))
    o = jnp.concatenate(outs, axis=1).astype(BF16)
    x2 = x + _dot(o, wxo_ref[...])
    x2_ref[...] = x2
    _router_tail(x2, gmoe_ref[...], wrt_ref[...], br_ref[...], h3_ref, tope_ref, gates_ref)


def _attn_prompt(x1, mk, mv, g_x, w_q, w_xo, g_moe, wrt, br):
    b, s, d = x1.shape
    n_mem = mk.shape[1]
    tile = ATT_TILE
    nq = s // tile
    full = lambda shape: pl.BlockSpec(shape, lambda i, j: (0,) * len(shape))
    tok = pl.BlockSpec((None, tile, d), lambda i, j: (i, j, 0))
    mem = pl.BlockSpec((None, n_mem, d), lambda i, j: (i, 0, 0))
    lanes = pl.BlockSpec((TOP_K, tile), lambda i, j: (0, i * nq + j))
    return pl.pallas_call(
        _attn_prompt_kernel,
        grid=(b, nq),
        in_specs=[tok, mem, mem, full((1, d)), full((d, d)), full((d, d)), full((1, d)),
                  full(wrt.shape), full(br.shape)],
        out_specs=[tok, pl.BlockSpec((None, tile, d // 2), lambda i, j: (i, j, 0)), lanes, lanes],
        out_shape=[jax.ShapeDtypeStruct((b, s, d), F32), jax.ShapeDtypeStruct((b, s, d // 2), U32),
                   jax.ShapeDtypeStruct((TOP_K, b * s), I32), jax.ShapeDtypeStruct((TOP_K, b * s), F32)],
        compiler_params=_cparams(("arbitrary", "arbitrary")),
        name="attn_prompt",
    )(x1, mk, mv, g_x, w_q, w_xo, g_moe, wrt, br)


def _attn_sample_kernel(q_ref, k_ref, v_ref, o_ref, *, group):
    hd = q_ref.shape[2]
    for g in range(group):
        q = q_ref[g]
        s = jnp.sum(k_ref[g] * q[None], axis=-1, keepdims=True) * (hd ** -0.5)
        e = jnp.exp(s - jnp.max(s, axis=0, keepdims=True))
        p = e / jnp.sum(e, axis=0, keepdims=True)
        o_ref[g] = jnp.sum(p * v_ref[g], axis=0)


def _attn_sample(q, ck, cv):
    n, heads, hd = q.shape
    n_mem = ck.shape[1]
    group = SAMPLE_ATT_GROUP
    qo = pl.BlockSpec((group, heads, hd), lambda i: (i, 0, 0))
    kv = pl.BlockSpec((group, n_mem, heads, hd), lambda i: (i, 0, 0, 0))
    return pl.pallas_call(
        functools.partial(_attn_sample_kernel, group=group),
        grid=(n // group,),
        in_specs=[qo, kv, kv],
        out_specs=qo,
        out_shape=jax.ShapeDtypeStruct((n, heads, hd), F32),
        compiler_params=_cparams(("arbitrary",)),
        name="attn_sample",
    )(q, ck, cv)


def _tail_sample_kernel(x1_ref, o_ref, wxo_ref, gmoe_ref, wrt_ref, br_ref,
                        x2_ref, h3_ref, tope_ref, gates_ref):
    x2 = x1_ref[...] + _dot(o_ref[...].astype(BF16), wxo_ref[...])
    x2_ref[...] = x2
    _router_tail(x2, gmoe_ref[...], wrt_ref[...], br_ref[...], h3_ref, tope_ref, gates_ref)


def _tail_sample(x1, o, w_xo, g_moe, wrt, br):
    n, d = x1.shape
    return pl.pallas_call(
        _tail_sample_kernel,
        out_shape=[jax.ShapeDtypeStruct((n, d), F32), jax.ShapeDtypeStruct((n, d // 2), U32),
                   jax.ShapeDtypeStruct((TOP_K, n), I32), jax.ShapeDtypeStruct((TOP_K, n), F32)],
        compiler_params=_cparams(),
        name="tail_sample",
    )(x1, o, w_xo, g_moe, wrt, br)


def _route_kernel(tope_ref, dest_ref, cnt_ref, *, n_blocks, tile, moe_block):
    e_idx = lax.broadcasted_iota(I32, (N_EXPERTS, tile), 0)
    earlier = jnp.where(lax.broadcasted_iota(I32, (tile, tile), 0)
                        < lax.broadcasted_iota(I32, (tile, tile), 1), 1.0, 0.0).astype(BF16)

    def onehot(k, off):
        return e_idx == tope_ref[pl.ds(k, 1), pl.ds(off, tile)]

    def count_body(j, cnt):
        off = pl.multiple_of(j * tile, tile)
        for k in range(TOP_K):
            cnt = cnt + jnp.sum(jnp.where(onehot(k, off), 1.0, 0.0), axis=1, keepdims=True)
        return cnt

    counts = lax.fori_loop(0, n_blocks, count_body, jnp.zeros((N_EXPERTS, 1), F32))
    cnt_ref[...] = jnp.broadcast_to(counts, cnt_ref.shape).astype(I32)

    n_blk = jnp.floor((counts + (moe_block - 1)) * (1.0 / moe_block))
    n_hi = jnp.floor(n_blk * (1.0 / 16.0))
    n_lo = n_blk - 16.0 * n_hi
    below = jnp.where(lax.broadcasted_iota(I32, (N_EXPERTS, N_EXPERTS), 1)
                      < lax.broadcasted_iota(I32, (N_EXPERTS, N_EXPERTS), 0), 1.0, 0.0).astype(BF16)
    wide = lambda c: jnp.broadcast_to(c, (N_EXPERTS, 128)).astype(BF16)
    start = (16.0 * _dot(below, wide(n_hi)) + _dot(below, wide(n_lo)))[:, 0:1] * float(moe_block)

    def dest_body(j, run):
        off = pl.multiple_of(j * tile, tile)
        for k in range(TOP_K):
            oh = onehot(k, off)
            ohf = jnp.where(oh, 1.0, 0.0)
            before = _dot(ohf.astype(BF16), earlier)
            dest = jnp.sum(jnp.where(oh, before + run, 0.0), axis=0, keepdims=True)
            dest_ref[pl.ds(k, 1), pl.ds(off, tile)] = dest.astype(I32)
            run = run + jnp.sum(ohf, axis=1, keepdims=True)
        return run

    lax.fori_loop(0, n_blocks, dest_body, start)


def _route(tope_pad):
    t_pad = tope_pad.shape[1]
    kern = functools.partial(_route_kernel, n_blocks=t_pad // ROUTE_TILE, tile=ROUTE_TILE,
                             moe_block=MOE_BLOCK)
    return pl.pallas_call(
        kern,
        out_shape=[jax.ShapeDtypeStruct((TOP_K, t_pad), I32),
                   jax.ShapeDtypeStruct((N_EXPERTS, 128), I32)],
        compiler_params=_cparams(),
        name="route",
    )(tope_pad)


def _dispatch(dest3, pad3, zero_rows, pad_meta, h_prompt, h_sample, meta, n_rows):
    w = h_prompt.shape[1]
    mw = meta.shape[2]
    n_prompt_chunks = h_prompt.shape[0] // SC_CHUNK
    n_chunks = dest3.shape[0]
    n_pad_chunks = pad3.shape[0]
    mesh = plsc.VectorSubcoreMesh(core_axis_name="c", subcore_axis_name="s")
    n_workers = mesh.num_cores * mesh.num_subcores

    n_steps = -(-n_chunks // n_workers)

    def body(dest_hbm, pad_hbm, zero_hbm, padmeta_hbm, hp_hbm, hs_hbm, meta_hbm, xs_hbm, inv_hbm,
             idx0, idx1, rows0, rows1, meta0, meta1, sem):
        wid = lax.axis_index("s") * mesh.num_cores + lax.axis_index("c")
        idx_v, rows_v, meta_v = (idx0, idx1), (rows0, rows1), (meta0, meta1)

        pltpu.sync_copy(zero_hbm, rows0)
        pltpu.sync_copy(padmeta_hbm, meta0.at[0])
        for j in range(-(-n_pad_chunks // n_workers)):
            c = wid + j * n_workers

            @pl.when(c < n_pad_chunks)
            def _():
                pltpu.sync_copy(pad_hbm.at[c], idx0.at[pl.ds(0, 1)])
                pltpu.sync_copy(rows0, xs_hbm.at[idx0.at[0]])
                pltpu.sync_copy(meta0.at[0], inv_hbm.at[idx0.at[0]])

        def chunk(j):
            return wid + j * n_workers

        def loads(j, slot):
            c = chunk(j)
            prompt = (hp_hbm, c)
            sample = (hs_hbm, c - n_prompt_chunks)
            return c, [
                (pltpu.make_async_copy(dest_hbm.at[c], idx_v[slot], sem.at[slot]), None),
                (pltpu.make_async_copy(meta_hbm.at[:, pl.ds(c * SC_CHUNK, SC_CHUNK)], meta_v[slot],
                                       sem.at[slot]), None),
                (pltpu.make_async_copy(prompt[0].at[pl.ds(prompt[1] * SC_CHUNK, SC_CHUNK)], rows_v[slot],
                                       sem.at[slot]), c < n_prompt_chunks),
                (pltpu.make_async_copy(sample[0].at[pl.ds(sample[1] * SC_CHUNK, SC_CHUNK)], rows_v[slot],
                                       sem.at[slot]), c >= n_prompt_chunks),
            ]

        def start_loads(j, slot):
            c, copies = loads(j, slot)
            for cp, cond in copies:
                pl.when((c < n_chunks) if cond is None else ((c < n_chunks) & cond))(cp.start)

        def wait_loads(j, slot):
            c, copies = loads(j, slot)
            for cp, cond in copies:
                pl.when((c < n_chunks) if cond is None else ((c < n_chunks) & cond))(cp.wait)

        def scatters(slot):
            rows = [pltpu.make_async_copy(rows_v[slot], xs_hbm.at[idx_v[slot].at[k]], sem.at[2 + slot])
                    for k in range(TOP_K)]
            recs = [pltpu.make_async_copy(meta_v[slot].at[k], inv_hbm.at[idx_v[slot].at[k]],
                                          sem.at[2 + slot]) for k in range(TOP_K)]
            return rows + recs

        start_loads(0, 0)
        for j in range(n_steps):
            slot = j % 2
            wait_loads(j, slot)
            if j >= 1:
                @pl.when(chunk(j - 1) < n_chunks)
                def _():
                    for cp in scatters(1 - slot):
                        cp.wait()
            if j + 1 < n_steps:
                start_loads(j + 1, 1 - slot)

            @pl.when(chunk(j) < n_chunks)
            def _():
                for cp in scatters(slot):
                    cp.start()

        @pl.when(chunk(n_steps - 1) < n_chunks)
        def _():
            for cp in scatters((n_steps - 1) % 2):
                cp.wait()

    return pl.kernel(
        body,
        out_type=[jax.ShapeDtypeStruct((n_rows, w), U32), jax.ShapeDtypeStruct((n_rows, mw), I32)],
        mesh=mesh,
        scratch_types=[pltpu.VMEM((TOP_K, SC_CHUNK), I32), pltpu.VMEM((TOP_K, SC_CHUNK), I32),
                       pltpu.VMEM((SC_CHUNK, w), U32), pltpu.VMEM((SC_CHUNK, w), U32),
                       pltpu.VMEM((TOP_K, SC_CHUNK, mw), I32), pltpu.VMEM((TOP_K, SC_CHUNK, mw), I32),
                       pltpu.SemaphoreType.DMA((4,))],
        name="dispatch_sc",
    )(dest3, pad3, zero_rows, pad_meta, h_prompt, h_sample, meta)


def _ffn_kernel(be_ref, nu_ref, code_ref, xs_ref, meta_ref, w1_ref, b1_ref, w2_ref, b2_ref, out_hbm,
                w1b, w2b, ybuf, sem):
    i = pl.program_id(0)
    nu = nu_ref[0]
    blk = xs_ref.shape[0]

    def row_copy(slot, r, dst_row):
        return pltpu.make_async_copy(ybuf.at[slot, pl.ds(r, 1)], out_hbm.at[pl.ds(dst_row, 1)],
                                     sem.at[slot])

    def wait_rows(slot):
        for r in range(blk):
            row_copy(slot, r, 0).wait()

    def issue_rows(slot):
        for r in range(blk):
            row_copy(slot, r, code_ref[0, 0, r]).start()

    def cast_weights_if_new_expert():
        e = be_ref[i]
        prev = be_ref[jnp.maximum(i - 1, 0)]

        @pl.when((i == 0) | (e != prev))
        def _():
            w1b[...] = w1_ref[...].astype(BF16)
            w2b[...] = w2_ref[...].astype(BF16)

    def compute(slot):
        w = xs_ref[...]
        half = w.shape[1]
        d_e = w2b.shape[0]
        lo = pltpu.bitcast(lax.shift_left(w, jnp.uint32(16)), F32).astype(BF16)
        hi = pltpu.bitcast(lax.bitwise_and(w, jnp.uint32(0xFFFF0000)), F32).astype(BF16)
        gu = _dot(lo, w1b[0:half, :]) + _dot(hi, w1b[half:, :]) + b1_ref[...]
        gate = jnp.minimum(gu[:, :d_e], SWIGLU_LIMIT)
        up = jnp.clip(gu[:, d_e:], -SWIGLU_LIMIT, SWIGLU_LIMIT)
        glu = gate * jax.nn.sigmoid(gate * SWIGLU_ALPHA)
        act = ((up + 1.0) * glu).astype(BF16)
        route_gate = pltpu.bitcast(meta_ref[:, 1:2], F32)
        ybuf[slot] = (_dot(act, w2b[...]) + b2_ref[...]) * route_gate

    for slot in range(2):
        @pl.when(i % 2 == slot)
        def _():
            @pl.when((i >= 2) & (i - 2 < nu))
            def _():
                wait_rows(slot)

            @pl.when(i == 0)
            def _():
                cast_weights_if_new_expert()
                compute(slot)

            @pl.when((i >= 1) & (i < nu))
            def _():
                cast_weights_if_new_expert()
                issue_rows(1 - slot)
                compute(slot)

            @pl.when((i >= 1) & (i == nu))
            def _():
                issue_rows(1 - slot)


def _expert_ffn(block_e, n_used, codes, xs, inv, w1, b1, w2, b2, n_out_rows):
    n_rows, half = xs.shape
    n_e, d, d2 = w1.shape
    d_e = w2.shape[1]
    blk = MOE_BLOCK
    cur = lambda i, be, nu: jnp.minimum(i, nu[0] - 1)
    prv = lambda i, be, nu: jnp.clip(i - 1, 0, nu[0] - 1)
    grid_spec = pltpu.PrefetchScalarGridSpec(
        num_scalar_prefetch=2,
        grid=(n_rows // blk + 1,),
        in_specs=[
            pl.BlockSpec((1, 1, blk), lambda i, be, nu: (prv(i, be, nu), 0, 0), memory_space=pltpu.SMEM),
            pl.BlockSpec((blk, half), lambda i, be, nu: (cur(i, be, nu), 0)),
            pl.BlockSpec((blk, inv.shape[1]), lambda i, be, nu: (cur(i, be, nu), 0)),
            pl.BlockSpec((None, d, d2), lambda i, be, nu: (be[cur(i, be, nu)], 0, 0)),
            pl.BlockSpec((None, 1, d2), lambda i, be, nu: (be[cur(i, be, nu)], 0, 0)),
            pl.BlockSpec((None, d_e, d), lambda i, be, nu: (be[cur(i, be, nu)], 0, 0)),
            pl.BlockSpec((None, 1, d), lambda i, be, nu: (be[cur(i, be, nu)], 0, 0)),
        ],
        out_specs=pl.BlockSpec(memory_space=pl.ANY),
        scratch_shapes=[pltpu.VMEM((d, d2), BF16), pltpu.VMEM((d_e, d), BF16),
                        pltpu.VMEM((2, blk, d), F32), pltpu.SemaphoreType.DMA((2,))],
    )
    return pl.pallas_call(
        _ffn_kernel,
        grid_spec=grid_spec,
        out_shape=jax.ShapeDtypeStruct((n_out_rows, d), F32),
        compiler_params=_cparams(("arbitrary",)),
        name="expert_ffn",
    )(block_e, n_used, codes, xs, inv, w1, b1.reshape(n_e, 1, d2), w2, b2.reshape(n_e, 1, d))


def _combine_kernel(x2_ref, y4_ref, gf_ref, out_ref):
    moe = (y4_ref[0] + y4_ref[1]) + (y4_ref[2] + y4_ref[3])
    out_ref[...] = _rms(x2_ref[...] + moe, gf_ref[...])


def _combine(x2, y4, g_final, tile, first_tile):
    n, d = x2.shape
    return pl.pallas_call(
        _combine_kernel,
        grid=(n // tile,),
        in_specs=[
            pl.BlockSpec((tile, d), lambda i: (i, 0)),
            pl.BlockSpec((TOP_K, tile, d), lambda i: (0, i + first_tile, 0)),
            pl.BlockSpec((1, d), lambda i: (0, 0)),
        ],
        out_specs=pl.BlockSpec((tile, d), lambda i: (i, 0)),
        out_shape=jax.ShapeDtypeStruct((n, d), F32),
        compiler_params=_cparams(("arbitrary",)),
        name="combine",
    )(x2, y4, g_final)


def kernel(x_prompt, x_sample, mem_prompt, cache_mem_k, cache_mem_v, state_conv, g_mix, w_in, g_v,
           w_spatial, b_spatial, w_conv, w_out, g_xattn, g_mem, w_q, w_k, w_v, w_xo, g_moe, w_router,
           b_router, w_gate_up, b_gate_up, w_down, b_down, g_final):
    depth = g_mix.shape[0]
    assert depth == 1, "one layer supported"
    b, s, d = x_prompt.shape
    nb, ns, _ = x_sample.shape
    assert ns == 1
    n_mem = mem_prompt.shape[1]
    a_heads, a_hd = g_v.shape[1], g_v.shape[2]
    a_width = a_heads * a_hd
    b_width = w_conv.shape[2]
    assert a_heads == A_HEADS and 2 * a_hd == CHUNK and w_spatial.shape[2] == CHUNK
    assert s % MIX_TILE == 0 and s % ATT_TILE == 0 and (b * n_mem) % KV_TILE == 0
    assert nb % SAMPLE_ATT_GROUP == 0
    l = 0
    row = lambda a: a.reshape(1, -1)

    w_in_b = w_in[l].astype(BF16)
    w_out_b = w_out[l].astype(BF16)
    w_q_b = w_q[l].astype(BF16)
    w_k_b = w_k[l].astype(BF16)
    w_v_b = w_v[l].astype(BF16)
    w_xo_b = w_xo[l].astype(BF16)
    wrt_b = w_router[l].T.astype(BF16)
    br_col = b_router[l].reshape(N_EXPERTS, 1).astype(F32)
    gv_row = row(g_v[l])
    ws = w_spatial[l]
    ws_pairs = jnp.concatenate([ws[0::2], ws[1::2]], axis=2)
    bs_full = jnp.repeat(b_spatial[l].T, a_hd, axis=1)
    w00 = row(jnp.repeat(ws[:, 0, 0], a_hd))
    b0 = row(jnp.repeat(b_spatial[l][:, 0], a_hd))

    mk2, mv2 = _mem_kv(mem_prompt.reshape(b * n_mem, d), row(g_mem[l]), w_k_b, w_v_b)
    x1_p, conv_p = _mixer_prompt(x_prompt, row(g_mix[l]), w_in_b, gv_row, ws_pairs, bs_full,
                                 w_conv[l], w_out_b)
    x2_p, h3_p, tope_p, gates_p = _attn_prompt(
        x1_p, mk2.reshape(b, n_mem, d), mv2.reshape(b, n_mem, d), row(g_xattn[l]), w_q_b, w_xo_b,
        row(g_moe[l]), wrt_b, br_col)

    xs2 = x_sample.reshape(nb, d)
    st = state_conv[l]
    x1_s, v_s, z_s, q_s = _mixer_sample(xs2, st[:, 0], st[:, 1], row(g_mix[l]), w_in_b, gv_row, w00, b0,
                                        w_conv[l], w_out_b, row(g_xattn[l]), w_q_b)
    x_heads, x_hd = cache_mem_k.shape[3], cache_mem_k.shape[4]
    assert depth == 1 and x_heads == X_HEADS
    o_s = _attn_sample(q_s.reshape(nb, x_heads, x_hd), cache_mem_k.reshape(nb, n_mem, x_heads, x_hd),
                       cache_mem_v.reshape(nb, n_mem, x_heads, x_hd)).reshape(nb, d)
    x2_s, h3_s, tope_s, gates_s = _tail_sample(x1_s, o_s, w_xo_b, row(g_moe[l]), wrt_b, br_col)

    t_p = b * s
    t_all = t_p + nb
    t_pad = -(-t_all // ROUTE_TILE) * ROUTE_TILE
    tope_all = jnp.concatenate([tope_p, tope_s, jnp.full((TOP_K, t_pad - t_all), -1, I32)], axis=1)
    dest, cnt = _route(tope_all)
    counts = cnt[:, 0]
    padded = (counts + MOE_BLOCK - 1) // MOE_BLOCK * MOE_BLOCK
    pad_ends = jnp.cumsum(padded)
    pad_starts = pad_ends - padded
    n_blocks = -(-(t_all * TOP_K) // MOE_BLOCK) + N_EXPERTS
    block_first = jnp.arange(n_blocks, dtype=I32) * MOE_BLOCK
    block_e = jnp.minimum(jnp.sum(pad_ends[None, :] <= block_first[:, None], axis=1),
                          N_EXPERTS - 1).astype(I32)
    n_used = (pad_ends[-1] // MOE_BLOCK).astype(I32).reshape(1)
    n_rows = n_blocks * MOE_BLOCK
    assert t_all % SC_CHUNK == 0
    dest3 = dest[:, :t_all].reshape(TOP_K, t_all // SC_CHUNK, SC_CHUNK).transpose(1, 0, 2)
    pad_rows = (pad_starts + counts)[:, None] + jnp.arange(MOE_BLOCK, dtype=I32)[None, :]
    pad3 = jnp.where(pad_rows < pad_ends[:, None], pad_rows, n_rows - 1).astype(I32)
    pad3 = pad3.reshape(-1, 1, SC_CHUNK)
    assert t_p % SC_CHUNK == 0
    t_out = t_all + SC_CHUNK
    gates_all = jnp.concatenate([gates_p, gates_s], axis=1)
    out_row = (jnp.arange(TOP_K, dtype=I32)[:, None] * t_out + jnp.arange(t_all, dtype=I32)[None, :])
    meta = jnp.zeros((TOP_K, t_all, META_WIDTH), I32)
    meta = meta.at[:, :, 0].set(out_row).at[:, :, 1].set(lax.bitcast_convert_type(gates_all, I32))
    pad_meta = jnp.zeros((SC_CHUNK, META_WIDTH), I32).at[:, 0].set(TOP_K * t_out - 1)
    xs_rows, inv = _dispatch(dest3, pad3, jnp.zeros((SC_CHUNK, d // 2), U32), pad_meta,
                             h3_p.reshape(t_p, d // 2), h3_s, meta, n_rows)
    codes = inv[:, 0].reshape(n_blocks, 1, MOE_BLOCK)
    y4 = _expert_ffn(block_e, n_used, codes, xs_rows, inv, w_gate_up[l], b_gate_up[l], w_down[l],
                     b_down[l], TOP_K * t_out).reshape(TOP_K, t_out, d)

    gf = row(g_final)
    assert t_p % COMBINE_TILE == 0 and t_p % nb == 0
    y_p = _combine(x2_p.reshape(t_p, d), y4, gf, COMBINE_TILE, 0)
    y_s = _combine(x2_s, y4, gf, nb, t_p // nb)

    x_heads = cache_mem_k.shape[3]
    return (y_p.reshape(b, s, d),
            y_s.reshape(nb, 1, d),
            mk2.reshape(1, b, n_mem, x_heads, d // x_heads),
            mv2.reshape(1, b, n_mem, x_heads, d // x_heads),
            conv_p.reshape(1, b, CONV_W - 1, b_width),
            jnp.stack([st[:, 1], z_s], axis=1).reshape(1, nb, CONV_W - 1, b_width),
            v_s.reshape(1, nb, 1, a_heads, a_hd))
```

```python
import functools

import jax
import jax.numpy as jnp
from jax import lax
from jax.experimental import pallas as pl
from jax.experimental.pallas import tpu as pltpu
from jax.experimental.pallas import tpu_sc as plsc

F32 = jnp.float32
BF16 = jnp.bfloat16
I32 = jnp.int32
U32 = jnp.uint32

A_HEADS = 8
CHUNK = 128
CONV_W = 3
X_HEADS = 4
N_EXPERTS = 32
TOP_K = 4
SWIGLU_LIMIT = 7.0
SWIGLU_ALPHA = 1.702
EPS = 1e-5

MIX_TILE = 512
ATT_TILE = 512
KV_TILE = 512
ROUTE_TILE = 512
MOE_BLOCK = 256
COMBINE_TILE = 256
SAMPLE_ATT_GROUP = 4
SC_CHUNK = 64
VMEM_LIMIT = 56 * 1024 * 1024


def _cparams(sem=None):
    return pltpu.CompilerParams(dimension_semantics=sem, vmem_limit_bytes=VMEM_LIMIT)


def _rms(x, g):
    r = lax.rsqrt(jnp.mean(x * x, axis=-1, keepdims=True) + EPS)
    return (x * r) * g


def _gelu(x):
    return 0.5 * x * (1.0 + lax.erf(x * 0.7071067811865476))


def _dot(a, b):
    return jnp.dot(a, b, preferred_element_type=F32)


def _dot_nt(a, b):
    return lax.dot_general(a, b, (((1,), (1,)), ((), ())), preferred_element_type=F32)


def _memkv_kernel(m_ref, g_ref, wk_ref, wv_ref, k_ref, v_ref):
    h = _rms(m_ref[...], g_ref[...]).astype(BF16)
    k_ref[...] = _dot(h, wk_ref[...])
    v_ref[...] = _dot(h, wv_ref[...])


def _mem_kv(mem2d, g_mem, wk, wv):
    n, d = mem2d.shape
    row = pl.BlockSpec((KV_TILE, d), lambda i: (i, 0))
    full = lambda shape: pl.BlockSpec(shape, lambda i: (0,) * len(shape))
    return pl.pallas_call(
        _memkv_kernel,
        grid=(n // KV_TILE,),
        in_specs=[row, full((1, d)), full((d, d)), full((d, d))],
        out_specs=[row, row],
        out_shape=[jax.ShapeDtypeStruct((n, d), F32)] * 2,
        compiler_params=_cparams(("arbitrary",)),
        name="mem_kv",
    )(mem2d, g_mem, wk, wv)


def _head_rms(v, gv, a_width):
    hd = a_width // A_HEADS
    r_i = lax.broadcasted_iota(I32, (a_width, a_width), 0) // hd
    c_i = lax.broadcasted_iota(I32, (a_width, a_width), 1) // hd
    ones_bd = jnp.where(r_i == c_i, 1.0, 0.0).astype(BF16)
    sq = v * v
    sq_hi = sq.astype(BF16)
    sq_lo = (sq - sq_hi.astype(F32)).astype(BF16)
    gs = _dot(sq_hi, ones_bd) + _dot(sq_lo, ones_bd)
    return (v * lax.rsqrt(gs * (1.0 / hd) + EPS)) * gv


def _mixer_prompt_kernel(x_ref, gmix_ref, win_ref, gv_ref, ws_ref, bs_ref, wc_ref, wout_ref,
                         x1_ref, cs_ref, zbuf, *, tile, a_width, b_width):
    j = pl.program_id(1)
    nj = pl.num_programs(1)
    x = x_ref[...]
    h = _rms(x, gmix_ref[...]).astype(BF16)
    proj = _dot(h, win_ref[...])
    u = _gelu(proj[:, :a_width])
    v = _head_rms(_gelu(proj[:, a_width:2 * a_width]), gv_ref[...], a_width)
    o = 2 * a_width
    hb = proj[:, o:o + b_width]
    bg = proj[:, o + b_width:o + 2 * b_width]
    cg = proj[:, o + 2 * b_width:o + 3 * b_width]

    lane = lax.broadcasted_iota(I32, (CHUNK, 2 * (a_width // A_HEADS)), 1)
    first = lane < (a_width // A_HEADS)
    t_i = lax.broadcasted_iota(I32, (CHUNK, 2 * CHUNK), 0)
    s_i = lax.broadcasted_iota(I32, (CHUNK, 2 * CHUNK), 1) % CHUNK
    causal = s_i <= t_i
    pair_cols = []
    for p in range(A_HEADS // 2):
        w_pair = jnp.where(causal, ws_ref[p], 0.0).astype(BF16)
        vp = v[:, p * CHUNK:(p + 1) * CHUNK]
        rows = []
        for c in range(tile // CHUNK):
            vc = vp[c * CHUNK:(c + 1) * CHUNK]
            rhs = jnp.concatenate([jnp.where(first, vc, 0.0), jnp.where(first, 0.0, vc)],
                                  axis=0).astype(BF16)
            rows.append(_dot(w_pair, rhs))
        pair_cols.append(jnp.concatenate(rows, axis=0))
    gate = jnp.concatenate(pair_cols, axis=1)
    bias = jnp.concatenate([bs_ref[...]] * (tile // CHUNK), axis=0)
    a_out = u * (gate + bias)

    @pl.when(j == 0)
    def _():
        zbuf[0:8, :] = jnp.zeros((8, b_width), F32)

    z = cg * hb
    zbuf[8:tile + 8, :] = z
    z1 = zbuf[7:tile + 7, :]
    z2 = zbuf[6:tile + 6, :]
    wc = wc_ref[...]
    conv = z2 * wc[0:1] + z1 * wc[1:2] + z * wc[2:3]
    b_out = bg * conv
    tail = zbuf[tile:tile + 8, :]
    zbuf[0:8, :] = tail

    @pl.when(j == nj - 1)
    def _():
        cs_ref[...] = tail[8 - (CONV_W - 1):, :]

    y = _dot(a_out.astype(BF16), wout_ref[0:a_width, :]) + _dot(b_out.astype(BF16), wout_ref[a_width:, :])
    x1_ref[...] = x + y


def _mixer_prompt(x, g_mix, w_in, g_v, ws_pairs, bs_full, w_conv, w_out):
    b, s, d = x.shape
    a_width = g_v.shape[1]
    b_width = w_conv.shape[1]
    in_width = w_in.shape[1]
    tile = MIX_TILE
    full = lambda shape: pl.BlockSpec(shape, lambda i, j: (0,) * len(shape))
    kern = functools.partial(_mixer_prompt_kernel, tile=tile, a_width=a_width, b_width=b_width)
    return pl.pallas_call(
        kern,
        grid=(b, s // tile),
        in_specs=[
            pl.BlockSpec((None, tile, d), lambda i, j: (i, j, 0)),
            full((1, d)), full((d, in_width)), full((1, a_width)),
            full(ws_pairs.shape), full(bs_full.shape), full(w_conv.shape), full(w_out.shape),
        ],
        out_specs=[
            pl.BlockSpec((None, tile, d), lambda i, j: (i, j, 0)),
            pl.BlockSpec((None, CONV_W - 1, b_width), lambda i, j: (i, 0, 0)),
        ],
        out_shape=[jax.ShapeDtypeStruct((b, s, d), F32),
                   jax.ShapeDtypeStruct((b, CONV_W - 1, b_width), F32)],
        scratch_shapes=[pltpu.VMEM((tile + 8, b_width), F32)],
        compiler_params=_cparams(("arbitrary", "arbitrary")),
        name="mixer_prompt",
    )(x, g_mix, w_in, g_v, ws_pairs, bs_full, w_conv, w_out)


def _mixer_sample_kernel(x_ref, s0_ref, s1_ref, gmix_ref, win_ref, gv_ref, w00_ref, b0_ref, wc_ref,
                         wout_ref, gx_ref, wq_ref, x1_ref, v_ref, z_ref, q_ref, *, a_width, b_width):
    x = x_ref[...]
    h = _rms(x, gmix_ref[...]).astype(BF16)
    proj = _dot(h, win_ref[...])
    u = _gelu(proj[:, :a_width])
    v = _head_rms(_gelu(proj[:, a_width:2 * a_width]), gv_ref[...], a_width)
    v_ref[...] = v
    o = 2 * a_width
    hb = proj[:, o:o + b_width]
    bg = proj[:, o + b_width:o + 2 * b_width]
    cg = proj[:, o + 2 * b_width:o + 3 * b_width]
    a_out = u * (v * w00_ref[...] + b0_ref[...])
    z = cg * hb
    z_ref[...] = z
    wc = wc_ref[...]
    conv = s0_ref[...] * wc[0:1] + s1_ref[...] * wc[1:2] + z * wc[2:3]
    b_out = bg * conv
    y = _dot(a_out.astype(BF16), wout_ref[0:a_width, :]) + _dot(b_out.astype(BF16), wout_ref[a_width:, :])
    x1 = x + y
    x1_ref[...] = x1
    q_ref[...] = _dot(_rms(x1, gx_ref[...]).astype(BF16), wq_ref[...])


def _mixer_sample(x, s0, s1, g_mix, w_in, g_v, w00, b0, w_conv, w_out, g_x, w_q):
    n, d = x.shape
    a_width = g_v.shape[1]
    b_width = w_conv.shape[1]
    kern = functools.partial(_mixer_sample_kernel, a_width=a_width, b_width=b_width)
    return pl.pallas_call(
        kern,
        out_shape=[jax.ShapeDtypeStruct((n, d), F32), jax.ShapeDtypeStruct((n, a_width), F32),
                   jax.ShapeDtypeStruct((n, b_width), F32), jax.ShapeDtypeStruct((n, d), F32)],
        compiler_params=_cparams(),
        name="mixer_sample",
    )(x, s0, s1, g_mix, w_in, g_v, w00, b0, w_conv, w_out, g_x, w_q)


def _router_tail(x2, gmoe, wrt, br, h3_ref, tope_ref, gates_ref):
    d = x2.shape[1]
    h3 = _rms(x2, gmoe).astype(BF16)
    h3f = h3.astype(F32)
    lo = lax.shift_right_logical(pltpu.bitcast(h3f[:, :d // 2], U32), jnp.uint32(16))
    hi = lax.bitwise_and(pltpu.bitcast(h3f[:, d // 2:], U32), jnp.uint32(0xFFFF0000))
    h3_ref[...] = lax.bitwise_or(lo, hi)

    logits = _dot_nt(wrt, h3) + br
    n_e, n_t = logits.shape
    e_idx = lax.broadcasted_iota(I32, (n_e, n_t), 0).astype(F32)
    tops, idxs = [], []
    for _ in range(TOP_K):
        m = jnp.max(logits, axis=0, keepdims=True)
        idx = jnp.min(jnp.where(logits == m, e_idx, float(n_e)), axis=0, keepdims=True)
        tops.append(m)
        idxs.append(idx)
        logits = jnp.where(e_idx == idx, -jnp.inf, logits)
    top = jnp.concatenate(tops, axis=0)
    ex = jnp.exp(top - top[0:1])
    gates_ref[...] = ex / jnp.sum(ex, axis=0, keepdims=True)
    tope_ref[...] = jnp.concatenate(idxs, axis=0).astype(I32)


def _attn_prompt_kernel(x1_ref, k_ref, v_ref, gx_ref, wq_ref, wxo_ref, gmoe_ref, wrt_ref, br_ref,
                        x2_ref, h3_ref, tope_ref, gates_ref):
    x = x1_ref[...]
    d = x.shape[1]
    hd = d // X_HEADS
    q = _dot(_rms(x, gx_ref[...]).astype(BF16), wq_ref[...]).astype(BF16)
    kb = k_ref[...].astype(BF16)
    vb = v_ref[...].astype(BF16)
    outs = []
    for hh in range(X_HEADS):
        sl = slice(hh * hd, (hh + 1) * hd)
        s = _dot_nt(q[:, sl], kb[:, sl]) * (hd ** -0.5)
        e = jnp.exp(s - jnp.max(s, axis=-1, keepdims=True))
        p = e / jnp.sum(e, axis=-1, keepdims=True)
        outs.append(_dot(p.astype(BF16), vb[:, sl]))
    o = jnp.concatenate(outs, axis=1).astype(BF16)
    x2 = x + _dot(o, wxo_ref[...])
    x2_ref[...] = x2
    _router_tail(x2, gmoe_ref[...], wrt_ref[...], br_ref[...], h3_ref, tope_ref, gates_ref)


def _attn_prompt(x1, mk, mv, g_x, w_q, w_xo, g_moe, wrt, br):
    b, s, d = x1.shape
    n_mem = mk.shape[1]
    tile = ATT_TILE
    nq = s // tile
    full = lambda shape: pl.BlockSpec(shape, lambda i, j: (0,) * len(shape))
    tok = pl.BlockSpec((None, tile, d), lambda i, j: (i, j, 0))
    mem = pl.BlockSpec((None, n_mem, d), lambda i, j: (i, 0, 0))
    lanes = pl.BlockSpec((TOP_K, tile), lambda i, j: (0, i * nq + j))
    return pl.pallas_call(
        _attn_prompt_kernel,
        grid=(b, nq),
        in_specs=[tok, mem, mem, full((1, d)), full((d, d)), full((d, d)), full((1, d)),
                  full(wrt.shape), full(br.shape)],
        out_specs=[tok, pl.BlockSpec((None, tile, d // 2), lambda i, j: (i, j, 0)), lanes, lanes],
        out_shape=[jax.ShapeDtypeStruct((b, s, d), F32), jax.ShapeDtypeStruct((b, s, d // 2), U32),
                   jax.ShapeDtypeStruct((TOP_K, b * s), I32), jax.ShapeDtypeStruct((TOP_K, b * s), F32)],
        compiler_params=_cparams(("arbitrary", "arbitrary")),
        name="attn_prompt",
    )(x1, mk, mv, g_x, w_q, w_xo, g_moe, wrt, br)


def _attn_sample_kernel(q_ref, k_ref, v_ref, o_ref, *, group):
    hd = q_ref.shape[2]
    for g in range(group):
        q = q_ref[g]
        s = jnp.sum(k_ref[g] * q[None], axis=-1, keepdims=True) * (hd ** -0.5)
        e = jnp.exp(s - jnp.max(s, axis=0, keepdims=True))
        p = e / jnp.sum(e, axis=0, keepdims=True)
        o_ref[g] = jnp.sum(p * v_ref[g], axis=0)


def _attn_sample(q, ck, cv):
    n, heads, hd = q.shape
    n_mem = ck.shape[1]
    group = SAMPLE_ATT_GROUP
    qo = pl.BlockSpec((group, heads, hd), lambda i: (i, 0, 0))
    kv = pl.BlockSpec((group, n_mem, heads, hd), lambda i: (i, 0, 0, 0))
    return pl.pallas_call(
        functools.partial(_attn_sample_kernel, group=group),
        grid=(n // group,),
        in_specs=[qo, kv, kv],
        out_specs=qo,
        out_shape=jax.ShapeDtypeStruct((n, heads, hd), F32),
        compiler_params=_cparams(("arbitrary",)),
        name="attn_sample",
    )(q, ck, cv)


def _tail_sample_kernel(x1_ref, o_ref, wxo_ref, gmoe_ref, wrt_ref, br_ref,
                        x2_ref, h3_ref, tope_ref, gates_ref):
    x2 = x1_ref[...] + _dot(o_ref[...].astype(BF16), wxo_ref[...])
    x2_ref[...] = x2
    _router_tail(x2, gmoe_ref[...], wrt_ref[...], br_ref[...], h3_ref, tope_ref, gates_ref)


def _tail_sample(x1, o, w_xo, g_moe, wrt, br):
    n, d = x1.shape
    return pl.pallas_call(
        _tail_sample_kernel,
        out_shape=[jax.ShapeDtypeStruct((n, d), F32), jax.ShapeDtypeStruct((n, d // 2), U32),
                   jax.ShapeDtypeStruct((TOP_K, n), I32), jax.ShapeDtypeStruct((TOP_K, n), F32)],
        compiler_params=_cparams(),
        name="tail_sample",
    )(x1, o, w_xo, g_moe, wrt, br)


def _route_kernel(tope_ref, dest_ref, cnt_ref, *, n_blocks, tile, moe_block):
    e_idx = lax.broadcasted_iota(I32, (N_EXPERTS, tile), 0)
    earlier = jnp.where(lax.broadcasted_iota(I32, (tile, tile), 0)
                        < lax.broadcasted_iota(I32, (tile, tile), 1), 1.0, 0.0).astype(BF16)

    def onehot(k, off):
        return e_idx == tope_ref[pl.ds(k, 1), pl.ds(off, tile)]

    def count_body(j, cnt):
        off = pl.multiple_of(j * tile, tile)
        for k in range(TOP_K):
            cnt = cnt + jnp.sum(jnp.where(onehot(k, off), 1.0, 0.0), axis=1, keepdims=True)
        return cnt

    counts = lax.fori_loop(0, n_blocks, count_body, jnp.zeros((N_EXPERTS, 1), F32))
    cnt_ref[...] = jnp.broadcast_to(counts, cnt_ref.shape).astype(I32)

    n_blk = jnp.floor((counts + (moe_block - 1)) * (1.0 / moe_block))
    n_hi = jnp.floor(n_blk * (1.0 / 16.0))
    n_lo = n_blk - 16.0 * n_hi
    below = jnp.where(lax.broadcasted_iota(I32, (N_EXPERTS, N_EXPERTS), 1)
                      < lax.broadcasted_iota(I32, (N_EXPERTS, N_EXPERTS), 0), 1.0, 0.0).astype(BF16)
    wide = lambda c: jnp.broadcast_to(c, (N_EXPERTS, 128)).astype(BF16)
    start = (16.0 * _dot(below, wide(n_hi)) + _dot(below, wide(n_lo)))[:, 0:1] * float(moe_block)

    def dest_body(j, run):
        off = pl.multiple_of(j * tile, tile)
        for k in range(TOP_K):
            oh = onehot(k, off)
            ohf = jnp.where(oh, 1.0, 0.0)
            before = _dot(ohf.astype(BF16), earlier)
            dest = jnp.sum(jnp.where(oh, before + run, 0.0), axis=0, keepdims=True)
            dest_ref[pl.ds(k, 1), pl.ds(off, tile)] = dest.astype(I32)
            run = run + jnp.sum(ohf, axis=1, keepdims=True)
        return run

    lax.fori_loop(0, n_blocks, dest_body, start)


def _route(tope_pad):
    t_pad = tope_pad.shape[1]
    kern = functools.partial(_route_kernel, n_blocks=t_pad // ROUTE_TILE, tile=ROUTE_TILE,
                             moe_block=MOE_BLOCK)
    return pl.pallas_call(
        kern,
        out_shape=[jax.ShapeDtypeStruct((TOP_K, t_pad), I32),
                   jax.ShapeDtypeStruct((N_EXPERTS, 128), I32)],
        compiler_params=_cparams(),
        name="route",
    )(tope_pad)


def _dispatch(dest3, pad3, zero_rows, h_prompt, h_sample, n_rows):
    w = h_prompt.shape[1]
    n_prompt_chunks = h_prompt.shape[0] // SC_CHUNK
    n_chunks = dest3.shape[0]
    n_pad_chunks = pad3.shape[0]
    mesh = plsc.VectorSubcoreMesh(core_axis_name="c", subcore_axis_name="s")
    n_workers = mesh.num_cores * mesh.num_subcores

    n_steps = -(-n_chunks // n_workers)

    def body(dest_hbm, pad_hbm, zero_hbm, hp_hbm, hs_hbm, xs_hbm, idx0, idx1, rows0, rows1, sem):
        wid = lax.axis_index("s") * mesh.num_cores + lax.axis_index("c")
        idx_v, rows_v = (idx0, idx1), (rows0, rows1)

        pltpu.sync_copy(zero_hbm, rows0)
        for j in range(-(-n_pad_chunks // n_workers)):
            c = wid + j * n_workers

            @pl.when(c < n_pad_chunks)
            def _():
                pltpu.sync_copy(pad_hbm.at[c], idx0.at[pl.ds(0, 1)])
                pltpu.sync_copy(rows0, xs_hbm.at[idx0.at[0]])

        def chunk(j):
            return wid + j * n_workers

        def loads(j, slot):
            c = chunk(j)
            prompt = (hp_hbm, c)
            sample = (hs_hbm, c - n_prompt_chunks)
            return c, [
                (pltpu.make_async_copy(dest_hbm.at[c], idx_v[slot], sem.at[slot]), None),
                (pltpu.make_async_copy(prompt[0].at[pl.ds(prompt[1] * SC_CHUNK, SC_CHUNK)], rows_v[slot],
                                       sem.at[slot]), c < n_prompt_chunks),
                (pltpu.make_async_copy(sample[0].at[pl.ds(sample[1] * SC_CHUNK, SC_CHUNK)], rows_v[slot],
                                       sem.at[slot]), c >= n_prompt_chunks),
            ]

        def start_loads(j, slot):
            c, copies = loads(j, slot)
            for cp, cond in copies:
                pl.when((c < n_chunks) if cond is None else ((c < n_chunks) & cond))(cp.start)

        def wait_loads(j, slot):
            c, copies = loads(j, slot)
            for cp, cond in copies:
                pl.when((c < n_chunks) if cond is None else ((c < n_chunks) & cond))(cp.wait)

        def scatters(slot):
            return [pltpu.make_async_copy(rows_v[slot], xs_hbm.at[idx_v[slot].at[k]], sem.at[2 + slot])
                    for k in range(TOP_K)]

        start_loads(0, 0)
        for j in range(n_steps):
            slot = j % 2
            wait_loads(j, slot)
            if j >= 1:
                @pl.when(chunk(j - 1) < n_chunks)
                def _():
                    for cp in scatters(1 - slot):
                        cp.wait()
            if j + 1 < n_steps:
                start_loads(j + 1, 1 - slot)

            @pl.when(chunk(j) < n_chunks)
            def _():
                for cp in scatters(slot):
                    cp.start()

        @pl.when(chunk(n_steps - 1) < n_chunks)
        def _():
            for cp in scatters((n_steps - 1) % 2):
                cp.wait()

    return pl.kernel(
        body,
        out_type=jax.ShapeDtypeStruct((n_rows, w), U32),
        mesh=mesh,
        scratch_types=[pltpu.VMEM((TOP_K, SC_CHUNK), I32), pltpu.VMEM((TOP_K, SC_CHUNK), I32),
                       pltpu.VMEM((SC_CHUNK, w), U32), pltpu.VMEM((SC_CHUNK, w), U32),
                       pltpu.SemaphoreType.DMA((4,))],
        name="dispatch_sc",
    )(dest3, pad3, zero_rows, h_prompt, h_sample)


def _ffn_kernel(be_ref, nu_ref, xs_ref, w1_ref, b1_ref, w2_ref, b2_ref, y_ref, w1b, w2b):
    i = pl.program_id(0)

    @pl.when(i >= nu_ref[0])
    def _():
        y_ref[...] = jnp.zeros(y_ref.shape, y_ref.dtype)

    @pl.when(i < nu_ref[0])
    def _():
        e = be_ref[i]
        prev = be_ref[jnp.maximum(i - 1, 0)]

        @pl.when((i == 0) | (e != prev))
        def _():
            w1b[...] = w1_ref[...].astype(BF16)
            w2b[...] = w2_ref[...].astype(BF16)

        w = xs_ref[...]
        half = w.shape[1]
        d_e = w2b.shape[0]
        lo = pltpu.bitcast(lax.shift_left(w, jnp.uint32(16)), F32).astype(BF16)
        hi = pltpu.bitcast(lax.bitwise_and(w, jnp.uint32(0xFFFF0000)), F32).astype(BF16)
        gu = _dot(lo, w1b[0:half, :]) + _dot(hi, w1b[half:, :]) + b1_ref[...]
        gate = jnp.minimum(gu[:, :d_e], SWIGLU_LIMIT)
        up = jnp.clip(gu[:, d_e:], -SWIGLU_LIMIT, SWIGLU_LIMIT)
        glu = gate * jax.nn.sigmoid(gate * SWIGLU_ALPHA)
        act = ((up + 1.0) * glu).astype(BF16)
        y_ref[...] = _dot(act, w2b[...]) + b2_ref[...]


def _expert_ffn(block_e, n_used, xs, w1, b1, w2, b2):
    n_rows, half = xs.shape
    n_e, d, d2 = w1.shape
    d_e = w2.shape[1]
    blk = MOE_BLOCK
    last = lambda i, be, nu: jnp.minimum(i, nu[0] - 1)
    grid_spec = pltpu.PrefetchScalarGridSpec(
        num_scalar_prefetch=2,
        grid=(n_rows // blk,),
        in_specs=[
            pl.BlockSpec((blk, half), lambda i, be, nu: (last(i, be, nu), 0)),
            pl.BlockSpec((None, d, d2), lambda i, be, nu: (be[last(i, be, nu)], 0, 0)),
            pl.BlockSpec((None, 1, d2), lambda i, be, nu: (be[last(i, be, nu)], 0, 0)),
            pl.BlockSpec((None, d_e, d), lambda i, be, nu: (be[last(i, be, nu)], 0, 0)),
            pl.BlockSpec((None, 1, d), lambda i, be, nu: (be[last(i, be, nu)], 0, 0)),
        ],
        out_specs=pl.BlockSpec((blk, d), lambda i, be, nu: (i, 0)),
        scratch_shapes=[pltpu.VMEM((d, d2), BF16), pltpu.VMEM((d_e, d), BF16)],
    )
    return pl.pallas_call(
        _ffn_kernel,
        grid_spec=grid_spec,
        out_shape=jax.ShapeDtypeStruct((n_rows, d), F32),
        compiler_params=_cparams(("arbitrary",)),
        name="expert_ffn",
    )(block_e, n_used, xs, w1, b1.reshape(n_e, 1, d2), w2, b2.reshape(n_e, 1, d))


def _combine_kernel(dest_ref, x2_ref, gates_ref, gf_ref, y_hbm, out_ref, buf, sem, *, tile):
    def row_copy(d, k, t):
        return pltpu.make_async_copy(y_hbm.at[pl.ds(d, 1)], buf.at[k, pl.ds(t, 1)], sem.at[0])

    for t in range(tile):
        for k in range(TOP_K):
            row_copy(dest_ref[k, t], k, t).start(priority=(t * TOP_K + k) % 2)
    for t in range(tile):
        for k in range(TOP_K):
            row_copy(0, k, t).wait()

    g = jnp.concatenate([gates_ref[...], jnp.zeros((128 - TOP_K, tile), F32)], axis=0)
    gt = g.T
    moe = gt[:, 0:1] * buf[0]
    for k in range(1, TOP_K):
        moe = moe + gt[:, k:k + 1] * buf[k]
    out_ref[...] = _rms(x2_ref[...] + moe, gf_ref[...])


def _combine(dest, x2, gates, g_final, y, tile):
    n, d = x2.shape
    grid_spec = pltpu.PrefetchScalarGridSpec(
        num_scalar_prefetch=0,
        grid=(n // tile,),
        in_specs=[
            pl.BlockSpec((TOP_K, tile), lambda i: (0, i), memory_space=pltpu.SMEM),
            pl.BlockSpec((tile, d), lambda i: (i, 0)),
            pl.BlockSpec((TOP_K, tile), lambda i: (0, i)),
            pl.BlockSpec((1, d), lambda i: (0, 0)),
            pl.BlockSpec(memory_space=pl.ANY),
        ],
        out_specs=pl.BlockSpec((tile, d), lambda i: (i, 0)),
        scratch_shapes=[pltpu.VMEM((TOP_K, tile, d), F32), pltpu.SemaphoreType.DMA((1,))],
    )
    return pl.pallas_call(
        functools.partial(_combine_kernel, tile=tile),
        grid_spec=grid_spec,
        out_shape=jax.ShapeDtypeStruct((n, d), F32),
        compiler_params=_cparams(("arbitrary",)),
        name="combine",
    )(dest, x2, gates, g_final, y)


def kernel(x_prompt, x_sample, mem_prompt, cache_mem_k, cache_mem_v, state_conv, g_mix, w_in, g_v,
           w_spatial, b_spatial, w_conv, w_out, g_xattn, g_mem, w_q, w_k, w_v, w_xo, g_moe, w_router,
           b_router, w_gate_up, b_gate_up, w_down, b_down, g_final):
    depth = g_mix.shape[0]
    assert depth == 1, "one layer supported"
    b, s, d = x_prompt.shape
    nb, ns, _ = x_sample.shape
    assert ns == 1
    n_mem = mem_prompt.shape[1]
    a_heads, a_hd = g_v.shape[1], g_v.shape[2]
    a_width = a_heads * a_hd
    b_width = w_conv.shape[2]
    assert a_heads == A_HEADS and 2 * a_hd == CHUNK and w_spatial.shape[2] == CHUNK
    assert s % MIX_TILE == 0 and s % ATT_TILE == 0 and (b * n_mem) % KV_TILE == 0
    assert nb % SAMPLE_ATT_GROUP == 0
    l = 0
    row = lambda a: a.reshape(1, -1)

    w_in_b = w_in[l].astype(BF16)
    w_out_b = w_out[l].astype(BF16)
    w_q_b = w_q[l].astype(BF16)
    w_k_b = w_k[l].astype(BF16)
    w_v_b = w_v[l].astype(BF16)
    w_xo_b = w_xo[l].astype(BF16)
    wrt_b = w_router[l].T.astype(BF16)
    br_col = b_router[l].reshape(N_EXPERTS, 1).astype(F32)
    gv_row = row(g_v[l])
    ws = w_spatial[l]
    ws_pairs = jnp.concatenate([ws[0::2], ws[1::2]], axis=2)
    bs_full = jnp.repeat(b_spatial[l].T, a_hd, axis=1)
    w00 = row(jnp.repeat(ws[:, 0, 0], a_hd))
    b0 = row(jnp.repeat(b_spatial[l][:, 0], a_hd))

    mk2, mv2 = _mem_kv(mem_prompt.reshape(b * n_mem, d), row(g_mem[l]), w_k_b, w_v_b)
    x1_p, conv_p = _mixer_prompt(x_prompt, row(g_mix[l]), w_in_b, gv_row, ws_pairs, bs_full,
                                 w_conv[l], w_out_b)
    x2_p, h3_p, tope_p, gates_p = _attn_prompt(
        x1_p, mk2.reshape(b, n_mem, d), mv2.reshape(b, n_mem, d), row(g_xattn[l]), w_q_b, w_xo_b,
        row(g_moe[l]), wrt_b, br_col)

    xs2 = x_sample.reshape(nb, d)
    st = state_conv[l]
    x1_s, v_s, z_s, q_s = _mixer_sample(xs2, st[:, 0], st[:, 1], row(g_mix[l]), w_in_b, gv_row, w00, b0,
                                        w_conv[l], w_out_b, row(g_xattn[l]), w_q_b)
    x_heads, x_hd = cache_mem_k.shape[3], cache_mem_k.shape[4]
    assert depth == 1 and x_heads == X_HEADS
    o_s = _attn_sample(q_s.reshape(nb, x_heads, x_hd), cache_mem_k.reshape(nb, n_mem, x_heads, x_hd),
                       cache_mem_v.reshape(nb, n_mem, x_heads, x_hd)).reshape(nb, d)
    x2_s, h3_s, tope_s, gates_s = _tail_sample(x1_s, o_s, w_xo_b, row(g_moe[l]), wrt_b, br_col)

    t_p = b * s
    t_all = t_p + nb
    t_pad = -(-t_all // ROUTE_TILE) * ROUTE_TILE
    tope_all = jnp.concatenate([tope_p, tope_s, jnp.full((TOP_K, t_pad - t_all), -1, I32)], axis=1)
    dest, cnt = _route(tope_all)
    counts = cnt[:, 0]
    padded = (counts + MOE_BLOCK - 1) // MOE_BLOCK * MOE_BLOCK
    pad_ends = jnp.cumsum(padded)
    pad_starts = pad_ends - padded
    n_blocks = -(-(t_all * TOP_K) // MOE_BLOCK) + N_EXPERTS
    block_first = jnp.arange(n_blocks, dtype=I32) * MOE_BLOCK
    block_e = jnp.minimum(jnp.sum(pad_ends[None, :] <= block_first[:, None], axis=1),
                          N_EXPERTS - 1).astype(I32)
    n_used = (pad_ends[-1] // MOE_BLOCK).astype(I32).reshape(1)
    n_rows = n_blocks * MOE_BLOCK
    assert t_all % SC_CHUNK == 0
    dest3 = dest[:, :t_all].reshape(TOP_K, t_all // SC_CHUNK, SC_CHUNK).transpose(1, 0, 2)
    pad_rows = (pad_starts + counts)[:, None] + jnp.arange(MOE_BLOCK, dtype=I32)[None, :]
    pad3 = jnp.where(pad_rows < pad_ends[:, None], pad_rows, n_rows - 1).astype(I32)
    pad3 = pad3.reshape(-1, 1, SC_CHUNK)
    assert t_p % SC_CHUNK == 0
    xs_rows = _dispatch(dest3, pad3, jnp.zeros((SC_CHUNK, d // 2), U32), h3_p.reshape(t_p, d // 2), h3_s,
                        n_rows)
    y_rows = _expert_ffn(block_e, n_used, xs_rows, w_gate_up[l], b_gate_up[l], w_down[l], b_down[l])

    gf = row(g_final)
    y_p = _combine(dest[:, :t_p], x2_p.reshape(t_p, d), gates_p, gf, y_rows, COMBINE_TILE)
    y_s = _combine(dest[:, t_p:t_all], x2_s, gates_s, gf, y_rows, nb)

    x_heads = cache_mem_k.shape[3]
    return (y_p.reshape(b, s, d),
            y_s.reshape(nb, 1, d),
            mk2.reshape(1, b, n_mem, x_heads, d // x_heads),
            mv2.reshape(1, b, n_mem, x_heads, d // x_heads),
            conv_p.reshape(1, b, CONV_W - 1, b_width),
            jnp.stack([st[:, 1], z_s], axis=1).reshape(1, nb, CONV_W - 1, b_width),
            v_s.reshape(1, nb, 1, a_heads, a_hd))
```

```python
import functools

import jax
import jax.numpy as jnp
from jax import lax
from jax.experimental import pallas as pl
from jax.experimental.pallas import tpu as pltpu
from jax.experimental.pallas import tpu_sc as plsc

F32 = jnp.float32
BF16 = jnp.bfloat16
I32 = jnp.int32
U32 = jnp.uint32

A_HEADS = 8
CHUNK = 128
CONV_W = 3
X_HEADS = 4
N_EXPERTS = 32
TOP_K = 4
SWIGLU_LIMIT = 7.0
SWIGLU_ALPHA = 1.702
EPS = 1e-5

MIX_TILE = 512
ATT_TILE = 512
KV_TILE = 512
ROUTE_TILE = 512
MOE_BLOCK = 256
COMBINE_TILE = 256
SAMPLE_ATT_GROUP = 4
SC_CHUNK = 64
VMEM_LIMIT = 56 * 1024 * 1024


def _cparams(sem=None):
    return pltpu.CompilerParams(dimension_semantics=sem, vmem_limit_bytes=VMEM_LIMIT)


def _rms(x, g):
    r = lax.rsqrt(jnp.mean(x * x, axis=-1, keepdims=True) + EPS)
    return (x * r) * g


def _gelu(x):
    return 0.5 * x * (1.0 + lax.erf(x * 0.7071067811865476))


def _dot(a, b):
    return jnp.dot(a, b, preferred_element_type=F32)


def _dot_nt(a, b):
    return lax.dot_general(a, b, (((1,), (1,)), ((), ())), preferred_element_type=F32)


def _memkv_kernel(m_ref, g_ref, wk_ref, wv_ref, k_ref, v_ref):
    h = _rms(m_ref[...], g_ref[...]).astype(BF16)
    k_ref[...] = _dot(h, wk_ref[...])
    v_ref[...] = _dot(h, wv_ref[...])


def _mem_kv(mem2d, g_mem, wk, wv):
    n, d = mem2d.shape
    row = pl.BlockSpec((KV_TILE, d), lambda i: (i, 0))
    full = lambda shape: pl.BlockSpec(shape, lambda i: (0,) * len(shape))
    return pl.pallas_call(
        _memkv_kernel,
        grid=(n // KV_TILE,),
        in_specs=[row, full((1, d)), full((d, d)), full((d, d))],
        out_specs=[row, row],
        out_shape=[jax.ShapeDtypeStruct((n, d), F32)] * 2,
        compiler_params=_cparams(("arbitrary",)),
        name="mem_kv",
    )(mem2d, g_mem, wk, wv)


def _head_rms(v, gv, a_width):
    hd = a_width // A_HEADS
    r_i = lax.broadcasted_iota(I32, (a_width, a_width), 0) // hd
    c_i = lax.broadcasted_iota(I32, (a_width, a_width), 1) // hd
    ones_bd = jnp.where(r_i == c_i, 1.0, 0.0).astype(BF16)
    sq = v * v
    sq_hi = sq.astype(BF16)
    sq_lo = (sq - sq_hi.astype(F32)).astype(BF16)
    gs = _dot(sq_hi, ones_bd) + _dot(sq_lo, ones_bd)
    return (v * lax.rsqrt(gs * (1.0 / hd) + EPS)) * gv


def _mixer_prompt_kernel(x_ref, gmix_ref, win_ref, gv_ref, ws_ref, bs_ref, wc_ref, wout_ref,
                         x1_ref, cs_ref, zbuf, *, tile, a_width, b_width):
    j = pl.program_id(1)
    nj = pl.num_programs(1)
    x = x_ref[...]
    h = _rms(x, gmix_ref[...]).astype(BF16)
    proj = _dot(h, win_ref[...])
    u = _gelu(proj[:, :a_width])
    v = _head_rms(_gelu(proj[:, a_width:2 * a_width]), gv_ref[...], a_width)
    o = 2 * a_width
    hb = proj[:, o:o + b_width]
    bg = proj[:, o + b_width:o + 2 * b_width]
    cg = proj[:, o + 2 * b_width:o + 3 * b_width]

    lane = lax.broadcasted_iota(I32, (CHUNK, 2 * (a_width // A_HEADS)), 1)
    first = lane < (a_width // A_HEADS)
    t_i = lax.broadcasted_iota(I32, (CHUNK, 2 * CHUNK), 0)
    s_i = lax.broadcasted_iota(I32, (CHUNK, 2 * CHUNK), 1) % CHUNK
    causal = s_i <= t_i
    pair_cols = []
    for p in range(A_HEADS // 2):
        w_pair = jnp.where(causal, ws_ref[p], 0.0).astype(BF16)
        vp = v[:, p * CHUNK:(p + 1) * CHUNK]
        rows = []
        for c in range(tile // CHUNK):
            vc = vp[c * CHUNK:(c + 1) * CHUNK]
            rhs = jnp.concatenate([jnp.where(first, vc, 0.0), jnp.where(first, 0.0, vc)],
                                  axis=0).astype(BF16)
            rows.append(_dot(w_pair, rhs))
        pair_cols.append(jnp.concatenate(rows, axis=0))
    gate = jnp.concatenate(pair_cols, axis=1)
    bias = jnp.concatenate([bs_ref[...]] * (tile // CHUNK), axis=0)
    a_out = u * (gate + bias)

    @pl.when(j == 0)
    def _():
        zbuf[0:8, :] = jnp.zeros((8, b_width), F32)

    z = cg * hb
    zbuf[8:tile + 8, :] = z
    z1 = zbuf[7:tile + 7, :]
    z2 = zbuf[6:tile + 6, :]
    wc = wc_ref[...]
    conv = z2 * wc[0:1] + z1 * wc[1:2] + z * wc[2:3]
    b_out = bg * conv
    tail = zbuf[tile:tile + 8, :]
    zbuf[0:8, :] = tail

    @pl.when(j == nj - 1)
    def _():
        cs_ref[...] = tail[8 - (CONV_W - 1):, :]

    y = _dot(a_out.astype(BF16), wout_ref[0:a_width, :]) + _dot(b_out.astype(BF16), wout_ref[a_width:, :])
    x1_ref[...] = x + y


def _mixer_prompt(x, g_mix, w_in, g_v, ws_pairs, bs_full, w_conv, w_out):
    b, s, d = x.shape
    a_width = g_v.shape[1]
    b_width = w_conv.shape[1]
    in_width = w_in.shape[1]
    tile = MIX_TILE
    full = lambda shape: pl.BlockSpec(shape, lambda i, j: (0,) * len(shape))
    kern = functools.partial(_mixer_prompt_kernel, tile=tile, a_width=a_width, b_width=b_width)
    return pl.pallas_call(
        kern,
        grid=(b, s // tile),
        in_specs=[
            pl.BlockSpec((None, tile, d), lambda i, j: (i, j, 0)),
            full((1, d)), full((d, in_width)), full((1, a_width)),
            full(ws_pairs.shape), full(bs_full.shape), full(w_conv.shape), full(w_out.shape),
        ],
        out_specs=[
            pl.BlockSpec((None, tile, d), lambda i, j: (i, j, 0)),
            pl.BlockSpec((None, CONV_W - 1, b_width), lambda i, j: (i, 0, 0)),
        ],
        out_shape=[jax.ShapeDtypeStruct((b, s, d), F32),
                   jax.ShapeDtypeStruct((b, CONV_W - 1, b_width), F32)],
        scratch_shapes=[pltpu.VMEM((tile + 8, b_width), F32)],
        compiler_params=_cparams(("arbitrary", "arbitrary")),
        name="mixer_prompt",
    )(x, g_mix, w_in, g_v, ws_pairs, bs_full, w_conv, w_out)


def _mixer_sample_kernel(x_ref, s0_ref, s1_ref, gmix_ref, win_ref, gv_ref, w00_ref, b0_ref, wc_ref,
                         wout_ref, gx_ref, wq_ref, x1_ref, v_ref, z_ref, q_ref, *, a_width, b_width):
    x = x_ref[...]
    h = _rms(x, gmix_ref[...]).astype(BF16)
    proj = _dot(h, win_ref[...])
    u = _gelu(proj[:, :a_width])
    v = _head_rms(_gelu(proj[:, a_width:2 * a_width]), gv_ref[...], a_width)
    v_ref[...] = v
    o = 2 * a_width
    hb = proj[:, o:o + b_width]
    bg = proj[:, o + b_width:o + 2 * b_width]
    cg = proj[:, o + 2 * b_width:o + 3 * b_width]
    a_out = u * (v * w00_ref[...] + b0_ref[...])
    z = cg * hb
    z_ref[...] = z
    wc = wc_ref[...]
    conv = s0_ref[...] * wc[0:1] + s1_ref[...] * wc[1:2] + z * wc[2:3]
    b_out = bg * conv
    y = _dot(a_out.astype(BF16), wout_ref[0:a_width, :]) + _dot(b_out.astype(BF16), wout_ref[a_width:, :])
    x1 = x + y
    x1_ref[...] = x1
    q_ref[...] = _dot(_rms(x1, gx_ref[...]).astype(BF16), wq_ref[...])


def _mixer_sample(x, s0, s1, g_mix, w_in, g_v, w00, b0, w_conv, w_out, g_x, w_q):
    n, d = x.shape
    a_width = g_v.shape[1]
    b_width = w_conv.shape[1]
    kern = functools.partial(_mixer_sample_kernel, a_width=a_width, b_width=b_width)
    return pl.pallas_call(
        kern,
        out_shape=[jax.ShapeDtypeStruct((n, d), F32), jax.ShapeDtypeStruct((n, a_width), F32),
                   jax.ShapeDtypeStruct((n, b_width), F32), jax.ShapeDtypeStruct((n, d), F32)],
        compiler_params=_cparams(),
        name="mixer_sample",
    )(x, s0, s1, g_mix, w_in, g_v, w00, b0, w_conv, w_out, g_x, w_q)


def _router_tail(x2, gmoe, wrt, br, h3_ref, tope_ref, gates_ref):
    d = x2.shape[1]
    h3 = _rms(x2, gmoe).astype(BF16)
    h3f = h3.astype(F32)
    lo = lax.shift_right_logical(pltpu.bitcast(h3f[:, :d // 2], U32), jnp.uint32(16))
    hi = lax.bitwise_and(pltpu.bitcast(h3f[:, d // 2:], U32), jnp.uint32(0xFFFF0000))
    h3_ref[...] = lax.bitwise_or(lo, hi)

    logits = _dot_nt(wrt, h3) + br
    n_e, n_t = logits.shape
    e_idx = lax.broadcasted_iota(I32, (n_e, n_t), 0).astype(F32)
    tops, idxs = [], []
    for _ in range(TOP_K):
        m = jnp.max(logits, axis=0, keepdims=True)
        idx = jnp.min(jnp.where(logits == m, e_idx, float(n_e)), axis=0, keepdims=True)
        tops.append(m)
        idxs.append(idx)
        logits = jnp.where(e_idx == idx, -jnp.inf, logits)
    top = jnp.concatenate(tops, axis=0)
    ex = jnp.exp(top - top[0:1])
    gates_ref[...] = ex / jnp.sum(ex, axis=0, keepdims=True)
    tope_ref[...] = jnp.concatenate(idxs, axis=0).astype(I32)


def _attn_prompt_kernel(x1_ref, k_ref, v_ref, gx_ref, wq_ref, wxo_ref, gmoe_ref, wrt_ref, br_ref,
                        x2_ref, h3_ref, tope_ref, gates_ref):
    x = x1_ref[...]
    d = x.shape[1]
    hd = d // X_HEADS
    q = _dot(_rms(x, gx_ref[...]).astype(BF16), wq_ref[...]).astype(BF16)
    kb = k_ref[...].astype(BF16)
    vb = v_ref[...].astype(BF16)
    outs = []
    for hh in range(X_HEADS):
        sl = slice(hh * hd, (hh + 1) * hd)
        s = _dot_nt(q[:, sl], kb[:, sl]) * (hd ** -0.5)
        e = jnp.exp(s - jnp.max(s, axis=-1, keepdims=True))
        p = e / jnp.sum(e, axis=-1, keepdims=True)
        outs.append(_dot(p.astype(BF16), vb[:, sl]))
    o = jnp.concatenate(outs, axis=1).astype(BF16)
    x2 = x + _dot(o, wxo_ref[...])
    x2_ref[...] = x2
    _router_tail(x2, gmoe_ref[...], wrt_ref[...], br_ref[...], h3_ref, tope_ref, gates_ref)


def _attn_prompt(x1, mk, mv, g_x, w_q, w_xo, g_moe, wrt, br):
    b, s, d = x1.shape
    n_mem = mk.shape[1]
    tile = ATT_TILE
    nq = s // tile
    full = lambda shape: pl.BlockSpec(shape, lambda i, j: (0,) * len(shape))
    tok = pl.BlockSpec((None, tile, d), lambda i, j: (i, j, 0))
    mem = pl.BlockSpec((None, n_mem, d), lambda i, j: (i, 0, 0))
    lanes = pl.BlockSpec((TOP_K, tile), lambda i, j: (0, i * nq + j))
    return pl.pallas_call(
        _attn_prompt_kernel,
        grid=(b, nq),
        in_specs=[tok, mem, mem, full((1, d)), full((d, d)), full((d, d)), full((1, d)),
                  full(wrt.shape), full(br.shape)],
        out_specs=[tok, pl.BlockSpec((None, tile, d // 2), lambda i, j: (i, j, 0)), lanes, lanes],
        out_shape=[jax.ShapeDtypeStruct((b, s, d), F32), jax.ShapeDtypeStruct((b, s, d // 2), U32),
                   jax.ShapeDtypeStruct((TOP_K, b * s), I32), jax.ShapeDtypeStruct((TOP_K, b * s), F32)],
        compiler_params=_cparams(("arbitrary", "arbitrary")),
        name="attn_prompt",
    )(x1, mk, mv, g_x, w_q, w_xo, g_moe, wrt, br)


def _attn_sample_kernel(q_ref, k_ref, v_ref, o_ref, *, group):
    hd = q_ref.shape[2]
    for g in range(group):
        q = q_ref[g]
        s = jnp.sum(k_ref[g] * q[None], axis=-1, keepdims=True) * (hd ** -0.5)
        e = jnp.exp(s - jnp.max(s, axis=0, keepdims=True))
        p = e / jnp.sum(e, axis=0, keepdims=True)
        o_ref[g] = jnp.sum(p * v_ref[g], axis=0)


def _attn_sample(q, ck, cv):
    n, heads, hd = q.shape
    n_mem = ck.shape[1]
    group = SAMPLE_ATT_GROUP
    qo = pl.BlockSpec((group, heads, hd), lambda i: (i, 0, 0))
    kv = pl.BlockSpec((group, n_mem, heads, hd), lambda i: (i, 0, 0, 0))
    return pl.pallas_call(
        functools.partial(_attn_sample_kernel, group=group),
        grid=(n // group,),
        in_specs=[qo, kv, kv],
        out_specs=qo,
        out_shape=jax.ShapeDtypeStruct((n, heads, hd), F32),
        compiler_params=_cparams(("arbitrary",)),
        name="attn_sample",
    )(q, ck, cv)


def _tail_sample_kernel(x1_ref, o_ref, wxo_ref, gmoe_ref, wrt_ref, br_ref,
                        x2_ref, h3_ref, tope_ref, gates_ref):
    x2 = x1_ref[...] + _dot(o_ref[...].astype(BF16), wxo_ref[...])
    x2_ref[...] = x2
    _router_tail(x2, gmoe_ref[...], wrt_ref[...], br_ref[...], h3_ref, tope_ref, gates_ref)


def _tail_sample(x1, o, w_xo, g_moe, wrt, br):
    n, d = x1.shape
    return pl.pallas_call(
        _tail_sample_kernel,
        out_shape=[jax.ShapeDtypeStruct((n, d), F32), jax.ShapeDtypeStruct((n, d // 2), U32),
                   jax.ShapeDtypeStruct((TOP_K, n), I32), jax.ShapeDtypeStruct((TOP_K, n), F32)],
        compiler_params=_cparams(),
        name="tail_sample",
    )(x1, o, w_xo, g_moe, wrt, br)


def _route_kernel(tope_ref, dest_ref, cnt_ref, *, n_blocks, tile, moe_block):
    e_idx = lax.broadcasted_iota(I32, (N_EXPERTS, tile), 0)
    earlier = jnp.where(lax.broadcasted_iota(I32, (tile, tile), 0)
                        < lax.broadcasted_iota(I32, (tile, tile), 1), 1.0, 0.0).astype(BF16)

    def onehot(k, off):
        return e_idx == tope_ref[pl.ds(k, 1), pl.ds(off, tile)]

    def count_body(j, cnt):
        off = pl.multiple_of(j * tile, tile)
        for k in range(TOP_K):
            cnt = cnt + jnp.sum(jnp.where(onehot(k, off), 1.0, 0.0), axis=1, keepdims=True)
        return cnt

    counts = lax.fori_loop(0, n_blocks, count_body, jnp.zeros((N_EXPERTS, 1), F32))
    cnt_ref[...] = jnp.broadcast_to(counts, cnt_ref.shape).astype(I32)

    n_blk = jnp.floor((counts + (moe_block - 1)) * (1.0 / moe_block))
    n_hi = jnp.floor(n_blk * (1.0 / 16.0))
    n_lo = n_blk - 16.0 * n_hi
    below = jnp.where(lax.broadcasted_iota(I32, (N_EXPERTS, N_EXPERTS), 1)
                      < lax.broadcasted_iota(I32, (N_EXPERTS, N_EXPERTS), 0), 1.0, 0.0).astype(BF16)
    wide = lambda c: jnp.broadcast_to(c, (N_EXPERTS, 128)).astype(BF16)
    start = (16.0 * _dot(below, wide(n_hi)) + _dot(below, wide(n_lo)))[:, 0:1] * float(moe_block)

    def dest_body(j, run):
        off = pl.multiple_of(j * tile, tile)
        for k in range(TOP_K):
            oh = onehot(k, off)
            ohf = jnp.where(oh, 1.0, 0.0)
            before = _dot(ohf.astype(BF16), earlier)
            dest = jnp.sum(jnp.where(oh, before + run, 0.0), axis=0, keepdims=True)
            dest_ref[pl.ds(k, 1), pl.ds(off, tile)] = dest.astype(I32)
            run = run + jnp.sum(ohf, axis=1, keepdims=True)
        return run

    lax.fori_loop(0, n_blocks, dest_body, start)


def _route(tope_pad):
    t_pad = tope_pad.shape[1]
    kern = functools.partial(_route_kernel, n_blocks=t_pad // ROUTE_TILE, tile=ROUTE_TILE,
                             moe_block=MOE_BLOCK)
    return pl.pallas_call(
        kern,
        out_shape=[jax.ShapeDtypeStruct((TOP_K, t_pad), I32),
                   jax.ShapeDtypeStruct((N_EXPERTS, 128), I32)],
        compiler_params=_cparams(),
        name="route",
    )(tope_pad)


def _dispatch(dest3, pad3, zero_rows, h_prompt, h_sample, n_rows):
    w = h_prompt.shape[1]
    n_prompt_chunks = h_prompt.shape[0] // SC_CHUNK
    n_chunks = dest3.shape[0]
    n_pad_chunks = pad3.shape[0]
    mesh = plsc.VectorSubcoreMesh(core_axis_name="c", subcore_axis_name="s")
    n_workers = mesh.num_cores * mesh.num_subcores

    n_steps = -(-n_chunks // n_workers)

    def body(dest_hbm, pad_hbm, zero_hbm, hp_hbm, hs_hbm, xs_hbm, idx0, idx1, rows0, rows1, sem):
        wid = lax.axis_index("s") * mesh.num_cores + lax.axis_index("c")
        idx_v, rows_v = (idx0, idx1), (rows0, rows1)

        pltpu.sync_copy(zero_hbm, rows0)
        for j in range(-(-n_pad_chunks // n_workers)):
            c = wid + j * n_workers

            @pl.when(c < n_pad_chunks)
            def _():
                pltpu.sync_copy(pad_hbm.at[c], idx0.at[pl.ds(0, 1)])
                pltpu.sync_copy(rows0, xs_hbm.at[idx0.at[0]])

        def chunk(j):
            return wid + j * n_workers

        def loads(j, slot):
            c = chunk(j)
            prompt = (hp_hbm, c)
            sample = (hs_hbm, c - n_prompt_chunks)
            return c, [
                (pltpu.make_async_copy(dest_hbm.at[c], idx_v[slot], sem.at[slot]), None),
                (pltpu.make_async_copy(prompt[0].at[pl.ds(prompt[1] * SC_CHUNK, SC_CHUNK)], rows_v[slot],
                                       sem.at[slot]), c < n_prompt_chunks),
                (pltpu.make_async_copy(sample[0].at[pl.ds(sample[1] * SC_CHUNK, SC_CHUNK)], rows_v[slot],
                                       sem.at[slot]), c >= n_prompt_chunks),
            ]

        def start_loads(j, slot):
            c, copies = loads(j, slot)
            for cp, cond in copies:
                pl.when((c < n_chunks) if cond is None else ((c < n_chunks) & cond))(cp.start)

        def wait_loads(j, slot):
            c, copies = loads(j, slot)
            for cp, cond in copies:
                pl.when((c < n_chunks) if cond is None else ((c < n_chunks) & cond))(cp.wait)

        def scatters(slot):
            return [pltpu.make_async_copy(rows_v[slot], xs_hbm.at[idx_v[slot].at[k]], sem.at[2 + slot])
                    for k in range(TOP_K)]

        start_loads(0, 0)
        for j in range(n_steps):
            slot = j % 2
            wait_loads(j, slot)
            if j >= 1:
                @pl.when(chunk(j - 1) < n_chunks)
                def _():
                    for cp in scatters(1 - slot):
                        cp.wait()
            if j + 1 < n_steps:
                start_loads(j + 1, 1 - slot)

            @pl.when(chunk(j) < n_chunks)
            def _():
                for cp in scatters(slot):
                    cp.start()

        @pl.when(chunk(n_steps - 1) < n_chunks)
        def _():
            for cp in scatters((n_steps - 1) % 2):
                cp.wait()

    return pl.kernel(
        body,
        out_type=jax.ShapeDtypeStruct((n_rows, w), U32),
        mesh=mesh,
        scratch_types=[pltpu.VMEM((TOP_K, SC_CHUNK), I32), pltpu.VMEM((TOP_K, SC_CHUNK), I32),
                       pltpu.VMEM((SC_CHUNK, w), U32), pltpu.VMEM((SC_CHUNK, w), U32),
                       pltpu.SemaphoreType.DMA((4,))],
        name="dispatch_sc",
    )(dest3, pad3, zero_rows, h_prompt, h_sample)


def _ffn_kernel(start_ref, nblk_ref, b1_ref, b2_ref, xs_hbm, w1_hbm, w2_hbm, y_hbm,
                w1f, w2f, w1b, w2b, xbuf, ybuf, sem_w, sem_x, sem_y, *, blk):
    e = pl.program_id(0)
    n_e = pl.num_programs(0)
    nb = nblk_ref[e]
    base = start_ref[e]
    wslot = e % 2

    def w_copies(expert, slot):
        return (pltpu.make_async_copy(w1_hbm.at[expert], w1f.at[slot], sem_w.at[slot]),
                pltpu.make_async_copy(w2_hbm.at[expert], w2f.at[slot], sem_w.at[slot]))

    def x_copy(row0, slot):
        return pltpu.make_async_copy(xs_hbm.at[pl.ds(pl.multiple_of(row0, blk), blk)], xbuf.at[slot],
                                     sem_x.at[slot])

    def y_copy(row0, slot):
        return pltpu.make_async_copy(ybuf.at[slot], y_hbm.at[pl.ds(pl.multiple_of(row0, blk), blk)],
                                     sem_y.at[slot])

    @pl.when(e == 0)
    def _():
        for cp in w_copies(0, 0):
            cp.start()

        @pl.when(nb > 0)
        def _():
            x_copy(base, 0).start()

    for cp in w_copies(e, wslot):
        cp.wait()

    @pl.when(e + 1 < n_e)
    def _():
        for cp in w_copies(e + 1, 1 - wslot):
            cp.start()

    w1b[...] = w1f[wslot].astype(BF16)
    w2b[...] = w2f[wslot].astype(BF16)
    b1 = b1_ref[e]
    b2 = b2_ref[e]

    def block(j, slot):
        x_copy(base, slot).wait()

        @pl.when(j + 1 < nb)
        def _():
            x_copy(base + (j + 1) * blk, 1 - slot).start()

        @pl.when(j >= 2)
        def _():
            y_copy(base, slot).wait()

        w = xbuf[slot]
        half = w.shape[1]
        d_e = w2b.shape[0]
        lo = pltpu.bitcast(lax.shift_left(w, jnp.uint32(16)), F32).astype(BF16)
        hi = pltpu.bitcast(lax.bitwise_and(w, jnp.uint32(0xFFFF0000)), F32).astype(BF16)
        gu = _dot(lo, w1b[0:half, :]) + _dot(hi, w1b[half:, :]) + b1
        gate = jnp.minimum(gu[:, :d_e], SWIGLU_LIMIT)
        up = jnp.clip(gu[:, d_e:], -SWIGLU_LIMIT, SWIGLU_LIMIT)
        glu = gate * jax.nn.sigmoid(gate * SWIGLU_ALPHA)
        act = ((up + 1.0) * glu).astype(BF16)
        ybuf[slot] = _dot(act, w2b[...]) + b2
        y_copy(base + j * blk, slot).start()

    def pair(j2, carry):
        j = 2 * j2
        block(j, 0)

        @pl.when(j + 1 < nb)
        def _():
            block(j + 1, 1)

        return carry

    lax.fori_loop(0, (nb + 1) // 2, pair, 0)

    @pl.when(nb >= 2)
    def _():
        y_copy(base, 0).wait()
        y_copy(base, 1).wait()

    @pl.when(nb == 1)
    def _():
        y_copy(base, 0).wait()

    nxt = jnp.minimum(e + 1, n_e - 1)

    @pl.when((e + 1 < n_e) & (nblk_ref[nxt] > 0))
    def _():
        x_copy(start_ref[nxt], 0).start()


def _expert_ffn(grp_start, grp_blocks, xs, w1, b1, w2, b2):
    n_rows, half = xs.shape
    n_e, d, d2 = w1.shape
    d_e = w2.shape[1]
    blk = MOE_BLOCK
    whole = lambda shape: pl.BlockSpec(shape, lambda e, s, n: (0,) * len(shape))
    hbm = pl.BlockSpec(memory_space=pl.ANY)
    grid_spec = pltpu.PrefetchScalarGridSpec(
        num_scalar_prefetch=2,
        grid=(n_e,),
        in_specs=[whole((n_e, 1, d2)), whole((n_e, 1, d)), hbm, hbm, hbm],
        out_specs=hbm,
        scratch_shapes=[pltpu.VMEM((2, d, d2), F32), pltpu.VMEM((2, d_e, d), F32),
                        pltpu.VMEM((d, d2), BF16), pltpu.VMEM((d_e, d), BF16),
                        pltpu.VMEM((2, blk, half), U32), pltpu.VMEM((2, blk, d), F32),
                        pltpu.SemaphoreType.DMA((2,)), pltpu.SemaphoreType.DMA((2,)),
                        pltpu.SemaphoreType.DMA((2,))],
    )
    return pl.pallas_call(
        functools.partial(_ffn_kernel, blk=blk),
        grid_spec=grid_spec,
        out_shape=jax.ShapeDtypeStruct((n_rows, d), F32),
        compiler_params=_cparams(("arbitrary",)),
        name="expert_ffn",
    )(grp_start, grp_blocks, b1.reshape(n_e, 1, d2), b2.reshape(n_e, 1, d), xs, w1, w2)


def _combine_kernel(dest_ref, x2_ref, gates_ref, gf_ref, y_hbm, out_ref, buf, sem, *, tile):
    def row_copy(d, k, t):
        return pltpu.make_async_copy(y_hbm.at[pl.ds(d, 1)], buf.at[k, pl.ds(t, 1)], sem.at[0])

    for t in range(tile):
        for k in range(TOP_K):
            row_copy(dest_ref[k, t], k, t).start(priority=(t * TOP_K + k) % 2)
    for t in range(tile):
        for k in range(TOP_K):
            row_copy(0, k, t).wait()

    g = jnp.concatenate([gates_ref[...], jnp.zeros((128 - TOP_K, tile), F32)], axis=0)
    gt = g.T
    moe = gt[:, 0:1] * buf[0]
    for k in range(1, TOP_K):
        moe = moe + gt[:, k:k + 1] * buf[k]
    out_ref[...] = _rms(x2_ref[...] + moe, gf_ref[...])


def _combine(dest, x2, gates, g_final, y, tile):
    n, d = x2.shape
    grid_spec = pltpu.PrefetchScalarGridSpec(
        num_scalar_prefetch=0,
        grid=(n // tile,),
        in_specs=[
            pl.BlockSpec((TOP_K, tile), lambda i: (0, i), memory_space=pltpu.SMEM),
            pl.BlockSpec((tile, d), lambda i: (i, 0)),
            pl.BlockSpec((TOP_K, tile), lambda i: (0, i)),
            pl.BlockSpec((1, d), lambda i: (0, 0)),
            pl.BlockSpec(memory_space=pl.ANY),
        ],
        out_specs=pl.BlockSpec((tile, d), lambda i: (i, 0)),
        scratch_shapes=[pltpu.VMEM((TOP_K, tile, d), F32), pltpu.SemaphoreType.DMA((1,))],
    )
    return pl.pallas_call(
        functools.partial(_combine_kernel, tile=tile),
        grid_spec=grid_spec,
        out_shape=jax.ShapeDtypeStruct((n, d), F32),
        compiler_params=_cparams(("arbitrary",)),
        name="combine",
    )(dest, x2, gates, g_final, y)


def kernel(x_prompt, x_sample, mem_prompt, cache_mem_k, cache_mem_v, state_conv, g_mix, w_in, g_v,
           w_spatial, b_spatial, w_conv, w_out, g_xattn, g_mem, w_q, w_k, w_v, w_xo, g_moe, w_router,
           b_router, w_gate_up, b_gate_up, w_down, b_down, g_final):
    depth = g_mix.shape[0]
    assert depth == 1, "one layer supported"
    b, s, d = x_prompt.shape
    nb, ns, _ = x_sample.shape
    assert ns == 1
    n_mem = mem_prompt.shape[1]
    a_heads, a_hd = g_v.shape[1], g_v.shape[2]
    a_width = a_heads * a_hd
    b_width = w_conv.shape[2]
    assert a_heads == A_HEADS and 2 * a_hd == CHUNK and w_spatial.shape[2] == CHUNK
    assert s % MIX_TILE == 0 and s % ATT_TILE == 0 and (b * n_mem) % KV_TILE == 0
    assert nb % SAMPLE_ATT_GROUP == 0
    l = 0
    row = lambda a: a.reshape(1, -1)

    w_in_b = w_in[l].astype(BF16)
    w_out_b = w_out[l].astype(BF16)
    w_q_b = w_q[l].astype(BF16)
    w_k_b = w_k[l].astype(BF16)
    w_v_b = w_v[l].astype(BF16)
    w_xo_b = w_xo[l].astype(BF16)
    wrt_b = w_router[l].T.astype(BF16)
    br_col = b_router[l].reshape(N_EXPERTS, 1).astype(F32)
    gv_row = row(g_v[l])
    ws = w_spatial[l]
    ws_pairs = jnp.concatenate([ws[0::2], ws[1::2]], axis=2)
    bs_full = jnp.repeat(b_spatial[l].T, a_hd, axis=1)
    w00 = row(jnp.repeat(ws[:, 0, 0], a_hd))
    b0 = row(jnp.repeat(b_spatial[l][:, 0], a_hd))

    mk2, mv2 = _mem_kv(mem_prompt.reshape(b * n_mem, d), row(g_mem[l]), w_k_b, w_v_b)
    x1_p, conv_p = _mixer_prompt(x_prompt, row(g_mix[l]), w_in_b, gv_row, ws_pairs, bs_full,
                                 w_conv[l], w_out_b)
    x2_p, h3_p, tope_p, gates_p = _attn_prompt(
        x1_p, mk2.reshape(b, n_mem, d), mv2.reshape(b, n_mem, d), row(g_xattn[l]), w_q_b, w_xo_b,
        row(g_moe[l]), wrt_b, br_col)

    xs2 = x_sample.reshape(nb, d)
    st = state_conv[l]
    x1_s, v_s, z_s, q_s = _mixer_sample(xs2, st[:, 0], st[:, 1], row(g_mix[l]), w_in_b, gv_row, w00, b0,
                                        w_conv[l], w_out_b, row(g_xattn[l]), w_q_b)
    x_heads, x_hd = cache_mem_k.shape[3], cache_mem_k.shape[4]
    assert depth == 1 and x_heads == X_HEADS
    o_s = _attn_sample(q_s.reshape(nb, x_heads, x_hd), cache_mem_k.reshape(nb, n_mem, x_heads, x_hd),
                       cache_mem_v.reshape(nb, n_mem, x_heads, x_hd)).reshape(nb, d)
    x2_s, h3_s, tope_s, gates_s = _tail_sample(x1_s, o_s, w_xo_b, row(g_moe[l]), wrt_b, br_col)

    t_p = b * s
    t_all = t_p + nb
    t_pad = -(-t_all // ROUTE_TILE) * ROUTE_TILE
    tope_all = jnp.concatenate([tope_p, tope_s, jnp.full((TOP_K, t_pad - t_all), -1, I32)], axis=1)
    dest, cnt = _route(tope_all)
    counts = cnt[:, 0]
    padded = (counts + MOE_BLOCK - 1) // MOE_BLOCK * MOE_BLOCK
    pad_ends = jnp.cumsum(padded)
    pad_starts = pad_ends - padded
    n_blocks = -(-(t_all * TOP_K) // MOE_BLOCK) + N_EXPERTS
    n_rows = n_blocks * MOE_BLOCK
    assert t_all % SC_CHUNK == 0
    dest3 = dest[:, :t_all].reshape(TOP_K, t_all // SC_CHUNK, SC_CHUNK).transpose(1, 0, 2)
    pad_rows = (pad_starts + counts)[:, None] + jnp.arange(MOE_BLOCK, dtype=I32)[None, :]
    pad3 = jnp.where(pad_rows < pad_ends[:, None], pad_rows, n_rows - 1).astype(I32)
    pad3 = pad3.reshape(-1, 1, SC_CHUNK)
    assert t_p % SC_CHUNK == 0
    xs_rows = _dispatch(dest3, pad3, jnp.zeros((SC_CHUNK, d // 2), U32), h3_p.reshape(t_p, d // 2), h3_s,
                        n_rows)
    y_rows = _expert_ffn(pad_starts.astype(I32), (padded // MOE_BLOCK).astype(I32), xs_rows,
                         w_gate_up[l], b_gate_up[l], w_down[l], b_down[l])

    gf = row(g_final)
    y_p = _combine(dest[:, :t_p], x2_p.reshape(t_p, d), gates_p, gf, y_rows, COMBINE_TILE)
    y_s = _combine(dest[:, t_p:t_all], x2_s, gates_s, gf, y_rows, nb)

    x_heads = cache_mem_k.shape[3]
    return (y_p.reshape(b, s, d),
            y_s.reshape(nb, 1, d),
            mk2.reshape(1, b, n_mem, x_heads, d // x_heads),
            mv2.reshape(1, b, n_mem, x_heads, d // x_heads),
            conv_p.reshape(1, b, CONV_W - 1, b_width),
            jnp.stack([st[:, 1], z_s], axis=1).reshape(1, nb, CONV_W - 1, b_width),
            v_s.reshape(1, nb, 1, a_heads, a_hd))
```

```python
import functools

import jax
import jax.numpy as jnp
from jax import lax
from jax.experimental import pallas as pl
from jax.experimental.pallas import tpu as pltpu
from jax.experimental.pallas import tpu_sc as plsc

F32 = jnp.float32
BF16 = jnp.bfloat16
I32 = jnp.int32
U32 = jnp.uint32

A_HEADS = 8
CHUNK = 128
CONV_W = 3
X_HEADS = 4
N_EXPERTS = 32
TOP_K = 4
SWIGLU_LIMIT = 7.0
SWIGLU_ALPHA = 1.702
EPS = 1e-5

MIX_TILE = 512
ATT_TILE = 512
KV_TILE = 512
ROUTE_TILE = 512
MOE_BLOCK = 512
COMBINE_TILE = 256
SAMPLE_ATT_GROUP = 4
SC_CHUNK = 64
VMEM_LIMIT = 56 * 1024 * 1024


def _cparams(sem=None):
    return pltpu.CompilerParams(dimension_semantics=sem, vmem_limit_bytes=VMEM_LIMIT)


def _rms(x, g):
    r = lax.rsqrt(jnp.mean(x * x, axis=-1, keepdims=True) + EPS)
    return (x * r) * g


def _gelu(x):
    return 0.5 * x * (1.0 + lax.erf(x * 0.7071067811865476))


def _dot(a, b):
    return jnp.dot(a, b, preferred_element_type=F32)


def _dot_nt(a, b):
    return lax.dot_general(a, b, (((1,), (1,)), ((), ())), preferred_element_type=F32)


def _memkv_kernel(m_ref, g_ref, wk_ref, wv_ref, k_ref, v_ref):
    h = _rms(m_ref[...], g_ref[...]).astype(BF16)
    k_ref[...] = _dot(h, wk_ref[...])
    v_ref[...] = _dot(h, wv_ref[...])


def _mem_kv(mem2d, g_mem, wk, wv):
    n, d = mem2d.shape
    row = pl.BlockSpec((KV_TILE, d), lambda i: (i, 0))
    full = lambda shape: pl.BlockSpec(shape, lambda i: (0,) * len(shape))
    return pl.pallas_call(
        _memkv_kernel,
        grid=(n // KV_TILE,),
        in_specs=[row, full((1, d)), full((d, d)), full((d, d))],
        out_specs=[row, row],
        out_shape=[jax.ShapeDtypeStruct((n, d), F32)] * 2,
        compiler_params=_cparams(("arbitrary",)),
        name="mem_kv",
    )(mem2d, g_mem, wk, wv)


def _head_rms(v, gv, a_width):
    hd = a_width // A_HEADS
    r_i = lax.broadcasted_iota(I32, (a_width, a_width), 0) // hd
    c_i = lax.broadcasted_iota(I32, (a_width, a_width), 1) // hd
    ones_bd = jnp.where(r_i == c_i, 1.0, 0.0).astype(BF16)
    sq = v * v
    sq_hi = sq.astype(BF16)
    sq_lo = (sq - sq_hi.astype(F32)).astype(BF16)
    gs = _dot(sq_hi, ones_bd) + _dot(sq_lo, ones_bd)
    return (v * lax.rsqrt(gs * (1.0 / hd) + EPS)) * gv


def _mixer_prompt_kernel(x_ref, gmix_ref, win_ref, gv_ref, ws_ref, bs_ref, wc_ref, wout_ref,
                         x1_ref, cs_ref, zbuf, *, tile, a_width, b_width):
    j = pl.program_id(1)
    nj = pl.num_programs(1)
    x = x_ref[...]
    h = _rms(x, gmix_ref[...]).astype(BF16)
    proj = _dot(h, win_ref[...])
    u = _gelu(proj[:, :a_width])
    v = _head_rms(_gelu(proj[:, a_width:2 * a_width]), gv_ref[...], a_width)
    o = 2 * a_width
    hb = proj[:, o:o + b_width]
    bg = proj[:, o + b_width:o + 2 * b_width]
    cg = proj[:, o + 2 * b_width:o + 3 * b_width]

    lane = lax.broadcasted_iota(I32, (CHUNK, 2 * (a_width // A_HEADS)), 1)
    first = lane < (a_width // A_HEADS)
    t_i = lax.broadcasted_iota(I32, (CHUNK, 2 * CHUNK), 0)
    s_i = lax.broadcasted_iota(I32, (CHUNK, 2 * CHUNK), 1) % CHUNK
    causal = s_i <= t_i
    pair_cols = []
    for p in range(A_HEADS // 2):
        w_pair = jnp.where(causal, ws_ref[p], 0.0).astype(BF16)
        vp = v[:, p * CHUNK:(p + 1) * CHUNK]
        rows = []
        for c in range(tile // CHUNK):
            vc = vp[c * CHUNK:(c + 1) * CHUNK]
            rhs = jnp.concatenate([jnp.where(first, vc, 0.0), jnp.where(first, 0.0, vc)],
                                  axis=0).astype(BF16)
            rows.append(_dot(w_pair, rhs))
        pair_cols.append(jnp.concatenate(rows, axis=0))
    gate = jnp.concatenate(pair_cols, axis=1)
    bias = jnp.concatenate([bs_ref[...]] * (tile // CHUNK), axis=0)
    a_out = u * (gate + bias)

    @pl.when(j == 0)
    def _():
        zbuf[0:8, :] = jnp.zeros((8, b_width), F32)

    z = cg * hb
    zbuf[8:tile + 8, :] = z
    z1 = zbuf[7:tile + 7, :]
    z2 = zbuf[6:tile + 6, :]
    wc = wc_ref[...]
    conv = z2 * wc[0:1] + z1 * wc[1:2] + z * wc[2:3]
    b_out = bg * conv
    tail = zbuf[tile:tile + 8, :]
    zbuf[0:8, :] = tail

    @pl.when(j == nj - 1)
    def _():
        cs_ref[...] = tail[8 - (CONV_W - 1):, :]

    y = _dot(a_out.astype(BF16), wout_ref[0:a_width, :]) + _dot(b_out.astype(BF16), wout_ref[a_width:, :])
    x1_ref[...] = x + y


def _mixer_prompt(x, g_mix, w_in, g_v, ws_pairs, bs_full, w_conv, w_out):
    b, s, d = x.shape
    a_width = g_v.shape[1]
    b_width = w_conv.shape[1]
    in_width = w_in.shape[1]
    tile = MIX_TILE
    full = lambda shape: pl.BlockSpec(shape, lambda i, j: (0,) * len(shape))
    kern = functools.partial(_mixer_prompt_kernel, tile=tile, a_width=a_width, b_width=b_width)
    return pl.pallas_call(
        kern,
        grid=(b, s // tile),
        in_specs=[
            pl.BlockSpec((None, tile, d), lambda i, j: (i, j, 0)),
            full((1, d)), full((d, in_width)), full((1, a_width)),
            full(ws_pairs.shape), full(bs_full.shape), full(w_conv.shape), full(w_out.shape),
        ],
        out_specs=[
            pl.BlockSpec((None, tile, d), lambda i, j: (i, j, 0)),
            pl.BlockSpec((None, CONV_W - 1, b_width), lambda i, j: (i, 0, 0)),
        ],
        out_shape=[jax.ShapeDtypeStruct((b, s, d), F32),
                   jax.ShapeDtypeStruct((b, CONV_W - 1, b_width), F32)],
        scratch_shapes=[pltpu.VMEM((tile + 8, b_width), F32)],
        compiler_params=_cparams(("arbitrary", "arbitrary")),
        name="mixer_prompt",
    )(x, g_mix, w_in, g_v, ws_pairs, bs_full, w_conv, w_out)


def _mixer_sample_kernel(x_ref, s0_ref, s1_ref, gmix_ref, win_ref, gv_ref, w00_ref, b0_ref, wc_ref,
                         wout_ref, gx_ref, wq_ref, x1_ref, v_ref, z_ref, q_ref, *, a_width, b_width):
    x = x_ref[...]
    h = _rms(x, gmix_ref[...]).astype(BF16)
    proj = _dot(h, win_ref[...])
    u = _gelu(proj[:, :a_width])
    v = _head_rms(_gelu(proj[:, a_width:2 * a_width]), gv_ref[...], a_width)
    v_ref[...] = v
    o = 2 * a_width
    hb = proj[:, o:o + b_width]
    bg = proj[:, o + b_width:o + 2 * b_width]
    cg = proj[:, o + 2 * b_width:o + 3 * b_width]
    a_out = u * (v * w00_ref[...] + b0_ref[...])
    z = cg * hb
    z_ref[...] = z
    wc = wc_ref[...]
    conv = s0_ref[...] * wc[0:1] + s1_ref[...] * wc[1:2] + z * wc[2:3]
    b_out = bg * conv
    y = _dot(a_out.astype(BF16), wout_ref[0:a_width, :]) + _dot(b_out.astype(BF16), wout_ref[a_width:, :])
    x1 = x + y
    x1_ref[...] = x1
    q_ref[...] = _dot(_rms(x1, gx_ref[...]).astype(BF16), wq_ref[...])


def _mixer_sample(x, s0, s1, g_mix, w_in, g_v, w00, b0, w_conv, w_out, g_x, w_q):
    n, d = x.shape
    a_width = g_v.shape[1]
    b_width = w_conv.shape[1]
    kern = functools.partial(_mixer_sample_kernel, a_width=a_width, b_width=b_width)
    return pl.pallas_call(
        kern,
        out_shape=[jax.ShapeDtypeStruct((n, d), F32), jax.ShapeDtypeStruct((n, a_width), F32),
                   jax.ShapeDtypeStruct((n, b_width), F32), jax.ShapeDtypeStruct((n, d), F32)],
        compiler_params=_cparams(),
        name="mixer_sample",
    )(x, s0, s1, g_mix, w_in, g_v, w00, b0, w_conv, w_out, g_x, w_q)


def _router_tail(x2, gmoe, wrt, br, h3_ref, tope_ref, gates_ref):
    d = x2.shape[1]
    h3 = _rms(x2, gmoe).astype(BF16)
    h3f = h3.astype(F32)
    lo = lax.shift_right_logical(pltpu.bitcast(h3f[:, :d // 2], U32), jnp.uint32(16))
    hi = lax.bitwise_and(pltpu.bitcast(h3f[:, d // 2:], U32), jnp.uint32(0xFFFF0000))
    h3_ref[...] = lax.bitwise_or(lo, hi)

    logits = _dot_nt(wrt, h3) + br
    n_e, n_t = logits.shape
    e_idx = lax.broadcasted_iota(I32, (n_e, n_t), 0).astype(F32)
    tops, idxs = [], []
    for _ in range(TOP_K):
        m = jnp.max(logits, axis=0, keepdims=True)
        idx = jnp.min(jnp.where(logits == m, e_idx, float(n_e)), axis=0, keepdims=True)
        tops.append(m)
        idxs.append(idx)
        logits = jnp.where(e_idx == idx, -jnp.inf, logits)
    top = jnp.concatenate(tops, axis=0)
    ex = jnp.exp(top - top[0:1])
    gates_ref[...] = ex / jnp.sum(ex, axis=0, keepdims=True)
    tope_ref[...] = jnp.concatenate(idxs, axis=0).astype(I32)


def _attn_prompt_kernel(x1_ref, k_ref, v_ref, gx_ref, wq_ref, wxo_ref, gmoe_ref, wrt_ref, br_ref,
                        x2_ref, h3_ref, tope_ref, gates_ref):
    x = x1_ref[...]
    d = x.shape[1]
    hd = d // X_HEADS
    q = _dot(_rms(x, gx_ref[...]).astype(BF16), wq_ref[...]).astype(BF16)
    kb = k_ref[...].astype(BF16)
    vb = v_ref[...].astype(BF16)
    outs = []
    for hh in range(X_HEADS):
        sl = slice(hh * hd, (hh + 1) * hd)
        s = _dot_nt(q[:, sl], kb[:, sl]) * (hd ** -0.5)
        e = jnp.exp(s - jnp.max(s, axis=-1, keepdims=True))
        p = e / jnp.sum(e, axis=-1, keepdims=True)
        outs.append(_dot(p.astype(BF16), vb[:, sl]))
    o = jnp.concatenate(outs, axis=1).astype(BF16)
    x2 = x + _dot(o, wxo_ref[...])
    x2_ref[...] = x2
    _router_tail(x2, gmoe_ref[...], wrt_ref[...], br_ref[...], h3_ref, tope_ref, gates_ref)


def _attn_prompt(x1, mk, mv, g_x, w_q, w_xo, g_moe, wrt, br):
    b, s, d = x1.shape
    n_mem = mk.shape[1]
    tile = ATT_TILE
    nq = s // tile
    full = lambda shape: pl.BlockSpec(shape, lambda i, j: (0,) * len(shape))
    tok = pl.BlockSpec((None, tile, d), lambda i, j: (i, j, 0))
    mem = pl.BlockSpec((None, n_mem, d), lambda i, j: (i, 0, 0))
    lanes = pl.BlockSpec((TOP_K, tile), lambda i, j: (0, i * nq + j))
    return pl.pallas_call(
        _attn_prompt_kernel,
        grid=(b, nq),
        in_specs=[tok, mem, mem, full((1, d)), full((d, d)), full((d, d)), full((1, d)),
                  full(wrt.shape), full(br.shape)],
        out_specs=[tok, pl.BlockSpec((None, tile, d // 2), lambda i, j: (i, j, 0)), lanes, lanes],
        out_shape=[jax.ShapeDtypeStruct((b, s, d), F32), jax.ShapeDtypeStruct((b, s, d // 2), U32),
                   jax.ShapeDtypeStruct((TOP_K, b * s), I32), jax.ShapeDtypeStruct((TOP_K, b * s), F32)],
        compiler_params=_cparams(("arbitrary", "arbitrary")),
        name="attn_prompt",
    )(x1, mk, mv, g_x, w_q, w_xo, g_moe, wrt, br)


def _attn_sample_kernel(q_ref, k_ref, v_ref, o_ref, *, group):
    hd = q_ref.shape[2]
    for g in range(group):
        q = q_ref[g]
        s = jnp.sum(k_ref[g] * q[None], axis=-1, keepdims=True) * (hd ** -0.5)
        e = jnp.exp(s - jnp.max(s, axis=0, keepdims=True))
        p = e / jnp.sum(e, axis=0, keepdims=True)
        o_ref[g] = jnp.sum(p * v_ref[g], axis=0)


def _attn_sample(q, ck, cv):
    n, heads, hd = q.shape
    n_mem = ck.shape[1]
    group = SAMPLE_ATT_GROUP
    qo = pl.BlockSpec((group, heads, hd), lambda i: (i, 0, 0))
    kv = pl.BlockSpec((group, n_mem, heads, hd), lambda i: (i, 0, 0, 0))
    return pl.pallas_call(
        functools.partial(_attn_sample_kernel, group=group),
        grid=(n // group,),
        in_specs=[qo, kv, kv],
        out_specs=qo,
        out_shape=jax.ShapeDtypeStruct((n, heads, hd), F32),
        compiler_params=_cparams(("arbitrary",)),
        name="attn_sample",
    )(q, ck, cv)


def _tail_sample_kernel(x1_ref, o_ref, wxo_ref, gmoe_ref, wrt_ref, br_ref,
                        x2_ref, h3_ref, tope_ref, gates_ref):
    x2 = x1_ref[...] + _dot(o_ref[...].astype(BF16), wxo_ref[...])
    x2_ref[...] = x2
    _router_tail(x2, gmoe_ref[...], wrt_ref[...], br_ref[...], h3_ref, tope_ref, gates_ref)


def _tail_sample(x1, o, w_xo, g_moe, wrt, br):
    n, d = x1.shape
    return pl.pallas_call(
        _tail_sample_kernel,
        out_shape=[jax.ShapeDtypeStruct((n, d), F32), jax.ShapeDtypeStruct((n, d // 2), U32),
                   jax.ShapeDtypeStruct((TOP_K, n), I32), jax.ShapeDtypeStruct((TOP_K, n), F32)],
        compiler_params=_cparams(),
        name="tail_sample",
    )(x1, o, w_xo, g_moe, wrt, br)


def _route_kernel(tope_ref, dest_ref, cnt_ref, *, n_blocks, tile, moe_block):
    e_idx = lax.broadcasted_iota(I32, (N_EXPERTS, tile), 0)
    earlier = jnp.where(lax.broadcasted_iota(I32, (tile, tile), 0)
                        < lax.broadcasted_iota(I32, (tile, tile), 1), 1.0, 0.0).astype(BF16)

    def onehot(k, off):
        return e_idx == tope_ref[pl.ds(k, 1), pl.ds(off, tile)]

    def count_body(j, cnt):
        off = pl.multiple_of(j * tile, tile)
        for k in range(TOP_K):
            cnt = cnt + jnp.sum(jnp.where(onehot(k, off), 1.0, 0.0), axis=1, keepdims=True)
        return cnt

    counts = lax.fori_loop(0, n_blocks, count_body, jnp.zeros((N_EXPERTS, 1), F32))
    cnt_ref[...] = jnp.broadcast_to(counts, cnt_ref.shape).astype(I32)

    n_blk = jnp.floor((counts + (moe_block - 1)) * (1.0 / moe_block))
    n_hi = jnp.floor(n_blk * (1.0 / 16.0))
    n_lo = n_blk - 16.0 * n_hi
    below = jnp.where(lax.broadcasted_iota(I32, (N_EXPERTS, N_EXPERTS), 1)
                      < lax.broadcasted_iota(I32, (N_EXPERTS, N_EXPERTS), 0), 1.0, 0.0).astype(BF16)
    wide = lambda c: jnp.broadcast_to(c, (N_EXPERTS, 128)).astype(BF16)
    start = (16.0 * _dot(below, wide(n_hi)) + _dot(below, wide(n_lo)))[:, 0:1] * float(moe_block)

    def dest_body(j, run):
        off = pl.multiple_of(j * tile, tile)
        for k in range(TOP_K):
            oh = onehot(k, off)
            ohf = jnp.where(oh, 1.0, 0.0)
            before = _dot(ohf.astype(BF16), earlier)
            dest = jnp.sum(jnp.where(oh, before + run, 0.0), axis=0, keepdims=True)
            dest_ref[pl.ds(k, 1), pl.ds(off, tile)] = dest.astype(I32)
            run = run + jnp.sum(ohf, axis=1, keepdims=True)
        return run

    lax.fori_loop(0, n_blocks, dest_body, start)


def _route(tope_pad):
    t_pad = tope_pad.shape[1]
    kern = functools.partial(_route_kernel, n_blocks=t_pad // ROUTE_TILE, tile=ROUTE_TILE,
                             moe_block=MOE_BLOCK)
    return pl.pallas_call(
        kern,
        out_shape=[jax.ShapeDtypeStruct((TOP_K, t_pad), I32),
                   jax.ShapeDtypeStruct((N_EXPERTS, 128), I32)],
        compiler_params=_cparams(),
        name="route",
    )(tope_pad)


def _dispatch(dest3, pad3, zero_rows, h_prompt, h_sample, n_rows):
    w = h_prompt.shape[1]
    n_prompt_chunks = h_prompt.shape[0] // SC_CHUNK
    n_chunks = dest3.shape[0]
    n_pad_chunks = pad3.shape[0]
    mesh = plsc.VectorSubcoreMesh(core_axis_name="c", subcore_axis_name="s")
    n_workers = mesh.num_cores * mesh.num_subcores

    n_steps = -(-n_chunks // n_workers)

    def body(dest_hbm, pad_hbm, zero_hbm, hp_hbm, hs_hbm, xs_hbm, idx0, idx1, rows0, rows1, sem):
        wid = lax.axis_index("s") * mesh.num_cores + lax.axis_index("c")
        idx_v, rows_v = (idx0, idx1), (rows0, rows1)

        pltpu.sync_copy(zero_hbm, rows0)
        for j in range(-(-n_pad_chunks // n_workers)):
            c = wid + j * n_workers

            @pl.when(c < n_pad_chunks)
            def _():
                pltpu.sync_copy(pad_hbm.at[c], idx0.at[pl.ds(0, 1)])
                pltpu.sync_copy(rows0, xs_hbm.at[idx0.at[0]])

        def chunk(j):
            return wid + j * n_workers

        def loads(j, slot):
            c = chunk(j)
            prompt = (hp_hbm, c)
            sample = (hs_hbm, c - n_prompt_chunks)
            return c, [
                (pltpu.make_async_copy(dest_hbm.at[c], idx_v[slot], sem.at[slot]), None),
                (pltpu.make_async_copy(prompt[0].at[pl.ds(prompt[1] * SC_CHUNK, SC_CHUNK)], rows_v[slot],
                                       sem.at[slot]), c < n_prompt_chunks),
                (pltpu.make_async_copy(sample[0].at[pl.ds(sample[1] * SC_CHUNK, SC_CHUNK)], rows_v[slot],
                                       sem.at[slot]), c >= n_prompt_chunks),
            ]

        def start_loads(j, slot):
            c, copies = loads(j, slot)
            for cp, cond in copies:
                pl.when((c < n_chunks) if cond is None else ((c < n_chunks) & cond))(cp.start)

        def wait_loads(j, slot):
            c, copies = loads(j, slot)
            for cp, cond in copies:
                pl.when((c < n_chunks) if cond is None else ((c < n_chunks) & cond))(cp.wait)

        def scatters(slot):
            return [pltpu.make_async_copy(rows_v[slot], xs_hbm.at[idx_v[slot].at[k]], sem.at[2 + slot])
                    for k in range(TOP_K)]

        start_loads(0, 0)
        for j in range(n_steps):
            slot = j % 2
            wait_loads(j, slot)
            if j >= 1:
                @pl.when(chunk(j - 1) < n_chunks)
                def _():
                    for cp in scatters(1 - slot):
                        cp.wait()
            if j + 1 < n_steps:
                start_loads(j + 1, 1 - slot)

            @pl.when(chunk(j) < n_chunks)
            def _():
                for cp in scatters(slot):
                    cp.start()

        @pl.when(chunk(n_steps - 1) < n_chunks)
        def _():
            for cp in scatters((n_steps - 1) % 2):
                cp.wait()

    return pl.kernel(
        body,
        out_type=jax.ShapeDtypeStruct((n_rows, w), U32),
        mesh=mesh,
        scratch_types=[pltpu.VMEM((TOP_K, SC_CHUNK), I32), pltpu.VMEM((TOP_K, SC_CHUNK), I32),
                       pltpu.VMEM((SC_CHUNK, w), U32), pltpu.VMEM((SC_CHUNK, w), U32),
                       pltpu.SemaphoreType.DMA((4,))],
        name="dispatch_sc",
    )(dest3, pad3, zero_rows, h_prompt, h_sample)


def _ffn_kernel(start_ref, nblk_ref, b1_ref, b2_ref, xs_hbm, w1_hbm, w2_hbm, y_hbm,
                w1f, w2f, w1b, w2b, xbuf, ybuf, sem_w, sem_x, sem_y, *, blk):
    e = pl.program_id(0)
    n_e = pl.num_programs(0)
    nb = nblk_ref[e]
    base = start_ref[e]
    wslot = e % 2

    def w_copies(expert, slot):
        return (pltpu.make_async_copy(w1_hbm.at[expert], w1f.at[slot], sem_w.at[slot]),
                pltpu.make_async_copy(w2_hbm.at[expert], w2f.at[slot], sem_w.at[slot]))

    def x_copy(row0, slot):
        return pltpu.make_async_copy(xs_hbm.at[pl.ds(pl.multiple_of(row0, blk), blk)], xbuf.at[slot],
                                     sem_x.at[slot])

    def y_copy(row0, slot):
        return pltpu.make_async_copy(ybuf.at[slot], y_hbm.at[pl.ds(pl.multiple_of(row0, blk), blk)],
                                     sem_y.at[slot])

    @pl.when(e == 0)
    def _():
        for cp in w_copies(0, 0):
            cp.start()

        @pl.when(nb > 0)
        def _():
            x_copy(base, 0).start()

    for cp in w_copies(e, wslot):
        cp.wait()

    @pl.when(e + 1 < n_e)
    def _():
        for cp in w_copies(e + 1, 1 - wslot):
            cp.start()

    w1b[...] = w1f[wslot].astype(BF16)
    w2b[...] = w2f[wslot].astype(BF16)
    b1 = b1_ref[e]
    b2 = b2_ref[e]

    def block(j, slot):
        x_copy(base, slot).wait()

        @pl.when(j + 1 < nb)
        def _():
            x_copy(base + (j + 1) * blk, 1 - slot).start()

        @pl.when(j >= 2)
        def _():
            y_copy(base, slot).wait()

        w = xbuf[slot]
        half = w.shape[1]
        d_e = w2b.shape[0]
        lo = pltpu.bitcast(lax.shift_left(w, jnp.uint32(16)), F32).astype(BF16)
        hi = pltpu.bitcast(lax.bitwise_and(w, jnp.uint32(0xFFFF0000)), F32).astype(BF16)
        gu = _dot(lo, w1b[0:half, :]) + _dot(hi, w1b[half:, :]) + b1
        gate = jnp.minimum(gu[:, :d_e], SWIGLU_LIMIT)
        up = jnp.clip(gu[:, d_e:], -SWIGLU_LIMIT, SWIGLU_LIMIT)
        glu = gate * jax.nn.sigmoid(gate * SWIGLU_ALPHA)
        act = ((up + 1.0) * glu).astype(BF16)
        ybuf[slot] = _dot(act, w2b[...]) + b2
        y_copy(base + j * blk, slot).start()

    def pair(j2, carry):
        j = 2 * j2
        block(j, 0)

        @pl.when(j + 1 < nb)
        def _():
            block(j + 1, 1)

        return carry

    lax.fori_loop(0, (nb + 1) // 2, pair, 0)

    @pl.when(nb >= 2)
    def _():
        y_copy(base, 0).wait()
        y_copy(base, 1).wait()

    @pl.when(nb == 1)
    def _():
        y_copy(base, 0).wait()

    nxt = jnp.minimum(e + 1, n_e - 1)

    @pl.when((e + 1 < n_e) & (nblk_ref[nxt] > 0))
    def _():
        x_copy(start_ref[nxt], 0).start()


def _expert_ffn(grp_start, grp_blocks, xs, w1, b1, w2, b2):
    n_rows, half = xs.shape
    n_e, d, d2 = w1.shape
    d_e = w2.shape[1]
    blk = MOE_BLOCK
    whole = lambda shape: pl.BlockSpec(shape, lambda e, s, n: (0,) * len(shape))
    hbm = pl.BlockSpec(memory_space=pl.ANY)
    grid_spec = pltpu.PrefetchScalarGridSpec(
        num_scalar_prefetch=2,
        grid=(n_e,),
        in_specs=[whole((n_e, 1, d2)), whole((n_e, 1, d)), hbm, hbm, hbm],
        out_specs=hbm,
        scratch_shapes=[pltpu.VMEM((2, d, d2), F32), pltpu.VMEM((2, d_e, d), F32),
                        pltpu.VMEM((d, d2), BF16), pltpu.VMEM((d_e, d), BF16),
                        pltpu.VMEM((2, blk, half), U32), pltpu.VMEM((2, blk, d), F32),
                        pltpu.SemaphoreType.DMA((2,)), pltpu.SemaphoreType.DMA((2,)),
                        pltpu.SemaphoreType.DMA((2,))],
    )
    return pl.pallas_call(
        functools.partial(_ffn_kernel, blk=blk),
        grid_spec=grid_spec,
        out_shape=jax.ShapeDtypeStruct((n_rows, d), F32),
        compiler_params=_cparams(("arbitrary",)),
        name="expert_ffn",
    )(grp_start, grp_blocks, b1.reshape(n_e, 1, d2), b2.reshape(n_e, 1, d), xs, w1, w2)


def _combine_kernel(dest_ref, x2_ref, gates_ref, gf_ref, y_hbm, out_ref, buf, sem, *, tile):
    def row_copy(d, k, t):
        return pltpu.make_async_copy(y_hbm.at[pl.ds(d, 1)], buf.at[k, pl.ds(t, 1)], sem.at[0])

    for t in range(tile):
        for k in range(TOP_K):
            row_copy(dest_ref[k, t], k, t).start(priority=(t * TOP_K + k) % 2)
    for t in range(tile):
        for k in range(TOP_K):
            row_copy(0, k, t).wait()

    g = jnp.concatenate([gates_ref[...], jnp.zeros((128 - TOP_K, tile), F32)], axis=0)
    gt = g.T
    moe = gt[:, 0:1] * buf[0]
    for k in range(1, TOP_K):
        moe = moe + gt[:, k:k + 1] * buf[k]
    out_ref[...] = _rms(x2_ref[...] + moe, gf_ref[...])


def _combine(dest, x2, gates, g_final, y, tile):
    n, d = x2.shape
    grid_spec = pltpu.PrefetchScalarGridSpec(
        num_scalar_prefetch=0,
        grid=(n // tile,),
        in_specs=[
            pl.BlockSpec((TOP_K, tile), lambda i: (0, i), memory_space=pltpu.SMEM),
            pl.BlockSpec((tile, d), lambda i: (i, 0)),
            pl.BlockSpec((TOP_K, tile), lambda i: (0, i)),
            pl.BlockSpec((1, d), lambda i: (0, 0)),
            pl.BlockSpec(memory_space=pl.ANY),
        ],
        out_specs=pl.BlockSpec((tile, d), lambda i: (i, 0)),
        scratch_shapes=[pltpu.VMEM((TOP_K, tile, d), F32), pltpu.SemaphoreType.DMA((1,))],
    )
    return pl.pallas_call(
        functools.partial(_combine_kernel, tile=tile),
        grid_spec=grid_spec,
        out_shape=jax.ShapeDtypeStruct((n, d), F32),
        compiler_params=_cparams(("arbitrary",)),
        name="combine",
    )(dest, x2, gates, g_final, y)


def kernel(x_prompt, x_sample, mem_prompt, cache_mem_k, cache_mem_v, state_conv, g_mix, w_in, g_v,
           w_spatial, b_spatial, w_conv, w_out, g_xattn, g_mem, w_q, w_k, w_v, w_xo, g_moe, w_router,
           b_router, w_gate_up, b_gate_up, w_down, b_down, g_final):
    depth = g_mix.shape[0]
    assert depth == 1, "one layer supported"
    b, s, d = x_prompt.shape
    nb, ns, _ = x_sample.shape
    assert ns == 1
    n_mem = mem_prompt.shape[1]
    a_heads, a_hd = g_v.shape[1], g_v.shape[2]
    a_width = a_heads * a_hd
    b_width = w_conv.shape[2]
    assert a_heads == A_HEADS and 2 * a_hd == CHUNK and w_spatial.shape[2] == CHUNK
    assert s % MIX_TILE == 0 and s % ATT_TILE == 0 and (b * n_mem) % KV_TILE == 0
    assert nb % SAMPLE_ATT_GROUP == 0
    l = 0
    row = lambda a: a.reshape(1, -1)

    w_in_b = w_in[l].astype(BF16)
    w_out_b = w_out[l].astype(BF16)
    w_q_b = w_q[l].astype(BF16)
    w_k_b = w_k[l].astype(BF16)
    w_v_b = w_v[l].astype(BF16)
    w_xo_b = w_xo[l].astype(BF16)
    wrt_b = w_router[l].T.astype(BF16)
    br_col = b_router[l].reshape(N_EXPERTS, 1).astype(F32)
    gv_row = row(g_v[l])
    ws = w_spatial[l]
    ws_pairs = jnp.concatenate([ws[0::2], ws[1::2]], axis=2)
    bs_full = jnp.repeat(b_spatial[l].T, a_hd, axis=1)
    w00 = row(jnp.repeat(ws[:, 0, 0], a_hd))
    b0 = row(jnp.repeat(b_spatial[l][:, 0], a_hd))

    mk2, mv2 = _mem_kv(mem_prompt.reshape(b * n_mem, d), row(g_mem[l]), w_k_b, w_v_b)
    x1_p, conv_p = _mixer_prompt(x_prompt, row(g_mix[l]), w_in_b, gv_row, ws_pairs, bs_full,
                                 w_conv[l], w_out_b)
    x2_p, h3_p, tope_p, gates_p = _attn_prompt(
        x1_p, mk2.reshape(b, n_mem, d), mv2.reshape(b, n_mem, d), row(g_xattn[l]), w_q_b, w_xo_b,
        row(g_moe[l]), wrt_b, br_col)

    xs2 = x_sample.reshape(nb, d)
    st = state_conv[l]
    x1_s, v_s, z_s, q_s = _mixer_sample(xs2, st[:, 0], st[:, 1], row(g_mix[l]), w_in_b, gv_row, w00, b0,
                                        w_conv[l], w_out_b, row(g_xattn[l]), w_q_b)
    x_heads, x_hd = cache_mem_k.shape[3], cache_mem_k.shape[4]
    assert depth == 1 and x_heads == X_HEADS
    o_s = _attn_sample(q_s.reshape(nb, x_heads, x_hd), cache_mem_k.reshape(nb, n_mem, x_heads, x_hd),
                       cache_mem_v.reshape(nb, n_mem, x_heads, x_hd)).reshape(nb, d)
    x2_s, h3_s, tope_s, gates_s = _tail_sample(x1_s, o_s, w_xo_b, row(g_moe[l]), wrt_b, br_col)

    t_p = b * s
    t_all = t_p + nb
    t_pad = -(-t_all // ROUTE_TILE) * ROUTE_TILE
    tope_all = jnp.concatenate([tope_p, tope_s, jnp.full((TOP_K, t_pad - t_all), -1, I32)], axis=1)
    dest, cnt = _route(tope_all)
    counts = cnt[:, 0]
    padded = (counts + MOE_BLOCK - 1) // MOE_BLOCK * MOE_BLOCK
    pad_ends = jnp.cumsum(padded)
    pad_starts = pad_ends - padded
    n_blocks = -(-(t_all * TOP_K) // MOE_BLOCK) + N_EXPERTS
    n_rows = n_blocks * MOE_BLOCK
    assert t_all % SC_CHUNK == 0
    dest3 = dest[:, :t_all].reshape(TOP_K, t_all // SC_CHUNK, SC_CHUNK).transpose(1, 0, 2)
    pad_rows = (pad_starts + counts)[:, None] + jnp.arange(MOE_BLOCK, dtype=I32)[None, :]
    pad3 = jnp.where(pad_rows < pad_ends[:, None], pad_rows, n_rows - 1).astype(I32)
    pad3 = pad3.reshape(-1, 1, SC_CHUNK)
    assert t_p % SC_CHUNK == 0
    xs_rows = _dispatch(dest3, pad3, jnp.zeros((SC_CHUNK, d // 2), U32), h3_p.reshape(t_p, d // 2), h3_s,
                        n_rows)
    y_rows = _expert_ffn(pad_starts.astype(I32), (padded // MOE_BLOCK).astype(I32), xs_rows,
                         w_gate_up[l], b_gate_up[l], w_down[l], b_down[l])

    gf = row(g_final)
    y_p = _combine(dest[:, :t_p], x2_p.reshape(t_p, d), gates_p, gf, y_rows, COMBINE_TILE)
    y_s = _combine(dest[:, t_p:t_all], x2_s, gates_s, gf, y_rows, nb)

    x_heads = cache_mem_k.shape[3]
    return (y_p.reshape(b, s, d),
            y_s.reshape(nb, 1, d),
            mk2.reshape(1, b, n_mem, x_heads, d // x_heads),
            mv2.reshape(1, b, n_mem, x_heads, d // x_heads),
            conv_p.reshape(1, b, CONV_W - 1, b_width),
            jnp.stack([st[:, 1], z_s], axis=1).reshape(1, nb, CONV_W - 1, b_width),
            v_s.reshape(1, nb, 1, a_heads, a_hd))
```

```python
import functools

import jax
import jax.numpy as jnp
from jax import lax
from jax.experimental import pallas as pl
from jax.experimental.pallas import tpu as pltpu
from jax.experimental.pallas import tpu_sc as plsc

F32 = jnp.float32
BF16 = jnp.bfloat16
I32 = jnp.int32
U32 = jnp.uint32

A_HEADS = 8
CHUNK = 128
CONV_W = 3
X_HEADS = 4
N_EXPERTS = 32
TOP_K = 4
SWIGLU_LIMIT = 7.0
SWIGLU_ALPHA = 1.702
EPS = 1e-5

MIX_TILE = 512
ATT_TILE = 512
KV_TILE = 512
ROUTE_TILE = 512
MOE_BLOCK = 256
COMBINE_TILE = 256
SAMPLE_ATT_GROUP = 4
SC_CHUNK = 64
VMEM_LIMIT = 56 * 1024 * 1024


def _cparams(sem=None):
    return pltpu.CompilerParams(dimension_semantics=sem, vmem_limit_bytes=VMEM_LIMIT)


def _rms(x, g):
    r = lax.rsqrt(jnp.mean(x * x, axis=-1, keepdims=True) + EPS)
    return (x * r) * g


def _gelu(x):
    return 0.5 * x * (1.0 + lax.erf(x * 0.7071067811865476))


def _dot(a, b):
    return jnp.dot(a, b, preferred_element_type=F32)


def _dot_nt(a, b):
    return lax.dot_general(a, b, (((1,), (1,)), ((), ())), preferred_element_type=F32)


def _memkv_kernel(m_ref, g_ref, wk_ref, wv_ref, k_ref, v_ref):
    h = _rms(m_ref[...], g_ref[...]).astype(BF16)
    k_ref[...] = _dot(h, wk_ref[...])
    v_ref[...] = _dot(h, wv_ref[...])


def _mem_kv(mem2d, g_mem, wk, wv):
    n, d = mem2d.shape
    row = pl.BlockSpec((KV_TILE, d), lambda i: (i, 0))
    full = lambda shape: pl.BlockSpec(shape, lambda i: (0,) * len(shape))
    return pl.pallas_call(
        _memkv_kernel,
        grid=(n // KV_TILE,),
        in_specs=[row, full((1, d)), full((d, d)), full((d, d))],
        out_specs=[row, row],
        out_shape=[jax.ShapeDtypeStruct((n, d), F32)] * 2,
        compiler_params=_cparams(("arbitrary",)),
        name="mem_kv",
    )(mem2d, g_mem, wk, wv)


def _head_rms(v, gv, a_width):
    hd = a_width // A_HEADS
    r_i = lax.broadcasted_iota(I32, (a_width, a_width), 0) // hd
    c_i = lax.broadcasted_iota(I32, (a_width, a_width), 1) // hd
    ones_bd = jnp.where(r_i == c_i, 1.0, 0.0).astype(BF16)
    sq = v * v
    sq_hi = sq.astype(BF16)
    sq_lo = (sq - sq_hi.astype(F32)).astype(BF16)
    gs = _dot(sq_hi, ones_bd) + _dot(sq_lo, ones_bd)
    return (v * lax.rsqrt(gs * (1.0 / hd) + EPS)) * gv


def _mixer_prompt_kernel(x_ref, gmix_ref, win_ref, gv_ref, ws_ref, bs_ref, wc_ref, wout_ref,
                         x1_ref, cs_ref, zbuf, *, tile, a_width, b_width):
    j = pl.program_id(1)
    nj = pl.num_programs(1)
    x = x_ref[...]
    h = _rms(x, gmix_ref[...]).astype(BF16)
    proj = _dot(h, win_ref[...])
    u = _gelu(proj[:, :a_width])
    v = _head_rms(_gelu(proj[:, a_width:2 * a_width]), gv_ref[...], a_width)
    o = 2 * a_width
    hb = proj[:, o:o + b_width]
    bg = proj[:, o + b_width:o + 2 * b_width]
    cg = proj[:, o + 2 * b_width:o + 3 * b_width]

    lane = lax.broadcasted_iota(I32, (CHUNK, 2 * (a_width // A_HEADS)), 1)
    first = lane < (a_width // A_HEADS)
    t_i = lax.broadcasted_iota(I32, (CHUNK, 2 * CHUNK), 0)
    s_i = lax.broadcasted_iota(I32, (CHUNK, 2 * CHUNK), 1) % CHUNK
    causal = s_i <= t_i
    pair_cols = []
    for p in range(A_HEADS // 2):
        w_pair = jnp.where(causal, ws_ref[p], 0.0).astype(BF16)
        vp = v[:, p * CHUNK:(p + 1) * CHUNK]
        rows = []
        for c in range(tile // CHUNK):
            vc = vp[c * CHUNK:(c + 1) * CHUNK]
            rhs = jnp.concatenate([jnp.where(first, vc, 0.0), jnp.where(first, 0.0, vc)],
                                  axis=0).astype(BF16)
            rows.append(_dot(w_pair, rhs))
        pair_cols.append(jnp.concatenate(rows, axis=0))
    gate = jnp.concatenate(pair_cols, axis=1)
    bias = jnp.concatenate([bs_ref[...]] * (tile // CHUNK), axis=0)
    a_out = u * (gate + bias)

    @pl.when(j == 0)
    def _():
        zbuf[0:8, :] = jnp.zeros((8, b_width), F32)

    z = cg * hb
    zbuf[8:tile + 8, :] = z
    z1 = zbuf[7:tile + 7, :]
    z2 = zbuf[6:tile + 6, :]
    wc = wc_ref[...]
    conv = z2 * wc[0:1] + z1 * wc[1:2] + z * wc[2:3]
    b_out = bg * conv
    tail = zbuf[tile:tile + 8, :]
    zbuf[0:8, :] = tail

    @pl.when(j == nj - 1)
    def _():
        cs_ref[...] = tail[8 - (CONV_W - 1):, :]

    y = _dot(a_out.astype(BF16), wout_ref[0:a_width, :]) + _dot(b_out.astype(BF16), wout_ref[a_width:, :])
    x1_ref[...] = x + y


def _mixer_prompt(x, g_mix, w_in, g_v, ws_pairs, bs_full, w_conv, w_out):
    b, s, d = x.shape
    a_width = g_v.shape[1]
    b_width = w_conv.shape[1]
    in_width = w_in.shape[1]
    tile = MIX_TILE
    full = lambda shape: pl.BlockSpec(shape, lambda i, j: (0,) * len(shape))
    kern = functools.partial(_mixer_prompt_kernel, tile=tile, a_width=a_width, b_width=b_width)
    return pl.pallas_call(
        kern,
        grid=(b, s // tile),
        in_specs=[
            pl.BlockSpec((None, tile, d), lambda i, j: (i, j, 0)),
            full((1, d)), full((d, in_width)), full((1, a_width)),
            full(ws_pairs.shape), full(bs_full.shape), full(w_conv.shape), full(w_out.shape),
        ],
        out_specs=[
            pl.BlockSpec((None, tile, d), lambda i, j: (i, j, 0)),
            pl.BlockSpec((None, CONV_W - 1, b_width), lambda i, j: (i, 0, 0)),
        ],
        out_shape=[jax.ShapeDtypeStruct((b, s, d), F32),
                   jax.ShapeDtypeStruct((b, CONV_W - 1, b_width), F32)],
        scratch_shapes=[pltpu.VMEM((tile + 8, b_width), F32)],
        compiler_params=_cparams(("arbitrary", "arbitrary")),
        name="mixer_prompt",
    )(x, g_mix, w_in, g_v, ws_pairs, bs_full, w_conv, w_out)


def _mixer_sample_kernel(x_ref, s0_ref, s1_ref, gmix_ref, win_ref, gv_ref, w00_ref, b0_ref, wc_ref,
                         wout_ref, gx_ref, wq_ref, x1_ref, v_ref, z_ref, q_ref, *, a_width, b_width):
    x = x_ref[...]
    h = _rms(x, gmix_ref[...]).astype(BF16)
    proj = _dot(h, win_ref[...])
    u = _gelu(proj[:, :a_width])
    v = _head_rms(_gelu(proj[:, a_width:2 * a_width]), gv_ref[...], a_width)
    v_ref[...] = v
    o = 2 * a_width
    hb = proj[:, o:o + b_width]
    bg = proj[:, o + b_width:o + 2 * b_width]
    cg = proj[:, o + 2 * b_width:o + 3 * b_width]
    a_out = u * (v * w00_ref[...] + b0_ref[...])
    z = cg * hb
    z_ref[...] = z
    wc = wc_ref[...]
    conv = s0_ref[...] * wc[0:1] + s1_ref[...] * wc[1:2] + z * wc[2:3]
    b_out = bg * conv
    y = _dot(a_out.astype(BF16), wout_ref[0:a_width, :]) + _dot(b_out.astype(BF16), wout_ref[a_width:, :])
    x1 = x + y
    x1_ref[...] = x1
    q_ref[...] = _dot(_rms(x1, gx_ref[...]).astype(BF16), wq_ref[...])


def _mixer_sample(x, s0, s1, g_mix, w_in, g_v, w00, b0, w_conv, w_out, g_x, w_q):
    n, d = x.shape
    a_width = g_v.shape[1]
    b_width = w_conv.shape[1]
    kern = functools.partial(_mixer_sample_kernel, a_width=a_width, b_width=b_width)
    return pl.pallas_call(
        kern,
        out_shape=[jax.ShapeDtypeStruct((n, d), F32), jax.ShapeDtypeStruct((n, a_width), F32),
                   jax.ShapeDtypeStruct((n, b_width), F32), jax.ShapeDtypeStruct((n, d), F32)],
        compiler_params=_cparams(),
        name="mixer_sample",
    )(x, s0, s1, g_mix, w_in, g_v, w00, b0, w_conv, w_out, g_x, w_q)


def _router_tail(x2, gmoe, wrt, br, h3_ref, tope_ref, gates_ref):
    d = x2.shape[1]
    h3 = _rms(x2, gmoe).astype(BF16)
    h3f = h3.astype(F32)
    lo = lax.shift_right_logical(pltpu.bitcast(h3f[:, :d // 2], U32), jnp.uint32(16))
    hi = lax.bitwise_and(pltpu.bitcast(h3f[:, d // 2:], U32), jnp.uint32(0xFFFF0000))
    h3_ref[...] = lax.bitwise_or(lo, hi)

    logits = _dot_nt(wrt, h3) + br
    n_e, n_t = logits.shape
    e_idx = lax.broadcasted_iota(I32, (n_e, n_t), 0).astype(F32)
    tops, idxs = [], []
    for _ in range(TOP_K):
        m = jnp.max(logits, axis=0, keepdims=True)
        idx = jnp.min(jnp.where(logits == m, e_idx, float(n_e)), axis=0, keepdims=True)
        tops.append(m)
        idxs.append(idx)
        logits = jnp.where(e_idx == idx, -jnp.inf, logits)
    top = jnp.concatenate(tops, axis=0)
    ex = jnp.exp(top - top[0:1])
    gates_ref[...] = ex / jnp.sum(ex, axis=0, keepdims=True)
    tope_ref[...] = jnp.concatenate(idxs, axis=0).astype(I32)


def _attn_prompt_kernel(x1_ref, k_ref, v_ref, gx_ref, wq_ref, wxo_ref, gmoe_ref, wrt_ref, br_ref,
                        x2_ref, h3_ref, tope_ref, gates_ref):
    x = x1_ref[...]
    d = x.shape[1]
    hd = d // X_HEADS
    q = _dot(_rms(x, gx_ref[...]).astype(BF16), wq_ref[...]).astype(BF16)
    kb = k_ref[...].astype(BF16)
    vb = v_ref[...].astype(BF16)
    outs = []
    for hh in range(X_HEADS):
        sl = slice(hh * hd, (hh + 1) * hd)
        s = _dot_nt(q[:, sl], kb[:, sl]) * (hd ** -0.5)
        e = jnp.exp(s - jnp.max(s, axis=-1, keepdims=True))
        p = e / jnp.sum(e, axis=-1, keepdims=True)
        outs.append(_dot(p.astype(BF16), vb[:, sl]))
    o = jnp.concatenate(outs, axis=1).astype(BF16)
    x2 = x + _dot(o, wxo_ref[...])
    x2_ref[...] = x2
    _router_tail(x2, gmoe_ref[...], wrt_ref[...], br_ref[...], h3_ref, tope_ref, gates_ref)


def _attn_prompt(x1, mk, mv, g_x, w_q, w_xo, g_moe, wrt, br):
    b, s, d = x1.shape
    n_mem = mk.shape[1]
    tile = ATT_TILE
    nq = s // tile
    full = lambda shape: pl.BlockSpec(shape, lambda i, j: (0,) * len(shape))
    tok = pl.BlockSpec((None, tile, d), lambda i, j: (i, j, 0))
    mem = pl.BlockSpec((None, n_mem, d), lambda i, j: (i, 0, 0))
    lanes = pl.BlockSpec((TOP_K, tile), lambda i, j: (0, i * nq + j))
    return pl.pallas_call(
        _attn_prompt_kernel,
        grid=(b, nq),
        in_specs=[tok, mem, mem, full((1, d)), full((d, d)), full((d, d)), full((1, d)),
                  full(wrt.shape), full(br.shape)],
        out_specs=[tok, pl.BlockSpec((None, tile, d // 2), lambda i, j: (i, j, 0)), lanes, lanes],
        out_shape=[jax.ShapeDtypeStruct((b, s, d), F32), jax.ShapeDtypeStruct((b, s, d // 2), U32),
                   jax.ShapeDtypeStruct((TOP_K, b * s), I32), jax.ShapeDtypeStruct((TOP_K, b * s), F32)],
        compiler_params=_cparams(("arbitrary", "arbitrary")),
        name="attn_prompt",
    )(x1, mk, mv, g_x, w_q, w_xo, g_moe, wrt, br)


def _attn_sample_kernel(q_ref, k_ref, v_ref, o_ref, *, group):
    hd = q_ref.shape[2]
    for g in range(group):
        q = q_ref[g]
        s = jnp.sum(k_ref[g] * q[None], axis=-1, keepdims=True) * (hd ** -0.5)
        e = jnp.exp(s - jnp.max(s, axis=0, keepdims=True))
        p = e / jnp.sum(e, axis=0, keepdims=True)
        o_ref[g] = jnp.sum(p * v_ref[g], axis=0)


def _attn_sample(q, ck, cv):
    n, heads, hd = q.shape
    n_mem = ck.shape[1]
    group = SAMPLE_ATT_GROUP
    qo = pl.BlockSpec((group, heads, hd), lambda i: (i, 0, 0))
    kv = pl.BlockSpec((group, n_mem, heads, hd), lambda i: (i, 0, 0, 0))
    return pl.pallas_call(
        functools.partial(_attn_sample_kernel, group=group),
        grid=(n // group,),
        in_specs=[qo, kv, kv],
        out_specs=qo,
        out_shape=jax.ShapeDtypeStruct((n, heads, hd), F32),
        compiler_params=_cparams(("arbitrary",)),
        name="attn_sample",
    )(q, ck, cv)


def _tail_sample_kernel(x1_ref, o_ref, wxo_ref, gmoe_ref, wrt_ref, br_ref,
                        x2_ref, h3_ref, tope_ref, gates_ref):
    x2 = x1_ref[...] + _dot(o_ref[...].astype(BF16), wxo_ref[...])
    x2_ref[...] = x2
    _router_tail(x2, gmoe_ref[...], wrt_ref[...], br_ref[...], h3_ref, tope_ref, gates_ref)


def _tail_sample(x1, o, w_xo, g_moe, wrt, br):
    n, d = x1.shape
    return pl.pallas_call(
        _tail_sample_kernel,
        out_shape=[jax.ShapeDtypeStruct((n, d), F32), jax.ShapeDtypeStruct((n, d // 2), U32),
                   jax.ShapeDtypeStruct((TOP_K, n), I32), jax.ShapeDtypeStruct((TOP_K, n), F32)],
        compiler_params=_cparams(),
        name="tail_sample",
    )(x1, o, w_xo, g_moe, wrt, br)


def _route_kernel(tope_ref, dest_ref, cnt_ref, *, n_blocks, tile, moe_block):
    e_idx = lax.broadcasted_iota(I32, (N_EXPERTS, tile), 0)
    earlier = jnp.where(lax.broadcasted_iota(I32, (tile, tile), 0)
                        < lax.broadcasted_iota(I32, (tile, tile), 1), 1.0, 0.0).astype(BF16)

    def onehot(k, off):
        return e_idx == tope_ref[pl.ds(k, 1), pl.ds(off, tile)]

    def count_body(j, cnt):
        off = pl.multiple_of(j * tile, tile)
        for k in range(TOP_K):
            cnt = cnt + jnp.sum(jnp.where(onehot(k, off), 1.0, 0.0), axis=1, keepdims=True)
        return cnt

    counts = lax.fori_loop(0, n_blocks, count_body, jnp.zeros((N_EXPERTS, 1), F32))
    cnt_ref[...] = jnp.broadcast_to(counts, cnt_ref.shape).astype(I32)

    n_blk = jnp.floor((counts + (moe_block - 1)) * (1.0 / moe_block))
    n_hi = jnp.floor(n_blk * (1.0 / 16.0))
    n_lo = n_blk - 16.0 * n_hi
    below = jnp.where(lax.broadcasted_iota(I32, (N_EXPERTS, N_EXPERTS), 1)
                      < lax.broadcasted_iota(I32, (N_EXPERTS, N_EXPERTS), 0), 1.0, 0.0).astype(BF16)
    wide = lambda c: jnp.broadcast_to(c, (N_EXPERTS, 128)).astype(BF16)
    start = (16.0 * _dot(below, wide(n_hi)) + _dot(below, wide(n_lo)))[:, 0:1] * float(moe_block)

    def dest_body(j, run):
        off = pl.multiple_of(j * tile, tile)
        for k in range(TOP_K):
            oh = onehot(k, off)
            ohf = jnp.where(oh, 1.0, 0.0)
            before = _dot(ohf.astype(BF16), earlier)
            dest = jnp.sum(jnp.where(oh, before + run, 0.0), axis=0, keepdims=True)
            dest_ref[pl.ds(k, 1), pl.ds(off, tile)] = dest.astype(I32)
            run = run + jnp.sum(ohf, axis=1, keepdims=True)
        return run

    lax.fori_loop(0, n_blocks, dest_body, start)


def _route(tope_pad):
    t_pad = tope_pad.shape[1]
    kern = functools.partial(_route_kernel, n_blocks=t_pad // ROUTE_TILE, tile=ROUTE_TILE,
                             moe_block=MOE_BLOCK)
    return pl.pallas_call(
        kern,
        out_shape=[jax.ShapeDtypeStruct((TOP_K, t_pad), I32),
                   jax.ShapeDtypeStruct((N_EXPERTS, 128), I32)],
        compiler_params=_cparams(),
        name="route",
    )(tope_pad)


def _dispatch(dest3, h_prompt, h_sample, n_rows):
    w = h_prompt.shape[1]
    n_prompt_chunks = h_prompt.shape[0] // SC_CHUNK
    n_chunks = dest3.shape[0]
    mesh = plsc.VectorSubcoreMesh(core_axis_name="c", subcore_axis_name="s")
    n_workers = mesh.num_cores * mesh.num_subcores

    n_steps = -(-n_chunks // n_workers)

    def body(dest_hbm, hp_hbm, hs_hbm, xs_hbm, idx0, idx1, rows0, rows1, sem):
        wid = lax.axis_index("s") * mesh.num_cores + lax.axis_index("c")
        idx_v, rows_v = (idx0, idx1), (rows0, rows1)

        def chunk(j):
            return wid + j * n_workers

        def loads(j, slot):
            c = chunk(j)
            prompt = (hp_hbm, c)
            sample = (hs_hbm, c - n_prompt_chunks)
            return c, [
                (pltpu.make_async_copy(dest_hbm.at[c], idx_v[slot], sem.at[slot]), None),
                (pltpu.make_async_copy(prompt[0].at[pl.ds(prompt[1] * SC_CHUNK, SC_CHUNK)], rows_v[slot],
                                       sem.at[slot]), c < n_prompt_chunks),
                (pltpu.make_async_copy(sample[0].at[pl.ds(sample[1] * SC_CHUNK, SC_CHUNK)], rows_v[slot],
                                       sem.at[slot]), c >= n_prompt_chunks),
            ]

        def start_loads(j, slot):
            c, copies = loads(j, slot)
            for cp, cond in copies:
                pl.when((c < n_chunks) if cond is None else ((c < n_chunks) & cond))(cp.start)

        def wait_loads(j, slot):
            c, copies = loads(j, slot)
            for cp, cond in copies:
                pl.when((c < n_chunks) if cond is None else ((c < n_chunks) & cond))(cp.wait)

        def scatters(slot):
            return [pltpu.make_async_copy(rows_v[slot], xs_hbm.at[idx_v[slot].at[k]], sem.at[2 + slot])
                    for k in range(TOP_K)]

        start_loads(0, 0)
        for j in range(n_steps):
            slot = j % 2
            wait_loads(j, slot)
            if j >= 1:
                @pl.when(chunk(j - 1) < n_chunks)
                def _():
                    for cp in scatters(1 - slot):
                        cp.wait()
            if j + 1 < n_steps:
                start_loads(j + 1, 1 - slot)

            @pl.when(chunk(j) < n_chunks)
            def _():
                for cp in scatters(slot):
                    cp.start()

        @pl.when(chunk(n_steps - 1) < n_chunks)
        def _():
            for cp in scatters((n_steps - 1) % 2):
                cp.wait()

    return pl.kernel(
        body,
        out_type=jax.ShapeDtypeStruct((n_rows, w), U32),
        mesh=mesh,
        scratch_types=[pltpu.VMEM((TOP_K, SC_CHUNK), I32), pltpu.VMEM((TOP_K, SC_CHUNK), I32),
                       pltpu.VMEM((SC_CHUNK, w), U32), pltpu.VMEM((SC_CHUNK, w), U32),
                       pltpu.SemaphoreType.DMA((4,))],
        name="dispatch_sc",
    )(dest3, h_prompt, h_sample)


def _ffn_kernel(start_ref, nblk_ref, count_ref, b1_ref, b2_ref, xs_hbm, w1_hbm, w2_hbm, y_hbm,
                w1f, w2f, w1b, w2b, xbuf, ybuf, sem_w, sem_x, sem_y, *, blk):
    e = pl.program_id(0)
    n_e = pl.num_programs(0)
    nb = nblk_ref[e]
    base = start_ref[e]
    wslot = e % 2

    def w_copies(expert, slot):
        return (pltpu.make_async_copy(w1_hbm.at[expert], w1f.at[slot], sem_w.at[slot]),
                pltpu.make_async_copy(w2_hbm.at[expert], w2f.at[slot], sem_w.at[slot]))

    def x_copy(row0, slot):
        return pltpu.make_async_copy(xs_hbm.at[pl.ds(pl.multiple_of(row0, blk), blk)], xbuf.at[slot],
                                     sem_x.at[slot])

    def y_copy(row0, slot):
        return pltpu.make_async_copy(ybuf.at[slot], y_hbm.at[pl.ds(pl.multiple_of(row0, blk), blk)],
                                     sem_y.at[slot])

    @pl.when(e == 0)
    def _():
        for cp in w_copies(0, 0):
            cp.start()

        @pl.when(nb > 0)
        def _():
            x_copy(base, 0).start()

    for cp in w_copies(e, wslot):
        cp.wait()

    @pl.when(e + 1 < n_e)
    def _():
        for cp in w_copies(e + 1, 1 - wslot):
            cp.start()

    w1b[...] = w1f[wslot].astype(BF16)
    w2b[...] = w2f[wslot].astype(BF16)
    b1 = b1_ref[e]
    b2 = b2_ref[e]

    def block(j, slot):
        x_copy(base, slot).wait()

        @pl.when(j + 1 < nb)
        def _():
            x_copy(base + (j + 1) * blk, 1 - slot).start()

        @pl.when(j >= 2)
        def _():
            y_copy(base, slot).wait()

        w = xbuf[slot]
        row = lax.broadcasted_iota(I32, w.shape, 0)
        w = jnp.where(row < count_ref[e] - j * blk, w, jnp.uint32(0))
        half = w.shape[1]
        d_e = w2b.shape[0]
        lo = pltpu.bitcast(lax.shift_left(w, jnp.uint32(16)), F32).astype(BF16)
        hi = pltpu.bitcast(lax.bitwise_and(w, jnp.uint32(0xFFFF0000)), F32).astype(BF16)
        gu = _dot(lo, w1b[0:half, :]) + _dot(hi, w1b[half:, :]) + b1
        gate = jnp.minimum(gu[:, :d_e], SWIGLU_LIMIT)
        up = jnp.clip(gu[:, d_e:], -SWIGLU_LIMIT, SWIGLU_LIMIT)
        glu = gate * jax.nn.sigmoid(gate * SWIGLU_ALPHA)
        act = ((up + 1.0) * glu).astype(BF16)
        ybuf[slot] = _dot(act, w2b[...]) + b2
        y_copy(base + j * blk, slot).start()

    def pair(j2, carry):
        j = 2 * j2
        block(j, 0)

        @pl.when(j + 1 < nb)
        def _():
            block(j + 1, 1)

        return carry

    lax.fori_loop(0, (nb + 1) // 2, pair, 0)

    @pl.when(nb >= 2)
    def _():
        y_copy(base, 0).wait()
        y_copy(base, 1).wait()

    @pl.when(nb == 1)
    def _():
        y_copy(base, 0).wait()

    nxt = jnp.minimum(e + 1, n_e - 1)

    @pl.when((e + 1 < n_e) & (nblk_ref[nxt] > 0))
    def _():
        x_copy(start_ref[nxt], 0).start()


def _expert_ffn(grp_start, grp_blocks, grp_count, xs, w1, b1, w2, b2):
    n_rows, half = xs.shape
    n_e, d, d2 = w1.shape
    d_e = w2.shape[1]
    blk = MOE_BLOCK
    whole = lambda shape: pl.BlockSpec(shape, lambda e, s, n, c: (0,) * len(shape))
    hbm = pl.BlockSpec(memory_space=pl.ANY)
    grid_spec = pltpu.PrefetchScalarGridSpec(
        num_scalar_prefetch=3,
        grid=(n_e,),
        in_specs=[whole((n_e, 1, d2)), whole((n_e, 1, d)), hbm, hbm, hbm],
        out_specs=hbm,
        scratch_shapes=[pltpu.VMEM((2, d, d2), F32), pltpu.VMEM((2, d_e, d), F32),
                        pltpu.VMEM((d, d2), BF16), pltpu.VMEM((d_e, d), BF16),
                        pltpu.VMEM((2, blk, half), U32), pltpu.VMEM((2, blk, d), F32),
                        pltpu.SemaphoreType.DMA((2,)), pltpu.SemaphoreType.DMA((2,)),
                        pltpu.SemaphoreType.DMA((2,))],
    )
    return pl.pallas_call(
        functools.partial(_ffn_kernel, blk=blk),
        grid_spec=grid_spec,
        out_shape=jax.ShapeDtypeStruct((n_rows, d), F32),
        compiler_params=_cparams(("arbitrary",)),
        name="expert_ffn",
    )(grp_start, grp_blocks, grp_count, b1.reshape(n_e, 1, d2), b2.reshape(n_e, 1, d), xs, w1, w2)


def _combine_kernel(dest_ref, x2_ref, gates_ref, gf_ref, y_hbm, out_ref, buf, sem, *, tile):
    def row_copy(d, k, t):
        return pltpu.make_async_copy(y_hbm.at[pl.ds(d, 1)], buf.at[k, pl.ds(t, 1)], sem.at[0])

    for t in range(tile):
        for k in range(TOP_K):
            row_copy(dest_ref[k, t], k, t).start(priority=(t * TOP_K + k) % 2)
    for t in range(tile):
        for k in range(TOP_K):
            row_copy(0, k, t).wait()

    g = jnp.concatenate([gates_ref[...], jnp.zeros((128 - TOP_K, tile), F32)], axis=0)
    gt = g.T
    moe = gt[:, 0:1] * buf[0]
    for k in range(1, TOP_K):
        moe = moe + gt[:, k:k + 1] * buf[k]
    out_ref[...] = _rms(x2_ref[...] + moe, gf_ref[...])


def _combine(dest, x2, gates, g_final, y, tile):
    n, d = x2.shape
    grid_spec = pltpu.PrefetchScalarGridSpec(
        num_scalar_prefetch=0,
        grid=(n // tile,),
        in_specs=[
            pl.BlockSpec((TOP_K, tile), lambda i: (0, i), memory_space=pltpu.SMEM),
            pl.BlockSpec((tile, d), lambda i: (i, 0)),
            pl.BlockSpec((TOP_K, tile), lambda i: (0, i)),
            pl.BlockSpec((1, d), lambda i: (0, 0)),
            pl.BlockSpec(memory_space=pl.ANY),
        ],
        out_specs=pl.BlockSpec((tile, d), lambda i: (i, 0)),
        scratch_shapes=[pltpu.VMEM((TOP_K, tile, d), F32), pltpu.SemaphoreType.DMA((1,))],
    )
    return pl.pallas_call(
        functools.partial(_combine_kernel, tile=tile),
        grid_spec=grid_spec,
        out_shape=jax.ShapeDtypeStruct((n, d), F32),
        compiler_params=_cparams(("arbitrary",)),
        name="combine",
    )(dest, x2, gates, g_final, y)


def kernel(x_prompt, x_sample, mem_prompt, cache_mem_k, cache_mem_v, state_conv, g_mix, w_in, g_v,
           w_spatial, b_spatial, w_conv, w_out, g_xattn, g_mem, w_q, w_k, w_v, w_xo, g_moe, w_router,
           b_router, w_gate_up, b_gate_up, w_down, b_down, g_final):
    depth = g_mix.shape[0]
    assert depth == 1, "one layer supported"
    b, s, d = x_prompt.shape
    nb, ns, _ = x_sample.shape
    assert ns == 1
    n_mem = mem_prompt.shape[1]
    a_heads, a_hd = g_v.shape[1], g_v.shape[2]
    a_width = a_heads * a_hd
    b_width = w_conv.shape[2]
    assert a_heads == A_HEADS and 2 * a_hd == CHUNK and w_spatial.shape[2] == CHUNK
    assert s % MIX_TILE == 0 and s % ATT_TILE == 0 and (b * n_mem) % KV_TILE == 0
    assert nb % SAMPLE_ATT_GROUP == 0
    l = 0
    row = lambda a: a.reshape(1, -1)

    w_in_b = w_in[l].astype(BF16)
    w_out_b = w_out[l].astype(BF16)
    w_q_b = w_q[l].astype(BF16)
    w_k_b = w_k[l].astype(BF16)
    w_v_b = w_v[l].astype(BF16)
    w_xo_b = w_xo[l].astype(BF16)
    wrt_b = w_router[l].T.astype(BF16)
    br_col = b_router[l].reshape(N_EXPERTS, 1).astype(F32)
    gv_row = row(g_v[l])
    ws = w_spatial[l]
    ws_pairs = jnp.concatenate([ws[0::2], ws[1::2]], axis=2)
    bs_full = jnp.repeat(b_spatial[l].T, a_hd, axis=1)
    w00 = row(jnp.repeat(ws[:, 0, 0], a_hd))
    b0 = row(jnp.repeat(b_spatial[l][:, 0], a_hd))

    mk2, mv2 = _mem_kv(mem_prompt.reshape(b * n_mem, d), row(g_mem[l]), w_k_b, w_v_b)
    x1_p, conv_p = _mixer_prompt(x_prompt, row(g_mix[l]), w_in_b, gv_row, ws_pairs, bs_full,
                                 w_conv[l], w_out_b)
    x2_p, h3_p, tope_p, gates_p = _attn_prompt(
        x1_p, mk2.reshape(b, n_mem, d), mv2.reshape(b, n_mem, d), row(g_xattn[l]), w_q_b, w_xo_b,
        row(g_moe[l]), wrt_b, br_col)

    xs2 = x_sample.reshape(nb, d)
    st = state_conv[l]
    x1_s, v_s, z_s, q_s = _mixer_sample(xs2, st[:, 0], st[:, 1], row(g_mix[l]), w_in_b, gv_row, w00, b0,
                                        w_conv[l], w_out_b, row(g_xattn[l]), w_q_b)
    x_heads, x_hd = cache_mem_k.shape[3], cache_mem_k.shape[4]
    assert depth == 1 and x_heads == X_HEADS
    o_s = _attn_sample(q_s.reshape(nb, x_heads, x_hd), cache_mem_k.reshape(nb, n_mem, x_heads, x_hd),
                       cache_mem_v.reshape(nb, n_mem, x_heads, x_hd)).reshape(nb, d)
    x2_s, h3_s, tope_s, gates_s = _tail_sample(x1_s, o_s, w_xo_b, row(g_moe[l]), wrt_b, br_col)

    t_p = b * s
    t_all = t_p + nb
    t_pad = -(-t_all // ROUTE_TILE) * ROUTE_TILE
    tope_all = jnp.concatenate([tope_p, tope_s, jnp.full((TOP_K, t_pad - t_all), -1, I32)], axis=1)
    dest, cnt = _route(tope_all)
    counts = cnt[:, 0]
    padded = (counts + MOE_BLOCK - 1) // MOE_BLOCK * MOE_BLOCK
    pad_ends = jnp.cumsum(padded)
    pad_starts = pad_ends - padded
    n_blocks = -(-(t_all * TOP_K) // MOE_BLOCK) + N_EXPERTS
    n_rows = n_blocks * MOE_BLOCK
    assert t_all % SC_CHUNK == 0 and t_p % SC_CHUNK == 0
    dest3 = dest[:, :t_all].reshape(TOP_K, t_all // SC_CHUNK, SC_CHUNK).transpose(1, 0, 2)
    xs_rows = _dispatch(dest3, h3_p.reshape(t_p, d // 2), h3_s, n_rows)
    y_rows = _expert_ffn(pad_starts.astype(I32), (padded // MOE_BLOCK).astype(I32), counts.astype(I32),
                         xs_rows, w_gate_up[l], b_gate_up[l], w_down[l], b_down[l])

    gf = row(g_final)
    y_p = _combine(dest[:, :t_p], x2_p.reshape(t_p, d), gates_p, gf, y_rows, COMBINE_TILE)
    y_s = _combine(dest[:, t_p:t_all], x2_s, gates_s, gf, y_rows, nb)

    x_heads = cache_mem_k.shape[3]
    return (y_p.reshape(b, s, d),
            y_s.reshape(nb, 1, d),
            mk2.reshape(1, b, n_mem, x_heads, d // x_heads),
            mv2.reshape(1, b, n_mem, x_heads, d // x_heads),
            conv_p.reshape(1, b, CONV_W - 1, b_width),
            jnp.stack([st[:, 1], z_s], axis=1).reshape(1, nb, CONV_W - 1, b_width),
            v_s.reshape(1, nb, 1, a_heads, a_hd))
```

```python
import functools

import jax
import jax.numpy as jnp
from jax import lax
from jax.experimental import pallas as pl
from jax.experimental.pallas import tpu as pltpu
from jax.experimental.pallas import tpu_sc as plsc

F32 = jnp.float32
BF16 = jnp.bfloat16
I32 = jnp.int32
U32 = jnp.uint32

A_HEADS = 8
CHUNK = 128
CONV_W = 3
X_HEADS = 4
N_EXPERTS = 32
TOP_K = 4
SWIGLU_LIMIT = 7.0
SWIGLU_ALPHA = 1.702
EPS = 1e-5

MIX_TILE = 512
ATT_TILE = 512
KV_TILE = 512
ROUTE_TILE = 512
MOE_BLOCK = 256
COMBINE_TILE = 512
SAMPLE_ATT_GROUP = 4
SC_CHUNK = 64
VMEM_LIMIT = 56 * 1024 * 1024


def _cparams(sem=None):
    return pltpu.CompilerParams(dimension_semantics=sem, vmem_limit_bytes=VMEM_LIMIT)


def _rms(x, g):
    r = lax.rsqrt(jnp.mean(x * x, axis=-1, keepdims=True) + EPS)
    return (x * r) * g


def _gelu(x):
    return 0.5 * x * (1.0 + lax.erf(x * 0.7071067811865476))


def _dot(a, b):
    return jnp.dot(a, b, preferred_element_type=F32)


def _dot_nt(a, b):
    return lax.dot_general(a, b, (((1,), (1,)), ((), ())), preferred_element_type=F32)


def _pack_bf16_pairs(x):
    d = x.shape[1]
    r = x.astype(BF16).astype(F32)
    lo = lax.shift_right_logical(pltpu.bitcast(r[:, :d // 2], U32), jnp.uint32(16))
    hi = lax.bitwise_and(pltpu.bitcast(r[:, d // 2:], U32), jnp.uint32(0xFFFF0000))
    return lax.bitwise_or(lo, hi)


def _unpack_bf16_pairs(w):
    lo = pltpu.bitcast(lax.shift_left(w, jnp.uint32(16)), F32)
    hi = pltpu.bitcast(lax.bitwise_and(w, jnp.uint32(0xFFFF0000)), F32)
    return lo, hi


def _memkv_kernel(m_ref, g_ref, wk_ref, wv_ref, k_ref, v_ref):
    h = _rms(m_ref[...], g_ref[...]).astype(BF16)
    k_ref[...] = _dot(h, wk_ref[...])
    v_ref[...] = _dot(h, wv_ref[...])


def _mem_kv(mem2d, g_mem, wk, wv):
    n, d = mem2d.shape
    row = pl.BlockSpec((KV_TILE, d), lambda i: (i, 0))
    full = lambda shape: pl.BlockSpec(shape, lambda i: (0,) * len(shape))
    return pl.pallas_call(
        _memkv_kernel,
        grid=(n // KV_TILE,),
        in_specs=[row, full((1, d)), full((d, d)), full((d, d))],
        out_specs=[row, row],
        out_shape=[jax.ShapeDtypeStruct((n, d), F32)] * 2,
        compiler_params=_cparams(("arbitrary",)),
        name="mem_kv",
    )(mem2d, g_mem, wk, wv)


def _head_rms(v, gv, a_width):
    hd = a_width // A_HEADS
    r_i = lax.broadcasted_iota(I32, (a_width, a_width), 0) // hd
    c_i = lax.broadcasted_iota(I32, (a_width, a_width), 1) // hd
    ones_bd = jnp.where(r_i == c_i, 1.0, 0.0).astype(BF16)
    sq = v * v
    sq_hi = sq.astype(BF16)
    sq_lo = (sq - sq_hi.astype(F32)).astype(BF16)
    gs = _dot(sq_hi, ones_bd) + _dot(sq_lo, ones_bd)
    return (v * lax.rsqrt(gs * (1.0 / hd) + EPS)) * gv


def _mixer_prompt_kernel(x_ref, gmix_ref, win_ref, gv_ref, ws_ref, bs_ref, wc_ref, wout_ref,
                         x1_ref, cs_ref, zbuf, *, tile, a_width, b_width):
    j = pl.program_id(1)
    nj = pl.num_programs(1)
    x = x_ref[...]
    h = _rms(x, gmix_ref[...]).astype(BF16)
    proj = _dot(h, win_ref[...])
    u = _gelu(proj[:, :a_width])
    v = _head_rms(_gelu(proj[:, a_width:2 * a_width]), gv_ref[...], a_width)
    o = 2 * a_width
    hb = proj[:, o:o + b_width]
    bg = proj[:, o + b_width:o + 2 * b_width]
    cg = proj[:, o + 2 * b_width:o + 3 * b_width]

    lane = lax.broadcasted_iota(I32, (CHUNK, 2 * (a_width // A_HEADS)), 1)
    first = lane < (a_width // A_HEADS)
    t_i = lax.broadcasted_iota(I32, (CHUNK, 2 * CHUNK), 0)
    s_i = lax.broadcasted_iota(I32, (CHUNK, 2 * CHUNK), 1) % CHUNK
    causal = s_i <= t_i
    pair_cols = []
    for p in range(A_HEADS // 2):
        w_pair = jnp.where(causal, ws_ref[p], 0.0).astype(BF16)
        vp = v[:, p * CHUNK:(p + 1) * CHUNK]
        rows = []
        for c in range(tile // CHUNK):
            vc = vp[c * CHUNK:(c + 1) * CHUNK]
            rhs = jnp.concatenate([jnp.where(first, vc, 0.0), jnp.where(first, 0.0, vc)],
                                  axis=0).astype(BF16)
            rows.append(_dot(w_pair, rhs))
        pair_cols.append(jnp.concatenate(rows, axis=0))
    gate = jnp.concatenate(pair_cols, axis=1)
    bias = jnp.concatenate([bs_ref[...]] * (tile // CHUNK), axis=0)
    a_out = u * (gate + bias)

    @pl.when(j == 0)
    def _():
        zbuf[0:8, :] = jnp.zeros((8, b_width), F32)

    z = cg * hb
    zbuf[8:tile + 8, :] = z
    z1 = zbuf[7:tile + 7, :]
    z2 = zbuf[6:tile + 6, :]
    wc = wc_ref[...]
    conv = z2 * wc[0:1] + z1 * wc[1:2] + z * wc[2:3]
    b_out = bg * conv
    tail = zbuf[tile:tile + 8, :]
    zbuf[0:8, :] = tail

    @pl.when(j == nj - 1)
    def _():
        cs_ref[...] = tail[8 - (CONV_W - 1):, :]

    y = _dot(a_out.astype(BF16), wout_ref[0:a_width, :]) + _dot(b_out.astype(BF16), wout_ref[a_width:, :])
    x1_ref[...] = x + y


def _mixer_prompt(x, g_mix, w_in, g_v, ws_pairs, bs_full, w_conv, w_out):
    b, s, d = x.shape
    a_width = g_v.shape[1]
    b_width = w_conv.shape[1]
    in_width = w_in.shape[1]
    tile = MIX_TILE
    full = lambda shape: pl.BlockSpec(shape, lambda i, j: (0,) * len(shape))
    kern = functools.partial(_mixer_prompt_kernel, tile=tile, a_width=a_width, b_width=b_width)
    return pl.pallas_call(
        kern,
        grid=(b, s // tile),
        in_specs=[
            pl.BlockSpec((None, tile, d), lambda i, j: (i, j, 0)),
            full((1, d)), full((d, in_width)), full((1, a_width)),
            full(ws_pairs.shape), full(bs_full.shape), full(w_conv.shape), full(w_out.shape),
        ],
        out_specs=[
            pl.BlockSpec((None, tile, d), lambda i, j: (i, j, 0)),
            pl.BlockSpec((None, CONV_W - 1, b_width), lambda i, j: (i, 0, 0)),
        ],
        out_shape=[jax.ShapeDtypeStruct((b, s, d), F32),
                   jax.ShapeDtypeStruct((b, CONV_W - 1, b_width), F32)],
        scratch_shapes=[pltpu.VMEM((tile + 8, b_width), F32)],
        compiler_params=_cparams(("arbitrary", "arbitrary")),
        name="mixer_prompt",
    )(x, g_mix, w_in, g_v, ws_pairs, bs_full, w_conv, w_out)


def _mixer_sample_kernel(x_ref, s0_ref, s1_ref, gmix_ref, win_ref, gv_ref, w00_ref, b0_ref, wc_ref,
                         wout_ref, gx_ref, wq_ref, x1_ref, v_ref, z_ref, q_ref, *, a_width, b_width):
    x = x_ref[...]
    h = _rms(x, gmix_ref[...]).astype(BF16)
    proj = _dot(h, win_ref[...])
    u = _gelu(proj[:, :a_width])
    v = _head_rms(_gelu(proj[:, a_width:2 * a_width]), gv_ref[...], a_width)
    v_ref[...] = v
    o = 2 * a_width
    hb = proj[:, o:o + b_width]
    bg = proj[:, o + b_width:o + 2 * b_width]
    cg = proj[:, o + 2 * b_width:o + 3 * b_width]
    a_out = u * (v * w00_ref[...] + b0_ref[...])
    z = cg * hb
    z_ref[...] = z
    wc = wc_ref[...]
    conv = s0_ref[...] * wc[0:1] + s1_ref[...] * wc[1:2] + z * wc[2:3]
    b_out = bg * conv
    y = _dot(a_out.astype(BF16), wout_ref[0:a_width, :]) + _dot(b_out.astype(BF16), wout_ref[a_width:, :])
    x1 = x + y
    x1_ref[...] = x1
    q_ref[...] = _dot(_rms(x1, gx_ref[...]).astype(BF16), wq_ref[...])


def _mixer_sample(x, s0, s1, g_mix, w_in, g_v, w00, b0, w_conv, w_out, g_x, w_q):
    n, d = x.shape
    a_width = g_v.shape[1]
    b_width = w_conv.shape[1]
    kern = functools.partial(_mixer_sample_kernel, a_width=a_width, b_width=b_width)
    return pl.pallas_call(
        kern,
        out_shape=[jax.ShapeDtypeStruct((n, d), F32), jax.ShapeDtypeStruct((n, a_width), F32),
                   jax.ShapeDtypeStruct((n, b_width), F32), jax.ShapeDtypeStruct((n, d), F32)],
        compiler_params=_cparams(),
        name="mixer_sample",
    )(x, s0, s1, g_mix, w_in, g_v, w00, b0, w_conv, w_out, g_x, w_q)


def _router_tail(x2, gmoe, wrt, br, h3_ref, tope_ref, gates_ref):
    h3f = _rms(x2, gmoe)
    h3 = h3f.astype(BF16)
    h3_ref[...] = _pack_bf16_pairs(h3f)

    logits = _dot_nt(wrt, h3) + br
    n_e, n_t = logits.shape
    e_idx = lax.broadcasted_iota(I32, (n_e, n_t), 0).astype(F32)
    tops, idxs = [], []
    for _ in range(TOP_K):
        m = jnp.max(logits, axis=0, keepdims=True)
        idx = jnp.min(jnp.where(logits == m, e_idx, float(n_e)), axis=0, keepdims=True)
        tops.append(m)
        idxs.append(idx)
        logits = jnp.where(e_idx == idx, -jnp.inf, logits)
    top = jnp.concatenate(tops, axis=0)
    ex = jnp.exp(top - top[0:1])
    gates_ref[...] = ex / jnp.sum(ex, axis=0, keepdims=True)
    tope_ref[...] = jnp.concatenate(idxs, axis=0).astype(I32)


def _attn_prompt_kernel(x1_ref, k_ref, v_ref, gx_ref, wq_ref, wxo_ref, gmoe_ref, wrt_ref, br_ref,
                        x2_ref, h3_ref, tope_ref, gates_ref):
    x = x1_ref[...]
    d = x.shape[1]
    hd = d // X_HEADS
    q = _dot(_rms(x, gx_ref[...]).astype(BF16), wq_ref[...]).astype(BF16)
    kb = k_ref[...].astype(BF16)
    vb = v_ref[...].astype(BF16)
    outs = []
    for hh in range(X_HEADS):
        sl = slice(hh * hd, (hh + 1) * hd)
        s = _dot_nt(q[:, sl], kb[:, sl]) * (hd ** -0.5)
        e = jnp.exp(s - jnp.max(s, axis=-1, keepdims=True))
        p = e / jnp.sum(e, axis=-1, keepdims=True)
        outs.append(_dot(p.astype(BF16), vb[:, sl]))
    o = jnp.concatenate(outs, axis=1).astype(BF16)
    x2 = x + _dot(o, wxo_ref[...])
    x2_ref[...] = x2
    _router_tail(x2, gmoe_ref[...], wrt_ref[...], br_ref[...], h3_ref, tope_ref, gates_ref)


def _attn_prompt(x1, mk, mv, g_x, w_q, w_xo, g_moe, wrt, br):
    b, s, d = x1.shape
    n_mem = mk.shape[1]
    tile = ATT_TILE
    nq = s // tile
    full = lambda shape: pl.BlockSpec(shape, lambda i, j: (0,) * len(shape))
    tok = pl.BlockSpec((None, tile, d), lambda i, j: (i, j, 0))
    mem = pl.BlockSpec((None, n_mem, d), lambda i, j: (i, 0, 0))
    lanes = pl.BlockSpec((TOP_K, tile), lambda i, j: (0, i * nq + j))
    return pl.pallas_call(
        _attn_prompt_kernel,
        grid=(b, nq),
        in_specs=[tok, mem, mem, full((1, d)), full((d, d)), full((d, d)), full((1, d)),
                  full(wrt.shape), full(br.shape)],
        out_specs=[tok, pl.BlockSpec((None, tile, d // 2), lambda i, j: (i, j, 0)), lanes, lanes],
        out_shape=[jax.ShapeDtypeStruct((b, s, d), F32), jax.ShapeDtypeStruct((b, s, d // 2), U32),
                   jax.ShapeDtypeStruct((TOP_K, b * s), I32), jax.ShapeDtypeStruct((TOP_K, b * s), F32)],
        compiler_params=_cparams(("arbitrary", "arbitrary")),
        name="attn_prompt",
    )(x1, mk, mv, g_x, w_q, w_xo, g_moe, wrt, br)


def _attn_sample_kernel(q_ref, k_ref, v_ref, o_ref, *, group):
    hd = q_ref.shape[2]
    for g in range(group):
        q = q_ref[g]
        s = jnp.sum(k_ref[g] * q[None], axis=-1, keepdims=True) * (hd ** -0.5)
        e = jnp.exp(s - jnp.max(s, axis=0, keepdims=True))
        p = e / jnp.sum(e, axis=0, keepdims=True)
        o_ref[g] = jnp.sum(p * v_ref[g], axis=0)


def _attn_sample(q, ck, cv):
    n, heads, hd = q.shape
    n_mem = ck.shape[1]
    group = SAMPLE_ATT_GROUP
    qo = pl.BlockSpec((group, heads, hd), lambda i: (i, 0, 0))
    kv = pl.BlockSpec((group, n_mem, heads, hd), lambda i: (i, 0, 0, 0))
    return pl.pallas_call(
        functools.partial(_attn_sample_kernel, group=group),
        grid=(n // group,),
        in_specs=[qo, kv, kv],
        out_specs=qo,
        out_shape=jax.ShapeDtypeStruct((n, heads, hd), F32),
        compiler_params=_cparams(("arbitrary",)),
        name="attn_sample",
    )(q, ck, cv)


def _tail_sample_kernel(x1_ref, o_ref, wxo_ref, gmoe_ref, wrt_ref, br_ref,
                        x2_ref, h3_ref, tope_ref, gates_ref):
    x2 = x1_ref[...] + _dot(o_ref[...].astype(BF16), wxo_ref[...])
    x2_ref[...] = x2
    _router_tail(x2, gmoe_ref[...], wrt_ref[...], br_ref[...], h3_ref, tope_ref, gates_ref)


def _tail_sample(x1, o, w_xo, g_moe, wrt, br):
    n, d = x1.shape
    return pl.pallas_call(
        _tail_sample_kernel,
        out_shape=[jax.ShapeDtypeStruct((n, d), F32), jax.ShapeDtypeStruct((n, d // 2), U32),
                   jax.ShapeDtypeStruct((TOP_K, n), I32), jax.ShapeDtypeStruct((TOP_K, n), F32)],
        compiler_params=_cparams(),
        name="tail_sample",
    )(x1, o, w_xo, g_moe, wrt, br)


def _route_kernel(tope_ref, dest_ref, cnt_ref, *, n_blocks, tile, moe_block):
    e_idx = lax.broadcasted_iota(I32, (N_EXPERTS, tile), 0)
    earlier = jnp.where(lax.broadcasted_iota(I32, (tile, tile), 0)
                        < lax.broadcasted_iota(I32, (tile, tile), 1), 1.0, 0.0).astype(BF16)

    def onehot(k, off):
        return e_idx == tope_ref[pl.ds(k, 1), pl.ds(off, tile)]

    def count_body(j, cnt):
        off = pl.multiple_of(j * tile, tile)
        for k in range(TOP_K):
            cnt = cnt + jnp.sum(jnp.where(onehot(k, off), 1.0, 0.0), axis=1, keepdims=True)
        return cnt

    counts = lax.fori_loop(0, n_blocks, count_body, jnp.zeros((N_EXPERTS, 1), F32))
    cnt_ref[...] = jnp.broadcast_to(counts, cnt_ref.shape).astype(I32)

    n_blk = jnp.floor((counts + (moe_block - 1)) * (1.0 / moe_block))
    n_hi = jnp.floor(n_blk * (1.0 / 16.0))
    n_lo = n_blk - 16.0 * n_hi
    below = jnp.where(lax.broadcasted_iota(I32, (N_EXPERTS, N_EXPERTS), 1)
                      < lax.broadcasted_iota(I32, (N_EXPERTS, N_EXPERTS), 0), 1.0, 0.0).astype(BF16)
    wide = lambda c: jnp.broadcast_to(c, (N_EXPERTS, 128)).astype(BF16)
    start = (16.0 * _dot(below, wide(n_hi)) + _dot(below, wide(n_lo)))[:, 0:1] * float(moe_block)

    def dest_body(j, run):
        off = pl.multiple_of(j * tile, tile)
        for k in range(TOP_K):
            oh = onehot(k, off)
            ohf = jnp.where(oh, 1.0, 0.0)
            before = _dot(ohf.astype(BF16), earlier)
            dest = jnp.sum(jnp.where(oh, before + run, 0.0), axis=0, keepdims=True)
            dest_ref[pl.ds(k, 1), pl.ds(off, tile)] = dest.astype(I32)
            run = run + jnp.sum(ohf, axis=1, keepdims=True)
        return run

    lax.fori_loop(0, n_blocks, dest_body, start)


def _route(tope_pad):
    t_pad = tope_pad.shape[1]
    kern = functools.partial(_route_kernel, n_blocks=t_pad // ROUTE_TILE, tile=ROUTE_TILE,
                             moe_block=MOE_BLOCK)
    return pl.pallas_call(
        kern,
        out_shape=[jax.ShapeDtypeStruct((TOP_K, t_pad), I32),
                   jax.ShapeDtypeStruct((N_EXPERTS, 128), I32)],
        compiler_params=_cparams(),
        name="route",
    )(tope_pad)


def _dispatch(dest3, h_prompt, h_sample, n_rows):
    w = h_prompt.shape[1]
    n_prompt_chunks = h_prompt.shape[0] // SC_CHUNK
    n_chunks = dest3.shape[0]
    mesh = plsc.VectorSubcoreMesh(core_axis_name="c", subcore_axis_name="s")
    n_workers = mesh.num_cores * mesh.num_subcores

    n_steps = -(-n_chunks // n_workers)

    def body(dest_hbm, hp_hbm, hs_hbm, xs_hbm, idx0, idx1, rows0, rows1, sem):
        wid = lax.axis_index("s") * mesh.num_cores + lax.axis_index("c")
        idx_v, rows_v = (idx0, idx1), (rows0, rows1)

        def chunk(j):
            return wid + j * n_workers

        def loads(j, slot):
            c = chunk(j)
            prompt = (hp_hbm, c)
            sample = (hs_hbm, c - n_prompt_chunks)
            return c, [
                (pltpu.make_async_copy(dest_hbm.at[c], idx_v[slot], sem.at[slot]), None),
                (pltpu.make_async_copy(prompt[0].at[pl.ds(prompt[1] * SC_CHUNK, SC_CHUNK)], rows_v[slot],
                                       sem.at[slot]), c < n_prompt_chunks),
                (pltpu.make_async_copy(sample[0].at[pl.ds(sample[1] * SC_CHUNK, SC_CHUNK)], rows_v[slot],
                                       sem.at[slot]), c >= n_prompt_chunks),
            ]

        def start_loads(j, slot):
            c, copies = loads(j, slot)
            for cp, cond in copies:
                pl.when((c < n_chunks) if cond is None else ((c < n_chunks) & cond))(cp.start)

        def wait_loads(j, slot):
            c, copies = loads(j, slot)
            for cp, cond in copies:
                pl.when((c < n_chunks) if cond is None else ((c < n_chunks) & cond))(cp.wait)

        def scatters(slot):
            return [pltpu.make_async_copy(rows_v[slot], xs_hbm.at[idx_v[slot].at[k]], sem.at[2 + slot])
                    for k in range(TOP_K)]

        start_loads(0, 0)
        for j in range(n_steps):
            slot = j % 2
            wait_loads(j, slot)
            if j >= 1:
                @pl.when(chunk(j - 1) < n_chunks)
                def _():
                    for cp in scatters(1 - slot):
                        cp.wait()
            if j + 1 < n_steps:
                start_loads(j + 1, 1 - slot)

            @pl.when(chunk(j) < n_chunks)
            def _():
                for cp in scatters(slot):
                    cp.start()

        @pl.when(chunk(n_steps - 1) < n_chunks)
        def _():
            for cp in scatters((n_steps - 1) % 2):
                cp.wait()

    return pl.kernel(
        body,
        out_type=jax.ShapeDtypeStruct((n_rows, w), U32),
        mesh=mesh,
        scratch_types=[pltpu.VMEM((TOP_K, SC_CHUNK), I32), pltpu.VMEM((TOP_K, SC_CHUNK), I32),
                       pltpu.VMEM((SC_CHUNK, w), U32), pltpu.VMEM((SC_CHUNK, w), U32),
                       pltpu.SemaphoreType.DMA((4,))],
        name="dispatch_sc",
    )(dest3, h_prompt, h_sample)


def _ffn_kernel(start_ref, nblk_ref, count_ref, b1_ref, b2_ref, xs_hbm, w1_hbm, w2_hbm, y_hbm,
                w1f, w2f, w1b, w2b, xbuf, ybuf, sem_w, sem_x, sem_y, *, blk):
    e = pl.program_id(0)
    n_e = pl.num_programs(0)
    nb = nblk_ref[e]
    base = start_ref[e]
    wslot = e % 2

    def w_copies(expert, slot):
        return (pltpu.make_async_copy(w1_hbm.at[expert], w1f.at[slot], sem_w.at[slot]),
                pltpu.make_async_copy(w2_hbm.at[expert], w2f.at[slot], sem_w.at[slot]))

    def x_copy(row0, slot):
        return pltpu.make_async_copy(xs_hbm.at[pl.ds(pl.multiple_of(row0, blk), blk)], xbuf.at[slot],
                                     sem_x.at[slot])

    def y_copy(row0, slot):
        return pltpu.make_async_copy(ybuf.at[slot], y_hbm.at[pl.ds(pl.multiple_of(row0, blk), blk)],
                                     sem_y.at[slot])

    @pl.when(e == 0)
    def _():
        for cp in w_copies(0, 0):
            cp.start()

        @pl.when(nb > 0)
        def _():
            x_copy(base, 0).start()

    for cp in w_copies(e, wslot):
        cp.wait()

    @pl.when(e + 1 < n_e)
    def _():
        for cp in w_copies(e + 1, 1 - wslot):
            cp.start()

    w1b[...] = w1f[wslot].astype(BF16)
    w2b[...] = w2f[wslot].astype(BF16)
    b1 = b1_ref[e]
    b2 = b2_ref[e]

    def block(j, slot):
        x_copy(base, slot).wait()

        @pl.when(j + 1 < nb)
        def _():
            x_copy(base + (j + 1) * blk, 1 - slot).start()

        @pl.when(j >= 2)
        def _():
            y_copy(base, slot).wait()

        w = xbuf[slot]
        row = lax.broadcasted_iota(I32, w.shape, 0)
        w = jnp.where(row < count_ref[e] - j * blk, w, jnp.uint32(0))
        half = w.shape[1]
        d_e = w2b.shape[0]
        lo, hi = _unpack_bf16_pairs(w)
        gu = _dot(lo.astype(BF16), w1b[0:half, :]) + _dot(hi.astype(BF16), w1b[half:, :]) + b1
        gate = jnp.minimum(gu[:, :d_e], SWIGLU_LIMIT)
        up = jnp.clip(gu[:, d_e:], -SWIGLU_LIMIT, SWIGLU_LIMIT)
        glu = gate * jax.nn.sigmoid(gate * SWIGLU_ALPHA)
        act = ((up + 1.0) * glu).astype(BF16)
        ybuf[slot] = _pack_bf16_pairs(_dot(act, w2b[...]) + b2)
        y_copy(base + j * blk, slot).start()

    def pair(j2, carry):
        j = 2 * j2
        block(j, 0)

        @pl.when(j + 1 < nb)
        def _():
            block(j + 1, 1)

        return carry

    lax.fori_loop(0, (nb + 1) // 2, pair, 0)

    @pl.when(nb >= 2)
    def _():
        y_copy(base, 0).wait()
        y_copy(base, 1).wait()

    @pl.when(nb == 1)
    def _():
        y_copy(base, 0).wait()

    nxt = jnp.minimum(e + 1, n_e - 1)

    @pl.when((e + 1 < n_e) & (nblk_ref[nxt] > 0))
    def _():
        x_copy(start_ref[nxt], 0).start()


def _expert_ffn(grp_start, grp_blocks, grp_count, xs, w1, b1, w2, b2):
    n_rows, half = xs.shape
    n_e, d, d2 = w1.shape
    d_e = w2.shape[1]
    blk = MOE_BLOCK
    whole = lambda shape: pl.BlockSpec(shape, lambda e, s, n, c: (0,) * len(shape))
    hbm = pl.BlockSpec(memory_space=pl.ANY)
    grid_spec = pltpu.PrefetchScalarGridSpec(
        num_scalar_prefetch=3,
        grid=(n_e,),
        in_specs=[whole((n_e, 1, d2)), whole((n_e, 1, d)), hbm, hbm, hbm],
        out_specs=hbm,
        scratch_shapes=[pltpu.VMEM((2, d, d2), F32), pltpu.VMEM((2, d_e, d), F32),
                        pltpu.VMEM((d, d2), BF16), pltpu.VMEM((d_e, d), BF16),
                        pltpu.VMEM((2, blk, half), U32), pltpu.VMEM((2, blk, d // 2), U32),
                        pltpu.SemaphoreType.DMA((2,)), pltpu.SemaphoreType.DMA((2,)),
                        pltpu.SemaphoreType.DMA((2,))],
    )
    return pl.pallas_call(
        functools.partial(_ffn_kernel, blk=blk),
        grid_spec=grid_spec,
        out_shape=jax.ShapeDtypeStruct((n_rows, d // 2), U32),
        compiler_params=_cparams(("arbitrary",)),
        name="expert_ffn",
    )(grp_start, grp_blocks, grp_count, b1.reshape(n_e, 1, d2), b2.reshape(n_e, 1, d), xs, w1, w2)


def _undispatch(dest3, y_rows):
    n_chunks = dest3.shape[0]
    w = y_rows.shape[1]
    mesh = plsc.VectorSubcoreMesh(core_axis_name="c", subcore_axis_name="s")
    n_workers = mesh.num_cores * mesh.num_subcores
    n_steps = -(-n_chunks // n_workers)

    def body(dest_hbm, y_hbm, out_hbm, idx0, idx1, rows0, rows1, sem):
        wid = lax.axis_index("s") * mesh.num_cores + lax.axis_index("c")
        idx_v, rows_v = (idx0, idx1), (rows0, rows1)

        def chunk(j):
            return wid + j * n_workers

        def gather(j, k, slot):
            return pltpu.make_async_copy(y_hbm.at[idx_v[j % 2].at[k]], rows_v[slot], sem.at[slot])

        def store(j, k, slot):
            return pltpu.make_async_copy(rows_v[slot], out_hbm.at[k, pl.ds(chunk(j) * SC_CHUNK, SC_CHUNK)],
                                         sem.at[2 + slot])

        units = [(j, k) for j in range(n_steps) for k in range(TOP_K)]

        def valid(j):
            return chunk(j) < n_chunks

        def load_idx(j):
            pl.when(valid(j))(lambda: pltpu.sync_copy(dest_hbm.at[chunk(j)], idx_v[j % 2]))

        load_idx(0)
        pl.when(valid(0))(gather(0, 0, 0).start)
        for u, (j, k) in enumerate(units):
            slot = u % 2
            pl.when(valid(j))(gather(j, k, slot).wait)
            if u + 1 < len(units):
                jn, kn = units[u + 1]
                if u >= 1:
                    jp, kp = units[u - 1]
                    pl.when(valid(jp))(store(jp, kp, 1 - slot).wait)
                if kn == 0:
                    load_idx(jn)
                pl.when(valid(jn))(gather(jn, kn, 1 - slot).start)
            pl.when(valid(j))(store(j, k, slot).start)
        for u in (len(units) - 2, len(units) - 1):
            j, k = units[u]
            pl.when(valid(j))(store(j, k, u % 2).wait)

    return pl.kernel(
        body,
        out_type=jax.ShapeDtypeStruct((TOP_K, n_chunks * SC_CHUNK, w), y_rows.dtype),
        mesh=mesh,
        scratch_types=[pltpu.VMEM((TOP_K, SC_CHUNK), I32), pltpu.VMEM((TOP_K, SC_CHUNK), I32),
                       pltpu.VMEM((SC_CHUNK, w), y_rows.dtype), pltpu.VMEM((SC_CHUNK, w), y_rows.dtype),
                       pltpu.SemaphoreType.DMA((4,))],
        name="undispatch_sc",
    )(dest3, y_rows)


def _combine_kernel(x2_ref, y4_ref, gates_ref, gf_ref, out_ref, *, tile):
    g = jnp.concatenate([gates_ref[...], jnp.zeros((128 - TOP_K, tile), F32)], axis=0)
    gt = g.T
    half = y4_ref.shape[2]
    moe_lo = jnp.zeros((tile, half), F32)
    moe_hi = jnp.zeros((tile, half), F32)
    for k in range(TOP_K):
        lo, hi = _unpack_bf16_pairs(y4_ref[k])
        moe_lo = moe_lo + gt[:, k:k + 1] * lo
        moe_hi = moe_hi + gt[:, k:k + 1] * hi
    x2 = x2_ref[...]
    out_ref[...] = _rms(x2 + jnp.concatenate([moe_lo, moe_hi], axis=1), gf_ref[...])


def _combine(x2, y4, gates, g_final, tile, first_tile):
    n, d = x2.shape
    return pl.pallas_call(
        functools.partial(_combine_kernel, tile=tile),
        grid=(n // tile,),
        in_specs=[
            pl.BlockSpec((tile, d), lambda i: (i, 0)),
            pl.BlockSpec((TOP_K, tile, d // 2), lambda i: (0, i + first_tile, 0)),
            pl.BlockSpec((TOP_K, tile), lambda i: (0, i)),
            pl.BlockSpec((1, d), lambda i: (0, 0)),
        ],
        out_specs=pl.BlockSpec((tile, d), lambda i: (i, 0)),
        out_shape=jax.ShapeDtypeStruct((n, d), F32),
        compiler_params=_cparams(("arbitrary",)),
        name="combine",
    )(x2, y4, gates, g_final)


def kernel(x_prompt, x_sample, mem_prompt, cache_mem_k, cache_mem_v, state_conv, g_mix, w_in, g_v,
           w_spatial, b_spatial, w_conv, w_out, g_xattn, g_mem, w_q, w_k, w_v, w_xo, g_moe, w_router,
           b_router, w_gate_up, b_gate_up, w_down, b_down, g_final):
    depth = g_mix.shape[0]
    assert depth == 1, "one layer supported"
    b, s, d = x_prompt.shape
    nb, ns, _ = x_sample.shape
    assert ns == 1
    n_mem = mem_prompt.shape[1]
    a_heads, a_hd = g_v.shape[1], g_v.shape[2]
    a_width = a_heads * a_hd
    b_width = w_conv.shape[2]
    assert a_heads == A_HEADS and 2 * a_hd == CHUNK and w_spatial.shape[2] == CHUNK
    assert s % MIX_TILE == 0 and s % ATT_TILE == 0 and (b * n_mem) % KV_TILE == 0
    assert nb % SAMPLE_ATT_GROUP == 0
    l = 0
    row = lambda a: a.reshape(1, -1)

    w_in_b = w_in[l].astype(BF16)
    w_out_b = w_out[l].astype(BF16)
    w_q_b = w_q[l].astype(BF16)
    w_k_b = w_k[l].astype(BF16)
    w_v_b = w_v[l].astype(BF16)
    w_xo_b = w_xo[l].astype(BF16)
    wrt_b = w_router[l].T.astype(BF16)
    br_col = b_router[l].reshape(N_EXPERTS, 1).astype(F32)
    gv_row = row(g_v[l])
    ws = w_spatial[l]
    ws_pairs = jnp.concatenate([ws[0::2], ws[1::2]], axis=2)
    bs_full = jnp.repeat(b_spatial[l].T, a_hd, axis=1)
    w00 = row(jnp.repeat(ws[:, 0, 0], a_hd))
    b0 = row(jnp.repeat(b_spatial[l][:, 0], a_hd))

    mk2, mv2 = _mem_kv(mem_prompt.reshape(b * n_mem, d), row(g_mem[l]), w_k_b, w_v_b)
    x1_p, conv_p = _mixer_prompt(x_prompt, row(g_mix[l]), w_in_b, gv_row, ws_pairs, bs_full,
                                 w_conv[l], w_out_b)
    x2_p, h3_p, tope_p, gates_p = _attn_prompt(
        x1_p, mk2.reshape(b, n_mem, d), mv2.reshape(b, n_mem, d), row(g_xattn[l]), w_q_b, w_xo_b,
        row(g_moe[l]), wrt_b, br_col)

    xs2 = x_sample.reshape(nb, d)
    st = state_conv[l]
    x1_s, v_s, z_s, q_s = _mixer_sample(xs2, st[:, 0], st[:, 1], row(g_mix[l]), w_in_b, gv_row, w00, b0,
                                        w_conv[l], w_out_b, row(g_xattn[l]), w_q_b)
    x_heads, x_hd = cache_mem_k.shape[3], cache_mem_k.shape[4]
    assert depth == 1 and x_heads == X_HEADS
    o_s = _attn_sample(q_s.reshape(nb, x_heads, x_hd), cache_mem_k.reshape(nb, n_mem, x_heads, x_hd),
                       cache_mem_v.reshape(nb, n_mem, x_heads, x_hd)).reshape(nb, d)
    x2_s, h3_s, tope_s, gates_s = _tail_sample(x1_s, o_s, w_xo_b, row(g_moe[l]), wrt_b, br_col)

    t_p = b * s
    t_all = t_p + nb
    t_pad = -(-t_all // ROUTE_TILE) * ROUTE_TILE
    tope_all = jnp.concatenate([tope_p, tope_s, jnp.full((TOP_K, t_pad - t_all), -1, I32)], axis=1)
    dest, cnt = _route(tope_all)
    counts = cnt[:, 0]
    padded = (counts + MOE_BLOCK - 1) // MOE_BLOCK * MOE_BLOCK
    pad_ends = jnp.cumsum(padded)
    pad_starts = pad_ends - padded
    n_blocks = -(-(t_all * TOP_K) // MOE_BLOCK) + N_EXPERTS
    n_rows = n_blocks * MOE_BLOCK
    assert t_all % SC_CHUNK == 0 and t_p % SC_CHUNK == 0
    dest3 = dest[:, :t_all].reshape(TOP_K, t_all // SC_CHUNK, SC_CHUNK).transpose(1, 0, 2)
    xs_rows = _dispatch(dest3, h3_p.reshape(t_p, d // 2), h3_s, n_rows)
    y_rows = _expert_ffn(pad_starts.astype(I32), (padded // MOE_BLOCK).astype(I32), counts.astype(I32),
                         xs_rows, w_gate_up[l], b_gate_up[l], w_down[l], b_down[l])

    y4 = _undispatch(dest3, y_rows)

    gf = row(g_final)
    assert t_p % COMBINE_TILE == 0 and t_p % nb == 0
    y_p = _combine(x2_p.reshape(t_p, d), y4, gates_p, gf, COMBINE_TILE, 0)
    y_s = _combine(x2_s, y4, gates_s, gf, nb, t_p // nb)

    x_heads = cache_mem_k.shape[3]
    return (y_p.reshape(b, s, d),
            y_s.reshape(nb, 1, d),
            mk2.reshape(1, b, n_mem, x_heads, d // x_heads),
            mv2.reshape(1, b, n_mem, x_heads, d // x_heads),
            conv_p.reshape(1, b, CONV_W - 1, b_width),
            jnp.stack([st[:, 1], z_s], axis=1).reshape(1, nb, CONV_W - 1, b_width),
            v_s.reshape(1, nb, 1, a_heads, a_hd))
```

```python
import functools

import jax
import jax.numpy as jnp
from jax import lax
from jax.experimental import pallas as pl
from jax.experimental.pallas import tpu as pltpu
from jax.experimental.pallas import tpu_sc as plsc

F32 = jnp.float32
BF16 = jnp.bfloat16
I32 = jnp.int32
U32 = jnp.uint32

A_HEADS = 8
CHUNK = 128
CONV_W = 3
X_HEADS = 4
N_EXPERTS = 32
TOP_K = 4
SWIGLU_LIMIT = 7.0
SWIGLU_ALPHA = 1.702
EPS = 1e-5

MIX_TILE = 512
ATT_TILE = 512
KV_TILE = 512
ROUTE_TILE = 512
MOE_BLOCK = 256
COMBINE_TILE = 512
SAMPLE_ATT_GROUP = 4
SC_CHUNK = 64
VMEM_LIMIT = 56 * 1024 * 1024


def _cparams(sem=None):
    return pltpu.CompilerParams(dimension_semantics=sem, vmem_limit_bytes=VMEM_LIMIT)


def _rms(x, g):
    r = lax.rsqrt(jnp.mean(x * x, axis=-1, keepdims=True) + EPS)
    return (x * r) * g


def _gelu(x):
    return 0.5 * x * (1.0 + lax.erf(x * 0.7071067811865476))


def _dot(a, b):
    return jnp.dot(a, b, preferred_element_type=F32)


def _dot_nt(a, b):
    return lax.dot_general(a, b, (((1,), (1,)), ((), ())), preferred_element_type=F32)


def _pack_bf16_pairs(x):
    d = x.shape[1]
    r = x.astype(BF16).astype(F32)
    lo = lax.shift_right_logical(pltpu.bitcast(r[:, :d // 2], U32), jnp.uint32(16))
    hi = lax.bitwise_and(pltpu.bitcast(r[:, d // 2:], U32), jnp.uint32(0xFFFF0000))
    return lax.bitwise_or(lo, hi)


def _unpack_bf16_pairs(w):
    lo = pltpu.bitcast(lax.shift_left(w, jnp.uint32(16)), F32)
    hi = pltpu.bitcast(lax.bitwise_and(w, jnp.uint32(0xFFFF0000)), F32)
    return lo, hi


def _memkv_kernel(m_ref, g_ref, wk_ref, wv_ref, k_ref, v_ref):
    h = _rms(m_ref[...], g_ref[...]).astype(BF16)
    k_ref[...] = _dot(h, wk_ref[...])
    v_ref[...] = _dot(h, wv_ref[...])


def _mem_kv(mem2d, g_mem, wk, wv):
    n, d = mem2d.shape
    row = pl.BlockSpec((KV_TILE, d), lambda i: (i, 0))
    full = lambda shape: pl.BlockSpec(shape, lambda i: (0,) * len(shape))
    return pl.pallas_call(
        _memkv_kernel,
        grid=(n // KV_TILE,),
        in_specs=[row, full((1, d)), full((d, d)), full((d, d))],
        out_specs=[row, row],
        out_shape=[jax.ShapeDtypeStruct((n, d), F32)] * 2,
        compiler_params=_cparams(("arbitrary",)),
        name="mem_kv",
    )(mem2d, g_mem, wk, wv)


def _head_rms(v, gv, a_width):
    hd = a_width // A_HEADS
    r_i = lax.broadcasted_iota(I32, (a_width, a_width), 0) // hd
    c_i = lax.broadcasted_iota(I32, (a_width, a_width), 1) // hd
    ones_bd = jnp.where(r_i == c_i, 1.0, 0.0).astype(BF16)
    sq = v * v
    sq_hi = sq.astype(BF16)
    sq_lo = (sq - sq_hi.astype(F32)).astype(BF16)
    gs = _dot(sq_hi, ones_bd) + _dot(sq_lo, ones_bd)
    return (v * lax.rsqrt(gs * (1.0 / hd) + EPS)) * gv


def _mixer_prompt_kernel(x_ref, gmix_ref, win_ref, gv_ref, ws_ref, bs_ref, wc_ref, wout_ref,
                         x1_ref, cs_ref, zbuf, *, tile, a_width, b_width):
    j = pl.program_id(1)
    nj = pl.num_programs(1)
    x = x_ref[...]
    h = _rms(x, gmix_ref[...]).astype(BF16)
    proj = _dot(h, win_ref[...])
    u = _gelu(proj[:, :a_width])
    v = _head_rms(_gelu(proj[:, a_width:2 * a_width]), gv_ref[...], a_width)
    o = 2 * a_width
    hb = proj[:, o:o + b_width]
    bg = proj[:, o + b_width:o + 2 * b_width]
    cg = proj[:, o + 2 * b_width:o + 3 * b_width]

    lane = lax.broadcasted_iota(I32, (CHUNK, 2 * (a_width // A_HEADS)), 1)
    first = lane < (a_width // A_HEADS)
    t_i = lax.broadcasted_iota(I32, (CHUNK, 2 * CHUNK), 0)
    s_i = lax.broadcasted_iota(I32, (CHUNK, 2 * CHUNK), 1) % CHUNK
    causal = s_i <= t_i
    pair_cols = []
    for p in range(A_HEADS // 2):
        w_pair = jnp.where(causal, ws_ref[p], 0.0).astype(BF16)
        vp = v[:, p * CHUNK:(p + 1) * CHUNK]
        rows = []
        for c in range(tile // CHUNK):
            vc = vp[c * CHUNK:(c + 1) * CHUNK]
            rhs = jnp.concatenate([jnp.where(first, vc, 0.0), jnp.where(first, 0.0, vc)],
                                  axis=0).astype(BF16)
            rows.append(_dot(w_pair, rhs))
        pair_cols.append(jnp.concatenate(rows, axis=0))
    gate = jnp.concatenate(pair_cols, axis=1)
    bias = jnp.concatenate([bs_ref[...]] * (tile // CHUNK), axis=0)
    a_out = u * (gate + bias)

    @pl.when(j == 0)
    def _():
        zbuf[0:8, :] = jnp.zeros((8, b_width), F32)

    z = cg * hb
    zbuf[8:tile + 8, :] = z
    z1 = zbuf[7:tile + 7, :]
    z2 = zbuf[6:tile + 6, :]
    wc = wc_ref[...]
    conv = z2 * wc[0:1] + z1 * wc[1:2] + z * wc[2:3]
    b_out = bg * conv
    tail = zbuf[tile:tile + 8, :]
    zbuf[0:8, :] = tail

    @pl.when(j == nj - 1)
    def _():
        cs_ref[...] = tail[8 - (CONV_W - 1):, :]

    y = _dot(a_out.astype(BF16), wout_ref[0:a_width, :]) + _dot(b_out.astype(BF16), wout_ref[a_width:, :])
    x1_ref[...] = x + y


def _mixer_prompt(x, g_mix, w_in, g_v, ws_pairs, bs_full, w_conv, w_out):
    b, s, d = x.shape
    a_width = g_v.shape[1]
    b_width = w_conv.shape[1]
    in_width = w_in.shape[1]
    tile = MIX_TILE
    full = lambda shape: pl.BlockSpec(shape, lambda i, j: (0,) * len(shape))
    kern = functools.partial(_mixer_prompt_kernel, tile=tile, a_width=a_width, b_width=b_width)
    return pl.pallas_call(
        kern,
        grid=(b, s // tile),
        in_specs=[
            pl.BlockSpec((None, tile, d), lambda i, j: (i, j, 0)),
            full((1, d)), full((d, in_width)), full((1, a_width)),
            full(ws_pairs.shape), full(bs_full.shape), full(w_conv.shape), full(w_out.shape),
        ],
        out_specs=[
            pl.BlockSpec((None, tile, d), lambda i, j: (i, j, 0)),
            pl.BlockSpec((None, CONV_W - 1, b_width), lambda i, j: (i, 0, 0)),
        ],
        out_shape=[jax.ShapeDtypeStruct((b, s, d), F32),
                   jax.ShapeDtypeStruct((b, CONV_W - 1, b_width), F32)],
        scratch_shapes=[pltpu.VMEM((tile + 8, b_width), F32)],
        compiler_params=_cparams(("arbitrary", "arbitrary")),
        name="mixer_prompt",
    )(x, g_mix, w_in, g_v, ws_pairs, bs_full, w_conv, w_out)


def _mixer_sample_kernel(x_ref, s0_ref, s1_ref, gmix_ref, win_ref, gv_ref, w00_ref, b0_ref, wc_ref,
                         wout_ref, gx_ref, wq_ref, x1_ref, v_ref, z_ref, q_ref, *, a_width, b_width):
    x = x_ref[...]
    h = _rms(x, gmix_ref[...]).astype(BF16)
    proj = _dot(h, win_ref[...])
    u = _gelu(proj[:, :a_width])
    v = _head_rms(_gelu(proj[:, a_width:2 * a_width]), gv_ref[...], a_width)
    v_ref[...] = v
    o = 2 * a_width
    hb = proj[:, o:o + b_width]
    bg = proj[:, o + b_width:o + 2 * b_width]
    cg = proj[:, o + 2 * b_width:o + 3 * b_width]
    a_out = u * (v * w00_ref[...] + b0_ref[...])
    z = cg * hb
    z_ref[...] = z
    wc = wc_ref[...]
    conv = s0_ref[...] * wc[0:1] + s1_ref[...] * wc[1:2] + z * wc[2:3]
    b_out = bg * conv
    y = _dot(a_out.astype(BF16), wout_ref[0:a_width, :]) + _dot(b_out.astype(BF16), wout_ref[a_width:, :])
    x1 = x + y
    x1_ref[...] = x1
    q_ref[...] = _dot(_rms(x1, gx_ref[...]).astype(BF16), wq_ref[...])


def _mixer_sample(x, s0, s1, g_mix, w_in, g_v, w00, b0, w_conv, w_out, g_x, w_q):
    n, d = x.shape
    a_width = g_v.shape[1]
    b_width = w_conv.shape[1]
    kern = functools.partial(_mixer_sample_kernel, a_width=a_width, b_width=b_width)
    return pl.pallas_call(
        kern,
        out_shape=[jax.ShapeDtypeStruct((n, d), F32), jax.ShapeDtypeStruct((n, a_width), F32),
                   jax.ShapeDtypeStruct((n, b_width), F32), jax.ShapeDtypeStruct((n, d), F32)],
        compiler_params=_cparams(),
        name="mixer_sample",
    )(x, s0, s1, g_mix, w_in, g_v, w00, b0, w_conv, w_out, g_x, w_q)


def _router_tail(x2, gmoe, wrt, br, h3_ref, tope_ref, gates_ref):
    h3f = _rms(x2, gmoe)
    h3 = h3f.astype(BF16)
    h3_ref[...] = _pack_bf16_pairs(h3f)

    logits = _dot_nt(wrt, h3) + br
    n_e, n_t = logits.shape
    e_idx = lax.broadcasted_iota(I32, (n_e, n_t), 0).astype(F32)
    tops, idxs = [], []
    for _ in range(TOP_K):
        m = jnp.max(logits, axis=0, keepdims=True)
        idx = jnp.min(jnp.where(logits == m, e_idx, float(n_e)), axis=0, keepdims=True)
        tops.append(m)
        idxs.append(idx)
        logits = jnp.where(e_idx == idx, -jnp.inf, logits)
    top = jnp.concatenate(tops, axis=0)
    ex = jnp.exp(top - top[0:1])
    gates_ref[...] = ex / jnp.sum(ex, axis=0, keepdims=True)
    tope_ref[...] = jnp.concatenate(idxs, axis=0).astype(I32)


def _attn_prompt_kernel(x1_ref, k_ref, v_ref, gx_ref, wq_ref, wxo_ref, gmoe_ref, wrt_ref, br_ref,
                        x2_ref, h3_ref, tope_ref, gates_ref):
    x = x1_ref[...]
    d = x.shape[1]
    hd = d // X_HEADS
    q = _dot(_rms(x, gx_ref[...]).astype(BF16), wq_ref[...]).astype(BF16)
    kb = k_ref[...].astype(BF16)
    vb = v_ref[...].astype(BF16)
    outs = []
    for hh in range(X_HEADS):
        sl = slice(hh * hd, (hh + 1) * hd)
        s = _dot_nt(q[:, sl], kb[:, sl]) * (hd ** -0.5)
        e = jnp.exp(s - jnp.max(s, axis=-1, keepdims=True))
        p = e / jnp.sum(e, axis=-1, keepdims=True)
        outs.append(_dot(p.astype(BF16), vb[:, sl]))
    o = jnp.concatenate(outs, axis=1).astype(BF16)
    x2 = x + _dot(o, wxo_ref[...])
    x2_ref[...] = x2
    _router_tail(x2, gmoe_ref[...], wrt_ref[...], br_ref[...], h3_ref, tope_ref, gates_ref)


def _attn_prompt(x1, mk, mv, g_x, w_q, w_xo, g_moe, wrt, br):
    b, s, d = x1.shape
    n_mem = mk.shape[1]
    tile = ATT_TILE
    nq = s // tile
    full = lambda shape: pl.BlockSpec(shape, lambda i, j: (0,) * len(shape))
    tok = pl.BlockSpec((None, tile, d), lambda i, j: (i, j, 0))
    mem = pl.BlockSpec((None, n_mem, d), lambda i, j: (i, 0, 0))
    lanes = pl.BlockSpec((TOP_K, tile), lambda i, j: (0, i * nq + j))
    return pl.pallas_call(
        _attn_prompt_kernel,
        grid=(b, nq),
        in_specs=[tok, mem, mem, full((1, d)), full((d, d)), full((d, d)), full((1, d)),
                  full(wrt.shape), full(br.shape)],
        out_specs=[tok, pl.BlockSpec((None, tile, d // 2), lambda i, j: (i, j, 0)), lanes, lanes],
        out_shape=[jax.ShapeDtypeStruct((b, s, d), F32), jax.ShapeDtypeStruct((b, s, d // 2), U32),
                   jax.ShapeDtypeStruct((TOP_K, b * s), I32), jax.ShapeDtypeStruct((TOP_K, b * s), F32)],
        compiler_params=_cparams(("arbitrary", "arbitrary")),
        name="attn_prompt",
    )(x1, mk, mv, g_x, w_q, w_xo, g_moe, wrt, br)


def _attn_sample_kernel(q_ref, k_ref, v_ref, o_ref, *, group):
    hd = q_ref.shape[2]
    for g in range(group):
        q = q_ref[g]
        s = jnp.sum(k_ref[g] * q[None], axis=-1, keepdims=True) * (hd ** -0.5)
        e = jnp.exp(s - jnp.max(s, axis=0, keepdims=True))
        p = e / jnp.sum(e, axis=0, keepdims=True)
        o_ref[g] = jnp.sum(p * v_ref[g], axis=0)


def _attn_sample(q, ck, cv):
    n, heads, hd = q.shape
    n_mem = ck.shape[1]
    group = SAMPLE_ATT_GROUP
    qo = pl.BlockSpec((group, heads, hd), lambda i: (i, 0, 0))
    kv = pl.BlockSpec((group, n_mem, heads, hd), lambda i: (i, 0, 0, 0))
    return pl.pallas_call(
        functools.partial(_attn_sample_kernel, group=group),
        grid=(n // group,),
        in_specs=[qo, kv, kv],
        out_specs=qo,
        out_shape=jax.ShapeDtypeStruct((n, heads, hd), F32),
        compiler_params=_cparams(("arbitrary",)),
        name="attn_sample",
    )(q, ck, cv)


def _tail_sample_kernel(x1_ref, o_ref, wxo_ref, gmoe_ref, wrt_ref, br_ref,
                        x2_ref, h3_ref, tope_ref, gates_ref):
    x2 = x1_ref[...] + _dot(o_ref[...].astype(BF16), wxo_ref[...])
    x2_ref[...] = x2
    _router_tail(x2, gmoe_ref[...], wrt_ref[...], br_ref[...], h3_ref, tope_ref, gates_ref)


def _tail_sample(x1, o, w_xo, g_moe, wrt, br):
    n, d = x1.shape
    return pl.pallas_call(
        _tail_sample_kernel,
        out_shape=[jax.ShapeDtypeStruct((n, d), F32), jax.ShapeDtypeStruct((n, d // 2), U32),
                   jax.ShapeDtypeStruct((TOP_K, n), I32), jax.ShapeDtypeStruct((TOP_K, n), F32)],
        compiler_params=_cparams(),
        name="tail_sample",
    )(x1, o, w_xo, g_moe, wrt, br)


def _route_kernel(tope_ref, dest_ref, cnt_ref, *, n_blocks, tile, moe_block):
    e_idx = lax.broadcasted_iota(I32, (N_EXPERTS, tile), 0)
    earlier = jnp.where(lax.broadcasted_iota(I32, (tile, tile), 0)
                        < lax.broadcasted_iota(I32, (tile, tile), 1), 1.0, 0.0).astype(BF16)

    def onehot(k, off):
        return e_idx == tope_ref[pl.ds(k, 1), pl.ds(off, tile)]

    def count_body(j, cnt):
        off = pl.multiple_of(j * tile, tile)
        for k in range(TOP_K):
            cnt = cnt + jnp.sum(jnp.where(onehot(k, off), 1.0, 0.0), axis=1, keepdims=True)
        return cnt

    counts = lax.fori_loop(0, n_blocks, count_body, jnp.zeros((N_EXPERTS, 1), F32))
    cnt_ref[...] = jnp.broadcast_to(counts, cnt_ref.shape).astype(I32)

    n_blk = jnp.floor((counts + (moe_block - 1)) * (1.0 / moe_block))
    n_hi = jnp.floor(n_blk * (1.0 / 16.0))
    n_lo = n_blk - 16.0 * n_hi
    below = jnp.where(lax.broadcasted_iota(I32, (N_EXPERTS, N_EXPERTS), 1)
                      < lax.broadcasted_iota(I32, (N_EXPERTS, N_EXPERTS), 0), 1.0, 0.0).astype(BF16)
    wide = lambda c: jnp.broadcast_to(c, (N_EXPERTS, 128)).astype(BF16)
    start = (16.0 * _dot(below, wide(n_hi)) + _dot(below, wide(n_lo)))[:, 0:1] * float(moe_block)

    def dest_body(j, run):
        off = pl.multiple_of(j * tile, tile)
        for k in range(TOP_K):
            oh = onehot(k, off)
            ohf = jnp.where(oh, 1.0, 0.0)
            before = _dot(ohf.astype(BF16), earlier)
            dest = jnp.sum(jnp.where(oh, before + run, 0.0), axis=0, keepdims=True)
            dest_ref[pl.ds(k, 1), pl.ds(off, tile)] = dest.astype(I32)
            run = run + jnp.sum(ohf, axis=1, keepdims=True)
        return run

    lax.fori_loop(0, n_blocks, dest_body, start)


def _route(tope_pad):
    t_pad = tope_pad.shape[1]
    kern = functools.partial(_route_kernel, n_blocks=t_pad // ROUTE_TILE, tile=ROUTE_TILE,
                             moe_block=MOE_BLOCK)
    return pl.pallas_call(
        kern,
        out_shape=[jax.ShapeDtypeStruct((TOP_K, t_pad), I32),
                   jax.ShapeDtypeStruct((N_EXPERTS, 128), I32)],
        compiler_params=_cparams(),
        name="route",
    )(tope_pad)


def _dispatch(dest3, h_prompt, h_sample, n_rows):
    w = h_prompt.shape[1]
    n_prompt_chunks = h_prompt.shape[0] // SC_CHUNK
    n_chunks = dest3.shape[0]
    mesh = plsc.VectorSubcoreMesh(core_axis_name="c", subcore_axis_name="s")
    n_workers = mesh.num_cores * mesh.num_subcores

    n_steps = -(-n_chunks // n_workers)

    def body(dest_hbm, hp_hbm, hs_hbm, xs_hbm, idx0, idx1, rows0, rows1, sem):
        wid = lax.axis_index("s") * mesh.num_cores + lax.axis_index("c")
        idx_v, rows_v = (idx0, idx1), (rows0, rows1)

        def chunk(j):
            return wid + j * n_workers

        def loads(j, slot):
            c = chunk(j)
            prompt = (hp_hbm, c)
            sample = (hs_hbm, c - n_prompt_chunks)
            return c, [
                (pltpu.make_async_copy(dest_hbm.at[c], idx_v[slot], sem.at[slot]), None),
                (pltpu.make_async_copy(prompt[0].at[pl.ds(prompt[1] * SC_CHUNK, SC_CHUNK)], rows_v[slot],
                                       sem.at[slot]), c < n_prompt_chunks),
                (pltpu.make_async_copy(sample[0].at[pl.ds(sample[1] * SC_CHUNK, SC_CHUNK)], rows_v[slot],
                                       sem.at[slot]), c >= n_prompt_chunks),
            ]

        def start_loads(j, slot):
            c, copies = loads(j, slot)
            for cp, cond in copies:
                pl.when((c < n_chunks) if cond is None else ((c < n_chunks) & cond))(cp.start)

        def wait_loads(j, slot):
            c, copies = loads(j, slot)
            for cp, cond in copies:
                pl.when((c < n_chunks) if cond is None else ((c < n_chunks) & cond))(cp.wait)

        def scatters(slot):
            return [pltpu.make_async_copy(rows_v[slot], xs_hbm.at[idx_v[slot].at[k]], sem.at[2 + slot])
                    for k in range(TOP_K)]

        start_loads(0, 0)
        for j in range(n_steps):
            slot = j % 2
            wait_loads(j, slot)
            if j >= 1:
                @pl.when(chunk(j - 1) < n_chunks)
                def _():
                    for cp in scatters(1 - slot):
                        cp.wait()
            if j + 1 < n_steps:
                start_loads(j + 1, 1 - slot)

            @pl.when(chunk(j) < n_chunks)
            def _():
                for cp in scatters(slot):
                    cp.start()

        @pl.when(chunk(n_steps - 1) < n_chunks)
        def _():
            for cp in scatters((n_steps - 1) % 2):
                cp.wait()

    return pl.kernel(
        body,
        out_type=jax.ShapeDtypeStruct((n_rows, w), U32),
        mesh=mesh,
        scratch_types=[pltpu.VMEM((TOP_K, SC_CHUNK), I32), pltpu.VMEM((TOP_K, SC_CHUNK), I32),
                       pltpu.VMEM((SC_CHUNK, w), U32), pltpu.VMEM((SC_CHUNK, w), U32),
                       pltpu.SemaphoreType.DMA((4,))],
        name="dispatch_sc",
    )(dest3, h_prompt, h_sample)


def _ffn_kernel(start_ref, nblk_ref, count_ref, b1_ref, b2_ref, xs_hbm, w1_hbm, w2_hbm, y_hbm,
                w1f, w2f, w1b, w2b, xbuf, ybuf, sem_w, sem_x, sem_y, *, blk):
    e = pl.program_id(0)
    n_e = pl.num_programs(0)
    nb = nblk_ref[e]
    base = start_ref[e]
    wslot = e % 2

    def w_copies(expert, slot):
        return (pltpu.make_async_copy(w1_hbm.at[expert], w1f.at[slot], sem_w.at[slot]),
                pltpu.make_async_copy(w2_hbm.at[expert], w2f.at[slot], sem_w.at[slot]))

    def x_copy(row0, slot):
        return pltpu.make_async_copy(xs_hbm.at[pl.ds(pl.multiple_of(row0, blk), blk)], xbuf.at[slot],
                                     sem_x.at[slot])

    def y_copy(row0, slot):
        return pltpu.make_async_copy(ybuf.at[slot], y_hbm.at[pl.ds(pl.multiple_of(row0, blk), blk)],
                                     sem_y.at[slot])

    @pl.when(e == 0)
    def _():
        for cp in w_copies(0, 0):
            cp.start()

        @pl.when(nb > 0)
        def _():
            x_copy(base, 0).start()

    for cp in w_copies(e, wslot):
        cp.wait()

    @pl.when(e + 1 < n_e)
    def _():
        for cp in w_copies(e + 1, 1 - wslot):
            cp.start(priority=1)

    w1b[...] = w1f[wslot].astype(BF16)
    w2b[...] = w2f[wslot].astype(BF16)
    b1 = b1_ref[e]
    b2 = b2_ref[e]

    def block(j, slot):
        x_copy(base, slot).wait()

        @pl.when(j + 1 < nb)
        def _():
            x_copy(base + (j + 1) * blk, 1 - slot).start()

        @pl.when(j >= 2)
        def _():
            y_copy(base, slot).wait()

        w = xbuf[slot]
        row = lax.broadcasted_iota(I32, w.shape, 0)
        w = jnp.where(row < count_ref[e] - j * blk, w, jnp.uint32(0))
        half = w.shape[1]
        d_e = w2b.shape[0]
        lo, hi = _unpack_bf16_pairs(w)
        gu = _dot(lo.astype(BF16), w1b[0:half, :]) + _dot(hi.astype(BF16), w1b[half:, :]) + b1
        gate = jnp.minimum(gu[:, :d_e], SWIGLU_LIMIT)
        up = jnp.clip(gu[:, d_e:], -SWIGLU_LIMIT, SWIGLU_LIMIT)
        glu = gate * jax.nn.sigmoid(gate * SWIGLU_ALPHA)
        act = ((up + 1.0) * glu).astype(BF16)
        ybuf[slot] = _pack_bf16_pairs(_dot(act, w2b[...]) + b2)
        y_copy(base + j * blk, slot).start()

    def pair(j2, carry):
        j = 2 * j2
        block(j, 0)

        @pl.when(j + 1 < nb)
        def _():
            block(j + 1, 1)

        return carry

    lax.fori_loop(0, (nb + 1) // 2, pair, 0)

    @pl.when(nb >= 2)
    def _():
        y_copy(base, 0).wait()
        y_copy(base, 1).wait()

    @pl.when(nb == 1)
    def _():
        y_copy(base, 0).wait()

    nxt = jnp.minimum(e + 1, n_e - 1)

    @pl.when((e + 1 < n_e) & (nblk_ref[nxt] > 0))
    def _():
        x_copy(start_ref[nxt], 0).start()


def _expert_ffn(grp_start, grp_blocks, grp_count, xs, w1, b1, w2, b2):
    n_rows, half = xs.shape
    n_e, d, d2 = w1.shape
    d_e = w2.shape[1]
    blk = MOE_BLOCK
    whole = lambda shape: pl.BlockSpec(shape, lambda e, s, n, c: (0,) * len(shape))
    hbm = pl.BlockSpec(memory_space=pl.ANY)
    grid_spec = pltpu.PrefetchScalarGridSpec(
        num_scalar_prefetch=3,
        grid=(n_e,),
        in_specs=[whole((n_e, 1, d2)), whole((n_e, 1, d)), hbm, hbm, hbm],
        out_specs=hbm,
        scratch_shapes=[pltpu.VMEM((2, d, d2), F32), pltpu.VMEM((2, d_e, d), F32),
                        pltpu.VMEM((d, d2), BF16), pltpu.VMEM((d_e, d), BF16),
                        pltpu.VMEM((2, blk, half), U32), pltpu.VMEM((2, blk, d // 2), U32),
                        pltpu.SemaphoreType.DMA((2,)), pltpu.SemaphoreType.DMA((2,)),
                        pltpu.SemaphoreType.DMA((2,))],
    )
    return pl.pallas_call(
        functools.partial(_ffn_kernel, blk=blk),
        grid_spec=grid_spec,
        out_shape=jax.ShapeDtypeStruct((n_rows, d // 2), U32),
        compiler_params=_cparams(("arbitrary",)),
        name="expert_ffn",
    )(grp_start, grp_blocks, grp_count, b1.reshape(n_e, 1, d2), b2.reshape(n_e, 1, d), xs, w1, w2)


def _undispatch(dest3, y_rows):
    n_chunks = dest3.shape[0]
    w = y_rows.shape[1]
    mesh = plsc.VectorSubcoreMesh(core_axis_name="c", subcore_axis_name="s")
    n_workers = mesh.num_cores * mesh.num_subcores
    n_steps = -(-n_chunks // n_workers)

    def body(dest_hbm, y_hbm, out_hbm, idx0, idx1, rows0, rows1, sem):
        wid = lax.axis_index("s") * mesh.num_cores + lax.axis_index("c")
        idx_v, rows_v = (idx0, idx1), (rows0, rows1)

        def chunk(j):
            return wid + j * n_workers

        def gather(j, k, slot):
            return pltpu.make_async_copy(y_hbm.at[idx_v[j % 2].at[k]], rows_v[slot], sem.at[slot])

        def store(j, k, slot):
            return pltpu.make_async_copy(rows_v[slot], out_hbm.at[k, pl.ds(chunk(j) * SC_CHUNK, SC_CHUNK)],
                                         sem.at[2 + slot])

        units = [(j, k) for j in range(n_steps) for k in range(TOP_K)]

        def valid(j):
            return chunk(j) < n_chunks

        def load_idx(j):
            pl.when(valid(j))(lambda: pltpu.sync_copy(dest_hbm.at[chunk(j)], idx_v[j % 2]))

        load_idx(0)
        pl.when(valid(0))(gather(0, 0, 0).start)
        for u, (j, k) in enumerate(units):
            slot = u % 2
            pl.when(valid(j))(gather(j, k, slot).wait)
            if u + 1 < len(units):
                jn, kn = units[u + 1]
                if u >= 1:
                    jp, kp = units[u - 1]
                    pl.when(valid(jp))(store(jp, kp, 1 - slot).wait)
                if kn == 0:
                    load_idx(jn)
                pl.when(valid(jn))(gather(jn, kn, 1 - slot).start)
            pl.when(valid(j))(store(j, k, slot).start)
        for u in (len(units) - 2, len(units) - 1):
            j, k = units[u]
            pl.when(valid(j))(store(j, k, u % 2).wait)

    return pl.kernel(
        body,
        out_type=jax.ShapeDtypeStruct((TOP_K, n_chunks * SC_CHUNK, w), y_rows.dtype),
        mesh=mesh,
        scratch_types=[pltpu.VMEM((TOP_K, SC_CHUNK), I32), pltpu.VMEM((TOP_K, SC_CHUNK), I32),
                       pltpu.VMEM((SC_CHUNK, w), y_rows.dtype), pltpu.VMEM((SC_CHUNK, w), y_rows.dtype),
                       pltpu.SemaphoreType.DMA((4,))],
        name="undispatch_sc",
    )(dest3, y_rows)


def _combine_kernel(x2_ref, y4_ref, gates_ref, gf_ref, out_ref, *, tile):
    g = jnp.concatenate([gates_ref[...], jnp.zeros((128 - TOP_K, tile), F32)], axis=0)
    gt = g.T
    half = y4_ref.shape[2]
    moe_lo = jnp.zeros((tile, half), F32)
    moe_hi = jnp.zeros((tile, half), F32)
    for k in range(TOP_K):
        lo, hi = _unpack_bf16_pairs(y4_ref[k])
        moe_lo = moe_lo + gt[:, k:k + 1] * lo
        moe_hi = moe_hi + gt[:, k:k + 1] * hi
    x2 = x2_ref[...]
    out_ref[...] = _rms(x2 + jnp.concatenate([moe_lo, moe_hi], axis=1), gf_ref[...])


def _combine(x2, y4, gates, g_final, tile, first_tile):
    n, d = x2.shape
    return pl.pallas_call(
        functools.partial(_combine_kernel, tile=tile),
        grid=(n // tile,),
        in_specs=[
            pl.BlockSpec((tile, d), lambda i: (i, 0)),
            pl.BlockSpec((TOP_K, tile, d // 2), lambda i: (0, i + first_tile, 0)),
            pl.BlockSpec((TOP_K, tile), lambda i: (0, i)),
            pl.BlockSpec((1, d), lambda i: (0, 0)),
        ],
        out_specs=pl.BlockSpec((tile, d), lambda i: (i, 0)),
        out_shape=jax.ShapeDtypeStruct((n, d), F32),
        compiler_params=_cparams(("arbitrary",)),
        name="combine",
    )(x2, y4, gates, g_final)


def kernel(x_prompt, x_sample, mem_prompt, cache_mem_k, cache_mem_v, state_conv, g_mix, w_in, g_v,
           w_spatial, b_spatial, w_conv, w_out, g_xattn, g_mem, w_q, w_k, w_v, w_xo, g_moe, w_router,
           b_router, w_gate_up, b_gate_up, w_down, b_down, g_final):
    depth = g_mix.shape[0]
    assert depth == 1, "one layer supported"
    b, s, d = x_prompt.shape
    nb, ns, _ = x_sample.shape
    assert ns == 1
    n_mem = mem_prompt.shape[1]
    a_heads, a_hd = g_v.shape[1], g_v.shape[2]
    a_width = a_heads * a_hd
    b_width = w_conv.shape[2]
    assert a_heads == A_HEADS and 2 * a_hd == CHUNK and w_spatial.shape[2] == CHUNK
    assert s % MIX_TILE == 0 and s % ATT_TILE == 0 and (b * n_mem) % KV_TILE == 0
    assert nb % SAMPLE_ATT_GROUP == 0
    l = 0
    row = lambda a: a.reshape(1, -1)

    w_in_b = w_in[l].astype(BF16)
    w_out_b = w_out[l].astype(BF16)
    w_q_b = w_q[l].astype(BF16)
    w_k_b = w_k[l].astype(BF16)
    w_v_b = w_v[l].astype(BF16)
    w_xo_b = w_xo[l].astype(BF16)
    wrt_b = w_router[l].T.astype(BF16)
    br_col = b_router[l].reshape(N_EXPERTS, 1).astype(F32)
    gv_row = row(g_v[l])
    ws = w_spatial[l]
    ws_pairs = jnp.concatenate([ws[0::2], ws[1::2]], axis=2)
    bs_full = jnp.repeat(b_spatial[l].T, a_hd, axis=1)
    w00 = row(jnp.repeat(ws[:, 0, 0], a_hd))
    b0 = row(jnp.repeat(b_spatial[l][:, 0], a_hd))

    mk2, mv2 = _mem_kv(mem_prompt.reshape(b * n_mem, d), row(g_mem[l]), w_k_b, w_v_b)
    x1_p, conv_p = _mixer_prompt(x_prompt, row(g_mix[l]), w_in_b, gv_row, ws_pairs, bs_full,
                                 w_conv[l], w_out_b)
    x2_p, h3_p, tope_p, gates_p = _attn_prompt(
        x1_p, mk2.reshape(b, n_mem, d), mv2.reshape(b, n_mem, d), row(g_xattn[l]), w_q_b, w_xo_b,
        row(g_moe[l]), wrt_b, br_col)

    xs2 = x_sample.reshape(nb, d)
    st = state_conv[l]
    x1_s, v_s, z_s, q_s = _mixer_sample(xs2, st[:, 0], st[:, 1], row(g_mix[l]), w_in_b, gv_row, w00, b0,
                                        w_conv[l], w_out_b, row(g_xattn[l]), w_q_b)
    x_heads, x_hd = cache_mem_k.shape[3], cache_mem_k.shape[4]
    assert depth == 1 and x_heads == X_HEADS
    o_s = _attn_sample(q_s.reshape(nb, x_heads, x_hd), cache_mem_k.reshape(nb, n_mem, x_heads, x_hd),
                       cache_mem_v.reshape(nb, n_mem, x_heads, x_hd)).reshape(nb, d)
    x2_s, h3_s, tope_s, gates_s = _tail_sample(x1_s, o_s, w_xo_b, row(g_moe[l]), wrt_b, br_col)

    t_p = b * s
    t_all = t_p + nb
    t_pad = -(-t_all // ROUTE_TILE) * ROUTE_TILE
    tope_all = jnp.concatenate([tope_p, tope_s, jnp.full((TOP_K, t_pad - t_all), -1, I32)], axis=1)
    dest, cnt = _route(tope_all)
    counts = cnt[:, 0]
    padded = (counts + MOE_BLOCK - 1) // MOE_BLOCK * MOE_BLOCK
    pad_ends = jnp.cumsum(padded)
    pad_starts = pad_ends - padded
    n_blocks = -(-(t_all * TOP_K) // MOE_BLOCK) + N_EXPERTS
    n_rows = n_blocks * MOE_BLOCK
    assert t_all % SC_CHUNK == 0 and t_p % SC_CHUNK == 0
    dest3 = dest[:, :t_all].reshape(TOP_K, t_all // SC_CHUNK, SC_CHUNK).transpose(1, 0, 2)
    xs_rows = _dispatch(dest3, h3_p.reshape(t_p, d // 2), h3_s, n_rows)
    y_rows = _expert_ffn(pad_starts.astype(I32), (padded // MOE_BLOCK).astype(I32), counts.astype(I32),
                         xs_rows, w_gate_up[l], b_gate_up[l], w_down[l], b_down[l])

    y4 = _undispatch(dest3, y_rows)

    gf = row(g_final)
    assert t_p % COMBINE_TILE == 0 and t_p % nb == 0
    y_p = _combine(x2_p.reshape(t_p, d), y4, gates_p, gf, COMBINE_TILE, 0)
    y_s = _combine(x2_s, y4, gates_s, gf, nb, t_p // nb)

    x_heads = cache_mem_k.shape[3]
    return (y_p.reshape(b, s, d),
            y_s.reshape(nb, 1, d),
            mk2.reshape(1, b, n_mem, x_heads, d // x_heads),
            mv2.reshape(1, b, n_mem, x_heads, d // x_heads),
            conv_p.reshape(1, b, CONV_W - 1, b_width),
            jnp.stack([st[:, 1], z_s], axis=1).reshape(1, nb, CONV_W - 1, b_width),
            v_s.reshape(1, nb, 1, a_heads, a_hd))
```

```python
import functools

import jax
import jax.numpy as jnp
from jax import lax
from jax.experimental import pallas as pl
from jax.experimental.pallas import tpu as pltpu
from jax.experimental.pallas import tpu_sc as plsc

F32 = jnp.float32
BF16 = jnp.bfloat16
I32 = jnp.int32
U32 = jnp.uint32

A_HEADS = 8
CHUNK = 128
CONV_W = 3
X_HEADS = 4
N_EXPERTS = 32
TOP_K = 4
SWIGLU_LIMIT = 7.0
SWIGLU_ALPHA = 1.702
EPS = 1e-5

MIX_TILE = 1024
ATT_TILE = 1024
KV_TILE = 512
ROUTE_TILE = 512
MOE_BLOCK = 256
COMBINE_TILE = 512
SAMPLE_ATT_GROUP = 4
SC_CHUNK = 64
VMEM_LIMIT = 56 * 1024 * 1024


def _cparams(sem=None):
    return pltpu.CompilerParams(dimension_semantics=sem, vmem_limit_bytes=VMEM_LIMIT)


def _rms(x, g):
    r = lax.rsqrt(jnp.mean(x * x, axis=-1, keepdims=True) + EPS)
    return (x * r) * g


def _gelu(x):
    return 0.5 * x * (1.0 + lax.erf(x * 0.7071067811865476))


def _dot(a, b):
    return jnp.dot(a, b, preferred_element_type=F32)


def _dot_nt(a, b):
    return lax.dot_general(a, b, (((1,), (1,)), ((), ())), preferred_element_type=F32)


def _pack_bf16_pairs(x):
    d = x.shape[1]
    r = x.astype(BF16).astype(F32)
    lo = lax.shift_right_logical(pltpu.bitcast(r[:, :d // 2], U32), jnp.uint32(16))
    hi = lax.bitwise_and(pltpu.bitcast(r[:, d // 2:], U32), jnp.uint32(0xFFFF0000))
    return lax.bitwise_or(lo, hi)


def _unpack_bf16_pairs(w):
    lo = pltpu.bitcast(lax.shift_left(w, jnp.uint32(16)), F32)
    hi = pltpu.bitcast(lax.bitwise_and(w, jnp.uint32(0xFFFF0000)), F32)
    return lo, hi


def _memkv_kernel(m_ref, g_ref, wk_ref, wv_ref, k_ref, v_ref):
    h = _rms(m_ref[...], g_ref[...]).astype(BF16)
    k_ref[...] = _dot(h, wk_ref[...])
    v_ref[...] = _dot(h, wv_ref[...])


def _mem_kv(mem2d, g_mem, wk, wv):
    n, d = mem2d.shape
    row = pl.BlockSpec((KV_TILE, d), lambda i: (i, 0))
    full = lambda shape: pl.BlockSpec(shape, lambda i: (0,) * len(shape))
    return pl.pallas_call(
        _memkv_kernel,
        grid=(n // KV_TILE,),
        in_specs=[row, full((1, d)), full((d, d)), full((d, d))],
        out_specs=[row, row],
        out_shape=[jax.ShapeDtypeStruct((n, d), F32)] * 2,
        compiler_params=_cparams(("arbitrary",)),
        name="mem_kv",
    )(mem2d, g_mem, wk, wv)


def _head_rms(v, gv, a_width):
    hd = a_width // A_HEADS
    r_i = lax.broadcasted_iota(I32, (a_width, a_width), 0) // hd
    c_i = lax.broadcasted_iota(I32, (a_width, a_width), 1) // hd
    ones_bd = jnp.where(r_i == c_i, 1.0, 0.0).astype(BF16)
    sq = v * v
    sq_hi = sq.astype(BF16)
    sq_lo = (sq - sq_hi.astype(F32)).astype(BF16)
    gs = _dot(sq_hi, ones_bd) + _dot(sq_lo, ones_bd)
    return (v * lax.rsqrt(gs * (1.0 / hd) + EPS)) * gv


def _mixer_prompt_kernel(x_ref, gmix_ref, win_ref, gv_ref, ws_ref, bs_ref, wc_ref, wout_ref,
                         x1_ref, cs_ref, zbuf, *, tile, a_width, b_width):
    j = pl.program_id(1)
    nj = pl.num_programs(1)
    x = x_ref[...]
    h = _rms(x, gmix_ref[...]).astype(BF16)
    proj = _dot(h, win_ref[...])
    u = _gelu(proj[:, :a_width])
    v = _head_rms(_gelu(proj[:, a_width:2 * a_width]), gv_ref[...], a_width)
    o = 2 * a_width
    hb = proj[:, o:o + b_width]
    bg = proj[:, o + b_width:o + 2 * b_width]
    cg = proj[:, o + 2 * b_width:o + 3 * b_width]

    lane = lax.broadcasted_iota(I32, (CHUNK, 2 * (a_width // A_HEADS)), 1)
    first = lane < (a_width // A_HEADS)
    t_i = lax.broadcasted_iota(I32, (CHUNK, 2 * CHUNK), 0)
    s_i = lax.broadcasted_iota(I32, (CHUNK, 2 * CHUNK), 1) % CHUNK
    causal = s_i <= t_i
    pair_cols = []
    for p in range(A_HEADS // 2):
        w_pair = jnp.where(causal, ws_ref[p], 0.0).astype(BF16)
        vp = v[:, p * CHUNK:(p + 1) * CHUNK]
        rows = []
        for c in range(tile // CHUNK):
            vc = vp[c * CHUNK:(c + 1) * CHUNK]
            rhs = jnp.concatenate([jnp.where(first, vc, 0.0), jnp.where(first, 0.0, vc)],
                                  axis=0).astype(BF16)
            rows.append(_dot(w_pair, rhs))
        pair_cols.append(jnp.concatenate(rows, axis=0))
    gate = jnp.concatenate(pair_cols, axis=1)
    bias = jnp.concatenate([bs_ref[...]] * (tile // CHUNK), axis=0)
    a_out = u * (gate + bias)

    @pl.when(j == 0)
    def _():
        zbuf[0:8, :] = jnp.zeros((8, b_width), F32)

    z = cg * hb
    zbuf[8:tile + 8, :] = z
    z1 = zbuf[7:tile + 7, :]
    z2 = zbuf[6:tile + 6, :]
    wc = wc_ref[...]
    conv = z2 * wc[0:1] + z1 * wc[1:2] + z * wc[2:3]
    b_out = bg * conv
    tail = zbuf[tile:tile + 8, :]
    zbuf[0:8, :] = tail

    @pl.when(j == nj - 1)
    def _():
        cs_ref[...] = tail[8 - (CONV_W - 1):, :]

    y = _dot(a_out.astype(BF16), wout_ref[0:a_width, :]) + _dot(b_out.astype(BF16), wout_ref[a_width:, :])
    x1_ref[...] = x + y


def _mixer_prompt(x, g_mix, w_in, g_v, ws_pairs, bs_full, w_conv, w_out):
    b, s, d = x.shape
    a_width = g_v.shape[1]
    b_width = w_conv.shape[1]
    in_width = w_in.shape[1]
    tile = MIX_TILE
    full = lambda shape: pl.BlockSpec(shape, lambda i, j: (0,) * len(shape))
    kern = functools.partial(_mixer_prompt_kernel, tile=tile, a_width=a_width, b_width=b_width)
    return pl.pallas_call(
        kern,
        grid=(b, s // tile),
        in_specs=[
            pl.BlockSpec((None, tile, d), lambda i, j: (i, j, 0)),
            full((1, d)), full((d, in_width)), full((1, a_width)),
            full(ws_pairs.shape), full(bs_full.shape), full(w_conv.shape), full(w_out.shape),
        ],
        out_specs=[
            pl.BlockSpec((None, tile, d), lambda i, j: (i, j, 0)),
            pl.BlockSpec((None, CONV_W - 1, b_width), lambda i, j: (i, 0, 0)),
        ],
        out_shape=[jax.ShapeDtypeStruct((b, s, d), F32),
                   jax.ShapeDtypeStruct((b, CONV_W - 1, b_width), F32)],
        scratch_shapes=[pltpu.VMEM((tile + 8, b_width), F32)],
        compiler_params=_cparams(("arbitrary", "arbitrary")),
        name="mixer_prompt",
    )(x, g_mix, w_in, g_v, ws_pairs, bs_full, w_conv, w_out)


def _mixer_sample_kernel(x_ref, s0_ref, s1_ref, gmix_ref, win_ref, gv_ref, w00_ref, b0_ref, wc_ref,
                         wout_ref, gx_ref, wq_ref, x1_ref, v_ref, z_ref, q_ref, *, a_width, b_width):
    x = x_ref[...]
    h = _rms(x, gmix_ref[...]).astype(BF16)
    proj = _dot(h, win_ref[...])
    u = _gelu(proj[:, :a_width])
    v = _head_rms(_gelu(proj[:, a_width:2 * a_width]), gv_ref[...], a_width)
    v_ref[...] = v
    o = 2 * a_width
    hb = proj[:, o:o + b_width]
    bg = proj[:, o + b_width:o + 2 * b_width]
    cg = proj[:, o + 2 * b_width:o + 3 * b_width]
    a_out = u * (v * w00_ref[...] + b0_ref[...])
    z = cg * hb
    z_ref[...] = z
    wc = wc_ref[...]
    conv = s0_ref[...] * wc[0:1] + s1_ref[...] * wc[1:2] + z * wc[2:3]
    b_out = bg * conv
    y = _dot(a_out.astype(BF16), wout_ref[0:a_width, :]) + _dot(b_out.astype(BF16), wout_ref[a_width:, :])
    x1 = x + y
    x1_ref[...] = x1
    q_ref[...] = _dot(_rms(x1, gx_ref[...]).astype(BF16), wq_ref[...])


def _mixer_sample(x, s0, s1, g_mix, w_in, g_v, w00, b0, w_conv, w_out, g_x, w_q):
    n, d = x.shape
    a_width = g_v.shape[1]
    b_width = w_conv.shape[1]
    kern = functools.partial(_mixer_sample_kernel, a_width=a_width, b_width=b_width)
    return pl.pallas_call(
        kern,
        out_shape=[jax.ShapeDtypeStruct((n, d), F32), jax.ShapeDtypeStruct((n, a_width), F32),
                   jax.ShapeDtypeStruct((n, b_width), F32), jax.ShapeDtypeStruct((n, d), F32)],
        compiler_params=_cparams(),
        name="mixer_sample",
    )(x, s0, s1, g_mix, w_in, g_v, w00, b0, w_conv, w_out, g_x, w_q)


def _router_tail(x2, gmoe, wrt, br, h3_ref, tope_ref, gates_ref):
    h3f = _rms(x2, gmoe)
    h3 = h3f.astype(BF16)
    h3_ref[...] = _pack_bf16_pairs(h3f)

    logits = _dot_nt(wrt, h3) + br
    n_e, n_t = logits.shape
    e_idx = lax.broadcasted_iota(I32, (n_e, n_t), 0).astype(F32)
    tops, idxs = [], []
    for _ in range(TOP_K):
        m = jnp.max(logits, axis=0, keepdims=True)
        idx = jnp.min(jnp.where(logits == m, e_idx, float(n_e)), axis=0, keepdims=True)
        tops.append(m)
        idxs.append(idx)
        logits = jnp.where(e_idx == idx, -jnp.inf, logits)
    top = jnp.concatenate(tops, axis=0)
    ex = jnp.exp(top - top[0:1])
    gates_ref[...] = ex / jnp.sum(ex, axis=0, keepdims=True)
    tope_ref[...] = jnp.concatenate(idxs, axis=0).astype(I32)


def _attn_prompt_kernel(x1_ref, k_ref, v_ref, gx_ref, wq_ref, wxo_ref, gmoe_ref, wrt_ref, br_ref,
                        x2_ref, h3_ref, tope_ref, gates_ref):
    x = x1_ref[...]
    d = x.shape[1]
    hd = d // X_HEADS
    q = _dot(_rms(x, gx_ref[...]).astype(BF16), wq_ref[...]).astype(BF16)
    kb = k_ref[...].astype(BF16)
    vb = v_ref[...].astype(BF16)
    outs = []
    for hh in range(X_HEADS):
        sl = slice(hh * hd, (hh + 1) * hd)
        s = _dot_nt(q[:, sl], kb[:, sl]) * (hd ** -0.5)
        e = jnp.exp(s - jnp.max(s, axis=-1, keepdims=True))
        p = e / jnp.sum(e, axis=-1, keepdims=True)
        outs.append(_dot(p.astype(BF16), vb[:, sl]))
    o = jnp.concatenate(outs, axis=1).astype(BF16)
    x2 = x + _dot(o, wxo_ref[...])
    x2_ref[...] = x2
    _router_tail(x2, gmoe_ref[...], wrt_ref[...], br_ref[...], h3_ref, tope_ref, gates_ref)


def _attn_prompt(x1, mk, mv, g_x, w_q, w_xo, g_moe, wrt, br):
    b, s, d = x1.shape
    n_mem = mk.shape[1]
    tile = ATT_TILE
    nq = s // tile
    full = lambda shape: pl.BlockSpec(shape, lambda i, j: (0,) * len(shape))
    tok = pl.BlockSpec((None, tile, d), lambda i, j: (i, j, 0))
    mem = pl.BlockSpec((None, n_mem, d), lambda i, j: (i, 0, 0))
    lanes = pl.BlockSpec((TOP_K, tile), lambda i, j: (0, i * nq + j))
    return pl.pallas_call(
        _attn_prompt_kernel,
        grid=(b, nq),
        in_specs=[tok, mem, mem, full((1, d)), full((d, d)), full((d, d)), full((1, d)),
                  full(wrt.shape), full(br.shape)],
        out_specs=[tok, pl.BlockSpec((None, tile, d // 2), lambda i, j: (i, j, 0)), lanes, lanes],
        out_shape=[jax.ShapeDtypeStruct((b, s, d), F32), jax.ShapeDtypeStruct((b, s, d // 2), U32),
                   jax.ShapeDtypeStruct((TOP_K, b * s), I32), jax.ShapeDtypeStruct((TOP_K, b * s), F32)],
        compiler_params=_cparams(("arbitrary", "arbitrary")),
        name="attn_prompt",
    )(x1, mk, mv, g_x, w_q, w_xo, g_moe, wrt, br)


def _attn_sample_kernel(q_ref, k_ref, v_ref, o_ref, *, group):
    hd = q_ref.shape[2]
    for g in range(group):
        q = q_ref[g]
        s = jnp.sum(k_ref[g] * q[None], axis=-1, keepdims=True) * (hd ** -0.5)
        e = jnp.exp(s - jnp.max(s, axis=0, keepdims=True))
        p = e / jnp.sum(e, axis=0, keepdims=True)
        o_ref[g] = jnp.sum(p * v_ref[g], axis=0)


def _attn_sample(q, ck, cv):
    n, heads, hd = q.shape
    n_mem = ck.shape[1]
    group = SAMPLE_ATT_GROUP
    qo = pl.BlockSpec((group, heads, hd), lambda i: (i, 0, 0))
    kv = pl.BlockSpec((group, n_mem, heads, hd), lambda i: (i, 0, 0, 0))
    return pl.pallas_call(
        functools.partial(_attn_sample_kernel, group=group),
        grid=(n // group,),
        in_specs=[qo, kv, kv],
        out_specs=qo,
        out_shape=jax.ShapeDtypeStruct((n, heads, hd), F32),
        compiler_params=_cparams(("arbitrary",)),
        name="attn_sample",
    )(q, ck, cv)


def _tail_sample_kernel(x1_ref, o_ref, wxo_ref, gmoe_ref, wrt_ref, br_ref,
                        x2_ref, h3_ref, tope_ref, gates_ref):
    x2 = x1_ref[...] + _dot(o_ref[...].astype(BF16), wxo_ref[...])
    x2_ref[...] = x2
    _router_tail(x2, gmoe_ref[...], wrt_ref[...], br_ref[...], h3_ref, tope_ref, gates_ref)


def _tail_sample(x1, o, w_xo, g_moe, wrt, br):
    n, d = x1.shape
    return pl.pallas_call(
        _tail_sample_kernel,
        out_shape=[jax.ShapeDtypeStruct((n, d), F32), jax.ShapeDtypeStruct((n, d // 2), U32),
                   jax.ShapeDtypeStruct((TOP_K, n), I32), jax.ShapeDtypeStruct((TOP_K, n), F32)],
        compiler_params=_cparams(),
        name="tail_sample",
    )(x1, o, w_xo, g_moe, wrt, br)


def _route_kernel(tope_ref, dest_ref, cnt_ref, *, n_blocks, tile, moe_block):
    e_idx = lax.broadcasted_iota(I32, (N_EXPERTS, tile), 0)
    earlier = jnp.where(lax.broadcasted_iota(I32, (tile, tile), 0)
                        < lax.broadcasted_iota(I32, (tile, tile), 1), 1.0, 0.0).astype(BF16)

    def onehot(k, off):
        return e_idx == tope_ref[pl.ds(k, 1), pl.ds(off, tile)]

    def count_body(j, cnt):
        off = pl.multiple_of(j * tile, tile)
        for k in range(TOP_K):
            cnt = cnt + jnp.sum(jnp.where(onehot(k, off), 1.0, 0.0), axis=1, keepdims=True)
        return cnt

    counts = lax.fori_loop(0, n_blocks, count_body, jnp.zeros((N_EXPERTS, 1), F32))
    cnt_ref[...] = jnp.broadcast_to(counts, cnt_ref.shape).astype(I32)

    n_blk = jnp.floor((counts + (moe_block - 1)) * (1.0 / moe_block))
    n_hi = jnp.floor(n_blk * (1.0 / 16.0))
    n_lo = n_blk - 16.0 * n_hi
    below = jnp.where(lax.broadcasted_iota(I32, (N_EXPERTS, N_EXPERTS), 1)
                      < lax.broadcasted_iota(I32, (N_EXPERTS, N_EXPERTS), 0), 1.0, 0.0).astype(BF16)
    wide = lambda c: jnp.broadcast_to(c, (N_EXPERTS, 128)).astype(BF16)
    start = (16.0 * _dot(below, wide(n_hi)) + _dot(below, wide(n_lo)))[:, 0:1] * float(moe_block)

    def dest_body(j, run):
        off = pl.multiple_of(j * tile, tile)
        for k in range(TOP_K):
            oh = onehot(k, off)
            ohf = jnp.where(oh, 1.0, 0.0)
            before = _dot(ohf.astype(BF16), earlier)
            dest = jnp.sum(jnp.where(oh, before + run, 0.0), axis=0, keepdims=True)
            dest_ref[pl.ds(k, 1), pl.ds(off, tile)] = dest.astype(I32)
            run = run + jnp.sum(ohf, axis=1, keepdims=True)
        return run

    lax.fori_loop(0, n_blocks, dest_body, start)


def _route(tope_pad):
    t_pad = tope_pad.shape[1]
    kern = functools.partial(_route_kernel, n_blocks=t_pad // ROUTE_TILE, tile=ROUTE_TILE,
                             moe_block=MOE_BLOCK)
    return pl.pallas_call(
        kern,
        out_shape=[jax.ShapeDtypeStruct((TOP_K, t_pad), I32),
                   jax.ShapeDtypeStruct((N_EXPERTS, 128), I32)],
        compiler_params=_cparams(),
        name="route",
    )(tope_pad)


def _dispatch(dest3, h_prompt, h_sample, n_rows):
    w = h_prompt.shape[1]
    n_prompt_chunks = h_prompt.shape[0] // SC_CHUNK
    n_chunks = dest3.shape[0]
    mesh = plsc.VectorSubcoreMesh(core_axis_name="c", subcore_axis_name="s")
    n_workers = mesh.num_cores * mesh.num_subcores

    n_steps = -(-n_chunks // n_workers)

    def body(dest_hbm, hp_hbm, hs_hbm, xs_hbm, idx0, idx1, rows0, rows1, sem):
        wid = lax.axis_index("s") * mesh.num_cores + lax.axis_index("c")
        idx_v, rows_v = (idx0, idx1), (rows0, rows1)

        def chunk(j):
            return wid + j * n_workers

        def loads(j, slot):
            c = chunk(j)
            prompt = (hp_hbm, c)
            sample = (hs_hbm, c - n_prompt_chunks)
            return c, [
                (pltpu.make_async_copy(dest_hbm.at[c], idx_v[slot], sem.at[slot]), None),
                (pltpu.make_async_copy(prompt[0].at[pl.ds(prompt[1] * SC_CHUNK, SC_CHUNK)], rows_v[slot],
                                       sem.at[slot]), c < n_prompt_chunks),
                (pltpu.make_async_copy(sample[0].at[pl.ds(sample[1] * SC_CHUNK, SC_CHUNK)], rows_v[slot],
                                       sem.at[slot]), c >= n_prompt_chunks),
            ]

        def start_loads(j, slot):
            c, copies = loads(j, slot)
            for cp, cond in copies:
                pl.when((c < n_chunks) if cond is None else ((c < n_chunks) & cond))(cp.start)

        def wait_loads(j, slot):
            c, copies = loads(j, slot)
            for cp, cond in copies:
                pl.when((c < n_chunks) if cond is None else ((c < n_chunks) & cond))(cp.wait)

        def scatters(slot):
            return [pltpu.make_async_copy(rows_v[slot], xs_hbm.at[idx_v[slot].at[k]], sem.at[2 + slot])
                    for k in range(TOP_K)]

        start_loads(0, 0)
        for j in range(n_steps):
            slot = j % 2
            wait_loads(j, slot)
            if j >= 1:
                @pl.when(chunk(j - 1) < n_chunks)
                def _():
                    for cp in scatters(1 - slot):
                        cp.wait()
            if j + 1 < n_steps:
                start_loads(j + 1, 1 - slot)

            @pl.when(chunk(j) < n_chunks)
            def _():
                for cp in scatters(slot):
                    cp.start()

        @pl.when(chunk(n_steps - 1) < n_chunks)
        def _():
            for cp in scatters((n_steps - 1) % 2):
                cp.wait()

    return pl.kernel(
        body,
        out_type=jax.ShapeDtypeStruct((n_rows, w), U32),
        mesh=mesh,
        scratch_types=[pltpu.VMEM((TOP_K, SC_CHUNK), I32), pltpu.VMEM((TOP_K, SC_CHUNK), I32),
                       pltpu.VMEM((SC_CHUNK, w), U32), pltpu.VMEM((SC_CHUNK, w), U32),
                       pltpu.SemaphoreType.DMA((4,))],
        name="dispatch_sc",
    )(dest3, h_prompt, h_sample)


def _ffn_kernel(start_ref, nblk_ref, count_ref, b1_ref, b2_ref, xs_hbm, w1_hbm, w2_hbm, y_hbm,
                w1f, w2f, w1b, w2b, xbuf, ybuf, sem_w, sem_x, sem_y, *, blk):
    e = pl.program_id(0)
    n_e = pl.num_programs(0)
    nb = nblk_ref[e]
    base = start_ref[e]
    wslot = e % 2

    def w_copies(expert, slot):
        return (pltpu.make_async_copy(w1_hbm.at[expert], w1f.at[slot], sem_w.at[slot]),
                pltpu.make_async_copy(w2_hbm.at[expert], w2f.at[slot], sem_w.at[slot]))

    def x_copy(row0, slot):
        return pltpu.make_async_copy(xs_hbm.at[pl.ds(pl.multiple_of(row0, blk), blk)], xbuf.at[slot],
                                     sem_x.at[slot])

    def y_copy(row0, slot):
        return pltpu.make_async_copy(ybuf.at[slot], y_hbm.at[pl.ds(pl.multiple_of(row0, blk), blk)],
                                     sem_y.at[slot])

    @pl.when(e == 0)
    def _():
        for cp in w_copies(0, 0):
            cp.start()

        @pl.when(nb > 0)
        def _():
            x_copy(base, 0).start()

    for cp in w_copies(e, wslot):
        cp.wait()

    @pl.when(e + 1 < n_e)
    def _():
        for cp in w_copies(e + 1, 1 - wslot):
            cp.start(priority=1)

    w1b[...] = w1f[wslot].astype(BF16)
    w2b[...] = w2f[wslot].astype(BF16)
    b1 = b1_ref[e]
    b2 = b2_ref[e]

    def block(j, slot):
        x_copy(base, slot).wait()

        @pl.when(j + 1 < nb)
        def _():
            x_copy(base + (j + 1) * blk, 1 - slot).start()

        @pl.when(j >= 2)
        def _():
            y_copy(base, slot).wait()

        w = xbuf[slot]
        row = lax.broadcasted_iota(I32, w.shape, 0)
        w = jnp.where(row < count_ref[e] - j * blk, w, jnp.uint32(0))
        half = w.shape[1]
        d_e = w2b.shape[0]
        lo, hi = _unpack_bf16_pairs(w)
        gu = _dot(lo.astype(BF16), w1b[0:half, :]) + _dot(hi.astype(BF16), w1b[half:, :]) + b1
        gate = jnp.minimum(gu[:, :d_e], SWIGLU_LIMIT)
        up = jnp.clip(gu[:, d_e:], -SWIGLU_LIMIT, SWIGLU_LIMIT)
        glu = gate * jax.nn.sigmoid(gate * SWIGLU_ALPHA)
        act = ((up + 1.0) * glu).astype(BF16)
        ybuf[slot] = _pack_bf16_pairs(_dot(act, w2b[...]) + b2)
        y_copy(base + j * blk, slot).start()

    def pair(j2, carry):
        j = 2 * j2
        block(j, 0)

        @pl.when(j + 1 < nb)
        def _():
            block(j + 1, 1)

        return carry

    lax.fori_loop(0, (nb + 1) // 2, pair, 0)

    @pl.when(nb >= 2)
    def _():
        y_copy(base, 0).wait()
        y_copy(base, 1).wait()

    @pl.when(nb == 1)
    def _():
        y_copy(base, 0).wait()

    nxt = jnp.minimum(e + 1, n_e - 1)

    @pl.when((e + 1 < n_e) & (nblk_ref[nxt] > 0))
    def _():
        x_copy(start_ref[nxt], 0).start()


def _expert_ffn(grp_start, grp_blocks, grp_count, xs, w1, b1, w2, b2):
    n_rows, half = xs.shape
    n_e, d, d2 = w1.shape
    d_e = w2.shape[1]
    blk = MOE_BLOCK
    whole = lambda shape: pl.BlockSpec(shape, lambda e, s, n, c: (0,) * len(shape))
    hbm = pl.BlockSpec(memory_space=pl.ANY)
    grid_spec = pltpu.PrefetchScalarGridSpec(
        num_scalar_prefetch=3,
        grid=(n_e,),
        in_specs=[whole((n_e, 1, d2)), whole((n_e, 1, d)), hbm, hbm, hbm],
        out_specs=hbm,
        scratch_shapes=[pltpu.VMEM((2, d, d2), F32), pltpu.VMEM((2, d_e, d), F32),
                        pltpu.VMEM((d, d2), BF16), pltpu.VMEM((d_e, d), BF16),
                        pltpu.VMEM((2, blk, half), U32), pltpu.VMEM((2, blk, d // 2), U32),
                        pltpu.SemaphoreType.DMA((2,)), pltpu.SemaphoreType.DMA((2,)),
                        pltpu.SemaphoreType.DMA((2,))],
    )
    return pl.pallas_call(
        functools.partial(_ffn_kernel, blk=blk),
        grid_spec=grid_spec,
        out_shape=jax.ShapeDtypeStruct((n_rows, d // 2), U32),
        compiler_params=_cparams(("arbitrary",)),
        name="expert_ffn",
    )(grp_start, grp_blocks, grp_count, b1.reshape(n_e, 1, d2), b2.reshape(n_e, 1, d), xs, w1, w2)


def _undispatch(dest3, y_rows):
    n_chunks = dest3.shape[0]
    w = y_rows.shape[1]
    mesh = plsc.VectorSubcoreMesh(core_axis_name="c", subcore_axis_name="s")
    n_workers = mesh.num_cores * mesh.num_subcores
    n_steps = -(-n_chunks // n_workers)

    def body(dest_hbm, y_hbm, out_hbm, idx0, idx1, rows0, rows1, sem):
        wid = lax.axis_index("s") * mesh.num_cores + lax.axis_index("c")
        idx_v, rows_v = (idx0, idx1), (rows0, rows1)

        def chunk(j):
            return wid + j * n_workers

        def gather(j, k, slot):
            return pltpu.make_async_copy(y_hbm.at[idx_v[j % 2].at[k]], rows_v[slot], sem.at[slot])

        def store(j, k, slot):
            return pltpu.make_async_copy(rows_v[slot], out_hbm.at[k, pl.ds(chunk(j) * SC_CHUNK, SC_CHUNK)],
                                         sem.at[2 + slot])

        units = [(j, k) for j in range(n_steps) for k in range(TOP_K)]

        def valid(j):
            return chunk(j) < n_chunks

        def load_idx(j):
            pl.when(valid(j))(lambda: pltpu.sync_copy(dest_hbm.at[chunk(j)], idx_v[j % 2]))

        load_idx(0)
        pl.when(valid(0))(gather(0, 0, 0).start)
        for u, (j, k) in enumerate(units):
            slot = u % 2
            pl.when(valid(j))(gather(j, k, slot).wait)
            if u + 1 < len(units):
                jn, kn = units[u + 1]
                if u >= 1:
                    jp, kp = units[u - 1]
                    pl.when(valid(jp))(store(jp, kp, 1 - slot).wait)
                if kn == 0:
                    load_idx(jn)
                pl.when(valid(jn))(gather(jn, kn, 1 - slot).start)
            pl.when(valid(j))(store(j, k, slot).start)
        for u in (len(units) - 2, len(units) - 1):
            j, k = units[u]
            pl.when(valid(j))(store(j, k, u % 2).wait)

    return pl.kernel(
        body,
        out_type=jax.ShapeDtypeStruct((TOP_K, n_chunks * SC_CHUNK, w), y_rows.dtype),
        mesh=mesh,
        scratch_types=[pltpu.VMEM((TOP_K, SC_CHUNK), I32), pltpu.VMEM((TOP_K, SC_CHUNK), I32),
                       pltpu.VMEM((SC_CHUNK, w), y_rows.dtype), pltpu.VMEM((SC_CHUNK, w), y_rows.dtype),
                       pltpu.SemaphoreType.DMA((4,))],
        name="undispatch_sc",
    )(dest3, y_rows)


def _combine_kernel(x2_ref, y4_ref, gates_ref, gf_ref, out_ref, *, tile):
    g = jnp.concatenate([gates_ref[...], jnp.zeros((128 - TOP_K, tile), F32)], axis=0)
    gt = g.T
    half = y4_ref.shape[2]
    moe_lo = jnp.zeros((tile, half), F32)
    moe_hi = jnp.zeros((tile, half), F32)
    for k in range(TOP_K):
        lo, hi = _unpack_bf16_pairs(y4_ref[k])
        moe_lo = moe_lo + gt[:, k:k + 1] * lo
        moe_hi = moe_hi + gt[:, k:k + 1] * hi
    x2 = x2_ref[...]
    out_ref[...] = _rms(x2 + jnp.concatenate([moe_lo, moe_hi], axis=1), gf_ref[...])


def _combine(x2, y4, gates, g_final, tile, first_tile):
    n, d = x2.shape
    return pl.pallas_call(
        functools.partial(_combine_kernel, tile=tile),
        grid=(n // tile,),
        in_specs=[
            pl.BlockSpec((tile, d), lambda i: (i, 0)),
            pl.BlockSpec((TOP_K, tile, d // 2), lambda i: (0, i + first_tile, 0)),
            pl.BlockSpec((TOP_K, tile), lambda i: (0, i)),
            pl.BlockSpec((1, d), lambda i: (0, 0)),
        ],
        out_specs=pl.BlockSpec((tile, d), lambda i: (i, 0)),
        out_shape=jax.ShapeDtypeStruct((n, d), F32),
        compiler_params=_cparams(("arbitrary",)),
        name="combine",
    )(x2, y4, gates, g_final)


def kernel(x_prompt, x_sample, mem_prompt, cache_mem_k, cache_mem_v, state_conv, g_mix, w_in, g_v,
           w_spatial, b_spatial, w_conv, w_out, g_xattn, g_mem, w_q, w_k, w_v, w_xo, g_moe, w_router,
           b_router, w_gate_up, b_gate_up, w_down, b_down, g_final):
    depth = g_mix.shape[0]
    assert depth == 1, "one layer supported"
    b, s, d = x_prompt.shape
    nb, ns, _ = x_sample.shape
    assert ns == 1
    n_mem = mem_prompt.shape[1]
    a_heads, a_hd = g_v.shape[1], g_v.shape[2]
    a_width = a_heads * a_hd
    b_width = w_conv.shape[2]
    assert a_heads == A_HEADS and 2 * a_hd == CHUNK and w_spatial.shape[2] == CHUNK
    assert s % MIX_TILE == 0 and s % ATT_TILE == 0 and (b * n_mem) % KV_TILE == 0
    assert nb % SAMPLE_ATT_GROUP == 0
    l = 0
    row = lambda a: a.reshape(1, -1)

    w_in_b = w_in[l].astype(BF16)
    w_out_b = w_out[l].astype(BF16)
    w_q_b = w_q[l].astype(BF16)
    w_k_b = w_k[l].astype(BF16)
    w_v_b = w_v[l].astype(BF16)
    w_xo_b = w_xo[l].astype(BF16)
    wrt_b = w_router[l].T.astype(BF16)
    br_col = b_router[l].reshape(N_EXPERTS, 1).astype(F32)
    gv_row = row(g_v[l])
    ws = w_spatial[l]
    ws_pairs = jnp.concatenate([ws[0::2], ws[1::2]], axis=2)
    bs_full = jnp.repeat(b_spatial[l].T, a_hd, axis=1)
    w00 = row(jnp.repeat(ws[:, 0, 0], a_hd))
    b0 = row(jnp.repeat(b_spatial[l][:, 0], a_hd))

    mk2, mv2 = _mem_kv(mem_prompt.reshape(b * n_mem, d), row(g_mem[l]), w_k_b, w_v_b)
    x1_p, conv_p = _mixer_prompt(x_prompt, row(g_mix[l]), w_in_b, gv_row, ws_pairs, bs_full,
                                 w_conv[l], w_out_b)
    x2_p, h3_p, tope_p, gates_p = _attn_prompt(
        x1_p, mk2.reshape(b, n_mem, d), mv2.reshape(b, n_mem, d), row(g_xattn[l]), w_q_b, w_xo_b,
        row(g_moe[l]), wrt_b, br_col)

    xs2 = x_sample.reshape(nb, d)
    st = state_conv[l]
    x1_s, v_s, z_s, q_s = _mixer_sample(xs2, st[:, 0], st[:, 1], row(g_mix[l]), w_in_b, gv_row, w00, b0,
                                        w_conv[l], w_out_b, row(g_xattn[l]), w_q_b)
    x_heads, x_hd = cache_mem_k.shape[3], cache_mem_k.shape[4]
    assert depth == 1 and x_heads == X_HEADS
    o_s = _attn_sample(q_s.reshape(nb, x_heads, x_hd), cache_mem_k.reshape(nb, n_mem, x_heads, x_hd),
                       cache_mem_v.reshape(nb, n_mem, x_heads, x_hd)).reshape(nb, d)
    x2_s, h3_s, tope_s, gates_s = _tail_sample(x1_s, o_s, w_xo_b, row(g_moe[l]), wrt_b, br_col)

    t_p = b * s
    t_all = t_p + nb
    t_pad = -(-t_all // ROUTE_TILE) * ROUTE_TILE
    tope_all = jnp.concatenate([tope_p, tope_s, jnp.full((TOP_K, t_pad - t_all), -1, I32)], axis=1)
    dest, cnt = _route(tope_all)
    counts = cnt[:, 0]
    padded = (counts + MOE_BLOCK - 1) // MOE_BLOCK * MOE_BLOCK
    pad_ends = jnp.cumsum(padded)
    pad_starts = pad_ends - padded
    n_blocks = -(-(t_all * TOP_K) // MOE_BLOCK) + N_EXPERTS
    n_rows = n_blocks * MOE_BLOCK
    assert t_all % SC_CHUNK == 0 and t_p % SC_CHUNK == 0
    dest3 = dest[:, :t_all].reshape(TOP_K, t_all // SC_CHUNK, SC_CHUNK).transpose(1, 0, 2)
    xs_rows = _dispatch(dest3, h3_p.reshape(t_p, d // 2), h3_s, n_rows)
    y_rows = _expert_ffn(pad_starts.astype(I32), (padded // MOE_BLOCK).astype(I32), counts.astype(I32),
                         xs_rows, w_gate_up[l], b_gate_up[l], w_down[l], b_down[l])

    y4 = _undispatch(dest3, y_rows)

    gf = row(g_final)
    assert t_p % COMBINE_TILE == 0 and t_p % nb == 0
    y_p = _combine(x2_p.reshape(t_p, d), y4, gates_p, gf, COMBINE_TILE, 0)
    y_s = _combine(x2_s, y4, gates_s, gf, nb, t_p // nb)

    x_heads = cache_mem_k.shape[3]
    return (y_p.reshape(b, s, d),
            y_s.reshape(nb, 1, d),
            mk2.reshape(1, b, n_mem, x_heads, d // x_heads),
            mv2.reshape(1, b, n_mem, x_heads, d // x_heads),
            conv_p.reshape(1, b, CONV_W - 1, b_width),
            jnp.stack([st[:, 1], z_s], axis=1).reshape(1, nb, CONV_W - 1, b_width),
            v_s.reshape(1, nb, 1, a_heads, a_hd))
```

```python
import functools

import jax
import jax.numpy as jnp
from jax import lax
from jax.experimental import pallas as pl
from jax.experimental.pallas import tpu as pltpu
from jax.experimental.pallas import tpu_sc as plsc

F32 = jnp.float32
BF16 = jnp.bfloat16
I32 = jnp.int32
U32 = jnp.uint32

A_HEADS = 8
CHUNK = 128
CONV_W = 3
X_HEADS = 4
N_EXPERTS = 32
TOP_K = 4
SWIGLU_LIMIT = 7.0
SWIGLU_ALPHA = 1.702
EPS = 1e-5

MIX_TILE = 1024
ATT_TILE = 1024
KV_TILE = 512
ROUTE_TILE = 512
MOE_BLOCK = 528
COMBINE_TILE = 512
SAMPLE_ATT_GROUP = 4
SC_CHUNK = 64
VMEM_LIMIT = 56 * 1024 * 1024


def _cparams(sem=None):
    return pltpu.CompilerParams(dimension_semantics=sem, vmem_limit_bytes=VMEM_LIMIT)


def _rms(x, g):
    r = lax.rsqrt(jnp.mean(x * x, axis=-1, keepdims=True) + EPS)
    return (x * r) * g


def _gelu(x):
    return 0.5 * x * (1.0 + lax.erf(x * 0.7071067811865476))


def _dot(a, b):
    return jnp.dot(a, b, preferred_element_type=F32)


def _dot_nt(a, b):
    return lax.dot_general(a, b, (((1,), (1,)), ((), ())), preferred_element_type=F32)


def _pack_bf16_pairs(x):
    d = x.shape[1]
    r = x.astype(BF16).astype(F32)
    lo = lax.shift_right_logical(pltpu.bitcast(r[:, :d // 2], U32), jnp.uint32(16))
    hi = lax.bitwise_and(pltpu.bitcast(r[:, d // 2:], U32), jnp.uint32(0xFFFF0000))
    return lax.bitwise_or(lo, hi)


def _unpack_bf16_pairs(w):
    lo = pltpu.bitcast(lax.shift_left(w, jnp.uint32(16)), F32)
    hi = pltpu.bitcast(lax.bitwise_and(w, jnp.uint32(0xFFFF0000)), F32)
    return lo, hi


def _memkv_kernel(m_ref, g_ref, wk_ref, wv_ref, k_ref, v_ref):
    h = _rms(m_ref[...], g_ref[...]).astype(BF16)
    k_ref[...] = _dot(h, wk_ref[...])
    v_ref[...] = _dot(h, wv_ref[...])


def _mem_kv(mem2d, g_mem, wk, wv):
    n, d = mem2d.shape
    row = pl.BlockSpec((KV_TILE, d), lambda i: (i, 0))
    full = lambda shape: pl.BlockSpec(shape, lambda i: (0,) * len(shape))
    return pl.pallas_call(
        _memkv_kernel,
        grid=(n // KV_TILE,),
        in_specs=[row, full((1, d)), full((d, d)), full((d, d))],
        out_specs=[row, row],
        out_shape=[jax.ShapeDtypeStruct((n, d), F32)] * 2,
        compiler_params=_cparams(("arbitrary",)),
        name="mem_kv",
    )(mem2d, g_mem, wk, wv)


def _head_rms(v, gv, a_width):
    hd = a_width // A_HEADS
    r_i = lax.broadcasted_iota(I32, (a_width, a_width), 0) // hd
    c_i = lax.broadcasted_iota(I32, (a_width, a_width), 1) // hd
    ones_bd = jnp.where(r_i == c_i, 1.0, 0.0).astype(BF16)
    sq = v * v
    sq_hi = sq.astype(BF16)
    sq_lo = (sq - sq_hi.astype(F32)).astype(BF16)
    gs = _dot(sq_hi, ones_bd) + _dot(sq_lo, ones_bd)
    return (v * lax.rsqrt(gs * (1.0 / hd) + EPS)) * gv


def _mixer_prompt_kernel(x_ref, gmix_ref, win_ref, gv_ref, ws_ref, bs_ref, wc_ref, wout_ref,
                         x1_ref, cs_ref, zbuf, *, tile, a_width, b_width):
    j = pl.program_id(1)
    nj = pl.num_programs(1)
    x = x_ref[...]
    h = _rms(x, gmix_ref[...]).astype(BF16)
    proj = _dot(h, win_ref[...])
    u = _gelu(proj[:, :a_width])
    v = _head_rms(_gelu(proj[:, a_width:2 * a_width]), gv_ref[...], a_width)
    o = 2 * a_width
    hb = proj[:, o:o + b_width]
    bg = proj[:, o + b_width:o + 2 * b_width]
    cg = proj[:, o + 2 * b_width:o + 3 * b_width]

    lane = lax.broadcasted_iota(I32, (CHUNK, 2 * (a_width // A_HEADS)), 1)
    first = lane < (a_width // A_HEADS)
    t_i = lax.broadcasted_iota(I32, (CHUNK, 2 * CHUNK), 0)
    s_i = lax.broadcasted_iota(I32, (CHUNK, 2 * CHUNK), 1) % CHUNK
    causal = s_i <= t_i
    pair_cols = []
    for p in range(A_HEADS // 2):
        w_pair = jnp.where(causal, ws_ref[p], 0.0).astype(BF16)
        vp = v[:, p * CHUNK:(p + 1) * CHUNK]
        rows = []
        for c in range(tile // CHUNK):
            vc = vp[c * CHUNK:(c + 1) * CHUNK]
            rhs = jnp.concatenate([jnp.where(first, vc, 0.0), jnp.where(first, 0.0, vc)],
                                  axis=0).astype(BF16)
            rows.append(_dot(w_pair, rhs))
        pair_cols.append(jnp.concatenate(rows, axis=0))
    gate = jnp.concatenate(pair_cols, axis=1)
    bias = jnp.concatenate([bs_ref[...]] * (tile // CHUNK), axis=0)
    a_out = u * (gate + bias)

    @pl.when(j == 0)
    def _():
        zbuf[0:8, :] = jnp.zeros((8, b_width), F32)

    z = cg * hb
    zbuf[8:tile + 8, :] = z
    z1 = zbuf[7:tile + 7, :]
    z2 = zbuf[6:tile + 6, :]
    wc = wc_ref[...]
    conv = z2 * wc[0:1] + z1 * wc[1:2] + z * wc[2:3]
    b_out = bg * conv
    tail = zbuf[tile:tile + 8, :]
    zbuf[0:8, :] = tail

    @pl.when(j == nj - 1)
    def _():
        cs_ref[...] = tail[8 - (CONV_W - 1):, :]

    y = _dot(a_out.astype(BF16), wout_ref[0:a_width, :]) + _dot(b_out.astype(BF16), wout_ref[a_width:, :])
    x1_ref[...] = x + y


def _mixer_prompt(x, g_mix, w_in, g_v, ws_pairs, bs_full, w_conv, w_out):
    b, s, d = x.shape
    a_width = g_v.shape[1]
    b_width = w_conv.shape[1]
    in_width = w_in.shape[1]
    tile = MIX_TILE
    full = lambda shape: pl.BlockSpec(shape, lambda i, j: (0,) * len(shape))
    kern = functools.partial(_mixer_prompt_kernel, tile=tile, a_width=a_width, b_width=b_width)
    return pl.pallas_call(
        kern,
        grid=(b, s // tile),
        in_specs=[
            pl.BlockSpec((None, tile, d), lambda i, j: (i, j, 0)),
            full((1, d)), full((d, in_width)), full((1, a_width)),
            full(ws_pairs.shape), full(bs_full.shape), full(w_conv.shape), full(w_out.shape),
        ],
        out_specs=[
            pl.BlockSpec((None, tile, d), lambda i, j: (i, j, 0)),
            pl.BlockSpec((None, CONV_W - 1, b_width), lambda i, j: (i, 0, 0)),
        ],
        out_shape=[jax.ShapeDtypeStruct((b, s, d), F32),
                   jax.ShapeDtypeStruct((b, CONV_W - 1, b_width), F32)],
        scratch_shapes=[pltpu.VMEM((tile + 8, b_width), F32)],
        compiler_params=_cparams(("arbitrary", "arbitrary")),
        name="mixer_prompt",
    )(x, g_mix, w_in, g_v, ws_pairs, bs_full, w_conv, w_out)


def _mixer_sample_kernel(x_ref, s0_ref, s1_ref, gmix_ref, win_ref, gv_ref, w00_ref, b0_ref, wc_ref,
                         wout_ref, gx_ref, wq_ref, x1_ref, v_ref, z_ref, q_ref, *, a_width, b_width):
    x = x_ref[...]
    h = _rms(x, gmix_ref[...]).astype(BF16)
    proj = _dot(h, win_ref[...])
    u = _gelu(proj[:, :a_width])
    v = _head_rms(_gelu(proj[:, a_width:2 * a_width]), gv_ref[...], a_width)
    v_ref[...] = v
    o = 2 * a_width
    hb = proj[:, o:o + b_width]
    bg = proj[:, o + b_width:o + 2 * b_width]
    cg = proj[:, o + 2 * b_width:o + 3 * b_width]
    a_out = u * (v * w00_ref[...] + b0_ref[...])
    z = cg * hb
    z_ref[...] = z
    wc = wc_ref[...]
    conv = s0_ref[...] * wc[0:1] + s1_ref[...] * wc[1:2] + z * wc[2:3]
    b_out = bg * conv
    y = _dot(a_out.astype(BF16), wout_ref[0:a_width, :]) + _dot(b_out.astype(BF16), wout_ref[a_width:, :])
    x1 = x + y
    x1_ref[...] = x1
    q_ref[...] = _dot(_rms(x1, gx_ref[...]).astype(BF16), wq_ref[...])


def _mixer_sample(x, s0, s1, g_mix, w_in, g_v, w00, b0, w_conv, w_out, g_x, w_q):
    n, d = x.shape
    a_width = g_v.shape[1]
    b_width = w_conv.shape[1]
    kern = functools.partial(_mixer_sample_kernel, a_width=a_width, b_width=b_width)
    return pl.pallas_call(
        kern,
        out_shape=[jax.ShapeDtypeStruct((n, d), F32), jax.ShapeDtypeStruct((n, a_width), F32),
                   jax.ShapeDtypeStruct((n, b_width), F32), jax.ShapeDtypeStruct((n, d), F32)],
        compiler_params=_cparams(),
        name="mixer_sample",
    )(x, s0, s1, g_mix, w_in, g_v, w00, b0, w_conv, w_out, g_x, w_q)


def _router_tail(x2, gmoe, wrt, br, h3_ref, tope_ref, gates_ref):
    h3f = _rms(x2, gmoe)
    h3 = h3f.astype(BF16)
    h3_ref[...] = _pack_bf16_pairs(h3f)

    logits = _dot_nt(wrt, h3) + br
    n_e, n_t = logits.shape
    e_idx = lax.broadcasted_iota(I32, (n_e, n_t), 0).astype(F32)
    tops, idxs = [], []
    for _ in range(TOP_K):
        m = jnp.max(logits, axis=0, keepdims=True)
        idx = jnp.min(jnp.where(logits == m, e_idx, float(n_e)), axis=0, keepdims=True)
        tops.append(m)
        idxs.append(idx)
        logits = jnp.where(e_idx == idx, -jnp.inf, logits)
    top = jnp.concatenate(tops, axis=0)
    ex = jnp.exp(top - top[0:1])
    gates_ref[...] = ex / jnp.sum(ex, axis=0, keepdims=True)
    tope_ref[...] = jnp.concatenate(idxs, axis=0).astype(I32)


def _attn_prompt_kernel(x1_ref, k_ref, v_ref, gx_ref, wq_ref, wxo_ref, gmoe_ref, wrt_ref, br_ref,
                        x2_ref, h3_ref, tope_ref, gates_ref):
    x = x1_ref[...]
    d = x.shape[1]
    hd = d // X_HEADS
    q = _dot(_rms(x, gx_ref[...]).astype(BF16), wq_ref[...]).astype(BF16)
    kb = k_ref[...].astype(BF16)
    vb = v_ref[...].astype(BF16)
    outs = []
    for hh in range(X_HEADS):
        sl = slice(hh * hd, (hh + 1) * hd)
        s = _dot_nt(q[:, sl], kb[:, sl]) * (hd ** -0.5)
        e = jnp.exp(s - jnp.max(s, axis=-1, keepdims=True))
        p = e / jnp.sum(e, axis=-1, keepdims=True)
        outs.append(_dot(p.astype(BF16), vb[:, sl]))
    o = jnp.concatenate(outs, axis=1).astype(BF16)
    x2 = x + _dot(o, wxo_ref[...])
    x2_ref[...] = x2
    _router_tail(x2, gmoe_ref[...], wrt_ref[...], br_ref[...], h3_ref, tope_ref, gates_ref)


def _attn_prompt(x1, mk, mv, g_x, w_q, w_xo, g_moe, wrt, br):
    b, s, d = x1.shape
    n_mem = mk.shape[1]
    tile = ATT_TILE
    nq = s // tile
    full = lambda shape: pl.BlockSpec(shape, lambda i, j: (0,) * len(shape))
    tok = pl.BlockSpec((None, tile, d), lambda i, j: (i, j, 0))
    mem = pl.BlockSpec((None, n_mem, d), lambda i, j: (i, 0, 0))
    lanes = pl.BlockSpec((TOP_K, tile), lambda i, j: (0, i * nq + j))
    return pl.pallas_call(
        _attn_prompt_kernel,
        grid=(b, nq),
        in_specs=[tok, mem, mem, full((1, d)), full((d, d)), full((d, d)), full((1, d)),
                  full(wrt.shape), full(br.shape)],
        out_specs=[tok, pl.BlockSpec((None, tile, d // 2), lambda i, j: (i, j, 0)), lanes, lanes],
        out_shape=[jax.ShapeDtypeStruct((b, s, d), F32), jax.ShapeDtypeStruct((b, s, d // 2), U32),
                   jax.ShapeDtypeStruct((TOP_K, b * s), I32), jax.ShapeDtypeStruct((TOP_K, b * s), F32)],
        compiler_params=_cparams(("arbitrary", "arbitrary")),
        name="attn_prompt",
    )(x1, mk, mv, g_x, w_q, w_xo, g_moe, wrt, br)


def _attn_sample_kernel(q_ref, k_ref, v_ref, o_ref, *, group):
    hd = q_ref.shape[2]
    for g in range(group):
        q = q_ref[g]
        s = jnp.sum(k_ref[g] * q[None], axis=-1, keepdims=True) * (hd ** -0.5)
        e = jnp.exp(s - jnp.max(s, axis=0, keepdims=True))
        p = e / jnp.sum(e, axis=0, keepdims=True)
        o_ref[g] = jnp.sum(p * v_ref[g], axis=0)


def _attn_sample(q, ck, cv):
    n, heads, hd = q.shape
    n_mem = ck.shape[1]
    group = SAMPLE_ATT_GROUP
    qo = pl.BlockSpec((group, heads, hd), lambda i: (i, 0, 0))
    kv = pl.BlockSpec((group, n_mem, heads, hd), lambda i: (i, 0, 0, 0))
    return pl.pallas_call(
        functools.partial(_attn_sample_kernel, group=group),
        grid=(n // group,),
        in_specs=[qo, kv, kv],
        out_specs=qo,
        out_shape=jax.ShapeDtypeStruct((n, heads, hd), F32),
        compiler_params=_cparams(("arbitrary",)),
        name="attn_sample",
    )(q, ck, cv)


def _tail_sample_kernel(x1_ref, o_ref, wxo_ref, gmoe_ref, wrt_ref, br_ref,
                        x2_ref, h3_ref, tope_ref, gates_ref):
    x2 = x1_ref[...] + _dot(o_ref[...].astype(BF16), wxo_ref[...])
    x2_ref[...] = x2
    _router_tail(x2, gmoe_ref[...], wrt_ref[...], br_ref[...], h3_ref, tope_ref, gates_ref)


def _tail_sample(x1, o, w_xo, g_moe, wrt, br):
    n, d = x1.shape
    return pl.pallas_call(
        _tail_sample_kernel,
        out_shape=[jax.ShapeDtypeStruct((n, d), F32), jax.ShapeDtypeStruct((n, d // 2), U32),
                   jax.ShapeDtypeStruct((TOP_K, n), I32), jax.ShapeDtypeStruct((TOP_K, n), F32)],
        compiler_params=_cparams(),
        name="tail_sample",
    )(x1, o, w_xo, g_moe, wrt, br)


def _route_kernel(tope_ref, dest_ref, cnt_ref, *, n_blocks, tile, moe_block):
    e_idx = lax.broadcasted_iota(I32, (N_EXPERTS, tile), 0)
    earlier = jnp.where(lax.broadcasted_iota(I32, (tile, tile), 0)
                        < lax.broadcasted_iota(I32, (tile, tile), 1), 1.0, 0.0).astype(BF16)

    def onehot(k, off):
        return e_idx == tope_ref[pl.ds(k, 1), pl.ds(off, tile)]

    def count_body(j, cnt):
        off = pl.multiple_of(j * tile, tile)
        for k in range(TOP_K):
            cnt = cnt + jnp.sum(jnp.where(onehot(k, off), 1.0, 0.0), axis=1, keepdims=True)
        return cnt

    counts = lax.fori_loop(0, n_blocks, count_body, jnp.zeros((N_EXPERTS, 1), F32))
    cnt_ref[...] = jnp.broadcast_to(counts, cnt_ref.shape).astype(I32)

    n_blk = jnp.floor((counts + (moe_block - 1)) * (1.0 / moe_block))
    n_hi = jnp.floor(n_blk * (1.0 / 16.0))
    n_lo = n_blk - 16.0 * n_hi
    below = jnp.where(lax.broadcasted_iota(I32, (N_EXPERTS, N_EXPERTS), 1)
                      < lax.broadcasted_iota(I32, (N_EXPERTS, N_EXPERTS), 0), 1.0, 0.0).astype(BF16)
    wide = lambda c: jnp.broadcast_to(c, (N_EXPERTS, 128)).astype(BF16)
    start = (16.0 * _dot(below, wide(n_hi)) + _dot(below, wide(n_lo)))[:, 0:1] * float(moe_block)

    def dest_body(j, run):
        off = pl.multiple_of(j * tile, tile)
        for k in range(TOP_K):
            oh = onehot(k, off)
            ohf = jnp.where(oh, 1.0, 0.0)
            before = _dot(ohf.astype(BF16), earlier)
            dest = jnp.sum(jnp.where(oh, before + run, 0.0), axis=0, keepdims=True)
            dest_ref[pl.ds(k, 1), pl.ds(off, tile)] = dest.astype(I32)
            run = run + jnp.sum(ohf, axis=1, keepdims=True)
        return run

    lax.fori_loop(0, n_blocks, dest_body, start)


def _route(tope_pad):
    t_pad = tope_pad.shape[1]
    kern = functools.partial(_route_kernel, n_blocks=t_pad // ROUTE_TILE, tile=ROUTE_TILE,
                             moe_block=MOE_BLOCK)
    return pl.pallas_call(
        kern,
        out_shape=[jax.ShapeDtypeStruct((TOP_K, t_pad), I32),
                   jax.ShapeDtypeStruct((N_EXPERTS, 128), I32)],
        compiler_params=_cparams(),
        name="route",
    )(tope_pad)


def _dispatch(dest3, h_prompt, h_sample, n_rows):
    w = h_prompt.shape[1]
    n_prompt_chunks = h_prompt.shape[0] // SC_CHUNK
    n_chunks = dest3.shape[0]
    mesh = plsc.VectorSubcoreMesh(core_axis_name="c", subcore_axis_name="s")
    n_workers = mesh.num_cores * mesh.num_subcores

    n_steps = -(-n_chunks // n_workers)

    def body(dest_hbm, hp_hbm, hs_hbm, xs_hbm, idx0, idx1, rows0, rows1, sem):
        wid = lax.axis_index("s") * mesh.num_cores + lax.axis_index("c")
        idx_v, rows_v = (idx0, idx1), (rows0, rows1)

        def chunk(j):
            return wid + j * n_workers

        def loads(j, slot):
            c = chunk(j)
            prompt = (hp_hbm, c)
            sample = (hs_hbm, c - n_prompt_chunks)
            return c, [
                (pltpu.make_async_copy(dest_hbm.at[c], idx_v[slot], sem.at[slot]), None),
                (pltpu.make_async_copy(prompt[0].at[pl.ds(prompt[1] * SC_CHUNK, SC_CHUNK)], rows_v[slot],
                                       sem.at[slot]), c < n_prompt_chunks),
                (pltpu.make_async_copy(sample[0].at[pl.ds(sample[1] * SC_CHUNK, SC_CHUNK)], rows_v[slot],
                                       sem.at[slot]), c >= n_prompt_chunks),
            ]

        def start_loads(j, slot):
            c, copies = loads(j, slot)
            for cp, cond in copies:
                pl.when((c < n_chunks) if cond is None else ((c < n_chunks) & cond))(cp.start)

        def wait_loads(j, slot):
            c, copies = loads(j, slot)
            for cp, cond in copies:
                pl.when((c < n_chunks) if cond is None else ((c < n_chunks) & cond))(cp.wait)

        def scatters(slot):
            return [pltpu.make_async_copy(rows_v[slot], xs_hbm.at[idx_v[slot].at[k]], sem.at[2 + slot])
                    for k in range(TOP_K)]

        start_loads(0, 0)
        for j in range(n_steps):
            slot = j % 2
            wait_loads(j, slot)
            if j >= 1:
                @pl.when(chunk(j - 1) < n_chunks)
                def _():
                    for cp in scatters(1 - slot):
                        cp.wait()
            if j + 1 < n_steps:
                start_loads(j + 1, 1 - slot)

            @pl.when(chunk(j) < n_chunks)
            def _():
                for cp in scatters(slot):
                    cp.start()

        @pl.when(chunk(n_steps - 1) < n_chunks)
        def _():
            for cp in scatters((n_steps - 1) % 2):
                cp.wait()

    return pl.kernel(
        body,
        out_type=jax.ShapeDtypeStruct((n_rows, w), U32),
        mesh=mesh,
        scratch_types=[pltpu.VMEM((TOP_K, SC_CHUNK), I32), pltpu.VMEM((TOP_K, SC_CHUNK), I32),
                       pltpu.VMEM((SC_CHUNK, w), U32), pltpu.VMEM((SC_CHUNK, w), U32),
                       pltpu.SemaphoreType.DMA((4,))],
        name="dispatch_sc",
    )(dest3, h_prompt, h_sample)


def _ffn_kernel(start_ref, nblk_ref, count_ref, b1_ref, b2_ref, xs_hbm, w1_hbm, w2_hbm, y_hbm,
                w1f, w2f, w1b, w2b, xbuf, ybuf, sem_w, sem_x, sem_y, *, blk):
    e = pl.program_id(0)
    n_e = pl.num_programs(0)
    nb = nblk_ref[e]
    base = start_ref[e]
    wslot = e % 2

    def w_copies(expert, slot):
        return (pltpu.make_async_copy(w1_hbm.at[expert], w1f.at[slot], sem_w.at[slot]),
                pltpu.make_async_copy(w2_hbm.at[expert], w2f.at[slot], sem_w.at[slot]))

    def x_copy(row0, slot):
        return pltpu.make_async_copy(xs_hbm.at[pl.ds(pl.multiple_of(row0, blk), blk)], xbuf.at[slot],
                                     sem_x.at[slot])

    def y_copy(row0, slot):
        return pltpu.make_async_copy(ybuf.at[slot], y_hbm.at[pl.ds(pl.multiple_of(row0, blk), blk)],
                                     sem_y.at[slot])

    @pl.when(e == 0)
    def _():
        for cp in w_copies(0, 0):
            cp.start()

        @pl.when(nb > 0)
        def _():
            x_copy(base, 0).start()

    for cp in w_copies(e, wslot):
        cp.wait()

    @pl.when(e + 1 < n_e)
    def _():
        for cp in w_copies(e + 1, 1 - wslot):
            cp.start(priority=1)

    w1b[...] = w1f[wslot].astype(BF16)
    w2b[...] = w2f[wslot].astype(BF16)
    b1 = b1_ref[e]
    b2 = b2_ref[e]

    def block(j, slot):
        x_copy(base, slot).wait()

        @pl.when(j + 1 < nb)
        def _():
            x_copy(base + (j + 1) * blk, 1 - slot).start()

        @pl.when(j >= 2)
        def _():
            y_copy(base, slot).wait()

        w = xbuf[slot]
        row = lax.broadcasted_iota(I32, w.shape, 0)
        w = jnp.where(row < count_ref[e] - j * blk, w, jnp.uint32(0))
        half = w.shape[1]
        d_e = w2b.shape[0]
        lo, hi = _unpack_bf16_pairs(w)
        gu = _dot(lo.astype(BF16), w1b[0:half, :]) + _dot(hi.astype(BF16), w1b[half:, :]) + b1
        gate = jnp.minimum(gu[:, :d_e], SWIGLU_LIMIT)
        up = jnp.clip(gu[:, d_e:], -SWIGLU_LIMIT, SWIGLU_LIMIT)
        glu = gate * jax.nn.sigmoid(gate * SWIGLU_ALPHA)
        act = ((up + 1.0) * glu).astype(BF16)
        ybuf[slot] = _pack_bf16_pairs(_dot(act, w2b[...]) + b2)
        y_copy(base + j * blk, slot).start()

    def pair(j2, carry):
        j = 2 * j2
        block(j, 0)

        @pl.when(j + 1 < nb)
        def _():
            block(j + 1, 1)

        return carry

    lax.fori_loop(0, (nb + 1) // 2, pair, 0)

    @pl.when(nb >= 2)
    def _():
        y_copy(base, 0).wait()
        y_copy(base, 1).wait()

    @pl.when(nb == 1)
    def _():
        y_copy(base, 0).wait()

    nxt = jnp.minimum(e + 1, n_e - 1)

    @pl.when((e + 1 < n_e) & (nblk_ref[nxt] > 0))
    def _():
        x_copy(start_ref[nxt], 0).start()


def _expert_ffn(grp_start, grp_blocks, grp_count, xs, w1, b1, w2, b2):
    n_rows, half = xs.shape
    n_e, d, d2 = w1.shape
    d_e = w2.shape[1]
    blk = MOE_BLOCK
    whole = lambda shape: pl.BlockSpec(shape, lambda e, s, n, c: (0,) * len(shape))
    hbm = pl.BlockSpec(memory_space=pl.ANY)
    grid_spec = pltpu.PrefetchScalarGridSpec(
        num_scalar_prefetch=3,
        grid=(n_e,),
        in_specs=[whole((n_e, 1, d2)), whole((n_e, 1, d)), hbm, hbm, hbm],
        out_specs=hbm,
        scratch_shapes=[pltpu.VMEM((2, d, d2), F32), pltpu.VMEM((2, d_e, d), F32),
                        pltpu.VMEM((d, d2), BF16), pltpu.VMEM((d_e, d), BF16),
                        pltpu.VMEM((2, blk, half), U32), pltpu.VMEM((2, blk, d // 2), U32),
                        pltpu.SemaphoreType.DMA((2,)), pltpu.SemaphoreType.DMA((2,)),
                        pltpu.SemaphoreType.DMA((2,))],
    )
    return pl.pallas_call(
        functools.partial(_ffn_kernel, blk=blk),
        grid_spec=grid_spec,
        out_shape=jax.ShapeDtypeStruct((n_rows, d // 2), U32),
        compiler_params=_cparams(("arbitrary",)),
        name="expert_ffn",
    )(grp_start, grp_blocks, grp_count, b1.reshape(n_e, 1, d2), b2.reshape(n_e, 1, d), xs, w1, w2)


def _undispatch(dest3, y_rows):
    n_chunks = dest3.shape[0]
    w = y_rows.shape[1]
    mesh = plsc.VectorSubcoreMesh(core_axis_name="c", subcore_axis_name="s")
    n_workers = mesh.num_cores * mesh.num_subcores
    n_steps = -(-n_chunks // n_workers)

    def body(dest_hbm, y_hbm, out_hbm, idx0, idx1, rows0, rows1, sem):
        wid = lax.axis_index("s") * mesh.num_cores + lax.axis_index("c")
        idx_v, rows_v = (idx0, idx1), (rows0, rows1)

        def chunk(j):
            return wid + j * n_workers

        def gather(j, k, slot):
            return pltpu.make_async_copy(y_hbm.at[idx_v[j % 2].at[k]], rows_v[slot], sem.at[slot])

        def store(j, k, slot):
            return pltpu.make_async_copy(rows_v[slot], out_hbm.at[k, pl.ds(chunk(j) * SC_CHUNK, SC_CHUNK)],
                                         sem.at[2 + slot])

        units = [(j, k) for j in range(n_steps) for k in range(TOP_K)]

        def valid(j):
            return chunk(j) < n_chunks

        def load_idx(j):
            pl.when(valid(j))(lambda: pltpu.sync_copy(dest_hbm.at[chunk(j)], idx_v[j % 2]))

        load_idx(0)
        pl.when(valid(0))(gather(0, 0, 0).start)
        for u, (j, k) in enumerate(units):
            slot = u % 2
            pl.when(valid(j))(gather(j, k, slot).wait)
            if u + 1 < len(units):
                jn, kn = units[u + 1]
                if u >= 1:
                    jp, kp = units[u - 1]
                    pl.when(valid(jp))(store(jp, kp, 1 - slot).wait)
                if kn == 0:
                    load_idx(jn)
                pl.when(valid(jn))(gather(jn, kn, 1 - slot).start)
            pl.when(valid(j))(store(j, k, slot).start)
        for u in (len(units) - 2, len(units) - 1):
            j, k = units[u]
            pl.when(valid(j))(store(j, k, u % 2).wait)

    return pl.kernel(
        body,
        out_type=jax.ShapeDtypeStruct((TOP_K, n_chunks * SC_CHUNK, w), y_rows.dtype),
        mesh=mesh,
        scratch_types=[pltpu.VMEM((TOP_K, SC_CHUNK), I32), pltpu.VMEM((TOP_K, SC_CHUNK), I32),
                       pltpu.VMEM((SC_CHUNK, w), y_rows.dtype), pltpu.VMEM((SC_CHUNK, w), y_rows.dtype),
                       pltpu.SemaphoreType.DMA((4,))],
        name="undispatch_sc",
    )(dest3, y_rows)


def _combine_kernel(x2_ref, y4_ref, gates_ref, gf_ref, *out_refs, tile):
    out_ref = out_refs[-1]
    g = jnp.concatenate([gates_ref[...], jnp.zeros((128 - TOP_K, tile), F32)], axis=0)
    gt = g.T
    half = y4_ref.shape[2]
    moe_lo = jnp.zeros((tile, half), F32)
    moe_hi = jnp.zeros((tile, half), F32)
    for k in range(TOP_K):
        lo, hi = _unpack_bf16_pairs(y4_ref[k])
        moe_lo = moe_lo + gt[:, k:k + 1] * lo
        moe_hi = moe_hi + gt[:, k:k + 1] * hi
    x2 = x2_ref[...]
    out_ref[...] = _rms(x2 + jnp.concatenate([moe_lo, moe_hi], axis=1), gf_ref[...])


def _combine(x2, y4, gates, g_final, tile, tok_tile0, n_tiles, y_tile0, out_prev=None):
    n, d = x2.shape
    in_specs = [
        pl.BlockSpec((tile, d), lambda i: (i + tok_tile0, 0)),
        pl.BlockSpec((TOP_K, tile, d // 2), lambda i: (0, i + y_tile0, 0)),
        pl.BlockSpec((TOP_K, tile), lambda i: (0, i + tok_tile0)),
        pl.BlockSpec((1, d), lambda i: (0, 0)),
    ]
    args = [x2, y4, gates, g_final]
    aliases = {}
    if out_prev is not None:
        in_specs.append(pl.BlockSpec(memory_space=pl.ANY))
        args.append(out_prev)
        aliases = {len(args) - 1: 0}
    return pl.pallas_call(
        functools.partial(_combine_kernel, tile=tile),
        grid=(n_tiles,),
        in_specs=in_specs,
        out_specs=pl.BlockSpec((tile, d), lambda i: (i + tok_tile0, 0)),
        out_shape=jax.ShapeDtypeStruct((n, d), F32),
        input_output_aliases=aliases,
        compiler_params=_cparams(("arbitrary",)),
        name="combine",
    )(*args)


def kernel(x_prompt, x_sample, mem_prompt, cache_mem_k, cache_mem_v, state_conv, g_mix, w_in, g_v,
           w_spatial, b_spatial, w_conv, w_out, g_xattn, g_mem, w_q, w_k, w_v, w_xo, g_moe, w_router,
           b_router, w_gate_up, b_gate_up, w_down, b_down, g_final):
    depth = g_mix.shape[0]
    assert depth == 1, "one layer supported"
    b, s, d = x_prompt.shape
    nb, ns, _ = x_sample.shape
    assert ns == 1
    n_mem = mem_prompt.shape[1]
    a_heads, a_hd = g_v.shape[1], g_v.shape[2]
    a_width = a_heads * a_hd
    b_width = w_conv.shape[2]
    assert a_heads == A_HEADS and 2 * a_hd == CHUNK and w_spatial.shape[2] == CHUNK
    assert s % MIX_TILE == 0 and s % ATT_TILE == 0 and (b * n_mem) % KV_TILE == 0
    assert nb % SAMPLE_ATT_GROUP == 0
    l = 0
    row = lambda a: a.reshape(1, -1)

    w_in_b = w_in[l].astype(BF16)
    w_out_b = w_out[l].astype(BF16)
    w_q_b = w_q[l].astype(BF16)
    w_k_b = w_k[l].astype(BF16)
    w_v_b = w_v[l].astype(BF16)
    w_xo_b = w_xo[l].astype(BF16)
    wrt_b = w_router[l].T.astype(BF16)
    br_col = b_router[l].reshape(N_EXPERTS, 1).astype(F32)
    gv_row = row(g_v[l])
    ws = w_spatial[l]
    ws_pairs = jnp.concatenate([ws[0::2], ws[1::2]], axis=2)
    bs_full = jnp.repeat(b_spatial[l].T, a_hd, axis=1)
    w00 = row(jnp.repeat(ws[:, 0, 0], a_hd))
    b0 = row(jnp.repeat(b_spatial[l][:, 0], a_hd))

    mk2, mv2 = _mem_kv(mem_prompt.reshape(b * n_mem, d), row(g_mem[l]), w_k_b, w_v_b)
    x1_p, conv_p = _mixer_prompt(x_prompt, row(g_mix[l]), w_in_b, gv_row, ws_pairs, bs_full,
                                 w_conv[l], w_out_b)
    x2_p, h3_p, tope_p, gates_p = _attn_prompt(
        x1_p, mk2.reshape(b, n_mem, d), mv2.reshape(b, n_mem, d), row(g_xattn[l]), w_q_b, w_xo_b,
        row(g_moe[l]), wrt_b, br_col)

    xs2 = x_sample.reshape(nb, d)
    st = state_conv[l]
    x1_s, v_s, z_s, q_s = _mixer_sample(xs2, st[:, 0], st[:, 1], row(g_mix[l]), w_in_b, gv_row, w00, b0,
                                        w_conv[l], w_out_b, row(g_xattn[l]), w_q_b)
    x_heads, x_hd = cache_mem_k.shape[3], cache_mem_k.shape[4]
    assert depth == 1 and x_heads == X_HEADS
    o_s = _attn_sample(q_s.reshape(nb, x_heads, x_hd), cache_mem_k.reshape(nb, n_mem, x_heads, x_hd),
                       cache_mem_v.reshape(nb, n_mem, x_heads, x_hd)).reshape(nb, d)
    x2_s, h3_s, tope_s, gates_s = _tail_sample(x1_s, o_s, w_xo_b, row(g_moe[l]), wrt_b, br_col)

    t_p = b * s
    t_all = t_p + nb
    t_pad = -(-t_all // ROUTE_TILE) * ROUTE_TILE
    tope_all = jnp.concatenate([tope_p, tope_s, jnp.full((TOP_K, t_pad - t_all), -1, I32)], axis=1)
    dest, cnt = _route(tope_all)
    counts = cnt[:, 0]
    padded = (counts + MOE_BLOCK - 1) // MOE_BLOCK * MOE_BLOCK
    pad_ends = jnp.cumsum(padded)
    pad_starts = pad_ends - padded
    n_blocks = -(-(t_all * TOP_K) // MOE_BLOCK) + N_EXPERTS
    n_rows = n_blocks * MOE_BLOCK
    assert t_all % SC_CHUNK == 0 and t_p % SC_CHUNK == 0
    dest3 = dest[:, :t_all].reshape(TOP_K, t_all // SC_CHUNK, SC_CHUNK).transpose(1, 0, 2)
    xs_rows = _dispatch(dest3, h3_p.reshape(t_p, d // 2), h3_s, n_rows)
    y_rows = _expert_ffn(pad_starts.astype(I32), (padded // MOE_BLOCK).astype(I32), counts.astype(I32),
                         xs_rows, w_gate_up[l], b_gate_up[l], w_down[l], b_down[l])

    t_a = t_p // 2
    assert t_a % SC_CHUNK == 0 and t_a % COMBINE_TILE == 0 and (t_p - t_a) % nb == 0
    y4_a = _undispatch(dest3[:t_a // SC_CHUNK], y_rows)
    y4_b = _undispatch(dest3[t_a // SC_CHUNK:], y_rows)

    gf = row(g_final)
    x2_p2 = x2_p.reshape(t_p, d)
    tiles_a = t_a // COMBINE_TILE
    y_p = _combine(x2_p2, y4_a, gates_p, gf, COMBINE_TILE, 0, tiles_a, 0)
    y_p = _combine(x2_p2, y4_b, gates_p, gf, COMBINE_TILE, tiles_a, t_p // COMBINE_TILE - tiles_a, 0,
                   out_prev=y_p)
    y_s = _combine(x2_s, y4_b, gates_s, gf, nb, 0, 1, (t_p - t_a) // nb)

    x_heads = cache_mem_k.shape[3]
    return (y_p.reshape(b, s, d),
            y_s.reshape(nb, 1, d),
            mk2.reshape(1, b, n_mem, x_heads, d // x_heads),
            mv2.reshape(1, b, n_mem, x_heads, d // x_heads),
            conv_p.reshape(1, b, CONV_W - 1, b_width),
            jnp.stack([st[:, 1], z_s], axis=1).reshape(1, nb, CONV_W - 1, b_width),
            v_s.reshape(1, nb, 1, a_heads, a_hd))
```

```python
import functools

import jax
import jax.numpy as jnp
from jax import lax
from jax.experimental import pallas as pl
from jax.experimental.pallas import tpu as pltpu
from jax.experimental.pallas import tpu_sc as plsc

F32 = jnp.float32
BF16 = jnp.bfloat16
I32 = jnp.int32
U32 = jnp.uint32

A_HEADS = 8
CHUNK = 128
CONV_W = 3
X_HEADS = 4
N_EXPERTS = 32
TOP_K = 4
SWIGLU_LIMIT = 7.0
SWIGLU_ALPHA = 1.702
EPS = 1e-5

MIX_TILE = 1024
ATT_TILE = 1024
KV_TILE = 512
ROUTE_TILE = 512
MOE_BLOCK = 256
COMBINE_TILE = 512
SAMPLE_ATT_GROUP = 4
SC_CHUNK = 64
VMEM_LIMIT = 56 * 1024 * 1024


def _cparams(sem=None):
    return pltpu.CompilerParams(dimension_semantics=sem, vmem_limit_bytes=VMEM_LIMIT)


def _rms(x, g):
    r = lax.rsqrt(jnp.mean(x * x, axis=-1, keepdims=True) + EPS)
    return (x * r) * g


def _gelu(x):
    return 0.5 * x * (1.0 + lax.erf(x * 0.7071067811865476))


def _dot(a, b):
    return jnp.dot(a, b, preferred_element_type=F32)


def _dot_nt(a, b):
    return lax.dot_general(a, b, (((1,), (1,)), ((), ())), preferred_element_type=F32)


def _pack_bf16_pairs(x):
    d = x.shape[1]
    r = x.astype(BF16).astype(F32)
    lo = lax.shift_right_logical(pltpu.bitcast(r[:, :d // 2], U32), jnp.uint32(16))
    hi = lax.bitwise_and(pltpu.bitcast(r[:, d // 2:], U32), jnp.uint32(0xFFFF0000))
    return lax.bitwise_or(lo, hi)


def _unpack_bf16_pairs(w):
    lo = pltpu.bitcast(lax.shift_left(w, jnp.uint32(16)), F32)
    hi = pltpu.bitcast(lax.bitwise_and(w, jnp.uint32(0xFFFF0000)), F32)
    return lo, hi


def _memkv_kernel(m_ref, g_ref, wk_ref, wv_ref, k_ref, v_ref):
    h = _rms(m_ref[...], g_ref[...]).astype(BF16)
    k_ref[...] = _dot(h, wk_ref[...])
    v_ref[...] = _dot(h, wv_ref[...])


def _mem_kv(mem2d, g_mem, wk, wv):
    n, d = mem2d.shape
    row = pl.BlockSpec((KV_TILE, d), lambda i: (i, 0))
    full = lambda shape: pl.BlockSpec(shape, lambda i: (0,) * len(shape))
    return pl.pallas_call(
        _memkv_kernel,
        grid=(n // KV_TILE,),
        in_specs=[row, full((1, d)), full((d, d)), full((d, d))],
        out_specs=[row, row],
        out_shape=[jax.ShapeDtypeStruct((n, d), F32)] * 2,
        compiler_params=_cparams(("arbitrary",)),
        name="mem_kv",
    )(mem2d, g_mem, wk, wv)


def _head_rms(v, gv, a_width):
    hd = a_width // A_HEADS
    r_i = lax.broadcasted_iota(I32, (a_width, a_width), 0) // hd
    c_i = lax.broadcasted_iota(I32, (a_width, a_width), 1) // hd
    ones_bd = jnp.where(r_i == c_i, 1.0, 0.0).astype(BF16)
    sq = v * v
    sq_hi = sq.astype(BF16)
    sq_lo = (sq - sq_hi.astype(F32)).astype(BF16)
    gs = _dot(sq_hi, ones_bd) + _dot(sq_lo, ones_bd)
    return (v * lax.rsqrt(gs * (1.0 / hd) + EPS)) * gv


def _mixer_prompt_kernel(x_ref, gmix_ref, win_ref, gv_ref, ws_ref, bs_ref, wc_ref, wout_ref,
                         x1_ref, cs_ref, zbuf, *, tile, a_width, b_width):
    j = pl.program_id(1)
    nj = pl.num_programs(1)
    x = x_ref[...]
    h = _rms(x, gmix_ref[...]).astype(BF16)
    proj = _dot(h, win_ref[...])
    u = _gelu(proj[:, :a_width])
    v = _head_rms(_gelu(proj[:, a_width:2 * a_width]), gv_ref[...], a_width)
    o = 2 * a_width
    hb = proj[:, o:o + b_width]
    bg = proj[:, o + b_width:o + 2 * b_width]
    cg = proj[:, o + 2 * b_width:o + 3 * b_width]

    lane = lax.broadcasted_iota(I32, (CHUNK, 2 * (a_width // A_HEADS)), 1)
    first = lane < (a_width // A_HEADS)
    t_i = lax.broadcasted_iota(I32, (CHUNK, 2 * CHUNK), 0)
    s_i = lax.broadcasted_iota(I32, (CHUNK, 2 * CHUNK), 1) % CHUNK
    causal = s_i <= t_i
    pair_cols = []
    for p in range(A_HEADS // 2):
        w_pair = jnp.where(causal, ws_ref[p], 0.0).astype(BF16)
        vp = v[:, p * CHUNK:(p + 1) * CHUNK]
        rows = []
        for c in range(tile // CHUNK):
            vc = vp[c * CHUNK:(c + 1) * CHUNK]
            rhs = jnp.concatenate([jnp.where(first, vc, 0.0), jnp.where(first, 0.0, vc)],
                                  axis=0).astype(BF16)
            rows.append(_dot(w_pair, rhs))
        pair_cols.append(jnp.concatenate(rows, axis=0))
    gate = jnp.concatenate(pair_cols, axis=1)
    bias = jnp.concatenate([bs_ref[...]] * (tile // CHUNK), axis=0)
    a_out = u * (gate + bias)

    @pl.when(j == 0)
    def _():
        zbuf[0:8, :] = jnp.zeros((8, b_width), F32)

    z = cg * hb
    zbuf[8:tile + 8, :] = z
    z1 = zbuf[7:tile + 7, :]
    z2 = zbuf[6:tile + 6, :]
    wc = wc_ref[...]
    conv = z2 * wc[0:1] + z1 * wc[1:2] + z * wc[2:3]
    b_out = bg * conv
    tail = zbuf[tile:tile + 8, :]
    zbuf[0:8, :] = tail

    @pl.when(j == nj - 1)
    def _():
        cs_ref[...] = tail[8 - (CONV_W - 1):, :]

    y = _dot(a_out.astype(BF16), wout_ref[0:a_width, :]) + _dot(b_out.astype(BF16), wout_ref[a_width:, :])
    x1_ref[...] = x + y


def _mixer_prompt(x, g_mix, w_in, g_v, ws_pairs, bs_full, w_conv, w_out):
    b, s, d = x.shape
    a_width = g_v.shape[1]
    b_width = w_conv.shape[1]
    in_width = w_in.shape[1]
    tile = MIX_TILE
    full = lambda shape: pl.BlockSpec(shape, lambda i, j: (0,) * len(shape))
    kern = functools.partial(_mixer_prompt_kernel, tile=tile, a_width=a_width, b_width=b_width)
    return pl.pallas_call(
        kern,
        grid=(b, s // tile),
        in_specs=[
            pl.BlockSpec((None, tile, d), lambda i, j: (i, j, 0)),
            full((1, d)), full((d, in_width)), full((1, a_width)),
            full(ws_pairs.shape), full(bs_full.shape), full(w_conv.shape), full(w_out.shape),
        ],
        out_specs=[
            pl.BlockSpec((None, tile, d), lambda i, j: (i, j, 0)),
            pl.BlockSpec((None, CONV_W - 1, b_width), lambda i, j: (i, 0, 0)),
        ],
        out_shape=[jax.ShapeDtypeStruct((b, s, d), F32),
                   jax.ShapeDtypeStruct((b, CONV_W - 1, b_width), F32)],
        scratch_shapes=[pltpu.VMEM((tile + 8, b_width), F32)],
        compiler_params=_cparams(("arbitrary", "arbitrary")),
        name="mixer_prompt",
    )(x, g_mix, w_in, g_v, ws_pairs, bs_full, w_conv, w_out)


def _mixer_sample_kernel(x_ref, s0_ref, s1_ref, gmix_ref, win_ref, gv_ref, w00_ref, b0_ref, wc_ref,
                         wout_ref, gx_ref, wq_ref, x1_ref, v_ref, z_ref, q_ref, *, a_width, b_width):
    x = x_ref[...]
    h = _rms(x, gmix_ref[...]).astype(BF16)
    proj = _dot(h, win_ref[...])
    u = _gelu(proj[:, :a_width])
    v = _head_rms(_gelu(proj[:, a_width:2 * a_width]), gv_ref[...], a_width)
    v_ref[...] = v
    o = 2 * a_width
    hb = proj[:, o:o + b_width]
    bg = proj[:, o + b_width:o + 2 * b_width]
    cg = proj[:, o + 2 * b_width:o + 3 * b_width]
    a_out = u * (v * w00_ref[...] + b0_ref[...])
    z = cg * hb
    z_ref[...] = z
    wc = wc_ref[...]
    conv = s0_ref[...] * wc[0:1] + s1_ref[...] * wc[1:2] + z * wc[2:3]
    b_out = bg * conv
    y = _dot(a_out.astype(BF16), wout_ref[0:a_width, :]) + _dot(b_out.astype(BF16), wout_ref[a_width:, :])
    x1 = x + y
    x1_ref[...] = x1
    q_ref[...] = _dot(_rms(x1, gx_ref[...]).astype(BF16), wq_ref[...])


def _mixer_sample(x, s0, s1, g_mix, w_in, g_v, w00, b0, w_conv, w_out, g_x, w_q):
    n, d = x.shape
    a_width = g_v.shape[1]
    b_width = w_conv.shape[1]
    kern = functools.partial(_mixer_sample_kernel, a_width=a_width, b_width=b_width)
    return pl.pallas_call(
        kern,
        out_shape=[jax.ShapeDtypeStruct((n, d), F32), jax.ShapeDtypeStruct((n, a_width), F32),
                   jax.ShapeDtypeStruct((n, b_width), F32), jax.ShapeDtypeStruct((n, d), F32)],
        compiler_params=_cparams(),
        name="mixer_sample",
    )(x, s0, s1, g_mix, w_in, g_v, w00, b0, w_conv, w_out, g_x, w_q)


def _router_tail(x2, gmoe, wrt, br, h3_ref, tope_ref, gates_ref):
    h3f = _rms(x2, gmoe)
    h3 = h3f.astype(BF16)
    h3_ref[...] = _pack_bf16_pairs(h3f)

    logits = _dot_nt(wrt, h3) + br
    n_e, n_t = logits.shape
    e_idx = lax.broadcasted_iota(I32, (n_e, n_t), 0).astype(F32)
    tops, idxs = [], []
    for _ in range(TOP_K):
        m = jnp.max(logits, axis=0, keepdims=True)
        idx = jnp.min(jnp.where(logits == m, e_idx, float(n_e)), axis=0, keepdims=True)
        tops.append(m)
        idxs.append(idx)
        logits = jnp.where(e_idx == idx, -jnp.inf, logits)
    top = jnp.concatenate(tops, axis=0)
    ex = jnp.exp(top - top[0:1])
    gates_ref[...] = ex / jnp.sum(ex, axis=0, keepdims=True)
    tope_ref[...] = jnp.concatenate(idxs, axis=0).astype(I32)


def _attn_prompt_kernel(x1_ref, k_ref, v_ref, gx_ref, wq_ref, wxo_ref, gmoe_ref, wrt_ref, br_ref,
                        x2_ref, h3_ref, tope_ref, gates_ref):
    x = x1_ref[...]
    d = x.shape[1]
    hd = d // X_HEADS
    q = _dot(_rms(x, gx_ref[...]).astype(BF16), wq_ref[...]).astype(BF16)
    kb = k_ref[...].astype(BF16)
    vb = v_ref[...].astype(BF16)
    outs = []
    for hh in range(X_HEADS):
        sl = slice(hh * hd, (hh + 1) * hd)
        s = _dot_nt(q[:, sl], kb[:, sl]) * (hd ** -0.5)
        e = jnp.exp(s - jnp.max(s, axis=-1, keepdims=True))
        p = e / jnp.sum(e, axis=-1, keepdims=True)
        outs.append(_dot(p.astype(BF16), vb[:, sl]))
    o = jnp.concatenate(outs, axis=1).astype(BF16)
    x2 = x + _dot(o, wxo_ref[...])
    x2_ref[...] = x2
    _router_tail(x2, gmoe_ref[...], wrt_ref[...], br_ref[...], h3_ref, tope_ref, gates_ref)


def _attn_prompt(x1, mk, mv, g_x, w_q, w_xo, g_moe, wrt, br):
    b, s, d = x1.shape
    n_mem = mk.shape[1]
    tile = ATT_TILE
    nq = s // tile
    full = lambda shape: pl.BlockSpec(shape, lambda i, j: (0,) * len(shape))
    tok = pl.BlockSpec((None, tile, d), lambda i, j: (i, j, 0))
    mem = pl.BlockSpec((None, n_mem, d), lambda i, j: (i, 0, 0))
    lanes = pl.BlockSpec((TOP_K, tile), lambda i, j: (0, i * nq + j))
    return pl.pallas_call(
        _attn_prompt_kernel,
        grid=(b, nq),
        in_specs=[tok, mem, mem, full((1, d)), full((d, d)), full((d, d)), full((1, d)),
                  full(wrt.shape), full(br.shape)],
        out_specs=[tok, pl.BlockSpec((None, tile, d // 2), lambda i, j: (i, j, 0)), lanes, lanes],
        out_shape=[jax.ShapeDtypeStruct((b, s, d), F32), jax.ShapeDtypeStruct((b, s, d // 2), U32),
                   jax.ShapeDtypeStruct((TOP_K, b * s), I32), jax.ShapeDtypeStruct((TOP_K, b * s), F32)],
        compiler_params=_cparams(("arbitrary", "arbitrary")),
        name="attn_prompt",
    )(x1, mk, mv, g_x, w_q, w_xo, g_moe, wrt, br)


def _attn_sample_kernel(q_ref, k_ref, v_ref, o_ref, *, group, heads, scale):
    for g in range(group):
        q = q_ref[g] * scale
        k = k_ref[g]
        part = jnp.broadcast_to(jnp.sum(k * q[None], axis=-1, keepdims=True), k.shape)
        s = part + pltpu.roll(part, heads, axis=1)
        e = jnp.exp(s - jnp.max(s, axis=0, keepdims=True))
        o_ref[g] = jnp.sum(e * v_ref[g], axis=0) / jnp.sum(e, axis=0)


def _to_rows8(a, heads, hd):
    lead = a.shape[:-2]
    n = len(lead)
    a = a.reshape(*lead, heads, hd // 128, 128)
    a = a.transpose(*range(n), n + 1, n, n + 2)
    return a.reshape(*lead, (hd // 128) * heads, 128)


def _from_rows8(a, heads, hd):
    lead = a.shape[:-2]
    n = len(lead)
    a = a.reshape(*lead, hd // 128, heads, 128)
    a = a.transpose(*range(n), n + 1, n, n + 2)
    return a.reshape(*lead, heads, hd)


def _attn_sample(q, ck, cv):
    n, heads, hd = q.shape
    assert hd == 2 * 128
    n_mem = ck.shape[1]
    group = SAMPLE_ATT_GROUP
    rows = (hd // 128) * heads
    qo = pl.BlockSpec((group, rows, 128), lambda i: (i, 0, 0))
    kv = pl.BlockSpec((group, n_mem, rows, 128), lambda i: (i, 0, 0, 0))
    o = pl.pallas_call(
        functools.partial(_attn_sample_kernel, group=group, heads=heads, scale=hd ** -0.5),
        grid=(n // group,),
        in_specs=[qo, kv, kv],
        out_specs=qo,
        out_shape=jax.ShapeDtypeStruct((n, rows, 128), F32),
        compiler_params=_cparams(("arbitrary",)),
        name="attn_sample",
    )(_to_rows8(q, heads, hd), _to_rows8(ck, heads, hd), _to_rows8(cv, heads, hd))
    return _from_rows8(o, heads, hd)


def _tail_sample_kernel(x1_ref, o_ref, wxo_ref, gmoe_ref, wrt_ref, br_ref,
                        x2_ref, h3_ref, tope_ref, gates_ref):
    x2 = x1_ref[...] + _dot(o_ref[...].astype(BF16), wxo_ref[...])
    x2_ref[...] = x2
    _router_tail(x2, gmoe_ref[...], wrt_ref[...], br_ref[...], h3_ref, tope_ref, gates_ref)


def _tail_sample(x1, o, w_xo, g_moe, wrt, br):
    n, d = x1.shape
    return pl.pallas_call(
        _tail_sample_kernel,
        out_shape=[jax.ShapeDtypeStruct((n, d), F32), jax.ShapeDtypeStruct((n, d // 2), U32),
                   jax.ShapeDtypeStruct((TOP_K, n), I32), jax.ShapeDtypeStruct((TOP_K, n), F32)],
        compiler_params=_cparams(),
        name="tail_sample",
    )(x1, o, w_xo, g_moe, wrt, br)


def _route_kernel(tope_ref, dest_ref, cnt_ref, *, n_blocks, tile, moe_block):
    e_idx = lax.broadcasted_iota(I32, (N_EXPERTS, tile), 0)
    earlier = jnp.where(lax.broadcasted_iota(I32, (tile, tile), 0)
                        < lax.broadcasted_iota(I32, (tile, tile), 1), 1.0, 0.0).astype(BF16)

    def onehot(k, off):
        return e_idx == tope_ref[pl.ds(k, 1), pl.ds(off, tile)]

    def count_body(j, cnt):
        off = pl.multiple_of(j * tile, tile)
        for k in range(TOP_K):
            cnt = cnt + jnp.sum(jnp.where(onehot(k, off), 1.0, 0.0), axis=1, keepdims=True)
        return cnt

    counts = lax.fori_loop(0, n_blocks, count_body, jnp.zeros((N_EXPERTS, 1), F32))
    cnt_ref[...] = jnp.broadcast_to(counts, cnt_ref.shape).astype(I32)

    n_blk = jnp.floor((counts + (moe_block - 1)) * (1.0 / moe_block))
    n_hi = jnp.floor(n_blk * (1.0 / 16.0))
    n_lo = n_blk - 16.0 * n_hi
    below = jnp.where(lax.broadcasted_iota(I32, (N_EXPERTS, N_EXPERTS), 1)
                      < lax.broadcasted_iota(I32, (N_EXPERTS, N_EXPERTS), 0), 1.0, 0.0).astype(BF16)
    wide = lambda c: jnp.broadcast_to(c, (N_EXPERTS, 128)).astype(BF16)
    start = (16.0 * _dot(below, wide(n_hi)) + _dot(below, wide(n_lo)))[:, 0:1] * float(moe_block)

    def dest_body(j, run):
        off = pl.multiple_of(j * tile, tile)
        for k in range(TOP_K):
            oh = onehot(k, off)
            ohf = jnp.where(oh, 1.0, 0.0)
            before = _dot(ohf.astype(BF16), earlier)
            dest = jnp.sum(jnp.where(oh, before + run, 0.0), axis=0, keepdims=True)
            dest_ref[pl.ds(k, 1), pl.ds(off, tile)] = dest.astype(I32)
            run = run + jnp.sum(ohf, axis=1, keepdims=True)
        return run

    lax.fori_loop(0, n_blocks, dest_body, start)


def _route(tope_pad):
    t_pad = tope_pad.shape[1]
    kern = functools.partial(_route_kernel, n_blocks=t_pad // ROUTE_TILE, tile=ROUTE_TILE,
                             moe_block=MOE_BLOCK)
    return pl.pallas_call(
        kern,
        out_shape=[jax.ShapeDtypeStruct((TOP_K, t_pad), I32),
                   jax.ShapeDtypeStruct((N_EXPERTS, 128), I32)],
        compiler_params=_cparams(),
        name="route",
    )(tope_pad)


def _dispatch(dest3, h_prompt, h_sample, n_rows):
    w = h_prompt.shape[1]
    n_prompt_chunks = h_prompt.shape[0] // SC_CHUNK
    n_chunks = dest3.shape[0]
    mesh = plsc.VectorSubcoreMesh(core_axis_name="c", subcore_axis_name="s")
    n_workers = mesh.num_cores * mesh.num_subcores

    n_steps = -(-n_chunks // n_workers)

    def body(dest_hbm, hp_hbm, hs_hbm, xs_hbm, idx0, idx1, rows0, rows1, sem):
        wid = lax.axis_index("s") * mesh.num_cores + lax.axis_index("c")
        idx_v, rows_v = (idx0, idx1), (rows0, rows1)

        def chunk(j):
            return wid + j * n_workers

        def loads(j, slot):
            c = chunk(j)
            prompt = (hp_hbm, c)
            sample = (hs_hbm, c - n_prompt_chunks)
            return c, [
                (pltpu.make_async_copy(dest_hbm.at[c], idx_v[slot], sem.at[slot]), None),
                (pltpu.make_async_copy(prompt[0].at[pl.ds(prompt[1] * SC_CHUNK, SC_CHUNK)], rows_v[slot],
                                       sem.at[slot]), c < n_prompt_chunks),
                (pltpu.make_async_copy(sample[0].at[pl.ds(sample[1] * SC_CHUNK, SC_CHUNK)], rows_v[slot],
                                       sem.at[slot]), c >= n_prompt_chunks),
            ]

        def start_loads(j, slot):
            c, copies = loads(j, slot)
            for cp, cond in copies:
                pl.when((c < n_chunks) if cond is None else ((c < n_chunks) & cond))(cp.start)

        def wait_loads(j, slot):
            c, copies = loads(j, slot)
            for cp, cond in copies:
                pl.when((c < n_chunks) if cond is None else ((c < n_chunks) & cond))(cp.wait)

        def scatters(slot):
            return [pltpu.make_async_copy(rows_v[slot], xs_hbm.at[idx_v[slot].at[k]], sem.at[2 + slot])
                    for k in range(TOP_K)]

        start_loads(0, 0)
        for j in range(n_steps):
            slot = j % 2
            wait_loads(j, slot)
            if j >= 1:
                @pl.when(chunk(j - 1) < n_chunks)
                def _():
                    for cp in scatters(1 - slot):
                        cp.wait()
            if j + 1 < n_steps:
                start_loads(j + 1, 1 - slot)

            @pl.when(chunk(j) < n_chunks)
            def _():
                for cp in scatters(slot):
                    cp.start()

        @pl.when(chunk(n_steps - 1) < n_chunks)
        def _():
            for cp in scatters((n_steps - 1) % 2):
                cp.wait()

    return pl.kernel(
        body,
        out_type=jax.ShapeDtypeStruct((n_rows, w), U32),
        mesh=mesh,
        scratch_types=[pltpu.VMEM((TOP_K, SC_CHUNK), I32), pltpu.VMEM((TOP_K, SC_CHUNK), I32),
                       pltpu.VMEM((SC_CHUNK, w), U32), pltpu.VMEM((SC_CHUNK, w), U32),
                       pltpu.SemaphoreType.DMA((4,))],
        name="dispatch_sc",
    )(dest3, h_prompt, h_sample)


def _ffn_kernel(start_ref, nblk_ref, count_ref, b1_ref, b2_ref, xs_hbm, w1_hbm, w2_hbm, y_hbm,
                w1f, w2f, w1b, w2b, xbuf, ybuf, sem_w, sem_x, sem_y, *, blk):
    e = pl.program_id(0)
    n_e = pl.num_programs(0)
    nb = nblk_ref[e]
    base = start_ref[e]
    wslot = e % 2

    def w_copies(expert, slot):
        return (pltpu.make_async_copy(w1_hbm.at[expert], w1f.at[slot], sem_w.at[slot]),
                pltpu.make_async_copy(w2_hbm.at[expert], w2f.at[slot], sem_w.at[slot]))

    def x_copy(row0, slot):
        return pltpu.make_async_copy(xs_hbm.at[pl.ds(pl.multiple_of(row0, blk), blk)], xbuf.at[slot],
                                     sem_x.at[slot])

    def y_copy(row0, slot):
        return pltpu.make_async_copy(ybuf.at[slot], y_hbm.at[pl.ds(pl.multiple_of(row0, blk), blk)],
                                     sem_y.at[slot])

    @pl.when(e == 0)
    def _():
        for cp in w_copies(0, 0):
            cp.start()

        @pl.when(nb > 0)
        def _():
            x_copy(base, 0).start()

    for cp in w_copies(e, wslot):
        cp.wait()

    @pl.when(e + 1 < n_e)
    def _():
        for cp in w_copies(e + 1, 1 - wslot):
            cp.start(priority=1)

    w1b[...] = w1f[wslot].astype(BF16)
    w2b[...] = w2f[wslot].astype(BF16)
    b1 = b1_ref[e]
    b2 = b2_ref[e]

    def block(j, slot):
        x_copy(base, slot).wait()

        @pl.when(j + 1 < nb)
        def _():
            x_copy(base + (j + 1) * blk, 1 - slot).start()

        @pl.when(j >= 2)
        def _():
            y_copy(base, slot).wait()

        w = xbuf[slot]
        row = lax.broadcasted_iota(I32, w.shape, 0)
        w = jnp.where(row < count_ref[e] - j * blk, w, jnp.uint32(0))
        half = w.shape[1]
        d_e = w2b.shape[0]
        lo, hi = _unpack_bf16_pairs(w)
        gu = _dot(lo.astype(BF16), w1b[0:half, :]) + _dot(hi.astype(BF16), w1b[half:, :]) + b1
        gate = jnp.minimum(gu[:, :d_e], SWIGLU_LIMIT)
        up = jnp.clip(gu[:, d_e:], -SWIGLU_LIMIT, SWIGLU_LIMIT)
        glu = gate * jax.nn.sigmoid(gate * SWIGLU_ALPHA)
        act = ((up + 1.0) * glu).astype(BF16)
        ybuf[slot] = _pack_bf16_pairs(_dot(act, w2b[...]) + b2)
        y_copy(base + j * blk, slot).start()

    def pair(j2, carry):
        j = 2 * j2
        block(j, 0)

        @pl.when(j + 1 < nb)
        def _():
            block(j + 1, 1)

        return carry

    lax.fori_loop(0, (nb + 1) // 2, pair, 0)

    @pl.when(nb >= 2)
    def _():
        y_copy(base, 0).wait()
        y_copy(base, 1).wait()

    @pl.when(nb == 1)
    def _():
        y_copy(base, 0).wait()

    nxt = jnp.minimum(e + 1, n_e - 1)

    @pl.when((e + 1 < n_e) & (nblk_ref[nxt] > 0))
    def _():
        x_copy(start_ref[nxt], 0).start()


def _expert_ffn(grp_start, grp_blocks, grp_count, xs, w1, b1, w2, b2):
    n_rows, half = xs.shape
    n_e, d, d2 = w1.shape
    d_e = w2.shape[1]
    blk = MOE_BLOCK
    whole = lambda shape: pl.BlockSpec(shape, lambda e, s, n, c: (0,) * len(shape))
    hbm = pl.BlockSpec(memory_space=pl.ANY)
    grid_spec = pltpu.PrefetchScalarGridSpec(
        num_scalar_prefetch=3,
        grid=(n_e,),
        in_specs=[whole((n_e, 1, d2)), whole((n_e, 1, d)), hbm, hbm, hbm],
        out_specs=hbm,
        scratch_shapes=[pltpu.VMEM((2, d, d2), F32), pltpu.VMEM((2, d_e, d), F32),
                        pltpu.VMEM((d, d2), BF16), pltpu.VMEM((d_e, d), BF16),
                        pltpu.VMEM((2, blk, half), U32), pltpu.VMEM((2, blk, d // 2), U32),
                        pltpu.SemaphoreType.DMA((2,)), pltpu.SemaphoreType.DMA((2,)),
                        pltpu.SemaphoreType.DMA((2,))],
    )
    return pl.pallas_call(
        functools.partial(_ffn_kernel, blk=blk),
        grid_spec=grid_spec,
        out_shape=jax.ShapeDtypeStruct((n_rows, d // 2), U32),
        compiler_params=_cparams(("arbitrary",)),
        name="expert_ffn",
    )(grp_start, grp_blocks, grp_count, b1.reshape(n_e, 1, d2), b2.reshape(n_e, 1, d), xs, w1, w2)


def _undispatch(dest3, y_rows):
    n_chunks = dest3.shape[0]
    w = y_rows.shape[1]
    mesh = plsc.VectorSubcoreMesh(core_axis_name="c", subcore_axis_name="s")
    n_workers = mesh.num_cores * mesh.num_subcores
    n_steps = -(-n_chunks // n_workers)

    def body(dest_hbm, y_hbm, out_hbm, idx0, idx1, rows0, rows1, sem):
        wid = lax.axis_index("s") * mesh.num_cores + lax.axis_index("c")
        idx_v, rows_v = (idx0, idx1), (rows0, rows1)

        def chunk(j):
            return wid + j * n_workers

        def gather(j, k, slot):
            return pltpu.make_async_copy(y_hbm.at[idx_v[j % 2].at[k]], rows_v[slot], sem.at[slot])

        def store(j, k, slot):
            return pltpu.make_async_copy(rows_v[slot], out_hbm.at[k, pl.ds(chunk(j) * SC_CHUNK, SC_CHUNK)],
                                         sem.at[2 + slot])

        units = [(j, k) for j in range(n_steps) for k in range(TOP_K)]

        def valid(j):
            return chunk(j) < n_chunks

        def load_idx(j):
            pl.when(valid(j))(lambda: pltpu.sync_copy(dest_hbm.at[chunk(j)], idx_v[j % 2]))

        load_idx(0)
        pl.when(valid(0))(gather(0, 0, 0).start)
        for u, (j, k) in enumerate(units):
            slot = u % 2
            pl.when(valid(j))(gather(j, k, slot).wait)
            if u + 1 < len(units):
                jn, kn = units[u + 1]
                if u >= 1:
                    jp, kp = units[u - 1]
                    pl.when(valid(jp))(store(jp, kp, 1 - slot).wait)
                if kn == 0:
                    load_idx(jn)
                pl.when(valid(jn))(gather(jn, kn, 1 - slot).start)
            pl.when(valid(j))(store(j, k, slot).start)
        for u in (len(units) - 2, len(units) - 1):
            j, k = units[u]
            pl.when(valid(j))(store(j, k, u % 2).wait)

    return pl.kernel(
        body,
        out_type=jax.ShapeDtypeStruct((TOP_K, n_chunks * SC_CHUNK, w), y_rows.dtype),
        mesh=mesh,
        scratch_types=[pltpu.VMEM((TOP_K, SC_CHUNK), I32), pltpu.VMEM((TOP_K, SC_CHUNK), I32),
                       pltpu.VMEM((SC_CHUNK, w), y_rows.dtype), pltpu.VMEM((SC_CHUNK, w), y_rows.dtype),
                       pltpu.SemaphoreType.DMA((4,))],
        name="undispatch_sc",
    )(dest3, y_rows)


def _combine_kernel(x2_ref, y4_ref, gates_ref, gf_ref, out_ref, *, tile):
    g = jnp.concatenate([gates_ref[...], jnp.zeros((128 - TOP_K, tile), F32)], axis=0)
    gt = g.T
    half = y4_ref.shape[2]
    moe_lo = jnp.zeros((tile, half), F32)
    moe_hi = jnp.zeros((tile, half), F32)
    for k in range(TOP_K):
        lo, hi = _unpack_bf16_pairs(y4_ref[k])
        moe_lo = moe_lo + gt[:, k:k + 1] * lo
        moe_hi = moe_hi + gt[:, k:k + 1] * hi
    x2 = x2_ref[...]
    out_ref[...] = _rms(x2 + jnp.concatenate([moe_lo, moe_hi], axis=1), gf_ref[...])


def _combine(x2, y4, gates, g_final, tile, first_tile):
    n, d = x2.shape
    return pl.pallas_call(
        functools.partial(_combine_kernel, tile=tile),
        grid=(n // tile,),
        in_specs=[
            pl.BlockSpec((tile, d), lambda i: (i, 0)),
            pl.BlockSpec((TOP_K, tile, d // 2), lambda i: (0, i + first_tile, 0)),
            pl.BlockSpec((TOP_K, tile), lambda i: (0, i)),
            pl.BlockSpec((1, d), lambda i: (0, 0)),
        ],
        out_specs=pl.BlockSpec((tile, d), lambda i: (i, 0)),
        out_shape=jax.ShapeDtypeStruct((n, d), F32),
        compiler_params=_cparams(("arbitrary",)),
        name="combine",
    )(x2, y4, gates, g_final)


def kernel(x_prompt, x_sample, mem_prompt, cache_mem_k, cache_mem_v, state_conv, g_mix, w_in, g_v,
           w_spatial, b_spatial, w_conv, w_out, g_xattn, g_mem, w_q, w_k, w_v, w_xo, g_moe, w_router,
           b_router, w_gate_up, b_gate_up, w_down, b_down, g_final):
    depth = g_mix.shape[0]
    assert depth == 1, "one layer supported"
    b, s, d = x_prompt.shape
    nb, ns, _ = x_sample.shape
    assert ns == 1
    n_mem = mem_prompt.shape[1]
    a_heads, a_hd = g_v.shape[1], g_v.shape[2]
    a_width = a_heads * a_hd
    b_width = w_conv.shape[2]
    assert a_heads == A_HEADS and 2 * a_hd == CHUNK and w_spatial.shape[2] == CHUNK
    assert s % MIX_TILE == 0 and s % ATT_TILE == 0 and (b * n_mem) % KV_TILE == 0
    assert nb % SAMPLE_ATT_GROUP == 0
    l = 0
    row = lambda a: a.reshape(1, -1)

    w_in_b = w_in[l].astype(BF16)
    w_out_b = w_out[l].astype(BF16)
    w_q_b = w_q[l].astype(BF16)
    w_k_b = w_k[l].astype(BF16)
    w_v_b = w_v[l].astype(BF16)
    w_xo_b = w_xo[l].astype(BF16)
    wrt_b = w_router[l].T.astype(BF16)
    br_col = b_router[l].reshape(N_EXPERTS, 1).astype(F32)
    gv_row = row(g_v[l])
    ws = w_spatial[l]
    ws_pairs = jnp.concatenate([ws[0::2], ws[1::2]], axis=2)
    bs_full = jnp.repeat(b_spatial[l].T, a_hd, axis=1)
    w00 = row(jnp.repeat(ws[:, 0, 0], a_hd))
    b0 = row(jnp.repeat(b_spatial[l][:, 0], a_hd))

    mk2, mv2 = _mem_kv(mem_prompt.reshape(b * n_mem, d), row(g_mem[l]), w_k_b, w_v_b)
    x1_p, conv_p = _mixer_prompt(x_prompt, row(g_mix[l]), w_in_b, gv_row, ws_pairs, bs_full,
                                 w_conv[l], w_out_b)
    x2_p, h3_p, tope_p, gates_p = _attn_prompt(
        x1_p, mk2.reshape(b, n_mem, d), mv2.reshape(b, n_mem, d), row(g_xattn[l]), w_q_b, w_xo_b,
        row(g_moe[l]), wrt_b, br_col)

    xs2 = x_sample.reshape(nb, d)
    st = state_conv[l]
    x1_s, v_s, z_s, q_s = _mixer_sample(xs2, st[:, 0], st[:, 1], row(g_mix[l]), w_in_b, gv_row, w00, b0,
                                        w_conv[l], w_out_b, row(g_xattn[l]), w_q_b)
    x_heads, x_hd = cache_mem_k.shape[3], cache_mem_k.shape[4]
    assert depth == 1 and x_heads == X_HEADS
    o_s = _attn_sample(q_s.reshape(nb, x_heads, x_hd), cache_mem_k.reshape(nb, n_mem, x_heads, x_hd),
                       cache_mem_v.reshape(nb, n_mem, x_heads, x_hd)).reshape(nb, d)
    x2_s, h3_s, tope_s, gates_s = _tail_sample(x1_s, o_s, w_xo_b, row(g_moe[l]), wrt_b, br_col)

    t_p = b * s
    t_all = t_p + nb
    t_pad = -(-t_all // ROUTE_TILE) * ROUTE_TILE
    tope_all = jnp.concatenate([tope_p, tope_s, jnp.full((TOP_K, t_pad - t_all), -1, I32)], axis=1)
    dest, cnt = _route(tope_all)
    counts = cnt[:, 0]
    padded = (counts + MOE_BLOCK - 1) // MOE_BLOCK * MOE_BLOCK
    pad_ends = jnp.cumsum(padded)
    pad_starts = pad_ends - padded
    n_blocks = -(-(t_all * TOP_K) // MOE_BLOCK) + N_EXPERTS
    n_rows = n_blocks * MOE_BLOCK
    assert t_all % SC_CHUNK == 0 and t_p % SC_CHUNK == 0
    dest3 = dest[:, :t_all].reshape(TOP_K, t_all // SC_CHUNK, SC_CHUNK).transpose(1, 0, 2)
    xs_rows = _dispatch(dest3, h3_p.reshape(t_p, d // 2), h3_s, n_rows)
    y_rows = _expert_ffn(pad_starts.astype(I32), (padded // MOE_BLOCK).astype(I32), counts.astype(I32),
                         xs_rows, w_gate_up[l], b_gate_up[l], w_down[l], b_down[l])

    y4 = _undispatch(dest3, y_rows)

    gf = row(g_final)
    assert t_p % COMBINE_TILE == 0 and t_p % nb == 0
    y_p = _combine(x2_p.reshape(t_p, d), y4, gates_p, gf, COMBINE_TILE, 0)
    y_s = _combine(x2_s, y4, gates_s, gf, nb, t_p // nb)

    x_heads = cache_mem_k.shape[3]
    return (y_p.reshape(b, s, d),
            y_s.reshape(nb, 1, d),
            mk2.reshape(1, b, n_mem, x_heads, d // x_heads),
            mv2.reshape(1, b, n_mem, x_heads, d // x_heads),
            conv_p.reshape(1, b, CONV_W - 1, b_width),
            jnp.stack([st[:, 1], z_s], axis=1).reshape(1, nb, CONV_W - 1, b_width),
            v_s.reshape(1, nb, 1, a_heads, a_hd))
```

```python
import functools

import jax
import jax.numpy as jnp
from jax import lax
from jax.experimental import pallas as pl
from jax.experimental.pallas import tpu as pltpu
from jax.experimental.pallas import tpu_sc as plsc

F32 = jnp.float32
BF16 = jnp.bfloat16
I32 = jnp.int32
U32 = jnp.uint32

A_HEADS = 8
CHUNK = 128
CONV_W = 3
X_HEADS = 4
N_EXPERTS = 32
TOP_K = 4
SWIGLU_LIMIT = 7.0
SWIGLU_ALPHA = 1.702
EPS = 1e-5

MIX_TILE = 1024
ATT_TILE = 1024
KV_TILE = 512
ROUTE_TILE = 512
MOE_BLOCK = 256
COMBINE_TILE = 512
SAMPLE_ATT_GROUP = 8
SC_CHUNK = 64
VMEM_LIMIT = 56 * 1024 * 1024


def _cparams(sem=None):
    return pltpu.CompilerParams(dimension_semantics=sem, vmem_limit_bytes=VMEM_LIMIT)


def _rms(x, g):
    r = lax.rsqrt(jnp.mean(x * x, axis=-1, keepdims=True) + EPS)
    return (x * r) * g


def _gelu(x):
    return 0.5 * x * (1.0 + lax.erf(x * 0.7071067811865476))


def _dot(a, b):
    return jnp.dot(a, b, preferred_element_type=F32)


def _dot_nt(a, b):
    return lax.dot_general(a, b, (((1,), (1,)), ((), ())), preferred_element_type=F32)


def _pack_bf16_pairs(x):
    d = x.shape[1]
    r = x.astype(BF16).astype(F32)
    lo = lax.shift_right_logical(pltpu.bitcast(r[:, :d // 2], U32), jnp.uint32(16))
    hi = lax.bitwise_and(pltpu.bitcast(r[:, d // 2:], U32), jnp.uint32(0xFFFF0000))
    return lax.bitwise_or(lo, hi)


def _unpack_bf16_pairs(w):
    lo = pltpu.bitcast(lax.shift_left(w, jnp.uint32(16)), F32)
    hi = pltpu.bitcast(lax.bitwise_and(w, jnp.uint32(0xFFFF0000)), F32)
    return lo, hi


def _memkv_kernel(m_ref, g_ref, wk_ref, wv_ref, k_ref, v_ref):
    h = _rms(m_ref[...], g_ref[...]).astype(BF16)
    k_ref[...] = _dot(h, wk_ref[...])
    v_ref[...] = _dot(h, wv_ref[...])


def _mem_kv(mem2d, g_mem, wk, wv):
    n, d = mem2d.shape
    row = pl.BlockSpec((KV_TILE, d), lambda i: (i, 0))
    full = lambda shape: pl.BlockSpec(shape, lambda i: (0,) * len(shape))
    return pl.pallas_call(
        _memkv_kernel,
        grid=(n // KV_TILE,),
        in_specs=[row, full((1, d)), full((d, d)), full((d, d))],
        out_specs=[row, row],
        out_shape=[jax.ShapeDtypeStruct((n, d), F32)] * 2,
        compiler_params=_cparams(("arbitrary",)),
        name="mem_kv",
    )(mem2d, g_mem, wk, wv)


def _head_rms(v, gv, a_width):
    hd = a_width // A_HEADS
    r_i = lax.broadcasted_iota(I32, (a_width, a_width), 0) // hd
    c_i = lax.broadcasted_iota(I32, (a_width, a_width), 1) // hd
    ones_bd = jnp.where(r_i == c_i, 1.0, 0.0).astype(BF16)
    sq = v * v
    sq_hi = sq.astype(BF16)
    sq_lo = (sq - sq_hi.astype(F32)).astype(BF16)
    gs = _dot(sq_hi, ones_bd) + _dot(sq_lo, ones_bd)
    return (v * lax.rsqrt(gs * (1.0 / hd) + EPS)) * gv


def _mixer_prompt_kernel(x_ref, gmix_ref, win_ref, gv_ref, ws_ref, bs_ref, wc_ref, wout_ref,
                         x1_ref, cs_ref, zbuf, *, tile, a_width, b_width):
    j = pl.program_id(1)
    nj = pl.num_programs(1)
    x = x_ref[...]
    h = _rms(x, gmix_ref[...]).astype(BF16)
    proj = _dot(h, win_ref[...])
    u = _gelu(proj[:, :a_width])
    v = _head_rms(_gelu(proj[:, a_width:2 * a_width]), gv_ref[...], a_width)
    o = 2 * a_width
    hb = proj[:, o:o + b_width]
    bg = proj[:, o + b_width:o + 2 * b_width]
    cg = proj[:, o + 2 * b_width:o + 3 * b_width]

    lane = lax.broadcasted_iota(I32, (CHUNK, 2 * (a_width // A_HEADS)), 1)
    first = lane < (a_width // A_HEADS)
    t_i = lax.broadcasted_iota(I32, (CHUNK, 2 * CHUNK), 0)
    s_i = lax.broadcasted_iota(I32, (CHUNK, 2 * CHUNK), 1) % CHUNK
    causal = s_i <= t_i
    pair_cols = []
    for p in range(A_HEADS // 2):
        w_pair = jnp.where(causal, ws_ref[p], 0.0).astype(BF16)
        vp = v[:, p * CHUNK:(p + 1) * CHUNK]
        rows = []
        for c in range(tile // CHUNK):
            vc = vp[c * CHUNK:(c + 1) * CHUNK]
            rhs = jnp.concatenate([jnp.where(first, vc, 0.0), jnp.where(first, 0.0, vc)],
                                  axis=0).astype(BF16)
            rows.append(_dot(w_pair, rhs))
        pair_cols.append(jnp.concatenate(rows, axis=0))
    gate = jnp.concatenate(pair_cols, axis=1)
    bias = jnp.concatenate([bs_ref[...]] * (tile // CHUNK), axis=0)
    a_out = u * (gate + bias)

    @pl.when(j == 0)
    def _():
        zbuf[0:8, :] = jnp.zeros((8, b_width), F32)

    z = cg * hb
    zbuf[8:tile + 8, :] = z
    z1 = zbuf[7:tile + 7, :]
    z2 = zbuf[6:tile + 6, :]
    wc = wc_ref[...]
    conv = z2 * wc[0:1] + z1 * wc[1:2] + z * wc[2:3]
    b_out = bg * conv
    tail = zbuf[tile:tile + 8, :]
    zbuf[0:8, :] = tail

    @pl.when(j == nj - 1)
    def _():
        cs_ref[...] = tail[8 - (CONV_W - 1):, :]

    y = _dot(a_out.astype(BF16), wout_ref[0:a_width, :]) + _dot(b_out.astype(BF16), wout_ref[a_width:, :])
    x1_ref[...] = x + y


def _mixer_prompt(x, g_mix, w_in, g_v, ws_pairs, bs_full, w_conv, w_out):
    b, s, d = x.shape
    a_width = g_v.shape[1]
    b_width = w_conv.shape[1]
    in_width = w_in.shape[1]
    tile = MIX_TILE
    full = lambda shape: pl.BlockSpec(shape, lambda i, j: (0,) * len(shape))
    kern = functools.partial(_mixer_prompt_kernel, tile=tile, a_width=a_width, b_width=b_width)
    return pl.pallas_call(
        kern,
        grid=(b, s // tile),
        in_specs=[
            pl.BlockSpec((None, tile, d), lambda i, j: (i, j, 0)),
            full((1, d)), full((d, in_width)), full((1, a_width)),
            full(ws_pairs.shape), full(bs_full.shape), full(w_conv.shape), full(w_out.shape),
        ],
        out_specs=[
            pl.BlockSpec((None, tile, d), lambda i, j: (i, j, 0)),
            pl.BlockSpec((None, CONV_W - 1, b_width), lambda i, j: (i, 0, 0)),
        ],
        out_shape=[jax.ShapeDtypeStruct((b, s, d), F32),
                   jax.ShapeDtypeStruct((b, CONV_W - 1, b_width), F32)],
        scratch_shapes=[pltpu.VMEM((tile + 8, b_width), F32)],
        compiler_params=_cparams(("arbitrary", "arbitrary")),
        name="mixer_prompt",
    )(x, g_mix, w_in, g_v, ws_pairs, bs_full, w_conv, w_out)


def _mixer_sample_kernel(x_ref, s0_ref, s1_ref, gmix_ref, win_ref, gv_ref, w00_ref, b0_ref, wc_ref,
                         wout_ref, gx_ref, wq_ref, x1_ref, v_ref, z_ref, q_ref, *, a_width, b_width):
    x = x_ref[...]
    h = _rms(x, gmix_ref[...]).astype(BF16)
    proj = _dot(h, win_ref[...])
    u = _gelu(proj[:, :a_width])
    v = _head_rms(_gelu(proj[:, a_width:2 * a_width]), gv_ref[...], a_width)
    v_ref[...] = v
    o = 2 * a_width
    hb = proj[:, o:o + b_width]
    bg = proj[:, o + b_width:o + 2 * b_width]
    cg = proj[:, o + 2 * b_width:o + 3 * b_width]
    a_out = u * (v * w00_ref[...] + b0_ref[...])
    z = cg * hb
    z_ref[...] = z
    wc = wc_ref[...]
    conv = s0_ref[...] * wc[0:1] + s1_ref[...] * wc[1:2] + z * wc[2:3]
    b_out = bg * conv
    y = _dot(a_out.astype(BF16), wout_ref[0:a_width, :]) + _dot(b_out.astype(BF16), wout_ref[a_width:, :])
    x1 = x + y
    x1_ref[...] = x1
    q_ref[...] = _dot(_rms(x1, gx_ref[...]).astype(BF16), wq_ref[...])


def _mixer_sample(x, s0, s1, g_mix, w_in, g_v, w00, b0, w_conv, w_out, g_x, w_q):
    n, d = x.shape
    a_width = g_v.shape[1]
    b_width = w_conv.shape[1]
    kern = functools.partial(_mixer_sample_kernel, a_width=a_width, b_width=b_width)
    return pl.pallas_call(
        kern,
        out_shape=[jax.ShapeDtypeStruct((n, d), F32), jax.ShapeDtypeStruct((n, a_width), F32),
                   jax.ShapeDtypeStruct((n, b_width), F32), jax.ShapeDtypeStruct((n, d), F32)],
        compiler_params=_cparams(),
        name="mixer_sample",
    )(x, s0, s1, g_mix, w_in, g_v, w00, b0, w_conv, w_out, g_x, w_q)


def _router_tail(x2, gmoe, wrt, br, h3_ref, tope_ref, gates_ref):
    h3f = _rms(x2, gmoe)
    h3 = h3f.astype(BF16)
    h3_ref[...] = _pack_bf16_pairs(h3f)

    logits = _dot_nt(wrt, h3) + br
    n_e, n_t = logits.shape
    e_idx = lax.broadcasted_iota(I32, (n_e, n_t), 0).astype(F32)
    tops, idxs = [], []
    for _ in range(TOP_K):
        m = jnp.max(logits, axis=0, keepdims=True)
        idx = jnp.min(jnp.where(logits == m, e_idx, float(n_e)), axis=0, keepdims=True)
        tops.append(m)
        idxs.append(idx)
        logits = jnp.where(e_idx == idx, -jnp.inf, logits)
    top = jnp.concatenate(tops, axis=0)
    ex = jnp.exp(top - top[0:1])
    gates_ref[...] = ex / jnp.sum(ex, axis=0, keepdims=True)
    tope_ref[...] = jnp.concatenate(idxs, axis=0).astype(I32)


def _attn_prompt_kernel(x1_ref, k_ref, v_ref, gx_ref, wq_ref, wxo_ref, gmoe_ref, wrt_ref, br_ref,
                        x2_ref, h3_ref, tope_ref, gates_ref):
    x = x1_ref[...]
    d = x.shape[1]
    hd = d // X_HEADS
    q = _dot(_rms(x, gx_ref[...]).astype(BF16), wq_ref[...]).astype(BF16)
    kb = k_ref[...].astype(BF16)
    vb = v_ref[...].astype(BF16)
    outs = []
    for hh in range(X_HEADS):
        sl = slice(hh * hd, (hh + 1) * hd)
        s = _dot_nt(q[:, sl], kb[:, sl]) * (hd ** -0.5)
        e = jnp.exp(s - jnp.max(s, axis=-1, keepdims=True))
        p = e / jnp.sum(e, axis=-1, keepdims=True)
        outs.append(_dot(p.astype(BF16), vb[:, sl]))
    o = jnp.concatenate(outs, axis=1).astype(BF16)
    x2 = x + _dot(o, wxo_ref[...])
    x2_ref[...] = x2
    _router_tail(x2, gmoe_ref[...], wrt_ref[...], br_ref[...], h3_ref, tope_ref, gates_ref)


def _attn_prompt(x1, mk, mv, g_x, w_q, w_xo, g_moe, wrt, br):
    b, s, d = x1.shape
    n_mem = mk.shape[1]
    tile = ATT_TILE
    nq = s // tile
    full = lambda shape: pl.BlockSpec(shape, lambda i, j: (0,) * len(shape))
    tok = pl.BlockSpec((None, tile, d), lambda i, j: (i, j, 0))
    mem = pl.BlockSpec((None, n_mem, d), lambda i, j: (i, 0, 0))
    lanes = pl.BlockSpec((TOP_K, tile), lambda i, j: (0, i * nq + j))
    return pl.pallas_call(
        _attn_prompt_kernel,
        grid=(b, nq),
        in_specs=[tok, mem, mem, full((1, d)), full((d, d)), full((d, d)), full((1, d)),
                  full(wrt.shape), full(br.shape)],
        out_specs=[tok, pl.BlockSpec((None, tile, d // 2), lambda i, j: (i, j, 0)), lanes, lanes],
        out_shape=[jax.ShapeDtypeStruct((b, s, d), F32), jax.ShapeDtypeStruct((b, s, d // 2), U32),
                   jax.ShapeDtypeStruct((TOP_K, b * s), I32), jax.ShapeDtypeStruct((TOP_K, b * s), F32)],
        compiler_params=_cparams(("arbitrary", "arbitrary")),
        name="attn_prompt",
    )(x1, mk, mv, g_x, w_q, w_xo, g_moe, wrt, br)


def _attn_sample_kernel(q_ref, k_ref, v_ref, o_ref, *, group, heads, scale):
    for g in range(group):
        q = q_ref[g] * scale
        k = k_ref[g]
        part = jnp.broadcast_to(jnp.sum(k * q[None], axis=-1, keepdims=True), k.shape)
        s = part + pltpu.roll(part, heads, axis=1)
        e = jnp.exp(s - jnp.max(s, axis=0, keepdims=True))
        o_ref[g] = jnp.sum(e * v_ref[g], axis=0) / jnp.sum(e, axis=0)


def _to_rows8(a, heads, hd):
    lead = a.shape[:-2]
    n = len(lead)
    a = a.reshape(*lead, heads, hd // 128, 128)
    a = a.transpose(*range(n), n + 1, n, n + 2)
    return a.reshape(*lead, (hd // 128) * heads, 128)


def _from_rows8(a, heads, hd):
    lead = a.shape[:-2]
    n = len(lead)
    a = a.reshape(*lead, hd // 128, heads, 128)
    a = a.transpose(*range(n), n + 1, n, n + 2)
    return a.reshape(*lead, heads, hd)


def _attn_sample(q, ck, cv):
    n, heads, hd = q.shape
    assert hd == 2 * 128
    n_mem = ck.shape[1]
    group = SAMPLE_ATT_GROUP
    rows = (hd // 128) * heads
    qo = pl.BlockSpec((group, rows, 128), lambda i: (i, 0, 0))
    kv = pl.BlockSpec((group, n_mem, rows, 128), lambda i: (i, 0, 0, 0))
    o = pl.pallas_call(
        functools.partial(_attn_sample_kernel, group=group, heads=heads, scale=hd ** -0.5),
        grid=(n // group,),
        in_specs=[qo, kv, kv],
        out_specs=qo,
        out_shape=jax.ShapeDtypeStruct((n, rows, 128), F32),
        compiler_params=_cparams(("arbitrary",)),
        name="attn_sample",
    )(_to_rows8(q, heads, hd), _to_rows8(ck, heads, hd), _to_rows8(cv, heads, hd))
    return _from_rows8(o, heads, hd)


def _tail_sample_kernel(x1_ref, o_ref, wxo_ref, gmoe_ref, wrt_ref, br_ref,
                        x2_ref, h3_ref, tope_ref, gates_ref):
    x2 = x1_ref[...] + _dot(o_ref[...].astype(BF16), wxo_ref[...])
    x2_ref[...] = x2
    _router_tail(x2, gmoe_ref[...], wrt_ref[...], br_ref[...], h3_ref, tope_ref, gates_ref)


def _tail_sample(x1, o, w_xo, g_moe, wrt, br):
    n, d = x1.shape
    return pl.pallas_call(
        _tail_sample_kernel,
        out_shape=[jax.ShapeDtypeStruct((n, d), F32), jax.ShapeDtypeStruct((n, d // 2), U32),
                   jax.ShapeDtypeStruct((TOP_K, n), I32), jax.ShapeDtypeStruct((TOP_K, n), F32)],
        compiler_params=_cparams(),
        name="tail_sample",
    )(x1, o, w_xo, g_moe, wrt, br)


def _route_kernel(tope_ref, dest_ref, cnt_ref, *, n_blocks, tile, moe_block):
    e_idx = lax.broadcasted_iota(I32, (N_EXPERTS, tile), 0)
    earlier = jnp.where(lax.broadcasted_iota(I32, (tile, tile), 0)
                        < lax.broadcasted_iota(I32, (tile, tile), 1), 1.0, 0.0).astype(BF16)

    def onehot(k, off):
        return e_idx == tope_ref[pl.ds(k, 1), pl.ds(off, tile)]

    def count_body(j, cnt):
        off = pl.multiple_of(j * tile, tile)
        for k in range(TOP_K):
            cnt = cnt + jnp.sum(jnp.where(onehot(k, off), 1.0, 0.0), axis=1, keepdims=True)
        return cnt

    counts = lax.fori_loop(0, n_blocks, count_body, jnp.zeros((N_EXPERTS, 1), F32))
    cnt_ref[...] = jnp.broadcast_to(counts, cnt_ref.shape).astype(I32)

    n_blk = jnp.floor((counts + (moe_block - 1)) * (1.0 / moe_block))
    n_hi = jnp.floor(n_blk * (1.0 / 16.0))
    n_lo = n_blk - 16.0 * n_hi
    below = jnp.where(lax.broadcasted_iota(I32, (N_EXPERTS, N_EXPERTS), 1)
                      < lax.broadcasted_iota(I32, (N_EXPERTS, N_EXPERTS), 0), 1.0, 0.0).astype(BF16)
    wide = lambda c: jnp.broadcast_to(c, (N_EXPERTS, 128)).astype(BF16)
    start = (16.0 * _dot(below, wide(n_hi)) + _dot(below, wide(n_lo)))[:, 0:1] * float(moe_block)

    def dest_body(j, run):
        off = pl.multiple_of(j * tile, tile)
        for k in range(TOP_K):
            oh = onehot(k, off)
            ohf = jnp.where(oh, 1.0, 0.0)
            before = _dot(ohf.astype(BF16), earlier)
            dest = jnp.sum(jnp.where(oh, before + run, 0.0), axis=0, keepdims=True)
            dest_ref[pl.ds(k, 1), pl.ds(off, tile)] = dest.astype(I32)
            run = run + jnp.sum(ohf, axis=1, keepdims=True)
        return run

    lax.fori_loop(0, n_blocks, dest_body, start)


def _route(tope_pad):
    t_pad = tope_pad.shape[1]
    kern = functools.partial(_route_kernel, n_blocks=t_pad // ROUTE_TILE, tile=ROUTE_TILE,
                             moe_block=MOE_BLOCK)
    return pl.pallas_call(
        kern,
        out_shape=[jax.ShapeDtypeStruct((TOP_K, t_pad), I32),
                   jax.ShapeDtypeStruct((N_EXPERTS, 128), I32)],
        compiler_params=_cparams(),
        name="route",
    )(tope_pad)


def _dispatch(dest3, h_prompt, h_sample, n_rows):
    w = h_prompt.shape[1]
    n_prompt_chunks = h_prompt.shape[0] // SC_CHUNK
    n_chunks = dest3.shape[0]
    mesh = plsc.VectorSubcoreMesh(core_axis_name="c", subcore_axis_name="s")
    n_workers = mesh.num_cores * mesh.num_subcores

    n_steps = -(-n_chunks // n_workers)

    def body(dest_hbm, hp_hbm, hs_hbm, xs_hbm, idx0, idx1, rows0, rows1, sem):
        wid = lax.axis_index("s") * mesh.num_cores + lax.axis_index("c")
        idx_v, rows_v = (idx0, idx1), (rows0, rows1)

        def chunk(j):
            return wid + j * n_workers

        def loads(j, slot):
            c = chunk(j)
            prompt = (hp_hbm, c)
            sample = (hs_hbm, c - n_prompt_chunks)
            return c, [
                (pltpu.make_async_copy(dest_hbm.at[c], idx_v[slot], sem.at[slot]), None),
                (pltpu.make_async_copy(prompt[0].at[pl.ds(prompt[1] * SC_CHUNK, SC_CHUNK)], rows_v[slot],
                                       sem.at[slot]), c < n_prompt_chunks),
                (pltpu.make_async_copy(sample[0].at[pl.ds(sample[1] * SC_CHUNK, SC_CHUNK)], rows_v[slot],
                                       sem.at[slot]), c >= n_prompt_chunks),
            ]

        def start_loads(j, slot):
            c, copies = loads(j, slot)
            for cp, cond in copies:
                pl.when((c < n_chunks) if cond is None else ((c < n_chunks) & cond))(cp.start)

        def wait_loads(j, slot):
            c, copies = loads(j, slot)
            for cp, cond in copies:
                pl.when((c < n_chunks) if cond is None else ((c < n_chunks) & cond))(cp.wait)

        def scatters(slot):
            return [pltpu.make_async_copy(rows_v[slot], xs_hbm.at[idx_v[slot].at[k]], sem.at[2 + slot])
                    for k in range(TOP_K)]

        start_loads(0, 0)
        for j in range(n_steps):
            slot = j % 2
            wait_loads(j, slot)
            if j >= 1:
                @pl.when(chunk(j - 1) < n_chunks)
                def _():
                    for cp in scatters(1 - slot):
                        cp.wait()
            if j + 1 < n_steps:
                start_loads(j + 1, 1 - slot)

            @pl.when(chunk(j) < n_chunks)
            def _():
                for cp in scatters(slot):
                    cp.start()

        @pl.when(chunk(n_steps - 1) < n_chunks)
        def _():
            for cp in scatters((n_steps - 1) % 2):
                cp.wait()

    return pl.kernel(
        body,
        out_type=jax.ShapeDtypeStruct((n_rows, w), U32),
        mesh=mesh,
        scratch_types=[pltpu.VMEM((TOP_K, SC_CHUNK), I32), pltpu.VMEM((TOP_K, SC_CHUNK), I32),
                       pltpu.VMEM((SC_CHUNK, w), U32), pltpu.VMEM((SC_CHUNK, w), U32),
                       pltpu.SemaphoreType.DMA((4,))],
        name="dispatch_sc",
    )(dest3, h_prompt, h_sample)


def _ffn_kernel(start_ref, nblk_ref, count_ref, b1_ref, b2_ref, xs_hbm, w1_hbm, w2_hbm, y_hbm,
                w1f, w2f, w1b, w2b, xbuf, ybuf, sem_w, sem_x, sem_y, *, blk):
    e = pl.program_id(0)
    n_e = pl.num_programs(0)
    nb = nblk_ref[e]
    base = start_ref[e]
    wslot = e % 2

    def w_copies(expert, slot):
        return (pltpu.make_async_copy(w1_hbm.at[expert], w1f.at[slot], sem_w.at[slot]),
                pltpu.make_async_copy(w2_hbm.at[expert], w2f.at[slot], sem_w.at[slot]))

    def x_copy(row0, slot):
        return pltpu.make_async_copy(xs_hbm.at[pl.ds(pl.multiple_of(row0, blk), blk)], xbuf.at[slot],
                                     sem_x.at[slot])

    def y_copy(row0, slot):
        return pltpu.make_async_copy(ybuf.at[slot], y_hbm.at[pl.ds(pl.multiple_of(row0, blk), blk)],
                                     sem_y.at[slot])

    @pl.when(e == 0)
    def _():
        for cp in w_copies(0, 0):
            cp.start()

        @pl.when(nb > 0)
        def _():
            x_copy(base, 0).start()

    for cp in w_copies(e, wslot):
        cp.wait()

    @pl.when(e + 1 < n_e)
    def _():
        for cp in w_copies(e + 1, 1 - wslot):
            cp.start(priority=1)

    w1b[...] = w1f[wslot].astype(BF16)
    w2b[...] = w2f[wslot].astype(BF16)
    b1 = b1_ref[e]
    b2 = b2_ref[e]

    def block(j, slot):
        x_copy(base, slot).wait()

        @pl.when(j + 1 < nb)
        def _():
            x_copy(base + (j + 1) * blk, 1 - slot).start()

        @pl.when(j >= 2)
        def _():
            y_copy(base, slot).wait()

        w = xbuf[slot]
        row = lax.broadcasted_iota(I32, w.shape, 0)
        w = jnp.where(row < count_ref[e] - j * blk, w, jnp.uint32(0))
        half = w.shape[1]
        d_e = w2b.shape[0]
        lo, hi = _unpack_bf16_pairs(w)
        gu = _dot(lo.astype(BF16), w1b[0:half, :]) + _dot(hi.astype(BF16), w1b[half:, :]) + b1
        gate = jnp.minimum(gu[:, :d_e], SWIGLU_LIMIT)
        up = jnp.clip(gu[:, d_e:], -SWIGLU_LIMIT, SWIGLU_LIMIT)
        glu = gate * jax.nn.sigmoid(gate * SWIGLU_ALPHA)
        act = ((up + 1.0) * glu).astype(BF16)
        ybuf[slot] = _pack_bf16_pairs(_dot(act, w2b[...]) + b2)
        y_copy(base + j * blk, slot).start()

    def pair(j2, carry):
        j = 2 * j2
        block(j, 0)

        @pl.when(j + 1 < nb)
        def _():
            block(j + 1, 1)

        return carry

    lax.fori_loop(0, (nb + 1) // 2, pair, 0)

    @pl.when(nb >= 2)
    def _():
        y_copy(base, 0).wait()
        y_copy(base, 1).wait()

    @pl.when(nb == 1)
    def _():
        y_copy(base, 0).wait()

    nxt = jnp.minimum(e + 1, n_e - 1)

    @pl.when((e + 1 < n_e) & (nblk_ref[nxt] > 0))
    def _():
        x_copy(start_ref[nxt], 0).start()


def _expert_ffn(grp_start, grp_blocks, grp_count, xs, w1, b1, w2, b2):
    n_rows, half = xs.shape
    n_e, d, d2 = w1.shape
    d_e = w2.shape[1]
    blk = MOE_BLOCK
    whole = lambda shape: pl.BlockSpec(shape, lambda e, s, n, c: (0,) * len(shape))
    hbm = pl.BlockSpec(memory_space=pl.ANY)
    grid_spec = pltpu.PrefetchScalarGridSpec(
        num_scalar_prefetch=3,
        grid=(n_e,),
        in_specs=[whole((n_e, 1, d2)), whole((n_e, 1, d)), hbm, hbm, hbm],
        out_specs=hbm,
        scratch_shapes=[pltpu.VMEM((2, d, d2), F32), pltpu.VMEM((2, d_e, d), F32),
                        pltpu.VMEM((d, d2), BF16), pltpu.VMEM((d_e, d), BF16),
                        pltpu.VMEM((2, blk, half), U32), pltpu.VMEM((2, blk, d // 2), U32),
                        pltpu.SemaphoreType.DMA((2,)), pltpu.SemaphoreType.DMA((2,)),
                        pltpu.SemaphoreType.DMA((2,))],
    )
    return pl.pallas_call(
        functools.partial(_ffn_kernel, blk=blk),
        grid_spec=grid_spec,
        out_shape=jax.ShapeDtypeStruct((n_rows, d // 2), U32),
        compiler_params=_cparams(("arbitrary",)),
        name="expert_ffn",
    )(grp_start, grp_blocks, grp_count, b1.reshape(n_e, 1, d2), b2.reshape(n_e, 1, d), xs, w1, w2)


def _undispatch(dest3, y_rows):
    n_chunks = dest3.shape[0]
    w = y_rows.shape[1]
    mesh = plsc.VectorSubcoreMesh(core_axis_name="c", subcore_axis_name="s")
    n_workers = mesh.num_cores * mesh.num_subcores
    n_steps = -(-n_chunks // n_workers)

    def body(dest_hbm, y_hbm, out_hbm, idx0, idx1, rows0, rows1, sem):
        wid = lax.axis_index("s") * mesh.num_cores + lax.axis_index("c")
        idx_v, rows_v = (idx0, idx1), (rows0, rows1)

        def chunk(j):
            return wid + j * n_workers

        def gather(j, k, slot):
            return pltpu.make_async_copy(y_hbm.at[idx_v[j % 2].at[k]], rows_v[slot], sem.at[slot])

        def store(j, k, slot):
            return pltpu.make_async_copy(rows_v[slot], out_hbm.at[k, pl.ds(chunk(j) * SC_CHUNK, SC_CHUNK)],
                                         sem.at[2 + slot])

        units = [(j, k) for j in range(n_steps) for k in range(TOP_K)]

        def valid(j):
            return chunk(j) < n_chunks

        def load_idx(j):
            pl.when(valid(j))(lambda: pltpu.sync_copy(dest_hbm.at[chunk(j)], idx_v[j % 2]))

        load_idx(0)
        pl.when(valid(0))(gather(0, 0, 0).start)
        for u, (j, k) in enumerate(units):
            slot = u % 2
            pl.when(valid(j))(gather(j, k, slot).wait)
            if u + 1 < len(units):
                jn, kn = units[u + 1]
                if u >= 1:
                    jp, kp = units[u - 1]
                    pl.when(valid(jp))(store(jp, kp, 1 - slot).wait)
                if kn == 0:
                    load_idx(jn)
                pl.when(valid(jn))(gather(jn, kn, 1 - slot).start)
            pl.when(valid(j))(store(j, k, slot).start)
        for u in (len(units) - 2, len(units) - 1):
            j, k = units[u]
            pl.when(valid(j))(store(j, k, u % 2).wait)

    return pl.kernel(
        body,
        out_type=jax.ShapeDtypeStruct((TOP_K, n_chunks * SC_CHUNK, w), y_rows.dtype),
        mesh=mesh,
        scratch_types=[pltpu.VMEM((TOP_K, SC_CHUNK), I32), pltpu.VMEM((TOP_K, SC_CHUNK), I32),
                       pltpu.VMEM((SC_CHUNK, w), y_rows.dtype), pltpu.VMEM((SC_CHUNK, w), y_rows.dtype),
                       pltpu.SemaphoreType.DMA((4,))],
        name="undispatch_sc",
    )(dest3, y_rows)


def _combine_kernel(x2_ref, y4_ref, gates_ref, gf_ref, out_ref, *, tile):
    g = jnp.concatenate([gates_ref[...], jnp.zeros((128 - TOP_K, tile), F32)], axis=0)
    gt = g.T
    half = y4_ref.shape[2]
    moe_lo = jnp.zeros((tile, half), F32)
    moe_hi = jnp.zeros((tile, half), F32)
    for k in range(TOP_K):
        lo, hi = _unpack_bf16_pairs(y4_ref[k])
        moe_lo = moe_lo + gt[:, k:k + 1] * lo
        moe_hi = moe_hi + gt[:, k:k + 1] * hi
    x2 = x2_ref[...]
    out_ref[...] = _rms(x2 + jnp.concatenate([moe_lo, moe_hi], axis=1), gf_ref[...])


def _combine(x2, y4, gates, g_final, tile, first_tile):
    n, d = x2.shape
    return pl.pallas_call(
        functools.partial(_combine_kernel, tile=tile),
        grid=(n // tile,),
        in_specs=[
            pl.BlockSpec((tile, d), lambda i: (i, 0)),
            pl.BlockSpec((TOP_K, tile, d // 2), lambda i: (0, i + first_tile, 0)),
            pl.BlockSpec((TOP_K, tile), lambda i: (0, i)),
            pl.BlockSpec((1, d), lambda i: (0, 0)),
        ],
        out_specs=pl.BlockSpec((tile, d), lambda i: (i, 0)),
        out_shape=jax.ShapeDtypeStruct((n, d), F32),
        compiler_params=_cparams(("arbitrary",)),
        name="combine",
    )(x2, y4, gates, g_final)


def kernel(x_prompt, x_sample, mem_prompt, cache_mem_k, cache_mem_v, state_conv, g_mix, w_in, g_v,
           w_spatial, b_spatial, w_conv, w_out, g_xattn, g_mem, w_q, w_k, w_v, w_xo, g_moe, w_router,
           b_router, w_gate_up, b_gate_up, w_down, b_down, g_final):
    depth = g_mix.shape[0]
    assert depth == 1, "one layer supported"
    b, s, d = x_prompt.shape
    nb, ns, _ = x_sample.shape
    assert ns == 1
    n_mem = mem_prompt.shape[1]
    a_heads, a_hd = g_v.shape[1], g_v.shape[2]
    a_width = a_heads * a_hd
    b_width = w_conv.shape[2]
    assert a_heads == A_HEADS and 2 * a_hd == CHUNK and w_spatial.shape[2] == CHUNK
    assert s % MIX_TILE == 0 and s % ATT_TILE == 0 and (b * n_mem) % KV_TILE == 0
    assert nb % SAMPLE_ATT_GROUP == 0
    l = 0
    row = lambda a: a.reshape(1, -1)

    w_in_b = w_in[l].astype(BF16)
    w_out_b = w_out[l].astype(BF16)
    w_q_b = w_q[l].astype(BF16)
    w_k_b = w_k[l].astype(BF16)
    w_v_b = w_v[l].astype(BF16)
    w_xo_b = w_xo[l].astype(BF16)
    wrt_b = w_router[l].T.astype(BF16)
    br_col = b_router[l].reshape(N_EXPERTS, 1).astype(F32)
    gv_row = row(g_v[l])
    ws = w_spatial[l]
    ws_pairs = jnp.concatenate([ws[0::2], ws[1::2]], axis=2)
    bs_full = jnp.repeat(b_spatial[l].T, a_hd, axis=1)
    w00 = row(jnp.repeat(ws[:, 0, 0], a_hd))
    b0 = row(jnp.repeat(b_spatial[l][:, 0], a_hd))

    mk2, mv2 = _mem_kv(mem_prompt.reshape(b * n_mem, d), row(g_mem[l]), w_k_b, w_v_b)
    x1_p, conv_p = _mixer_prompt(x_prompt, row(g_mix[l]), w_in_b, gv_row, ws_pairs, bs_full,
                                 w_conv[l], w_out_b)
    x2_p, h3_p, tope_p, gates_p = _attn_prompt(
        x1_p, mk2.reshape(b, n_mem, d), mv2.reshape(b, n_mem, d), row(g_xattn[l]), w_q_b, w_xo_b,
        row(g_moe[l]), wrt_b, br_col)

    xs2 = x_sample.reshape(nb, d)
    st = state_conv[l]
    x1_s, v_s, z_s, q_s = _mixer_sample(xs2, st[:, 0], st[:, 1], row(g_mix[l]), w_in_b, gv_row, w00, b0,
                                        w_conv[l], w_out_b, row(g_xattn[l]), w_q_b)
    x_heads, x_hd = cache_mem_k.shape[3], cache_mem_k.shape[4]
    assert depth == 1 and x_heads == X_HEADS
    o_s = _attn_sample(q_s.reshape(nb, x_heads, x_hd), cache_mem_k.reshape(nb, n_mem, x_heads, x_hd),
                       cache_mem_v.reshape(nb, n_mem, x_heads, x_hd)).reshape(nb, d)
    x2_s, h3_s, tope_s, gates_s = _tail_sample(x1_s, o_s, w_xo_b, row(g_moe[l]), wrt_b, br_col)

    t_p = b * s
    t_all = t_p + nb
    t_pad = -(-t_all // ROUTE_TILE) * ROUTE_TILE
    tope_all = jnp.concatenate([tope_p, tope_s, jnp.full((TOP_K, t_pad - t_all), -1, I32)], axis=1)
    dest, cnt = _route(tope_all)
    counts = cnt[:, 0]
    padded = (counts + MOE_BLOCK - 1) // MOE_BLOCK * MOE_BLOCK
    pad_ends = jnp.cumsum(padded)
    pad_starts = pad_ends - padded
    n_blocks = -(-(t_all * TOP_K) // MOE_BLOCK) + N_EXPERTS
    n_rows = n_blocks * MOE_BLOCK
    assert t_all % SC_CHUNK == 0 and t_p % SC_CHUNK == 0
    dest3 = dest[:, :t_all].reshape(TOP_K, t_all // SC_CHUNK, SC_CHUNK).transpose(1, 0, 2)
    xs_rows = _dispatch(dest3, h3_p.reshape(t_p, d // 2), h3_s, n_rows)
    x_heads = cache_mem_k.shape[3]
    mem_k_out = mk2.reshape(1, b, n_mem, x_heads, d // x_heads)
    mem_v_out = mv2.reshape(1, b, n_mem, x_heads, d // x_heads)
    xs_rows, mem_k_out, mem_v_out = lax.optimization_barrier((xs_rows, mem_k_out, mem_v_out))
    y_rows = _expert_ffn(pad_starts.astype(I32), (padded // MOE_BLOCK).astype(I32), counts.astype(I32),
                         xs_rows, w_gate_up[l], b_gate_up[l], w_down[l], b_down[l])

    y4 = _undispatch(dest3, y_rows)

    gf = row(g_final)
    assert t_p % COMBINE_TILE == 0 and t_p % nb == 0
    y_p = _combine(x2_p.reshape(t_p, d), y4, gates_p, gf, COMBINE_TILE, 0)
    y_s = _combine(x2_s, y4, gates_s, gf, nb, t_p // nb)

    return (y_p.reshape(b, s, d),
            y_s.reshape(nb, 1, d),
            mem_k_out,
            mem_v_out,
            conv_p.reshape(1, b, CONV_W - 1, b_width),
            jnp.stack([st[:, 1], z_s], axis=1).reshape(1, nb, CONV_W - 1, b_width),
            v_s.reshape(1, nb, 1, a_heads, a_hd))
```

```python
import functools

import jax
import jax.numpy as jnp
from jax import lax
from jax.experimental import pallas as pl
from jax.experimental.pallas import tpu as pltpu
from jax.experimental.pallas import tpu_sc as plsc

F32 = jnp.float32
BF16 = jnp.bfloat16
I32 = jnp.int32
U32 = jnp.uint32

A_HEADS = 8
CHUNK = 128
CONV_W = 3
X_HEADS = 4
N_EXPERTS = 32
TOP_K = 4
SWIGLU_LIMIT = 7.0
SWIGLU_ALPHA = 1.702
EPS = 1e-5

MIX_TILE = 1024
ATT_TILE = 1024
KV_TILE = 512
ROUTE_TILE = 512
MOE_BLOCK = 256
COMBINE_TILE = 512
GATHER_TILE = 256
GATHER_TILES = 20
SAMPLE_ATT_GROUP = 8
SC_CHUNK = 64
VMEM_LIMIT = 56 * 1024 * 1024


def _cparams(sem=None):
    return pltpu.CompilerParams(dimension_semantics=sem, vmem_limit_bytes=VMEM_LIMIT)


def _rms(x, g):
    r = lax.rsqrt(jnp.mean(x * x, axis=-1, keepdims=True) + EPS)
    return (x * r) * g


def _gelu(x):
    return 0.5 * x * (1.0 + lax.erf(x * 0.7071067811865476))


def _dot(a, b):
    return jnp.dot(a, b, preferred_element_type=F32)


def _dot_nt(a, b):
    return lax.dot_general(a, b, (((1,), (1,)), ((), ())), preferred_element_type=F32)


def _pack_bf16_pairs(x):
    d = x.shape[1]
    r = x.astype(BF16).astype(F32)
    lo = lax.shift_right_logical(pltpu.bitcast(r[:, :d // 2], U32), jnp.uint32(16))
    hi = lax.bitwise_and(pltpu.bitcast(r[:, d // 2:], U32), jnp.uint32(0xFFFF0000))
    return lax.bitwise_or(lo, hi)


def _unpack_bf16_pairs(w):
    lo = pltpu.bitcast(lax.shift_left(w, jnp.uint32(16)), F32)
    hi = pltpu.bitcast(lax.bitwise_and(w, jnp.uint32(0xFFFF0000)), F32)
    return lo, hi


def _memkv_kernel(m_ref, g_ref, wk_ref, wv_ref, k_ref, v_ref):
    h = _rms(m_ref[...], g_ref[...]).astype(BF16)
    k_ref[...] = _dot(h, wk_ref[...])
    v_ref[...] = _dot(h, wv_ref[...])


def _mem_kv(mem2d, g_mem, wk, wv):
    n, d = mem2d.shape
    row = pl.BlockSpec((KV_TILE, d), lambda i: (i, 0))
    full = lambda shape: pl.BlockSpec(shape, lambda i: (0,) * len(shape))
    return pl.pallas_call(
        _memkv_kernel,
        grid=(n // KV_TILE,),
        in_specs=[row, full((1, d)), full((d, d)), full((d, d))],
        out_specs=[row, row],
        out_shape=[jax.ShapeDtypeStruct((n, d), F32)] * 2,
        compiler_params=_cparams(("arbitrary",)),
        name="mem_kv",
    )(mem2d, g_mem, wk, wv)


def _head_rms(v, gv, a_width):
    hd = a_width // A_HEADS
    r_i = lax.broadcasted_iota(I32, (a_width, a_width), 0) // hd
    c_i = lax.broadcasted_iota(I32, (a_width, a_width), 1) // hd
    ones_bd = jnp.where(r_i == c_i, 1.0, 0.0).astype(BF16)
    sq = v * v
    sq_hi = sq.astype(BF16)
    sq_lo = (sq - sq_hi.astype(F32)).astype(BF16)
    gs = _dot(sq_hi, ones_bd) + _dot(sq_lo, ones_bd)
    return (v * lax.rsqrt(gs * (1.0 / hd) + EPS)) * gv


def _mixer_prompt_kernel(x_ref, gmix_ref, win_ref, gv_ref, ws_ref, bs_ref, wc_ref, wout_ref,
                         x1_ref, cs_ref, zbuf, *, tile, a_width, b_width):
    j = pl.program_id(1)
    nj = pl.num_programs(1)
    x = x_ref[...]
    h = _rms(x, gmix_ref[...]).astype(BF16)
    proj = _dot(h, win_ref[...])
    u = _gelu(proj[:, :a_width])
    v = _head_rms(_gelu(proj[:, a_width:2 * a_width]), gv_ref[...], a_width)
    o = 2 * a_width
    hb = proj[:, o:o + b_width]
    bg = proj[:, o + b_width:o + 2 * b_width]
    cg = proj[:, o + 2 * b_width:o + 3 * b_width]

    lane = lax.broadcasted_iota(I32, (CHUNK, 2 * (a_width // A_HEADS)), 1)
    first = lane < (a_width // A_HEADS)
    t_i = lax.broadcasted_iota(I32, (CHUNK, 2 * CHUNK), 0)
    s_i = lax.broadcasted_iota(I32, (CHUNK, 2 * CHUNK), 1) % CHUNK
    causal = s_i <= t_i
    pair_cols = []
    for p in range(A_HEADS // 2):
        w_pair = jnp.where(causal, ws_ref[p], 0.0).astype(BF16)
        vp = v[:, p * CHUNK:(p + 1) * CHUNK]
        rows = []
        for c in range(tile // CHUNK):
            vc = vp[c * CHUNK:(c + 1) * CHUNK]
            rhs = jnp.concatenate([jnp.where(first, vc, 0.0), jnp.where(first, 0.0, vc)],
                                  axis=0).astype(BF16)
            rows.append(_dot(w_pair, rhs))
        pair_cols.append(jnp.concatenate(rows, axis=0))
    gate = jnp.concatenate(pair_cols, axis=1)
    bias = jnp.concatenate([bs_ref[...]] * (tile // CHUNK), axis=0)
    a_out = u * (gate + bias)

    @pl.when(j == 0)
    def _():
        zbuf[0:8, :] = jnp.zeros((8, b_width), F32)

    z = cg * hb
    zbuf[8:tile + 8, :] = z
    z1 = zbuf[7:tile + 7, :]
    z2 = zbuf[6:tile + 6, :]
    wc = wc_ref[...]
    conv = z2 * wc[0:1] + z1 * wc[1:2] + z * wc[2:3]
    b_out = bg * conv
    tail = zbuf[tile:tile + 8, :]
    zbuf[0:8, :] = tail

    @pl.when(j == nj - 1)
    def _():
        cs_ref[...] = tail[8 - (CONV_W - 1):, :]

    y = _dot(a_out.astype(BF16), wout_ref[0:a_width, :]) + _dot(b_out.astype(BF16), wout_ref[a_width:, :])
    x1_ref[...] = x + y


def _mixer_prompt(x, g_mix, w_in, g_v, ws_pairs, bs_full, w_conv, w_out):
    b, s, d = x.shape
    a_width = g_v.shape[1]
    b_width = w_conv.shape[1]
    in_width = w_in.shape[1]
    tile = MIX_TILE
    full = lambda shape: pl.BlockSpec(shape, lambda i, j: (0,) * len(shape))
    kern = functools.partial(_mixer_prompt_kernel, tile=tile, a_width=a_width, b_width=b_width)
    return pl.pallas_call(
        kern,
        grid=(b, s // tile),
        in_specs=[
            pl.BlockSpec((None, tile, d), lambda i, j: (i, j, 0)),
            full((1, d)), full((d, in_width)), full((1, a_width)),
            full(ws_pairs.shape), full(bs_full.shape), full(w_conv.shape), full(w_out.shape),
        ],
        out_specs=[
            pl.BlockSpec((None, tile, d), lambda i, j: (i, j, 0)),
            pl.BlockSpec((None, CONV_W - 1, b_width), lambda i, j: (i, 0, 0)),
        ],
        out_shape=[jax.ShapeDtypeStruct((b, s, d), F32),
                   jax.ShapeDtypeStruct((b, CONV_W - 1, b_width), F32)],
        scratch_shapes=[pltpu.VMEM((tile + 8, b_width), F32)],
        compiler_params=_cparams(("arbitrary", "arbitrary")),
        name="mixer_prompt",
    )(x, g_mix, w_in, g_v, ws_pairs, bs_full, w_conv, w_out)


def _mixer_sample_kernel(x_ref, s0_ref, s1_ref, gmix_ref, win_ref, gv_ref, w00_ref, b0_ref, wc_ref,
                         wout_ref, gx_ref, wq_ref, x1_ref, v_ref, z_ref, q_ref, *, a_width, b_width):
    x = x_ref[...]
    h = _rms(x, gmix_ref[...]).astype(BF16)
    proj = _dot(h, win_ref[...])
    u = _gelu(proj[:, :a_width])
    v = _head_rms(_gelu(proj[:, a_width:2 * a_width]), gv_ref[...], a_width)
    v_ref[...] = v
    o = 2 * a_width
    hb = proj[:, o:o + b_width]
    bg = proj[:, o + b_width:o + 2 * b_width]
    cg = proj[:, o + 2 * b_width:o + 3 * b_width]
    a_out = u * (v * w00_ref[...] + b0_ref[...])
    z = cg * hb
    z_ref[...] = z
    wc = wc_ref[...]
    conv = s0_ref[...] * wc[0:1] + s1_ref[...] * wc[1:2] + z * wc[2:3]
    b_out = bg * conv
    y = _dot(a_out.astype(BF16), wout_ref[0:a_width, :]) + _dot(b_out.astype(BF16), wout_ref[a_width:, :])
    x1 = x + y
    x1_ref[...] = x1
    q_ref[...] = _dot(_rms(x1, gx_ref[...]).astype(BF16), wq_ref[...])


def _mixer_sample(x, s0, s1, g_mix, w_in, g_v, w00, b0, w_conv, w_out, g_x, w_q):
    n, d = x.shape
    a_width = g_v.shape[1]
    b_width = w_conv.shape[1]
    kern = functools.partial(_mixer_sample_kernel, a_width=a_width, b_width=b_width)
    return pl.pallas_call(
        kern,
        out_shape=[jax.ShapeDtypeStruct((n, d), F32), jax.ShapeDtypeStruct((n, a_width), F32),
                   jax.ShapeDtypeStruct((n, b_width), F32), jax.ShapeDtypeStruct((n, d), F32)],
        compiler_params=_cparams(),
        name="mixer_sample",
    )(x, s0, s1, g_mix, w_in, g_v, w00, b0, w_conv, w_out, g_x, w_q)


def _router_tail(x2, gmoe, wrt, br, h3_ref, tope_ref, gates_ref):
    h3f = _rms(x2, gmoe)
    h3 = h3f.astype(BF16)
    h3_ref[...] = _pack_bf16_pairs(h3f)

    logits = _dot_nt(wrt, h3) + br
    n_e, n_t = logits.shape
    e_idx = lax.broadcasted_iota(I32, (n_e, n_t), 0).astype(F32)
    tops, idxs = [], []
    for _ in range(TOP_K):
        m = jnp.max(logits, axis=0, keepdims=True)
        idx = jnp.min(jnp.where(logits == m, e_idx, float(n_e)), axis=0, keepdims=True)
        tops.append(m)
        idxs.append(idx)
        logits = jnp.where(e_idx == idx, -jnp.inf, logits)
    top = jnp.concatenate(tops, axis=0)
    ex = jnp.exp(top - top[0:1])
    gates_ref[...] = ex / jnp.sum(ex, axis=0, keepdims=True)
    tope_ref[...] = jnp.concatenate(idxs, axis=0).astype(I32)


def _attn_prompt_kernel(x1_ref, k_ref, v_ref, gx_ref, wq_ref, wxo_ref, gmoe_ref, wrt_ref, br_ref,
                        x2_ref, h3_ref, tope_ref, gates_ref):
    x = x1_ref[...]
    d = x.shape[1]
    hd = d // X_HEADS
    q = _dot(_rms(x, gx_ref[...]).astype(BF16), wq_ref[...]).astype(BF16)
    kb = k_ref[...].astype(BF16)
    vb = v_ref[...].astype(BF16)
    outs = []
    for hh in range(X_HEADS):
        sl = slice(hh * hd, (hh + 1) * hd)
        s = _dot_nt(q[:, sl], kb[:, sl]) * (hd ** -0.5)
        e = jnp.exp(s - jnp.max(s, axis=-1, keepdims=True))
        p = e / jnp.sum(e, axis=-1, keepdims=True)
        outs.append(_dot(p.astype(BF16), vb[:, sl]))
    o = jnp.concatenate(outs, axis=1).astype(BF16)
    x2 = x + _dot(o, wxo_ref[...])
    x2_ref[...] = x2
    _router_tail(x2, gmoe_ref[...], wrt_ref[...], br_ref[...], h3_ref, tope_ref, gates_ref)


def _attn_prompt(x1, mk, mv, g_x, w_q, w_xo, g_moe, wrt, br):
    b, s, d = x1.shape
    n_mem = mk.shape[1]
    tile = ATT_TILE
    nq = s // tile
    full = lambda shape: pl.BlockSpec(shape, lambda i, j: (0,) * len(shape))
    tok = pl.BlockSpec((None, tile, d), lambda i, j: (i, j, 0))
    mem = pl.BlockSpec((None, n_mem, d), lambda i, j: (i, 0, 0))
    lanes = pl.BlockSpec((TOP_K, tile), lambda i, j: (0, i * nq + j))
    return pl.pallas_call(
        _attn_prompt_kernel,
        grid=(b, nq),
        in_specs=[tok, mem, mem, full((1, d)), full((d, d)), full((d, d)), full((1, d)),
                  full(wrt.shape), full(br.shape)],
        out_specs=[tok, pl.BlockSpec((None, tile, d // 2), lambda i, j: (i, j, 0)), lanes, lanes],
        out_shape=[jax.ShapeDtypeStruct((b, s, d), F32), jax.ShapeDtypeStruct((b, s, d // 2), U32),
                   jax.ShapeDtypeStruct((TOP_K, b * s), I32), jax.ShapeDtypeStruct((TOP_K, b * s), F32)],
        compiler_params=_cparams(("arbitrary", "arbitrary")),
        name="attn_prompt",
    )(x1, mk, mv, g_x, w_q, w_xo, g_moe, wrt, br)


def _attn_sample_kernel(q_ref, k_ref, v_ref, o_ref, *, group, heads, scale):
    for g in range(group):
        q = q_ref[g] * scale
        k = k_ref[g]
        part = jnp.broadcast_to(jnp.sum(k * q[None], axis=-1, keepdims=True), k.shape)
        s = part + pltpu.roll(part, heads, axis=1)
        e = jnp.exp(s - jnp.max(s, axis=0, keepdims=True))
        o_ref[g] = jnp.sum(e * v_ref[g], axis=0) / jnp.sum(e, axis=0)


def _to_rows8(a, heads, hd):
    lead = a.shape[:-2]
    n = len(lead)
    a = a.reshape(*lead, heads, hd // 128, 128)
    a = a.transpose(*range(n), n + 1, n, n + 2)
    return a.reshape(*lead, (hd // 128) * heads, 128)


def _from_rows8(a, heads, hd):
    lead = a.shape[:-2]
    n = len(lead)
    a = a.reshape(*lead, hd // 128, heads, 128)
    a = a.transpose(*range(n), n + 1, n, n + 2)
    return a.reshape(*lead, heads, hd)


def _attn_sample(q, ck, cv):
    n, heads, hd = q.shape
    assert hd == 2 * 128
    n_mem = ck.shape[1]
    group = SAMPLE_ATT_GROUP
    rows = (hd // 128) * heads
    qo = pl.BlockSpec((group, rows, 128), lambda i: (i, 0, 0))
    kv = pl.BlockSpec((group, n_mem, rows, 128), lambda i: (i, 0, 0, 0))
    o = pl.pallas_call(
        functools.partial(_attn_sample_kernel, group=group, heads=heads, scale=hd ** -0.5),
        grid=(n // group,),
        in_specs=[qo, kv, kv],
        out_specs=qo,
        out_shape=jax.ShapeDtypeStruct((n, rows, 128), F32),
        compiler_params=_cparams(("arbitrary",)),
        name="attn_sample",
    )(_to_rows8(q, heads, hd), _to_rows8(ck, heads, hd), _to_rows8(cv, heads, hd))
    return _from_rows8(o, heads, hd)


def _tail_sample_kernel(x1_ref, o_ref, wxo_ref, gmoe_ref, wrt_ref, br_ref,
                        x2_ref, h3_ref, tope_ref, gates_ref):
    x2 = x1_ref[...] + _dot(o_ref[...].astype(BF16), wxo_ref[...])
    x2_ref[...] = x2
    _router_tail(x2, gmoe_ref[...], wrt_ref[...], br_ref[...], h3_ref, tope_ref, gates_ref)


def _tail_sample(x1, o, w_xo, g_moe, wrt, br):
    n, d = x1.shape
    return pl.pallas_call(
        _tail_sample_kernel,
        out_shape=[jax.ShapeDtypeStruct((n, d), F32), jax.ShapeDtypeStruct((n, d // 2), U32),
                   jax.ShapeDtypeStruct((TOP_K, n), I32), jax.ShapeDtypeStruct((TOP_K, n), F32)],
        compiler_params=_cparams(),
        name="tail_sample",
    )(x1, o, w_xo, g_moe, wrt, br)


def _route_kernel(tope_ref, dest_ref, cnt_ref, *, n_blocks, tile, moe_block):
    e_idx = lax.broadcasted_iota(I32, (N_EXPERTS, tile), 0)
    earlier = jnp.where(lax.broadcasted_iota(I32, (tile, tile), 0)
                        < lax.broadcasted_iota(I32, (tile, tile), 1), 1.0, 0.0).astype(BF16)

    def onehot(k, off):
        return e_idx == tope_ref[pl.ds(k, 1), pl.ds(off, tile)]

    def count_body(j, cnt):
        off = pl.multiple_of(j * tile, tile)
        for k in range(TOP_K):
            cnt = cnt + jnp.sum(jnp.where(onehot(k, off), 1.0, 0.0), axis=1, keepdims=True)
        return cnt

    counts = lax.fori_loop(0, n_blocks, count_body, jnp.zeros((N_EXPERTS, 1), F32))
    cnt_ref[...] = jnp.broadcast_to(counts, cnt_ref.shape).astype(I32)

    n_blk = jnp.floor((counts + (moe_block - 1)) * (1.0 / moe_block))
    n_hi = jnp.floor(n_blk * (1.0 / 16.0))
    n_lo = n_blk - 16.0 * n_hi
    below = jnp.where(lax.broadcasted_iota(I32, (N_EXPERTS, N_EXPERTS), 1)
                      < lax.broadcasted_iota(I32, (N_EXPERTS, N_EXPERTS), 0), 1.0, 0.0).astype(BF16)
    wide = lambda c: jnp.broadcast_to(c, (N_EXPERTS, 128)).astype(BF16)
    start = (16.0 * _dot(below, wide(n_hi)) + _dot(below, wide(n_lo)))[:, 0:1] * float(moe_block)

    def dest_body(j, run):
        off = pl.multiple_of(j * tile, tile)
        for k in range(TOP_K):
            oh = onehot(k, off)
            ohf = jnp.where(oh, 1.0, 0.0)
            before = _dot(ohf.astype(BF16), earlier)
            dest = jnp.sum(jnp.where(oh, before + run, 0.0), axis=0, keepdims=True)
            dest_ref[pl.ds(k, 1), pl.ds(off, tile)] = dest.astype(I32)
            run = run + jnp.sum(ohf, axis=1, keepdims=True)
        return run

    lax.fori_loop(0, n_blocks, dest_body, start)


def _route(tope_pad):
    t_pad = tope_pad.shape[1]
    kern = functools.partial(_route_kernel, n_blocks=t_pad // ROUTE_TILE, tile=ROUTE_TILE,
                             moe_block=MOE_BLOCK)
    return pl.pallas_call(
        kern,
        out_shape=[jax.ShapeDtypeStruct((TOP_K, t_pad), I32),
                   jax.ShapeDtypeStruct((N_EXPERTS, 128), I32)],
        compiler_params=_cparams(),
        name="route",
    )(tope_pad)


def _dispatch(dest3, h_prompt, h_sample, n_rows):
    w = h_prompt.shape[1]
    n_prompt_chunks = h_prompt.shape[0] // SC_CHUNK
    n_chunks = dest3.shape[0]
    mesh = plsc.VectorSubcoreMesh(core_axis_name="c", subcore_axis_name="s")
    n_workers = mesh.num_cores * mesh.num_subcores

    n_steps = -(-n_chunks // n_workers)

    def body(dest_hbm, hp_hbm, hs_hbm, xs_hbm, idx0, idx1, rows0, rows1, sem):
        wid = lax.axis_index("s") * mesh.num_cores + lax.axis_index("c")
        idx_v, rows_v = (idx0, idx1), (rows0, rows1)

        def chunk(j):
            return wid + j * n_workers

        def loads(j, slot):
            c = chunk(j)
            prompt = (hp_hbm, c)
            sample = (hs_hbm, c - n_prompt_chunks)
            return c, [
                (pltpu.make_async_copy(dest_hbm.at[c], idx_v[slot], sem.at[slot]), None),
                (pltpu.make_async_copy(prompt[0].at[pl.ds(prompt[1] * SC_CHUNK, SC_CHUNK)], rows_v[slot],
                                       sem.at[slot]), c < n_prompt_chunks),
                (pltpu.make_async_copy(sample[0].at[pl.ds(sample[1] * SC_CHUNK, SC_CHUNK)], rows_v[slot],
                                       sem.at[slot]), c >= n_prompt_chunks),
            ]

        def start_loads(j, slot):
            c, copies = loads(j, slot)
            for cp, cond in copies:
                pl.when((c < n_chunks) if cond is None else ((c < n_chunks) & cond))(cp.start)

        def wait_loads(j, slot):
            c, copies = loads(j, slot)
            for cp, cond in copies:
                pl.when((c < n_chunks) if cond is None else ((c < n_chunks) & cond))(cp.wait)

        def scatters(slot):
            return [pltpu.make_async_copy(rows_v[slot], xs_hbm.at[idx_v[slot].at[k]], sem.at[2 + slot])
                    for k in range(TOP_K)]

        start_loads(0, 0)
        for j in range(n_steps):
            slot = j % 2
            wait_loads(j, slot)
            if j >= 1:
                @pl.when(chunk(j - 1) < n_chunks)
                def _():
                    for cp in scatters(1 - slot):
                        cp.wait()
            if j + 1 < n_steps:
                start_loads(j + 1, 1 - slot)

            @pl.when(chunk(j) < n_chunks)
            def _():
                for cp in scatters(slot):
                    cp.start()

        @pl.when(chunk(n_steps - 1) < n_chunks)
        def _():
            for cp in scatters((n_steps - 1) % 2):
                cp.wait()

    return pl.kernel(
        body,
        out_type=jax.ShapeDtypeStruct((n_rows, w), U32),
        mesh=mesh,
        scratch_types=[pltpu.VMEM((TOP_K, SC_CHUNK), I32), pltpu.VMEM((TOP_K, SC_CHUNK), I32),
                       pltpu.VMEM((SC_CHUNK, w), U32), pltpu.VMEM((SC_CHUNK, w), U32),
                       pltpu.SemaphoreType.DMA((4,))],
        name="dispatch_sc",
    )(dest3, h_prompt, h_sample)


def _ffn_kernel(start_ref, nblk_ref, count_ref, b1_ref, b2_ref, xs_hbm, w1_hbm, w2_hbm, y_hbm,
                w1f, w2f, w1b, w2b, xbuf, ybuf, sem_w, sem_x, sem_y, *, blk):
    e = pl.program_id(0)
    n_e = pl.num_programs(0)
    nb = nblk_ref[e]
    base = start_ref[e]
    wslot = e % 2

    def w_copies(expert, slot):
        return (pltpu.make_async_copy(w1_hbm.at[expert], w1f.at[slot], sem_w.at[slot]),
                pltpu.make_async_copy(w2_hbm.at[expert], w2f.at[slot], sem_w.at[slot]))

    def x_copy(row0, slot):
        return pltpu.make_async_copy(xs_hbm.at[pl.ds(pl.multiple_of(row0, blk), blk)], xbuf.at[slot],
                                     sem_x.at[slot])

    def y_copy(row0, slot):
        return pltpu.make_async_copy(ybuf.at[slot], y_hbm.at[pl.ds(pl.multiple_of(row0, blk), blk)],
                                     sem_y.at[slot])

    @pl.when(e == 0)
    def _():
        for cp in w_copies(0, 0):
            cp.start()

        @pl.when(nb > 0)
        def _():
            x_copy(base, 0).start()

    for cp in w_copies(e, wslot):
        cp.wait()

    @pl.when(e + 1 < n_e)
    def _():
        for cp in w_copies(e + 1, 1 - wslot):
            cp.start(priority=1)

    w1b[...] = w1f[wslot].astype(BF16)
    w2b[...] = w2f[wslot].astype(BF16)
    b1 = b1_ref[e]
    b2 = b2_ref[e]

    def block(j, slot):
        x_copy(base, slot).wait()

        @pl.when(j + 1 < nb)
        def _():
            x_copy(base + (j + 1) * blk, 1 - slot).start()

        @pl.when(j >= 2)
        def _():
            y_copy(base, slot).wait()

        w = xbuf[slot]
        row = lax.broadcasted_iota(I32, w.shape, 0)
        w = jnp.where(row < count_ref[e] - j * blk, w, jnp.uint32(0))
        half = w.shape[1]
        d_e = w2b.shape[0]
        lo, hi = _unpack_bf16_pairs(w)
        gu = _dot(lo.astype(BF16), w1b[0:half, :]) + _dot(hi.astype(BF16), w1b[half:, :]) + b1
        gate = jnp.minimum(gu[:, :d_e], SWIGLU_LIMIT)
        up = jnp.clip(gu[:, d_e:], -SWIGLU_LIMIT, SWIGLU_LIMIT)
        glu = gate * jax.nn.sigmoid(gate * SWIGLU_ALPHA)
        act = ((up + 1.0) * glu).astype(BF16)
        ybuf[slot] = _pack_bf16_pairs(_dot(act, w2b[...]) + b2)
        y_copy(base + j * blk, slot).start()

    def pair(j2, carry):
        j = 2 * j2
        block(j, 0)

        @pl.when(j + 1 < nb)
        def _():
            block(j + 1, 1)

        return carry

    lax.fori_loop(0, (nb + 1) // 2, pair, 0)

    @pl.when(nb >= 2)
    def _():
        y_copy(base, 0).wait()
        y_copy(base, 1).wait()

    @pl.when(nb == 1)
    def _():
        y_copy(base, 0).wait()

    nxt = jnp.minimum(e + 1, n_e - 1)

    @pl.when((e + 1 < n_e) & (nblk_ref[nxt] > 0))
    def _():
        x_copy(start_ref[nxt], 0).start()


def _expert_ffn(grp_start, grp_blocks, grp_count, xs, w1, b1, w2, b2):
    n_rows, half = xs.shape
    n_e, d, d2 = w1.shape
    d_e = w2.shape[1]
    blk = MOE_BLOCK
    whole = lambda shape: pl.BlockSpec(shape, lambda e, s, n, c: (0,) * len(shape))
    hbm = pl.BlockSpec(memory_space=pl.ANY)
    grid_spec = pltpu.PrefetchScalarGridSpec(
        num_scalar_prefetch=3,
        grid=(n_e,),
        in_specs=[whole((n_e, 1, d2)), whole((n_e, 1, d)), hbm, hbm, hbm],
        out_specs=hbm,
        scratch_shapes=[pltpu.VMEM((2, d, d2), F32), pltpu.VMEM((2, d_e, d), F32),
                        pltpu.VMEM((d, d2), BF16), pltpu.VMEM((d_e, d), BF16),
                        pltpu.VMEM((2, blk, half), U32), pltpu.VMEM((2, blk, d // 2), U32),
                        pltpu.SemaphoreType.DMA((2,)), pltpu.SemaphoreType.DMA((2,)),
                        pltpu.SemaphoreType.DMA((2,))],
    )
    return pl.pallas_call(
        functools.partial(_ffn_kernel, blk=blk),
        grid_spec=grid_spec,
        out_shape=jax.ShapeDtypeStruct((n_rows, d // 2), U32),
        compiler_params=_cparams(("arbitrary",)),
        name="expert_ffn",
    )(grp_start, grp_blocks, grp_count, b1.reshape(n_e, 1, d2), b2.reshape(n_e, 1, d), xs, w1, w2)


def _undispatch(dest3, y_rows):
    n_chunks = dest3.shape[0]
    w = y_rows.shape[1]
    mesh = plsc.VectorSubcoreMesh(core_axis_name="c", subcore_axis_name="s")
    n_workers = mesh.num_cores * mesh.num_subcores
    n_steps = -(-n_chunks // n_workers)

    def body(dest_hbm, y_hbm, out_hbm, idx0, idx1, rows0, rows1, sem):
        wid = lax.axis_index("s") * mesh.num_cores + lax.axis_index("c")
        idx_v, rows_v = (idx0, idx1), (rows0, rows1)

        def chunk(j):
            return wid + j * n_workers

        def gather(j, k, slot):
            return pltpu.make_async_copy(y_hbm.at[idx_v[j % 2].at[k]], rows_v[slot], sem.at[slot])

        def store(j, k, slot):
            return pltpu.make_async_copy(rows_v[slot], out_hbm.at[k, pl.ds(chunk(j) * SC_CHUNK, SC_CHUNK)],
                                         sem.at[2 + slot])

        units = [(j, k) for j in range(n_steps) for k in range(TOP_K)]

        def valid(j):
            return chunk(j) < n_chunks

        def load_idx(j):
            pl.when(valid(j))(lambda: pltpu.sync_copy(dest_hbm.at[chunk(j)], idx_v[j % 2]))

        load_idx(0)
        pl.when(valid(0))(gather(0, 0, 0).start)
        for u, (j, k) in enumerate(units):
            slot = u % 2
            pl.when(valid(j))(gather(j, k, slot).wait)
            if u + 1 < len(units):
                jn, kn = units[u + 1]
                if u >= 1:
                    jp, kp = units[u - 1]
                    pl.when(valid(jp))(store(jp, kp, 1 - slot).wait)
                if kn == 0:
                    load_idx(jn)
                pl.when(valid(jn))(gather(jn, kn, 1 - slot).start)
            pl.when(valid(j))(store(j, k, slot).start)
        for u in (len(units) - 2, len(units) - 1):
            j, k = units[u]
            pl.when(valid(j))(store(j, k, u % 2).wait)

    return pl.kernel(
        body,
        out_type=jax.ShapeDtypeStruct((TOP_K, n_chunks * SC_CHUNK, w), y_rows.dtype),
        mesh=mesh,
        scratch_types=[pltpu.VMEM((TOP_K, SC_CHUNK), I32), pltpu.VMEM((TOP_K, SC_CHUNK), I32),
                       pltpu.VMEM((SC_CHUNK, w), y_rows.dtype), pltpu.VMEM((SC_CHUNK, w), y_rows.dtype),
                       pltpu.SemaphoreType.DMA((4,))],
        name="undispatch_sc",
    )(dest3, y_rows)


def _gated_sum_norm(x2, y4, gates, g_final, tile):
    g = jnp.concatenate([gates, jnp.zeros((128 - TOP_K, tile), F32)], axis=0)
    gt = g.T
    half = y4.shape[2]
    moe_lo = jnp.zeros((tile, half), F32)
    moe_hi = jnp.zeros((tile, half), F32)
    for k in range(TOP_K):
        lo, hi = _unpack_bf16_pairs(y4[k])
        moe_lo = moe_lo + gt[:, k:k + 1] * lo
        moe_hi = moe_hi + gt[:, k:k + 1] * hi
    return _rms(x2 + jnp.concatenate([moe_lo, moe_hi], axis=1), g_final)


def _combine_kernel(x2_ref, y4_ref, gates_ref, gf_ref, *out_refs, tile):
    out_ref = out_refs[-1]
    out_ref[...] = _gated_sum_norm(x2_ref[...], y4_ref[...], gates_ref[...], gf_ref[...], tile)


def _combine(x2, y4, gates, g_final, tile, tok_tile0, n_tiles, y_tile0, out_prev=None):
    n, d = x2.shape
    in_specs = [
        pl.BlockSpec((tile, d), lambda i: (i + tok_tile0, 0)),
        pl.BlockSpec((TOP_K, tile, d // 2), lambda i: (0, i + y_tile0, 0)),
        pl.BlockSpec((TOP_K, tile), lambda i: (0, i + tok_tile0)),
        pl.BlockSpec((1, d), lambda i: (0, 0)),
    ]
    args = [x2, y4, gates, g_final]
    aliases = {}
    if out_prev is not None:
        in_specs.append(pl.BlockSpec(memory_space=pl.ANY))
        args.append(out_prev)
        aliases = {len(args) - 1: 0}
    return pl.pallas_call(
        functools.partial(_combine_kernel, tile=tile),
        grid=(n_tiles,),
        in_specs=in_specs,
        out_specs=pl.BlockSpec((tile, d), lambda i: (i + tok_tile0, 0)),
        out_shape=jax.ShapeDtypeStruct((n, d), F32),
        input_output_aliases=aliases,
        compiler_params=_cparams(("arbitrary",)),
        name="combine",
    )(*args)


def _combine_gather_kernel(dest_ref, next_ref, x2_ref, gates_ref, gf_ref, y_hbm, out_ref, buf, sem, *,
                           tile):
    i = pl.program_id(0)
    n = pl.num_programs(0)

    def row_copy(d, slot, k, t):
        return pltpu.make_async_copy(y_hbm.at[pl.ds(d, 1)], buf.at[slot, k, pl.ds(t, 1)], sem.at[slot])

    def issue(rows_ref, slot):
        for t in range(tile):
            for k in range(TOP_K):
                row_copy(rows_ref[k, t], slot, k, t).start(priority=(t * TOP_K + k) % 2)

    def wait(slot):
        for t in range(tile):
            for k in range(TOP_K):
                row_copy(0, slot, k, t).wait()

    @pl.when(i == 0)
    def _():
        issue(dest_ref, 0)

    for slot in range(2):
        @pl.when(i % 2 == slot)
        def _():
            @pl.when(i + 1 < n)
            def _():
                issue(next_ref, 1 - slot)

            wait(slot)
            out_ref[...] = _gated_sum_norm(x2_ref[...], buf[slot], gates_ref[...], gf_ref[...], tile)


def _combine_gather(dest, x2, gates, g_final, y_rows, tile, n_tiles):
    n, d = x2.shape
    rows = lambda nxt: pl.BlockSpec((TOP_K, tile), lambda i: (0, jnp.minimum(i + nxt, n_tiles - 1)),
                                    memory_space=pltpu.SMEM)
    grid_spec = pltpu.PrefetchScalarGridSpec(
        num_scalar_prefetch=0,
        grid=(n_tiles,),
        in_specs=[
            rows(0), rows(1),
            pl.BlockSpec((tile, d), lambda i: (i, 0)),
            pl.BlockSpec((TOP_K, tile), lambda i: (0, i)),
            pl.BlockSpec((1, d), lambda i: (0, 0)),
            pl.BlockSpec(memory_space=pl.ANY),
        ],
        out_specs=pl.BlockSpec((tile, d), lambda i: (i, 0)),
        scratch_shapes=[pltpu.VMEM((2, TOP_K, tile, d // 2), y_rows.dtype),
                        pltpu.SemaphoreType.DMA((2,))],
    )
    return pl.pallas_call(
        functools.partial(_combine_gather_kernel, tile=tile),
        grid_spec=grid_spec,
        out_shape=jax.ShapeDtypeStruct((n, d), F32),
        compiler_params=_cparams(("arbitrary",)),
        name="combine_gather",
    )(dest, dest, x2, gates, g_final, y_rows)


def kernel(x_prompt, x_sample, mem_prompt, cache_mem_k, cache_mem_v, state_conv, g_mix, w_in, g_v,
           w_spatial, b_spatial, w_conv, w_out, g_xattn, g_mem, w_q, w_k, w_v, w_xo, g_moe, w_router,
           b_router, w_gate_up, b_gate_up, w_down, b_down, g_final):
    depth = g_mix.shape[0]
    assert depth == 1, "one layer supported"
    b, s, d = x_prompt.shape
    nb, ns, _ = x_sample.shape
    assert ns == 1
    n_mem = mem_prompt.shape[1]
    a_heads, a_hd = g_v.shape[1], g_v.shape[2]
    a_width = a_heads * a_hd
    b_width = w_conv.shape[2]
    assert a_heads == A_HEADS and 2 * a_hd == CHUNK and w_spatial.shape[2] == CHUNK
    assert s % MIX_TILE == 0 and s % ATT_TILE == 0 and (b * n_mem) % KV_TILE == 0
    assert nb % SAMPLE_ATT_GROUP == 0
    l = 0
    row = lambda a: a.reshape(1, -1)

    w_in_b = w_in[l].astype(BF16)
    w_out_b = w_out[l].astype(BF16)
    w_q_b = w_q[l].astype(BF16)
    w_k_b = w_k[l].astype(BF16)
    w_v_b = w_v[l].astype(BF16)
    w_xo_b = w_xo[l].astype(BF16)
    wrt_b = w_router[l].T.astype(BF16)
    br_col = b_router[l].reshape(N_EXPERTS, 1).astype(F32)
    gv_row = row(g_v[l])
    ws = w_spatial[l]
    ws_pairs = jnp.concatenate([ws[0::2], ws[1::2]], axis=2)
    bs_full = jnp.repeat(b_spatial[l].T, a_hd, axis=1)
    w00 = row(jnp.repeat(ws[:, 0, 0], a_hd))
    b0 = row(jnp.repeat(b_spatial[l][:, 0], a_hd))

    mk2, mv2 = _mem_kv(mem_prompt.reshape(b * n_mem, d), row(g_mem[l]), w_k_b, w_v_b)
    x1_p, conv_p = _mixer_prompt(x_prompt, row(g_mix[l]), w_in_b, gv_row, ws_pairs, bs_full,
                                 w_conv[l], w_out_b)
    x2_p, h3_p, tope_p, gates_p = _attn_prompt(
        x1_p, mk2.reshape(b, n_mem, d), mv2.reshape(b, n_mem, d), row(g_xattn[l]), w_q_b, w_xo_b,
        row(g_moe[l]), wrt_b, br_col)

    xs2 = x_sample.reshape(nb, d)
    st = state_conv[l]
    x1_s, v_s, z_s, q_s = _mixer_sample(xs2, st[:, 0], st[:, 1], row(g_mix[l]), w_in_b, gv_row, w00, b0,
                                        w_conv[l], w_out_b, row(g_xattn[l]), w_q_b)
    x_heads, x_hd = cache_mem_k.shape[3], cache_mem_k.shape[4]
    assert depth == 1 and x_heads == X_HEADS
    o_s = _attn_sample(q_s.reshape(nb, x_heads, x_hd), cache_mem_k.reshape(nb, n_mem, x_heads, x_hd),
                       cache_mem_v.reshape(nb, n_mem, x_heads, x_hd)).reshape(nb, d)
    x2_s, h3_s, tope_s, gates_s = _tail_sample(x1_s, o_s, w_xo_b, row(g_moe[l]), wrt_b, br_col)

    t_p = b * s
    t_all = t_p + nb
    t_pad = -(-t_all // ROUTE_TILE) * ROUTE_TILE
    tope_all = jnp.concatenate([tope_p, tope_s, jnp.full((TOP_K, t_pad - t_all), -1, I32)], axis=1)
    dest, cnt = _route(tope_all)
    counts = cnt[:, 0]
    padded = (counts + MOE_BLOCK - 1) // MOE_BLOCK * MOE_BLOCK
    pad_ends = jnp.cumsum(padded)
    pad_starts = pad_ends - padded
    n_blocks = -(-(t_all * TOP_K) // MOE_BLOCK) + N_EXPERTS
    n_rows = n_blocks * MOE_BLOCK
    assert t_all % SC_CHUNK == 0 and t_p % SC_CHUNK == 0
    dest3 = dest[:, :t_all].reshape(TOP_K, t_all // SC_CHUNK, SC_CHUNK).transpose(1, 0, 2)
    xs_rows = _dispatch(dest3, h3_p.reshape(t_p, d // 2), h3_s, n_rows)
    x_heads = cache_mem_k.shape[3]
    mem_k_out = mk2.reshape(1, b, n_mem, x_heads, d // x_heads)
    mem_v_out = mv2.reshape(1, b, n_mem, x_heads, d // x_heads)
    xs_rows, mem_k_out, mem_v_out = lax.optimization_barrier((xs_rows, mem_k_out, mem_v_out))
    y_rows = _expert_ffn(pad_starts.astype(I32), (padded // MOE_BLOCK).astype(I32), counts.astype(I32),
                         xs_rows, w_gate_up[l], b_gate_up[l], w_down[l], b_down[l])

    t_g = GATHER_TILES * GATHER_TILE
    assert t_g % SC_CHUNK == 0 and t_g % COMBINE_TILE == 0 and (t_p - t_g) % nb == 0
    y4 = _undispatch(dest3[t_g // SC_CHUNK:], y_rows)
    gf = row(g_final)
    x2_p2 = x2_p.reshape(t_p, d)
    y_p = _combine_gather(dest, x2_p2, gates_p, gf, y_rows, GATHER_TILE, GATHER_TILES)
    y_p = _combine(x2_p2, y4, gates_p, gf, COMBINE_TILE, t_g // COMBINE_TILE,
                   (t_p - t_g) // COMBINE_TILE, 0, out_prev=y_p)
    y_s = _combine(x2_s, y4, gates_s, gf, nb, 0, 1, (t_p - t_g) // nb)

    return (y_p.reshape(b, s, d),
            y_s.reshape(nb, 1, d),
            mem_k_out,
            mem_v_out,
            conv_p.reshape(1, b, CONV_W - 1, b_width),
            jnp.stack([st[:, 1], z_s], axis=1).reshape(1, nb, CONV_W - 1, b_width),
            v_s.reshape(1, nb, 1, a_heads, a_hd))
```

```python
import functools

import jax
import jax.numpy as jnp
from jax import lax
from jax.experimental import pallas as pl
from jax.experimental.pallas import tpu as pltpu
from jax.experimental.pallas import tpu_sc as plsc

F32 = jnp.float32
BF16 = jnp.bfloat16
I32 = jnp.int32
U32 = jnp.uint32

A_HEADS = 8
CHUNK = 128
CONV_W = 3
X_HEADS = 4
N_EXPERTS = 32
TOP_K = 4
SWIGLU_LIMIT = 7.0
SWIGLU_ALPHA = 1.702
EPS = 1e-5

MIX_TILE = 1024
ATT_TILE = 1024
KV_TILE = 512
ROUTE_TILE = 512
MOE_BLOCK = 256
COMBINE_TILE = 512
GATHER_TILE = 256
GATHER_TILES = 20
SAMPLE_ATT_GROUP = 8
SC_CHUNK = 64
VMEM_LIMIT = 56 * 1024 * 1024


def _cparams(sem=None):
    return pltpu.CompilerParams(dimension_semantics=sem, vmem_limit_bytes=VMEM_LIMIT)


def _rms(x, g):
    r = lax.rsqrt(jnp.mean(x * x, axis=-1, keepdims=True) + EPS)
    return (x * r) * g


def _gelu(x):
    return 0.5 * x * (1.0 + lax.erf(x * 0.7071067811865476))


def _dot(a, b):
    return jnp.dot(a, b, preferred_element_type=F32)


def _dot_nt(a, b):
    return lax.dot_general(a, b, (((1,), (1,)), ((), ())), preferred_element_type=F32)


def _pack_bf16_pairs(x):
    d = x.shape[1]
    r = x.astype(BF16).astype(F32)
    lo = lax.shift_right_logical(pltpu.bitcast(r[:, :d // 2], U32), jnp.uint32(16))
    hi = lax.bitwise_and(pltpu.bitcast(r[:, d // 2:], U32), jnp.uint32(0xFFFF0000))
    return lax.bitwise_or(lo, hi)


def _unpack_bf16_pairs(w):
    lo = pltpu.bitcast(lax.shift_left(w, jnp.uint32(16)), F32)
    hi = pltpu.bitcast(lax.bitwise_and(w, jnp.uint32(0xFFFF0000)), F32)
    return lo, hi


def _memkv_kernel(m_ref, g_ref, wk_ref, wv_ref, k_ref, v_ref, wk_b, wv_b):
    @pl.when(pl.program_id(0) == 0)
    def _():
        wk_b[...] = wk_ref[...].astype(BF16)
        wv_b[...] = wv_ref[...].astype(BF16)

    h = _rms(m_ref[...], g_ref[...]).astype(BF16)
    k_ref[...] = _dot(h, wk_b[...])
    v_ref[...] = _dot(h, wv_b[...])


def _resident(shape, n_grid):
    zeros = (0,) * len(shape)
    index_map = (lambda i: zeros) if n_grid == 1 else (lambda i, j: zeros)
    return pl.BlockSpec(shape, index_map, pipeline_mode=pl.Buffered(1))


def _mem_kv(mem2d, g_mem, wk, wv):
    n, d = mem2d.shape
    row = pl.BlockSpec((KV_TILE, d), lambda i: (i, 0))
    return pl.pallas_call(
        _memkv_kernel,
        grid=(n // KV_TILE,),
        in_specs=[row, _resident((1, d), 1), _resident((d, d), 1), _resident((d, d), 1)],
        out_specs=[row, row],
        out_shape=[jax.ShapeDtypeStruct((n, d), F32)] * 2,
        scratch_shapes=[pltpu.VMEM((d, d), BF16), pltpu.VMEM((d, d), BF16)],
        compiler_params=_cparams(("arbitrary",)),
        name="mem_kv",
    )(mem2d, g_mem, wk, wv)


def _head_rms(v, gv, a_width):
    hd = a_width // A_HEADS
    r_i = lax.broadcasted_iota(I32, (a_width, a_width), 0) // hd
    c_i = lax.broadcasted_iota(I32, (a_width, a_width), 1) // hd
    ones_bd = jnp.where(r_i == c_i, 1.0, 0.0).astype(BF16)
    sq = v * v
    sq_hi = sq.astype(BF16)
    sq_lo = (sq - sq_hi.astype(F32)).astype(BF16)
    gs = _dot(sq_hi, ones_bd) + _dot(sq_lo, ones_bd)
    return (v * lax.rsqrt(gs * (1.0 / hd) + EPS)) * gv


def _mixer_prompt_kernel(x_ref, gmix_ref, win_f32, gv_ref, ws_ref, bs_ref, wc_ref, wout_f32,
                         x1_ref, cs_ref, zbuf, win_ref, wout_ref, *, tile, a_width, b_width):
    j = pl.program_id(1)
    nj = pl.num_programs(1)

    @pl.when((pl.program_id(0) == 0) & (j == 0))
    def _():
        win_ref[...] = win_f32[...].astype(BF16)
        wout_ref[...] = wout_f32[...].astype(BF16)

    x = x_ref[...]
    h = _rms(x, gmix_ref[...]).astype(BF16)
    proj = _dot(h, win_ref[...])
    u = _gelu(proj[:, :a_width])
    v = _head_rms(_gelu(proj[:, a_width:2 * a_width]), gv_ref[...], a_width)
    o = 2 * a_width
    hb = proj[:, o:o + b_width]
    bg = proj[:, o + b_width:o + 2 * b_width]
    cg = proj[:, o + 2 * b_width:o + 3 * b_width]

    lane = lax.broadcasted_iota(I32, (CHUNK, 2 * (a_width // A_HEADS)), 1)
    first = lane < (a_width // A_HEADS)
    t_i = lax.broadcasted_iota(I32, (CHUNK, 2 * CHUNK), 0)
    s_i = lax.broadcasted_iota(I32, (CHUNK, 2 * CHUNK), 1) % CHUNK
    causal = s_i <= t_i
    pair_cols = []
    for p in range(A_HEADS // 2):
        w_pair = jnp.where(causal, ws_ref[p], 0.0).astype(BF16)
        vp = v[:, p * CHUNK:(p + 1) * CHUNK]
        rows = []
        for c in range(tile // CHUNK):
            vc = vp[c * CHUNK:(c + 1) * CHUNK]
            rhs = jnp.concatenate([jnp.where(first, vc, 0.0), jnp.where(first, 0.0, vc)],
                                  axis=0).astype(BF16)
            rows.append(_dot(w_pair, rhs))
        pair_cols.append(jnp.concatenate(rows, axis=0))
    gate = jnp.concatenate(pair_cols, axis=1)
    bias = jnp.concatenate([bs_ref[...]] * (tile // CHUNK), axis=0)
    a_out = u * (gate + bias)

    @pl.when(j == 0)
    def _():
        zbuf[0:8, :] = jnp.zeros((8, b_width), F32)

    z = cg * hb
    zbuf[8:tile + 8, :] = z
    z1 = zbuf[7:tile + 7, :]
    z2 = zbuf[6:tile + 6, :]
    wc = wc_ref[...]
    conv = z2 * wc[0:1] + z1 * wc[1:2] + z * wc[2:3]
    b_out = bg * conv
    tail = zbuf[tile:tile + 8, :]
    zbuf[0:8, :] = tail

    @pl.when(j == nj - 1)
    def _():
        cs_ref[...] = tail[8 - (CONV_W - 1):, :]

    y = _dot(a_out.astype(BF16), wout_ref[0:a_width, :]) + _dot(b_out.astype(BF16), wout_ref[a_width:, :])
    x1_ref[...] = x + y


def _mixer_prompt(x, g_mix, w_in, g_v, ws_pairs, bs_full, w_conv, w_out):
    b, s, d = x.shape
    a_width = g_v.shape[1]
    b_width = w_conv.shape[1]
    in_width = w_in.shape[1]
    tile = MIX_TILE
    full = lambda shape: _resident(shape, 2)
    kern = functools.partial(_mixer_prompt_kernel, tile=tile, a_width=a_width, b_width=b_width)
    return pl.pallas_call(
        kern,
        grid=(b, s // tile),
        in_specs=[
            pl.BlockSpec((None, tile, d), lambda i, j: (i, j, 0)),
            full((1, d)), full((d, in_width)), full((1, a_width)),
            full(ws_pairs.shape), full(bs_full.shape), full(w_conv.shape), full(w_out.shape),
        ],
        out_specs=[
            pl.BlockSpec((None, tile, d), lambda i, j: (i, j, 0)),
            pl.BlockSpec((None, CONV_W - 1, b_width), lambda i, j: (i, 0, 0)),
        ],
        out_shape=[jax.ShapeDtypeStruct((b, s, d), F32),
                   jax.ShapeDtypeStruct((b, CONV_W - 1, b_width), F32)],
        scratch_shapes=[pltpu.VMEM((tile + 8, b_width), F32), pltpu.VMEM((d, in_width), BF16),
                        pltpu.VMEM(w_out.shape, BF16)],
        compiler_params=_cparams(("arbitrary", "arbitrary")),
        name="mixer_prompt",
    )(x, g_mix, w_in, g_v, ws_pairs, bs_full, w_conv, w_out)


def _mixer_sample_kernel(x_ref, s0_ref, s1_ref, gmix_ref, win_ref, gv_ref, w00_ref, b0_ref, wc_ref,
                         wout_ref, gx_ref, wq_ref, x1_ref, v_ref, z_ref, q_ref, *, a_width, b_width):
    x = x_ref[...]
    h = _rms(x, gmix_ref[...]).astype(BF16)
    proj = _dot(h, win_ref[...].astype(BF16))
    u = _gelu(proj[:, :a_width])
    v = _head_rms(_gelu(proj[:, a_width:2 * a_width]), gv_ref[...], a_width)
    v_ref[...] = v
    o = 2 * a_width
    hb = proj[:, o:o + b_width]
    bg = proj[:, o + b_width:o + 2 * b_width]
    cg = proj[:, o + 2 * b_width:o + 3 * b_width]
    a_out = u * (v * w00_ref[...] + b0_ref[...])
    z = cg * hb
    z_ref[...] = z
    wc = wc_ref[...]
    conv = s0_ref[...] * wc[0:1] + s1_ref[...] * wc[1:2] + z * wc[2:3]
    b_out = bg * conv
    y = (_dot(a_out.astype(BF16), wout_ref[0:a_width, :].astype(BF16))
         + _dot(b_out.astype(BF16), wout_ref[a_width:, :].astype(BF16)))
    x1 = x + y
    x1_ref[...] = x1
    q_ref[...] = _dot(_rms(x1, gx_ref[...]).astype(BF16), wq_ref[...].astype(BF16))


def _mixer_sample(x, s0, s1, g_mix, w_in, g_v, w00, b0, w_conv, w_out, g_x, w_q):
    n, d = x.shape
    a_width = g_v.shape[1]
    b_width = w_conv.shape[1]
    kern = functools.partial(_mixer_sample_kernel, a_width=a_width, b_width=b_width)
    return pl.pallas_call(
        kern,
        out_shape=[jax.ShapeDtypeStruct((n, d), F32), jax.ShapeDtypeStruct((n, a_width), F32),
                   jax.ShapeDtypeStruct((n, b_width), F32), jax.ShapeDtypeStruct((n, d), F32)],
        compiler_params=_cparams(),
        name="mixer_sample",
    )(x, s0, s1, g_mix, w_in, g_v, w00, b0, w_conv, w_out, g_x, w_q)


def _router_tail(x2, gmoe, wrt, br, h3_ref, tope_ref, gates_ref):
    h3f = _rms(x2, gmoe)
    h3 = h3f.astype(BF16)
    h3_ref[...] = _pack_bf16_pairs(h3f)

    logits = _dot_nt(wrt, h3) + br
    n_e, n_t = logits.shape
    e_idx = lax.broadcasted_iota(I32, (n_e, n_t), 0).astype(F32)
    tops, idxs = [], []
    for _ in range(TOP_K):
        m = jnp.max(logits, axis=0, keepdims=True)
        idx = jnp.min(jnp.where(logits == m, e_idx, float(n_e)), axis=0, keepdims=True)
        tops.append(m)
        idxs.append(idx)
        logits = jnp.where(e_idx == idx, -jnp.inf, logits)
    top = jnp.concatenate(tops, axis=0)
    ex = jnp.exp(top - top[0:1])
    gates_ref[...] = ex / jnp.sum(ex, axis=0, keepdims=True)
    tope_ref[...] = jnp.concatenate(idxs, axis=0).astype(I32)


def _attn_prompt_kernel(x1_ref, k_ref, v_ref, gx_ref, wq_f32, wxo_f32, gmoe_ref, wrt_ref, br_ref,
                        x2_ref, h3_ref, tope_ref, gates_ref, wq_ref, wxo_ref):
    @pl.when((pl.program_id(0) == 0) & (pl.program_id(1) == 0))
    def _():
        wq_ref[...] = wq_f32[...].astype(BF16)
        wxo_ref[...] = wxo_f32[...].astype(BF16)

    x = x1_ref[...]
    d = x.shape[1]
    hd = d // X_HEADS
    q = _dot(_rms(x, gx_ref[...]).astype(BF16), wq_ref[...]).astype(BF16)
    kb = k_ref[...].astype(BF16)
    vb = v_ref[...].astype(BF16)
    outs = []
    for hh in range(X_HEADS):
        sl = slice(hh * hd, (hh + 1) * hd)
        s = _dot_nt(q[:, sl], kb[:, sl]) * (hd ** -0.5)
        e = jnp.exp(s - jnp.max(s, axis=-1, keepdims=True))
        p = e / jnp.sum(e, axis=-1, keepdims=True)
        outs.append(_dot(p.astype(BF16), vb[:, sl]))
    o = jnp.concatenate(outs, axis=1).astype(BF16)
    x2 = x + _dot(o, wxo_ref[...])
    x2_ref[...] = x2
    _router_tail(x2, gmoe_ref[...], wrt_ref[...], br_ref[...], h3_ref, tope_ref, gates_ref)


def _attn_prompt(x1, mk, mv, g_x, w_q, w_xo, g_moe, wrt, br):
    b, s, d = x1.shape
    n_mem = mk.shape[1]
    tile = ATT_TILE
    nq = s // tile
    full = lambda shape: _resident(shape, 2)
    tok = pl.BlockSpec((None, tile, d), lambda i, j: (i, j, 0))
    mem = pl.BlockSpec((None, n_mem, d), lambda i, j: (i, 0, 0))
    lanes = pl.BlockSpec((TOP_K, tile), lambda i, j: (0, i * nq + j))
    return pl.pallas_call(
        _attn_prompt_kernel,
        grid=(b, nq),
        in_specs=[tok, mem, mem, full((1, d)), full((d, d)), full((d, d)), full((1, d)),
                  full(wrt.shape), full(br.shape)],
        out_specs=[tok, pl.BlockSpec((None, tile, d // 2), lambda i, j: (i, j, 0)), lanes, lanes],
        out_shape=[jax.ShapeDtypeStruct((b, s, d), F32), jax.ShapeDtypeStruct((b, s, d // 2), U32),
                   jax.ShapeDtypeStruct((TOP_K, b * s), I32), jax.ShapeDtypeStruct((TOP_K, b * s), F32)],
        scratch_shapes=[pltpu.VMEM((d, d), BF16), pltpu.VMEM((d, d), BF16)],
        compiler_params=_cparams(("arbitrary", "arbitrary")),
        name="attn_prompt",
    )(x1, mk, mv, g_x, w_q, w_xo, g_moe, wrt, br)


def _attn_sample_kernel(q_ref, k_ref, v_ref, o_ref, *, group, heads, scale):
    for g in range(group):
        q = q_ref[g] * scale
        k = k_ref[g]
        part = jnp.broadcast_to(jnp.sum(k * q[None], axis=-1, keepdims=True), k.shape)
        s = part + pltpu.roll(part, heads, axis=1)
        e = jnp.exp(s - jnp.max(s, axis=0, keepdims=True))
        o_ref[g] = jnp.sum(e * v_ref[g], axis=0) / jnp.sum(e, axis=0)


def _to_rows8(a, heads, hd):
    lead = a.shape[:-2]
    n = len(lead)
    a = a.reshape(*lead, heads, hd // 128, 128)
    a = a.transpose(*range(n), n + 1, n, n + 2)
    return a.reshape(*lead, (hd // 128) * heads, 128)


def _from_rows8(a, heads, hd):
    lead = a.shape[:-2]
    n = len(lead)
    a = a.reshape(*lead, hd // 128, heads, 128)
    a = a.transpose(*range(n), n + 1, n, n + 2)
    return a.reshape(*lead, heads, hd)


def _attn_sample(q, ck, cv):
    n, heads, hd = q.shape
    assert hd == 2 * 128
    n_mem = ck.shape[1]
    group = SAMPLE_ATT_GROUP
    rows = (hd // 128) * heads
    qo = pl.BlockSpec((group, rows, 128), lambda i: (i, 0, 0))
    kv = pl.BlockSpec((group, n_mem, rows, 128), lambda i: (i, 0, 0, 0))
    o = pl.pallas_call(
        functools.partial(_attn_sample_kernel, group=group, heads=heads, scale=hd ** -0.5),
        grid=(n // group,),
        in_specs=[qo, kv, kv],
        out_specs=qo,
        out_shape=jax.ShapeDtypeStruct((n, rows, 128), F32),
        compiler_params=_cparams(("arbitrary",)),
        name="attn_sample",
    )(_to_rows8(q, heads, hd), _to_rows8(ck, heads, hd), _to_rows8(cv, heads, hd))
    return _from_rows8(o, heads, hd)


def _tail_sample_kernel(x1_ref, o_ref, wxo_ref, gmoe_ref, wrt_ref, br_ref,
                        x2_ref, h3_ref, tope_ref, gates_ref):
    x2 = x1_ref[...] + _dot(o_ref[...].astype(BF16), wxo_ref[...].astype(BF16))
    x2_ref[...] = x2
    _router_tail(x2, gmoe_ref[...], wrt_ref[...], br_ref[...], h3_ref, tope_ref, gates_ref)


def _tail_sample(x1, o, w_xo, g_moe, wrt, br):
    n, d = x1.shape
    return pl.pallas_call(
        _tail_sample_kernel,
        out_shape=[jax.ShapeDtypeStruct((n, d), F32), jax.ShapeDtypeStruct((n, d // 2), U32),
                   jax.ShapeDtypeStruct((TOP_K, n), I32), jax.ShapeDtypeStruct((TOP_K, n), F32)],
        compiler_params=_cparams(),
        name="tail_sample",
    )(x1, o, w_xo, g_moe, wrt, br)


def _route_kernel(tope_ref, dest_ref, cnt_ref, *, n_blocks, tile, moe_block):
    e_idx = lax.broadcasted_iota(I32, (N_EXPERTS, tile), 0)
    earlier = jnp.where(lax.broadcasted_iota(I32, (tile, tile), 0)
                        < lax.broadcasted_iota(I32, (tile, tile), 1), 1.0, 0.0).astype(BF16)

    def onehot(k, off):
        return e_idx == tope_ref[pl.ds(k, 1), pl.ds(off, tile)]

    def count_body(j, cnt):
        off = pl.multiple_of(j * tile, tile)
        for k in range(TOP_K):
            cnt = cnt + jnp.sum(jnp.where(onehot(k, off), 1.0, 0.0), axis=1, keepdims=True)
        return cnt

    counts = lax.fori_loop(0, n_blocks, count_body, jnp.zeros((N_EXPERTS, 1), F32))
    cnt_ref[...] = jnp.broadcast_to(counts, cnt_ref.shape).astype(I32)

    n_blk = jnp.floor((counts + (moe_block - 1)) * (1.0 / moe_block))
    n_hi = jnp.floor(n_blk * (1.0 / 16.0))
    n_lo = n_blk - 16.0 * n_hi
    below = jnp.where(lax.broadcasted_iota(I32, (N_EXPERTS, N_EXPERTS), 1)
                      < lax.broadcasted_iota(I32, (N_EXPERTS, N_EXPERTS), 0), 1.0, 0.0).astype(BF16)
    wide = lambda c: jnp.broadcast_to(c, (N_EXPERTS, 128)).astype(BF16)
    start = (16.0 * _dot(below, wide(n_hi)) + _dot(below, wide(n_lo)))[:, 0:1] * float(moe_block)

    def dest_body(j, run):
        off = pl.multiple_of(j * tile, tile)
        for k in range(TOP_K):
            oh = onehot(k, off)
            ohf = jnp.where(oh, 1.0, 0.0)
            before = _dot(ohf.astype(BF16), earlier)
            dest = jnp.sum(jnp.where(oh, before + run, 0.0), axis=0, keepdims=True)
            dest_ref[pl.ds(k, 1), pl.ds(off, tile)] = dest.astype(I32)
            run = run + jnp.sum(ohf, axis=1, keepdims=True)
        return run

    lax.fori_loop(0, n_blocks, dest_body, start)


def _route(tope_pad):
    t_pad = tope_pad.shape[1]
    kern = functools.partial(_route_kernel, n_blocks=t_pad // ROUTE_TILE, tile=ROUTE_TILE,
                             moe_block=MOE_BLOCK)
    return pl.pallas_call(
        kern,
        out_shape=[jax.ShapeDtypeStruct((TOP_K, t_pad), I32),
                   jax.ShapeDtypeStruct((N_EXPERTS, 128), I32)],
        compiler_params=_cparams(),
        name="route",
    )(tope_pad)


def _dispatch(dest3, h_prompt, h_sample, n_rows):
    w = h_prompt.shape[1]
    n_prompt_chunks = h_prompt.shape[0] // SC_CHUNK
    n_chunks = dest3.shape[0]
    mesh = plsc.VectorSubcoreMesh(core_axis_name="c", subcore_axis_name="s")
    n_workers = mesh.num_cores * mesh.num_subcores

    n_steps = -(-n_chunks // n_workers)

    def body(dest_hbm, hp_hbm, hs_hbm, xs_hbm, idx0, idx1, rows0, rows1, sem):
        wid = lax.axis_index("s") * mesh.num_cores + lax.axis_index("c")
        idx_v, rows_v = (idx0, idx1), (rows0, rows1)

        def chunk(j):
            return wid + j * n_workers

        def loads(j, slot):
            c = chunk(j)
            prompt = (hp_hbm, c)
            sample = (hs_hbm, c - n_prompt_chunks)
            return c, [
                (pltpu.make_async_copy(dest_hbm.at[c], idx_v[slot], sem.at[slot]), None),
                (pltpu.make_async_copy(prompt[0].at[pl.ds(prompt[1] * SC_CHUNK, SC_CHUNK)], rows_v[slot],
                                       sem.at[slot]), c < n_prompt_chunks),
                (pltpu.make_async_copy(sample[0].at[pl.ds(sample[1] * SC_CHUNK, SC_CHUNK)], rows_v[slot],
                                       sem.at[slot]), c >= n_prompt_chunks),
            ]

        def start_loads(j, slot):
            c, copies = loads(j, slot)
            for cp, cond in copies:
                pl.when((c < n_chunks) if cond is None else ((c < n_chunks) & cond))(cp.start)

        def wait_loads(j, slot):
            c, copies = loads(j, slot)
            for cp, cond in copies:
                pl.when((c < n_chunks) if cond is None else ((c < n_chunks) & cond))(cp.wait)

        def scatters(slot):
            return [pltpu.make_async_copy(rows_v[slot], xs_hbm.at[idx_v[slot].at[k]], sem.at[2 + slot])
                    for k in range(TOP_K)]

        start_loads(0, 0)
        for j in range(n_steps):
            slot = j % 2
            wait_loads(j, slot)
            if j >= 1:
                @pl.when(chunk(j - 1) < n_chunks)
                def _():
                    for cp in scatters(1 - slot):
                        cp.wait()
            if j + 1 < n_steps:
                start_loads(j + 1, 1 - slot)

            @pl.when(chunk(j) < n_chunks)
            def _():
                for cp in scatters(slot):
                    cp.start()

        @pl.when(chunk(n_steps - 1) < n_chunks)
        def _():
            for cp in scatters((n_steps - 1) % 2):
                cp.wait()

    return pl.kernel(
        body,
        out_type=jax.ShapeDtypeStruct((n_rows, w), U32),
        mesh=mesh,
        scratch_types=[pltpu.VMEM((TOP_K, SC_CHUNK), I32), pltpu.VMEM((TOP_K, SC_CHUNK), I32),
                       pltpu.VMEM((SC_CHUNK, w), U32), pltpu.VMEM((SC_CHUNK, w), U32),
                       pltpu.SemaphoreType.DMA((4,))],
        name="dispatch_sc",
    )(dest3, h_prompt, h_sample)


def _ffn_kernel(start_ref, nblk_ref, count_ref, b1_ref, b2_ref, xs_hbm, w1_hbm, w2_hbm, y_hbm,
                w1f, w2f, w1b, w2b, xbuf, ybuf, sem_w, sem_x, sem_y, *, blk):
    e = pl.program_id(0)
    n_e = pl.num_programs(0)
    nb = nblk_ref[e]
    base = start_ref[e]
    wslot = e % 2

    def w_copies(expert, slot):
        return (pltpu.make_async_copy(w1_hbm.at[expert], w1f.at[slot], sem_w.at[slot]),
                pltpu.make_async_copy(w2_hbm.at[expert], w2f.at[slot], sem_w.at[slot]))

    def x_copy(row0, slot):
        return pltpu.make_async_copy(xs_hbm.at[pl.ds(pl.multiple_of(row0, blk), blk)], xbuf.at[slot],
                                     sem_x.at[slot])

    def y_copy(row0, slot):
        return pltpu.make_async_copy(ybuf.at[slot], y_hbm.at[pl.ds(pl.multiple_of(row0, blk), blk)],
                                     sem_y.at[slot])

    @pl.when(e == 0)
    def _():
        for cp in w_copies(0, 0):
            cp.start()

        @pl.when(nb > 0)
        def _():
            x_copy(base, 0).start()

    for cp in w_copies(e, wslot):
        cp.wait()

    @pl.when(e + 1 < n_e)
    def _():
        for cp in w_copies(e + 1, 1 - wslot):
            cp.start(priority=1)

    w1b[...] = w1f[wslot].astype(BF16)
    w2b[...] = w2f[wslot].astype(BF16)
    b1 = b1_ref[e]
    b2 = b2_ref[e]

    def block(j, slot):
        x_copy(base, slot).wait()

        @pl.when(j + 1 < nb)
        def _():
            x_copy(base + (j + 1) * blk, 1 - slot).start()

        @pl.when(j >= 2)
        def _():
            y_copy(base, slot).wait()

        w = xbuf[slot]
        row = lax.broadcasted_iota(I32, w.shape, 0)
        w = jnp.where(row < count_ref[e] - j * blk, w, jnp.uint32(0))
        half = w.shape[1]
        d_e = w2b.shape[0]
        lo, hi = _unpack_bf16_pairs(w)
        gu = _dot(lo.astype(BF16), w1b[0:half, :]) + _dot(hi.astype(BF16), w1b[half:, :]) + b1
        gate = jnp.minimum(gu[:, :d_e], SWIGLU_LIMIT)
        up = jnp.clip(gu[:, d_e:], -SWIGLU_LIMIT, SWIGLU_LIMIT)
        glu = gate * jax.nn.sigmoid(gate * SWIGLU_ALPHA)
        act = ((up + 1.0) * glu).astype(BF16)
        ybuf[slot] = _pack_bf16_pairs(_dot(act, w2b[...]) + b2)
        y_copy(base + j * blk, slot).start()

    def pair(j2, carry):
        j = 2 * j2
        block(j, 0)

        @pl.when(j + 1 < nb)
        def _():
            block(j + 1, 1)

        return carry

    lax.fori_loop(0, (nb + 1) // 2, pair, 0)

    @pl.when(nb >= 2)
    def _():
        y_copy(base, 0).wait()
        y_copy(base, 1).wait()

    @pl.when(nb == 1)
    def _():
        y_copy(base, 0).wait()

    nxt = jnp.minimum(e + 1, n_e - 1)

    @pl.when((e + 1 < n_e) & (nblk_ref[nxt] > 0))
    def _():
        x_copy(start_ref[nxt], 0).start()


def _expert_ffn(grp_start, grp_blocks, grp_count, xs, w1, b1, w2, b2):
    n_rows, half = xs.shape
    n_e, d, d2 = w1.shape
    d_e = w2.shape[1]
    blk = MOE_BLOCK
    whole = lambda shape: pl.BlockSpec(shape, lambda e, s, n, c: (0,) * len(shape))
    hbm = pl.BlockSpec(memory_space=pl.ANY)
    grid_spec = pltpu.PrefetchScalarGridSpec(
        num_scalar_prefetch=3,
        grid=(n_e,),
        in_specs=[whole((n_e, 1, d2)), whole((n_e, 1, d)), hbm, hbm, hbm],
        out_specs=hbm,
        scratch_shapes=[pltpu.VMEM((2, d, d2), F32), pltpu.VMEM((2, d_e, d), F32),
                        pltpu.VMEM((d, d2), BF16), pltpu.VMEM((d_e, d), BF16),
                        pltpu.VMEM((2, blk, half), U32), pltpu.VMEM((2, blk, d // 2), U32),
                        pltpu.SemaphoreType.DMA((2,)), pltpu.SemaphoreType.DMA((2,)),
                        pltpu.SemaphoreType.DMA((2,))],
    )
    return pl.pallas_call(
        functools.partial(_ffn_kernel, blk=blk),
        grid_spec=grid_spec,
        out_shape=jax.ShapeDtypeStruct((n_rows, d // 2), U32),
        compiler_params=_cparams(("arbitrary",)),
        name="expert_ffn",
    )(grp_start, grp_blocks, grp_count, b1.reshape(n_e, 1, d2), b2.reshape(n_e, 1, d), xs, w1, w2)


def _undispatch(dest3, y_rows):
    n_chunks = dest3.shape[0]
    w = y_rows.shape[1]
    mesh = plsc.VectorSubcoreMesh(core_axis_name="c", subcore_axis_name="s")
    n_workers = mesh.num_cores * mesh.num_subcores
    n_steps = -(-n_chunks // n_workers)

    def body(dest_hbm, y_hbm, out_hbm, idx0, idx1, rows0, rows1, sem):
        wid = lax.axis_index("s") * mesh.num_cores + lax.axis_index("c")
        idx_v, rows_v = (idx0, idx1), (rows0, rows1)

        def chunk(j):
            return wid + j * n_workers

        def gather(j, k, slot):
            return pltpu.make_async_copy(y_hbm.at[idx_v[j % 2].at[k]], rows_v[slot], sem.at[slot])

        def store(j, k, slot):
            return pltpu.make_async_copy(rows_v[slot], out_hbm.at[k, pl.ds(chunk(j) * SC_CHUNK, SC_CHUNK)],
                                         sem.at[2 + slot])

        units = [(j, k) for j in range(n_steps) for k in range(TOP_K)]

        def valid(j):
            return chunk(j) < n_chunks

        def load_idx(j):
            pl.when(valid(j))(lambda: pltpu.sync_copy(dest_hbm.at[chunk(j)], idx_v[j % 2]))

        load_idx(0)
        pl.when(valid(0))(gather(0, 0, 0).start)
        for u, (j, k) in enumerate(units):
            slot = u % 2
            pl.when(valid(j))(gather(j, k, slot).wait)
            if u + 1 < len(units):
                jn, kn = units[u + 1]
                if u >= 1:
                    jp, kp = units[u - 1]
                    pl.when(valid(jp))(store(jp, kp, 1 - slot).wait)
                if kn == 0:
                    load_idx(jn)
                pl.when(valid(jn))(gather(jn, kn, 1 - slot).start)
            pl.when(valid(j))(store(j, k, slot).start)
        for u in (len(units) - 2, len(units) - 1):
            j, k = units[u]
            pl.when(valid(j))(store(j, k, u % 2).wait)

    return pl.kernel(
        body,
        out_type=jax.ShapeDtypeStruct((TOP_K, n_chunks * SC_CHUNK, w), y_rows.dtype),
        mesh=mesh,
        scratch_types=[pltpu.VMEM((TOP_K, SC_CHUNK), I32), pltpu.VMEM((TOP_K, SC_CHUNK), I32),
                       pltpu.VMEM((SC_CHUNK, w), y_rows.dtype), pltpu.VMEM((SC_CHUNK, w), y_rows.dtype),
                       pltpu.SemaphoreType.DMA((4,))],
        name="undispatch_sc",
    )(dest3, y_rows)


def _gated_sum_norm(x2, y4, gates, g_final, tile):
    g = jnp.concatenate([gates, jnp.zeros((128 - TOP_K, tile), F32)], axis=0)
    gt = g.T
    half = y4.shape[2]
    moe_lo = jnp.zeros((tile, half), F32)
    moe_hi = jnp.zeros((tile, half), F32)
    for k in range(TOP_K):
        lo, hi = _unpack_bf16_pairs(y4[k])
        moe_lo = moe_lo + gt[:, k:k + 1] * lo
        moe_hi = moe_hi + gt[:, k:k + 1] * hi
    return _rms(x2 + jnp.concatenate([moe_lo, moe_hi], axis=1), g_final)


def _combine_kernel(x2_ref, y4_ref, gates_ref, gf_ref, *out_refs, tile):
    out_ref = out_refs[-1]
    out_ref[...] = _gated_sum_norm(x2_ref[...], y4_ref[...], gates_ref[...], gf_ref[...], tile)


def _combine(x2, y4, gates, g_final, tile, tok_tile0, n_tiles, y_tile0, out_prev=None):
    n, d = x2.shape
    in_specs = [
        pl.BlockSpec((tile, d), lambda i: (i + tok_tile0, 0)),
        pl.BlockSpec((TOP_K, tile, d // 2), lambda i: (0, i + y_tile0, 0)),
        pl.BlockSpec((TOP_K, tile), lambda i: (0, i + tok_tile0)),
        pl.BlockSpec((1, d), lambda i: (0, 0)),
    ]
    args = [x2, y4, gates, g_final]
    aliases = {}
    if out_prev is not None:
        in_specs.append(pl.BlockSpec(memory_space=pl.ANY))
        args.append(out_prev)
        aliases = {len(args) - 1: 0}
    return pl.pallas_call(
        functools.partial(_combine_kernel, tile=tile),
        grid=(n_tiles,),
        in_specs=in_specs,
        out_specs=pl.BlockSpec((tile, d), lambda i: (i + tok_tile0, 0)),
        out_shape=jax.ShapeDtypeStruct((n, d), F32),
        input_output_aliases=aliases,
        compiler_params=_cparams(("arbitrary",)),
        name="combine",
    )(*args)


def _combine_gather_kernel(dest_ref, next_ref, x2_ref, gates_ref, gf_ref, y_hbm, out_ref, buf, sem, *,
                           tile):
    i = pl.program_id(0)
    n = pl.num_programs(0)

    def row_copy(d, slot, k, t):
        return pltpu.make_async_copy(y_hbm.at[pl.ds(d, 1)], buf.at[slot, k, pl.ds(t, 1)], sem.at[slot])

    def issue(rows_ref, slot):
        for t in range(tile):
            for k in range(TOP_K):
                row_copy(rows_ref[k, t], slot, k, t).start(priority=(t * TOP_K + k) % 2)

    def wait(slot):
        for t in range(tile):
            for k in range(TOP_K):
                row_copy(0, slot, k, t).wait()

    @pl.when(i == 0)
    def _():
        issue(dest_ref, 0)

    for slot in range(2):
        @pl.when(i % 2 == slot)
        def _():
            @pl.when(i + 1 < n)
            def _():
                issue(next_ref, 1 - slot)

            wait(slot)
            out_ref[...] = _gated_sum_norm(x2_ref[...], buf[slot], gates_ref[...], gf_ref[...], tile)


def _combine_gather(dest, x2, gates, g_final, y_rows, tile, n_tiles):
    n, d = x2.shape
    rows = lambda nxt: pl.BlockSpec((TOP_K, tile), lambda i: (0, jnp.minimum(i + nxt, n_tiles - 1)),
                                    memory_space=pltpu.SMEM)
    grid_spec = pltpu.PrefetchScalarGridSpec(
        num_scalar_prefetch=0,
        grid=(n_tiles,),
        in_specs=[
            rows(0), rows(1),
            pl.BlockSpec((tile, d), lambda i: (i, 0)),
            pl.BlockSpec((TOP_K, tile), lambda i: (0, i)),
            pl.BlockSpec((1, d), lambda i: (0, 0)),
            pl.BlockSpec(memory_space=pl.ANY),
        ],
        out_specs=pl.BlockSpec((tile, d), lambda i: (i, 0)),
        scratch_shapes=[pltpu.VMEM((2, TOP_K, tile, d // 2), y_rows.dtype),
                        pltpu.SemaphoreType.DMA((2,))],
    )
    return pl.pallas_call(
        functools.partial(_combine_gather_kernel, tile=tile),
        grid_spec=grid_spec,
        out_shape=jax.ShapeDtypeStruct((n, d), F32),
        compiler_params=_cparams(("arbitrary",)),
        name="combine_gather",
    )(dest, dest, x2, gates, g_final, y_rows)


def kernel(x_prompt, x_sample, mem_prompt, cache_mem_k, cache_mem_v, state_conv, g_mix, w_in, g_v,
           w_spatial, b_spatial, w_conv, w_out, g_xattn, g_mem, w_q, w_k, w_v, w_xo, g_moe, w_router,
           b_router, w_gate_up, b_gate_up, w_down, b_down, g_final):
    depth = g_mix.shape[0]
    assert depth == 1, "one layer supported"
    b, s, d = x_prompt.shape
    nb, ns, _ = x_sample.shape
    assert ns == 1
    n_mem = mem_prompt.shape[1]
    a_heads, a_hd = g_v.shape[1], g_v.shape[2]
    a_width = a_heads * a_hd
    b_width = w_conv.shape[2]
    assert a_heads == A_HEADS and 2 * a_hd == CHUNK and w_spatial.shape[2] == CHUNK
    assert s % MIX_TILE == 0 and s % ATT_TILE == 0 and (b * n_mem) % KV_TILE == 0
    assert nb % SAMPLE_ATT_GROUP == 0
    l = 0
    row = lambda a: a.reshape(1, -1)

    w_in_b, w_out_b, w_q_b, w_k_b, w_v_b, w_xo_b = w_in[l], w_out[l], w_q[l], w_k[l], w_v[l], w_xo[l]
    wrt_b = w_router[l].T.astype(BF16)
    br_col = b_router[l].reshape(N_EXPERTS, 1).astype(F32)
    gv_row = row(g_v[l])
    ws = w_spatial[l]
    ws_pairs = jnp.concatenate([ws[0::2], ws[1::2]], axis=2)
    bs_full = jnp.repeat(b_spatial[l].T, a_hd, axis=1)
    w00 = row(jnp.repeat(ws[:, 0, 0], a_hd))
    b0 = row(jnp.repeat(b_spatial[l][:, 0], a_hd))

    mk2, mv2 = _mem_kv(mem_prompt.reshape(b * n_mem, d), row(g_mem[l]), w_k_b, w_v_b)
    x1_p, conv_p = _mixer_prompt(x_prompt, row(g_mix[l]), w_in_b, gv_row, ws_pairs, bs_full,
                                 w_conv[l], w_out_b)
    x2_p, h3_p, tope_p, gates_p = _attn_prompt(
        x1_p, mk2.reshape(b, n_mem, d), mv2.reshape(b, n_mem, d), row(g_xattn[l]), w_q_b, w_xo_b,
        row(g_moe[l]), wrt_b, br_col)

    xs2 = x_sample.reshape(nb, d)
    st = state_conv[l]
    x1_s, v_s, z_s, q_s = _mixer_sample(xs2, st[:, 0], st[:, 1], row(g_mix[l]), w_in_b, gv_row, w00, b0,
                                        w_conv[l], w_out_b, row(g_xattn[l]), w_q_b)
    x_heads, x_hd = cache_mem_k.shape[3], cache_mem_k.shape[4]
    assert depth == 1 and x_heads == X_HEADS
    o_s = _attn_sample(q_s.reshape(nb, x_heads, x_hd), cache_mem_k.reshape(nb, n_mem, x_heads, x_hd),
                       cache_mem_v.reshape(nb, n_mem, x_heads, x_hd)).reshape(nb, d)
    x2_s, h3_s, tope_s, gates_s = _tail_sample(x1_s, o_s, w_xo_b, row(g_moe[l]), wrt_b, br_col)

    t_p = b * s
    t_all = t_p + nb
    t_pad = -(-t_all // ROUTE_TILE) * ROUTE_TILE
    tope_all = jnp.concatenate([tope_p, tope_s, jnp.full((TOP_K, t_pad - t_all), -1, I32)], axis=1)
    dest, cnt = _route(tope_all)
    counts = cnt[:, 0]
    padded = (counts + MOE_BLOCK - 1) // MOE_BLOCK * MOE_BLOCK
    pad_ends = jnp.cumsum(padded)
    pad_starts = pad_ends - padded
    n_blocks = -(-(t_all * TOP_K) // MOE_BLOCK) + N_EXPERTS
    n_rows = n_blocks * MOE_BLOCK
    assert t_all % SC_CHUNK == 0 and t_p % SC_CHUNK == 0
    dest3 = dest[:, :t_all].reshape(TOP_K, t_all // SC_CHUNK, SC_CHUNK).transpose(1, 0, 2)
    xs_rows = _dispatch(dest3, h3_p.reshape(t_p, d // 2), h3_s, n_rows)
    x_heads = cache_mem_k.shape[3]
    mem_k_out = mk2.reshape(1, b, n_mem, x_heads, d // x_heads)
    mem_v_out = mv2.reshape(1, b, n_mem, x_heads, d // x_heads)
    xs_rows, mem_k_out, mem_v_out = lax.optimization_barrier((xs_rows, mem_k_out, mem_v_out))
    y_rows = _expert_ffn(pad_starts.astype(I32), (padded // MOE_BLOCK).astype(I32), counts.astype(I32),
                         xs_rows, w_gate_up[l], b_gate_up[l], w_down[l], b_down[l])

    t_g = GATHER_TILES * GATHER_TILE
    assert t_g % SC_CHUNK == 0 and t_g % COMBINE_TILE == 0 and (t_p - t_g) % nb == 0
    y4 = _undispatch(dest3[t_g // SC_CHUNK:], y_rows)
    gf = row(g_final)
    x2_p2 = x2_p.reshape(t_p, d)
    y_p = _combine_gather(dest, x2_p2, gates_p, gf, y_rows, GATHER_TILE, GATHER_TILES)
    y_p = _combine(x2_p2, y4, gates_p, gf, COMBINE_TILE, t_g // COMBINE_TILE,
                   (t_p - t_g) // COMBINE_TILE, 0, out_prev=y_p)
    y_s = _combine(x2_s, y4, gates_s, gf, nb, 0, 1, (t_p - t_g) // nb)

    return (y_p.reshape(b, s, d),
            y_s.reshape(nb, 1, d),
            mem_k_out,
            mem_v_out,
            conv_p.reshape(1, b, CONV_W - 1, b_width),
            jnp.stack([st[:, 1], z_s], axis=1).reshape(1, nb, CONV_W - 1, b_width),
            v_s.reshape(1, nb, 1, a_heads, a_hd))
```

```python
import functools

import jax
import jax.numpy as jnp
from jax import lax
from jax.experimental import pallas as pl
from jax.experimental.pallas import tpu as pltpu
from jax.experimental.pallas import tpu_sc as plsc

F32 = jnp.float32
BF16 = jnp.bfloat16
I32 = jnp.int32
U32 = jnp.uint32

A_HEADS = 8
CHUNK = 128
CONV_W = 3
X_HEADS = 4
N_EXPERTS = 32
TOP_K = 4
SWIGLU_LIMIT = 7.0
SWIGLU_ALPHA = 1.702
EPS = 1e-5

SUBLANES = 8
LANES = 128

MIX_TILE = 1024
ATT_TILE = 1024
KV_TILE = 512
ROUTE_TILE = 512
MOE_BLOCK = 256
COMBINE_TILE = 512
GATHER_TILE = 256
GATHER_TILES = 20
SAMPLE_ATT_GROUP = 8
SC_CHUNK = 64
V7X_VMEM_BYTES = 64 * 1024 * 1024
VMEM_LIMIT = V7X_VMEM_BYTES - 8 * 1024 * 1024


def _cparams(sem=None):
    return pltpu.CompilerParams(dimension_semantics=sem, vmem_limit_bytes=VMEM_LIMIT)


def _rms(x, g):
    r = lax.rsqrt(jnp.mean(x * x, axis=-1, keepdims=True) + EPS)
    return (x * r) * g


def _gelu(x):
    return 0.5 * x * (1.0 + lax.erf(x * 0.7071067811865476))


def _dot(a, b):
    return jnp.dot(a, b, preferred_element_type=F32)


def _dot_nt(a, b):
    return lax.dot_general(a, b, (((1,), (1,)), ((), ())), preferred_element_type=F32)


def _pack_bf16_pairs(x):
    d = x.shape[1]
    r = x.astype(BF16).astype(F32)
    lo = lax.shift_right_logical(pltpu.bitcast(r[:, :d // 2], U32), jnp.uint32(16))
    hi = lax.bitwise_and(pltpu.bitcast(r[:, d // 2:], U32), jnp.uint32(0xFFFF0000))
    return lax.bitwise_or(lo, hi)


def _unpack_bf16_pairs(w):
    lo = pltpu.bitcast(lax.shift_left(w, jnp.uint32(16)), F32)
    hi = pltpu.bitcast(lax.bitwise_and(w, jnp.uint32(0xFFFF0000)), F32)
    return lo, hi


def _memkv_kernel(m_ref, g_ref, wk_ref, wv_ref, k_ref, v_ref, wk_b, wv_b):
    @pl.when(pl.program_id(0) == 0)
    def _():
        wk_b[...] = wk_ref[...].astype(BF16)
        wv_b[...] = wv_ref[...].astype(BF16)

    h = _rms(m_ref[...], g_ref[...]).astype(BF16)
    k_ref[...] = _dot(h, wk_b[...])
    v_ref[...] = _dot(h, wv_b[...])


def _resident(shape, n_grid):
    zeros = (0,) * len(shape)
    index_map = (lambda i: zeros) if n_grid == 1 else (lambda i, j: zeros)
    return pl.BlockSpec(shape, index_map, pipeline_mode=pl.Buffered(1))


def _mem_kv(mem2d, g_mem, wk, wv):
    n, d = mem2d.shape
    row = pl.BlockSpec((KV_TILE, d), lambda i: (i, 0))
    return pl.pallas_call(
        _memkv_kernel,
        grid=(n // KV_TILE,),
        in_specs=[row, _resident((1, d), 1), _resident((d, d), 1), _resident((d, d), 1)],
        out_specs=[row, row],
        out_shape=[jax.ShapeDtypeStruct((n, d), F32)] * 2,
        scratch_shapes=[pltpu.VMEM((d, d), BF16), pltpu.VMEM((d, d), BF16)],
        compiler_params=_cparams(("arbitrary",)),
        name="mem_kv",
    )(mem2d, g_mem, wk, wv)


def _head_rms(v, gv, a_width):
    hd = a_width // A_HEADS
    r_i = lax.broadcasted_iota(I32, (a_width, a_width), 0) // hd
    c_i = lax.broadcasted_iota(I32, (a_width, a_width), 1) // hd
    ones_bd = jnp.where(r_i == c_i, 1.0, 0.0).astype(BF16)
    sq = v * v
    sq_hi = sq.astype(BF16)
    sq_lo = (sq - sq_hi.astype(F32)).astype(BF16)
    gs = _dot(sq_hi, ones_bd) + _dot(sq_lo, ones_bd)
    return (v * lax.rsqrt(gs * (1.0 / hd) + EPS)) * gv


def _mixer_prompt_kernel(x_ref, gmix_ref, win_f32, gv_ref, ws_ref, bs_ref, wc_ref, wout_f32,
                         x1_ref, cs_ref, zbuf, win_ref, wout_ref, *, tile, a_width, b_width):
    j = pl.program_id(1)
    nj = pl.num_programs(1)

    @pl.when((pl.program_id(0) == 0) & (j == 0))
    def _():
        win_ref[...] = win_f32[...].astype(BF16)
        wout_ref[...] = wout_f32[...].astype(BF16)

    x = x_ref[...]
    h = _rms(x, gmix_ref[...]).astype(BF16)
    proj = _dot(h, win_ref[...])
    u = _gelu(proj[:, :a_width])
    v = _head_rms(_gelu(proj[:, a_width:2 * a_width]), gv_ref[...], a_width)
    o = 2 * a_width
    hb = proj[:, o:o + b_width]
    bg = proj[:, o + b_width:o + 2 * b_width]
    cg = proj[:, o + 2 * b_width:o + 3 * b_width]

    lane = lax.broadcasted_iota(I32, (CHUNK, 2 * (a_width // A_HEADS)), 1)
    first = lane < (a_width // A_HEADS)
    t_i = lax.broadcasted_iota(I32, (CHUNK, 2 * CHUNK), 0)
    s_i = lax.broadcasted_iota(I32, (CHUNK, 2 * CHUNK), 1) % CHUNK
    causal = s_i <= t_i
    pair_cols = []
    for p in range(A_HEADS // 2):
        w_pair = jnp.where(causal, ws_ref[p], 0.0).astype(BF16)
        vp = v[:, p * CHUNK:(p + 1) * CHUNK]
        rows = []
        for c in range(tile // CHUNK):
            vc = vp[c * CHUNK:(c + 1) * CHUNK]
            rhs = jnp.concatenate([jnp.where(first, vc, 0.0), jnp.where(first, 0.0, vc)],
                                  axis=0).astype(BF16)
            rows.append(_dot(w_pair, rhs))
        pair_cols.append(jnp.concatenate(rows, axis=0))
    gate = jnp.concatenate(pair_cols, axis=1)
    bias = jnp.concatenate([bs_ref[...]] * (tile // CHUNK), axis=0)
    a_out = u * (gate + bias)

    head = SUBLANES

    @pl.when(j == 0)
    def _():
        zbuf[0:head, :] = jnp.zeros((head, b_width), F32)

    z = cg * hb
    zbuf[head:tile + head, :] = z
    z1 = zbuf[head - 1:tile + head - 1, :]
    z2 = zbuf[head - 2:tile + head - 2, :]
    wc = wc_ref[...]
    conv = z2 * wc[0:1] + z1 * wc[1:2] + z * wc[2:3]
    b_out = bg * conv
    tail = zbuf[tile:tile + head, :]
    zbuf[0:head, :] = tail

    @pl.when(j == nj - 1)
    def _():
        cs_ref[...] = tail[head - (CONV_W - 1):, :]

    y = _dot(a_out.astype(BF16), wout_ref[0:a_width, :]) + _dot(b_out.astype(BF16), wout_ref[a_width:, :])
    x1_ref[...] = x + y


def _mixer_prompt(x, g_mix, w_in, g_v, ws_pairs, bs_full, w_conv, w_out):
    b, s, d = x.shape
    a_width = g_v.shape[1]
    b_width = w_conv.shape[1]
    in_width = w_in.shape[1]
    tile = MIX_TILE
    full = lambda shape: _resident(shape, 2)
    kern = functools.partial(_mixer_prompt_kernel, tile=tile, a_width=a_width, b_width=b_width)
    return pl.pallas_call(
        kern,
        grid=(b, s // tile),
        in_specs=[
            pl.BlockSpec((None, tile, d), lambda i, j: (i, j, 0)),
            full((1, d)), full((d, in_width)), full((1, a_width)),
            full(ws_pairs.shape), full(bs_full.shape), full(w_conv.shape), full(w_out.shape),
        ],
        out_specs=[
            pl.BlockSpec((None, tile, d), lambda i, j: (i, j, 0)),
            pl.BlockSpec((None, CONV_W - 1, b_width), lambda i, j: (i, 0, 0)),
        ],
        out_shape=[jax.ShapeDtypeStruct((b, s, d), F32),
                   jax.ShapeDtypeStruct((b, CONV_W - 1, b_width), F32)],
        scratch_shapes=[pltpu.VMEM((tile + SUBLANES, b_width), F32), pltpu.VMEM((d, in_width), BF16),
                        pltpu.VMEM(w_out.shape, BF16)],
        compiler_params=_cparams(("arbitrary", "arbitrary")),
        name="mixer_prompt",
    )(x, g_mix, w_in, g_v, ws_pairs, bs_full, w_conv, w_out)


def _mixer_sample_kernel(x_ref, s0_ref, s1_ref, gmix_ref, win_ref, gv_ref, w00_ref, b0_ref, wc_ref,
                         wout_ref, gx_ref, wq_ref, x1_ref, v_ref, z_ref, q_ref, *, a_width, b_width):
    x = x_ref[...]
    h = _rms(x, gmix_ref[...]).astype(BF16)
    proj = _dot(h, win_ref[...].astype(BF16))
    u = _gelu(proj[:, :a_width])
    v = _head_rms(_gelu(proj[:, a_width:2 * a_width]), gv_ref[...], a_width)
    v_ref[...] = v
    o = 2 * a_width
    hb = proj[:, o:o + b_width]
    bg = proj[:, o + b_width:o + 2 * b_width]
    cg = proj[:, o + 2 * b_width:o + 3 * b_width]
    a_out = u * (v * w00_ref[...] + b0_ref[...])
    z = cg * hb
    z_ref[...] = z
    wc = wc_ref[...]
    conv = s0_ref[...] * wc[0:1] + s1_ref[...] * wc[1:2] + z * wc[2:3]
    b_out = bg * conv
    y = (_dot(a_out.astype(BF16), wout_ref[0:a_width, :].astype(BF16))
         + _dot(b_out.astype(BF16), wout_ref[a_width:, :].astype(BF16)))
    x1 = x + y
    x1_ref[...] = x1
    q_ref[...] = _dot(_rms(x1, gx_ref[...]).astype(BF16), wq_ref[...].astype(BF16))


def _mixer_sample(x, s0, s1, g_mix, w_in, g_v, w00, b0, w_conv, w_out, g_x, w_q):
    n, d = x.shape
    a_width = g_v.shape[1]
    b_width = w_conv.shape[1]
    kern = functools.partial(_mixer_sample_kernel, a_width=a_width, b_width=b_width)
    return pl.pallas_call(
        kern,
        out_shape=[jax.ShapeDtypeStruct((n, d), F32), jax.ShapeDtypeStruct((n, a_width), F32),
                   jax.ShapeDtypeStruct((n, b_width), F32), jax.ShapeDtypeStruct((n, d), F32)],
        compiler_params=_cparams(),
        name="mixer_sample",
    )(x, s0, s1, g_mix, w_in, g_v, w00, b0, w_conv, w_out, g_x, w_q)


def _router_tail(x2, gmoe, wrt, br, h3_ref, tope_ref, gates_ref):
    h3f = _rms(x2, gmoe)
    h3 = h3f.astype(BF16)
    h3_ref[...] = _pack_bf16_pairs(h3f)

    logits = _dot_nt(wrt, h3) + br
    n_e, n_t = logits.shape
    e_idx = lax.broadcasted_iota(I32, (n_e, n_t), 0).astype(F32)
    tops, idxs = [], []
    for _ in range(TOP_K):
        m = jnp.max(logits, axis=0, keepdims=True)
        idx = jnp.min(jnp.where(logits == m, e_idx, float(n_e)), axis=0, keepdims=True)
        tops.append(m)
        idxs.append(idx)
        logits = jnp.where(e_idx == idx, -jnp.inf, logits)
    top = jnp.concatenate(tops, axis=0)
    ex = jnp.exp(top - top[0:1])
    gates_ref[...] = ex / jnp.sum(ex, axis=0, keepdims=True)
    tope_ref[...] = jnp.concatenate(idxs, axis=0).astype(I32)


def _attn_prompt_kernel(x1_ref, k_ref, v_ref, gx_ref, wq_f32, wxo_f32, gmoe_ref, wrt_ref, br_ref,
                        x2_ref, h3_ref, tope_ref, gates_ref, wq_ref, wxo_ref):
    @pl.when((pl.program_id(0) == 0) & (pl.program_id(1) == 0))
    def _():
        wq_ref[...] = wq_f32[...].astype(BF16)
        wxo_ref[...] = wxo_f32[...].astype(BF16)

    x = x1_ref[...]
    d = x.shape[1]
    hd = d // X_HEADS
    q = _dot(_rms(x, gx_ref[...]).astype(BF16), wq_ref[...]).astype(BF16)
    kb = k_ref[...].astype(BF16)
    vb = v_ref[...].astype(BF16)
    outs = []
    for hh in range(X_HEADS):
        sl = slice(hh * hd, (hh + 1) * hd)
        s = _dot_nt(q[:, sl], kb[:, sl]) * (hd ** -0.5)
        e = jnp.exp(s - jnp.max(s, axis=-1, keepdims=True))
        p = e / jnp.sum(e, axis=-1, keepdims=True)
        outs.append(_dot(p.astype(BF16), vb[:, sl]))
    o = jnp.concatenate(outs, axis=1).astype(BF16)
    x2 = x + _dot(o, wxo_ref[...])
    x2_ref[...] = x2
    _router_tail(x2, gmoe_ref[...], wrt_ref[...], br_ref[...], h3_ref, tope_ref, gates_ref)


def _attn_prompt(x1, mk, mv, g_x, w_q, w_xo, g_moe, wrt, br):
    b, s, d = x1.shape
    n_mem = mk.shape[1]
    tile = ATT_TILE
    nq = s // tile
    full = lambda shape: _resident(shape, 2)
    tok = pl.BlockSpec((None, tile, d), lambda i, j: (i, j, 0))
    mem = pl.BlockSpec((None, n_mem, d), lambda i, j: (i, 0, 0))
    lanes = pl.BlockSpec((TOP_K, tile), lambda i, j: (0, i * nq + j))
    return pl.pallas_call(
        _attn_prompt_kernel,
        grid=(b, nq),
        in_specs=[tok, mem, mem, full((1, d)), full((d, d)), full((d, d)), full((1, d)),
                  full(wrt.shape), full(br.shape)],
        out_specs=[tok, pl.BlockSpec((None, tile, d // 2), lambda i, j: (i, j, 0)), lanes, lanes],
        out_shape=[jax.ShapeDtypeStruct((b, s, d), F32), jax.ShapeDtypeStruct((b, s, d // 2), U32),
                   jax.ShapeDtypeStruct((TOP_K, b * s), I32), jax.ShapeDtypeStruct((TOP_K, b * s), F32)],
        scratch_shapes=[pltpu.VMEM((d, d), BF16), pltpu.VMEM((d, d), BF16)],
        compiler_params=_cparams(("arbitrary", "arbitrary")),
        name="attn_prompt",
    )(x1, mk, mv, g_x, w_q, w_xo, g_moe, wrt, br)


def _attn_sample_kernel(q_ref, k_ref, v_ref, o_ref, *, group, heads, scale):
    for g in range(group):
        q = q_ref[g] * scale
        k = k_ref[g]
        part = jnp.broadcast_to(jnp.sum(k * q[None], axis=-1, keepdims=True), k.shape)
        s = part + pltpu.roll(part, heads, axis=1)
        e = jnp.exp(s - jnp.max(s, axis=0, keepdims=True))
        o_ref[g] = jnp.sum(e * v_ref[g], axis=0) / jnp.sum(e, axis=0)


def _to_rows8(a, heads, hd):
    lead = a.shape[:-2]
    n = len(lead)
    a = a.reshape(*lead, heads, hd // LANES, LANES)
    a = a.transpose(*range(n), n + 1, n, n + 2)
    return a.reshape(*lead, (hd // LANES) * heads, LANES)


def _from_rows8(a, heads, hd):
    lead = a.shape[:-2]
    n = len(lead)
    a = a.reshape(*lead, hd // LANES, heads, LANES)
    a = a.transpose(*range(n), n + 1, n, n + 2)
    return a.reshape(*lead, heads, hd)


def _attn_sample(q, ck, cv):
    n, heads, hd = q.shape
    assert hd == 2 * LANES and 2 * heads == SUBLANES
    n_mem = ck.shape[1]
    group = SAMPLE_ATT_GROUP
    rows = (hd // LANES) * heads
    qo = pl.BlockSpec((group, rows, LANES), lambda i: (i, 0, 0))
    kv = pl.BlockSpec((group, n_mem, rows, LANES), lambda i: (i, 0, 0, 0))
    o = pl.pallas_call(
        functools.partial(_attn_sample_kernel, group=group, heads=heads, scale=hd ** -0.5),
        grid=(n // group,),
        in_specs=[qo, kv, kv],
        out_specs=qo,
        out_shape=jax.ShapeDtypeStruct((n, rows, LANES), F32),
        compiler_params=_cparams(("arbitrary",)),
        name="attn_sample",
    )(_to_rows8(q, heads, hd), _to_rows8(ck, heads, hd), _to_rows8(cv, heads, hd))
    return _from_rows8(o, heads, hd)


def _tail_sample_kernel(x1_ref, o_ref, wxo_ref, gmoe_ref, wrt_ref, br_ref,
                        x2_ref, h3_ref, tope_ref, gates_ref):
    x2 = x1_ref[...] + _dot(o_ref[...].astype(BF16), wxo_ref[...].astype(BF16))
    x2_ref[...] = x2
    _router_tail(x2, gmoe_ref[...], wrt_ref[...], br_ref[...], h3_ref, tope_ref, gates_ref)


def _tail_sample(x1, o, w_xo, g_moe, wrt, br):
    n, d = x1.shape
    return pl.pallas_call(
        _tail_sample_kernel,
        out_shape=[jax.ShapeDtypeStruct((n, d), F32), jax.ShapeDtypeStruct((n, d // 2), U32),
                   jax.ShapeDtypeStruct((TOP_K, n), I32), jax.ShapeDtypeStruct((TOP_K, n), F32)],
        compiler_params=_cparams(),
        name="tail_sample",
    )(x1, o, w_xo, g_moe, wrt, br)


def _route_kernel(tope_ref, dest_ref, cnt_ref, *, n_blocks, tile, moe_block):
    e_idx = lax.broadcasted_iota(I32, (N_EXPERTS, tile), 0)
    earlier = jnp.where(lax.broadcasted_iota(I32, (tile, tile), 0)
                        < lax.broadcasted_iota(I32, (tile, tile), 1), 1.0, 0.0).astype(BF16)

    def onehot(k, off):
        return e_idx == tope_ref[pl.ds(k, 1), pl.ds(off, tile)]

    def count_body(j, cnt):
        off = pl.multiple_of(j * tile, tile)
        for k in range(TOP_K):
            cnt = cnt + jnp.sum(jnp.where(onehot(k, off), 1.0, 0.0), axis=1, keepdims=True)
        return cnt

    counts = lax.fori_loop(0, n_blocks, count_body, jnp.zeros((N_EXPERTS, 1), F32))
    cnt_ref[...] = jnp.broadcast_to(counts, cnt_ref.shape).astype(I32)

    n_blk = jnp.floor((counts + (moe_block - 1)) * (1.0 / moe_block))
    n_hi = jnp.floor(n_blk * (1.0 / 16.0))
    n_lo = n_blk - 16.0 * n_hi
    below = jnp.where(lax.broadcasted_iota(I32, (N_EXPERTS, N_EXPERTS), 1)
                      < lax.broadcasted_iota(I32, (N_EXPERTS, N_EXPERTS), 0), 1.0, 0.0).astype(BF16)
    wide = lambda c: jnp.broadcast_to(c, (N_EXPERTS, LANES)).astype(BF16)
    start = (16.0 * _dot(below, wide(n_hi)) + _dot(below, wide(n_lo)))[:, 0:1] * float(moe_block)

    def dest_body(j, run):
        off = pl.multiple_of(j * tile, tile)
        for k in range(TOP_K):
            oh = onehot(k, off)
            ohf = jnp.where(oh, 1.0, 0.0)
            before = _dot(ohf.astype(BF16), earlier)
            dest = jnp.sum(jnp.where(oh, before + run, 0.0), axis=0, keepdims=True)
            dest_ref[pl.ds(k, 1), pl.ds(off, tile)] = dest.astype(I32)
            run = run + jnp.sum(ohf, axis=1, keepdims=True)
        return run

    lax.fori_loop(0, n_blocks, dest_body, start)


def _route(tope_pad):
    t_pad = tope_pad.shape[1]
    kern = functools.partial(_route_kernel, n_blocks=t_pad // ROUTE_TILE, tile=ROUTE_TILE,
                             moe_block=MOE_BLOCK)
    return pl.pallas_call(
        kern,
        out_shape=[jax.ShapeDtypeStruct((TOP_K, t_pad), I32),
                   jax.ShapeDtypeStruct((N_EXPERTS, LANES), I32)],
        compiler_params=_cparams(),
        name="route",
    )(tope_pad)


def _dispatch(dest3, h_prompt, h_sample, n_rows):
    w = h_prompt.shape[1]
    n_prompt_chunks = h_prompt.shape[0] // SC_CHUNK
    n_chunks = dest3.shape[0]
    mesh = plsc.VectorSubcoreMesh(core_axis_name="c", subcore_axis_name="s")
    n_workers = mesh.num_cores * mesh.num_subcores

    n_steps = -(-n_chunks // n_workers)

    def body(dest_hbm, hp_hbm, hs_hbm, xs_hbm, idx0, idx1, rows0, rows1, sem):
        wid = lax.axis_index("s") * mesh.num_cores + lax.axis_index("c")
        idx_v, rows_v = (idx0, idx1), (rows0, rows1)

        def chunk(j):
            return wid + j * n_workers

        def loads(j, slot):
            c = chunk(j)
            prompt = (hp_hbm, c)
            sample = (hs_hbm, c - n_prompt_chunks)
            return c, [
                (pltpu.make_async_copy(dest_hbm.at[c], idx_v[slot], sem.at[slot]), None),
                (pltpu.make_async_copy(prompt[0].at[pl.ds(prompt[1] * SC_CHUNK, SC_CHUNK)], rows_v[slot],
                                       sem.at[slot]), c < n_prompt_chunks),
                (pltpu.make_async_copy(sample[0].at[pl.ds(sample[1] * SC_CHUNK, SC_CHUNK)], rows_v[slot],
                                       sem.at[slot]), c >= n_prompt_chunks),
            ]

        def start_loads(j, slot):
            c, copies = loads(j, slot)
            for cp, cond in copies:
                pl.when((c < n_chunks) if cond is None else ((c < n_chunks) & cond))(cp.start)

        def wait_loads(j, slot):
            c, copies = loads(j, slot)
            for cp, cond in copies:
                pl.when((c < n_chunks) if cond is None else ((c < n_chunks) & cond))(cp.wait)

        def scatters(slot):
            return [pltpu.make_async_copy(rows_v[slot], xs_hbm.at[idx_v[slot].at[k]], sem.at[2 + slot])
                    for k in range(TOP_K)]

        start_loads(0, 0)
        for j in range(n_steps):
            slot = j % 2
            wait_loads(j, slot)
            if j >= 1:
                @pl.when(chunk(j - 1) < n_chunks)
                def _():
                    for cp in scatters(1 - slot):
                        cp.wait()
            if j + 1 < n_steps:
                start_loads(j + 1, 1 - slot)

            @pl.when(chunk(j) < n_chunks)
            def _():
                for cp in scatters(slot):
                    cp.start()

        @pl.when(chunk(n_steps - 1) < n_chunks)
        def _():
            for cp in scatters((n_steps - 1) % 2):
                cp.wait()

    return pl.kernel(
        body,
        out_type=jax.ShapeDtypeStruct((n_rows, w), U32),
        mesh=mesh,
        scratch_types=[pltpu.VMEM((TOP_K, SC_CHUNK), I32), pltpu.VMEM((TOP_K, SC_CHUNK), I32),
                       pltpu.VMEM((SC_CHUNK, w), U32), pltpu.VMEM((SC_CHUNK, w), U32),
                       pltpu.SemaphoreType.DMA((4,))],
        name="dispatch_sc",
    )(dest3, h_prompt, h_sample)


def _ffn_kernel(start_ref, nblk_ref, count_ref, b1_ref, b2_ref, xs_hbm, w1_hbm, w2_hbm, y_hbm,
                w1f, w2f, w1b, w2b, xbuf, ybuf, sem_w, sem_x, sem_y, *, blk):
    e = pl.program_id(0)
    n_e = pl.num_programs(0)
    nb = nblk_ref[e]
    base = start_ref[e]
    wslot = e % 2

    def w_copies(expert, slot):
        return (pltpu.make_async_copy(w1_hbm.at[expert], w1f.at[slot], sem_w.at[slot]),
                pltpu.make_async_copy(w2_hbm.at[expert], w2f.at[slot], sem_w.at[slot]))

    def x_copy(row0, slot):
        return pltpu.make_async_copy(xs_hbm.at[pl.ds(pl.multiple_of(row0, blk), blk)], xbuf.at[slot],
                                     sem_x.at[slot])

    def y_copy(row0, slot):
        return pltpu.make_async_copy(ybuf.at[slot], y_hbm.at[pl.ds(pl.multiple_of(row0, blk), blk)],
                                     sem_y.at[slot])

    @pl.when(e == 0)
    def _():
        for cp in w_copies(0, 0):
            cp.start()

        @pl.when(nb > 0)
        def _():
            x_copy(base, 0).start()

    for cp in w_copies(e, wslot):
        cp.wait()

    @pl.when(e + 1 < n_e)
    def _():
        for cp in w_copies(e + 1, 1 - wslot):
            cp.start(priority=1)

    w1b[...] = w1f[wslot].astype(BF16)
    w2b[...] = w2f[wslot].astype(BF16)
    b1 = b1_ref[e]
    b2 = b2_ref[e]

    def block(j, slot):
        x_copy(base, slot).wait()

        @pl.when(j + 1 < nb)
        def _():
            x_copy(base + (j + 1) * blk, 1 - slot).start()

        @pl.when(j >= 2)
        def _():
            y_copy(base, slot).wait()

        w = xbuf[slot]
        row = lax.broadcasted_iota(I32, w.shape, 0)
        w = jnp.where(row < count_ref[e] - j * blk, w, jnp.uint32(0))
        half = w.shape[1]
        d_e = w2b.shape[0]
        lo, hi = _unpack_bf16_pairs(w)
        gu = _dot(lo.astype(BF16), w1b[0:half, :]) + _dot(hi.astype(BF16), w1b[half:, :]) + b1
        gate = jnp.minimum(gu[:, :d_e], SWIGLU_LIMIT)
        up = jnp.clip(gu[:, d_e:], -SWIGLU_LIMIT, SWIGLU_LIMIT)
        glu = gate * jax.nn.sigmoid(gate * SWIGLU_ALPHA)
        act = ((up + 1.0) * glu).astype(BF16)
        ybuf[slot] = _pack_bf16_pairs(_dot(act, w2b[...]) + b2)
        y_copy(base + j * blk, slot).start()

    def pair(j2, carry):
        j = 2 * j2
        block(j, 0)

        @pl.when(j + 1 < nb)
        def _():
            block(j + 1, 1)

        return carry

    lax.fori_loop(0, (nb + 1) // 2, pair, 0)

    @pl.when(nb >= 2)
    def _():
        y_copy(base, 0).wait()
        y_copy(base, 1).wait()

    @pl.when(nb == 1)
    def _():
        y_copy(base, 0).wait()

    nxt = jnp.minimum(e + 1, n_e - 1)

    @pl.when((e + 1 < n_e) & (nblk_ref[nxt] > 0))
    def _():
        x_copy(start_ref[nxt], 0).start()


def _expert_ffn(grp_start, grp_blocks, grp_count, xs, w1, b1, w2, b2):
    n_rows, half = xs.shape
    n_e, d, d2 = w1.shape
    d_e = w2.shape[1]
    blk = MOE_BLOCK
    whole = lambda shape: pl.BlockSpec(shape, lambda e, s, n, c: (0,) * len(shape))
    hbm = pl.BlockSpec(memory_space=pl.ANY)
    grid_spec = pltpu.PrefetchScalarGridSpec(
        num_scalar_prefetch=3,
        grid=(n_e,),
        in_specs=[whole((n_e, 1, d2)), whole((n_e, 1, d)), hbm, hbm, hbm],
        out_specs=hbm,
        scratch_shapes=[pltpu.VMEM((2, d, d2), F32), pltpu.VMEM((2, d_e, d), F32),
                        pltpu.VMEM((d, d2), BF16), pltpu.VMEM((d_e, d), BF16),
                        pltpu.VMEM((2, blk, half), U32), pltpu.VMEM((2, blk, d // 2), U32),
                        pltpu.SemaphoreType.DMA((2,)), pltpu.SemaphoreType.DMA((2,)),
                        pltpu.SemaphoreType.DMA((2,))],
    )
    return pl.pallas_call(
        functools.partial(_ffn_kernel, blk=blk),
        grid_spec=grid_spec,
        out_shape=jax.ShapeDtypeStruct((n_rows, d // 2), U32),
        compiler_params=_cparams(("arbitrary",)),
        name="expert_ffn",
    )(grp_start, grp_blocks, grp_count, b1.reshape(n_e, 1, d2), b2.reshape(n_e, 1, d), xs, w1, w2)


def _undispatch(dest3, y_rows):
    n_chunks = dest3.shape[0]
    w = y_rows.shape[1]
    mesh = plsc.VectorSubcoreMesh(core_axis_name="c", subcore_axis_name="s")
    n_workers = mesh.num_cores * mesh.num_subcores
    n_steps = -(-n_chunks // n_workers)

    def body(dest_hbm, y_hbm, out_hbm, idx0, idx1, rows0, rows1, sem):
        wid = lax.axis_index("s") * mesh.num_cores + lax.axis_index("c")
        idx_v, rows_v = (idx0, idx1), (rows0, rows1)

        def chunk(j):
            return wid + j * n_workers

        def gather(j, k, slot):
            return pltpu.make_async_copy(y_hbm.at[idx_v[j % 2].at[k]], rows_v[slot], sem.at[slot])

        def store(j, k, slot):
            return pltpu.make_async_copy(rows_v[slot], out_hbm.at[k, pl.ds(chunk(j) * SC_CHUNK, SC_CHUNK)],
                                         sem.at[2 + slot])

        units = [(j, k) for j in range(n_steps) for k in range(TOP_K)]

        def valid(j):
            return chunk(j) < n_chunks

        def load_idx(j):
            pl.when(valid(j))(lambda: pltpu.sync_copy(dest_hbm.at[chunk(j)], idx_v[j % 2]))

        load_idx(0)
        pl.when(valid(0))(gather(0, 0, 0).start)
        for u, (j, k) in enumerate(units):
            slot = u % 2
            pl.when(valid(j))(gather(j, k, slot).wait)
            if u + 1 < len(units):
                jn, kn = units[u + 1]
                if u >= 1:
                    jp, kp = units[u - 1]
                    pl.when(valid(jp))(store(jp, kp, 1 - slot).wait)
                if kn == 0:
                    load_idx(jn)
                pl.when(valid(jn))(gather(jn, kn, 1 - slot).start)
            pl.when(valid(j))(store(j, k, slot).start)
        for u in (len(units) - 2, len(units) - 1):
            j, k = units[u]
            pl.when(valid(j))(store(j, k, u % 2).wait)

    return pl.kernel(
        body,
        out_type=jax.ShapeDtypeStruct((TOP_K, n_chunks * SC_CHUNK, w), y_rows.dtype),
        mesh=mesh,
        scratch_types=[pltpu.VMEM((TOP_K, SC_CHUNK), I32), pltpu.VMEM((TOP_K, SC_CHUNK), I32),
                       pltpu.VMEM((SC_CHUNK, w), y_rows.dtype), pltpu.VMEM((SC_CHUNK, w), y_rows.dtype),
                       pltpu.SemaphoreType.DMA((4,))],
        name="undispatch_sc",
    )(dest3, y_rows)


def _gated_sum_norm(x2, y4, gates, g_final, tile):
    g = jnp.concatenate([gates, jnp.zeros((LANES - TOP_K, tile), F32)], axis=0)
    gt = g.T
    half = y4.shape[2]
    moe_lo = jnp.zeros((tile, half), F32)
    moe_hi = jnp.zeros((tile, half), F32)
    for k in range(TOP_K):
        lo, hi = _unpack_bf16_pairs(y4[k])
        moe_lo = moe_lo + gt[:, k:k + 1] * lo
        moe_hi = moe_hi + gt[:, k:k + 1] * hi
    return _rms(x2 + jnp.concatenate([moe_lo, moe_hi], axis=1), g_final)


def _combine_kernel(x2_ref, y4_ref, gates_ref, gf_ref, *out_refs, tile):
    out_ref = out_refs[-1]
    out_ref[...] = _gated_sum_norm(x2_ref[...], y4_ref[...], gates_ref[...], gf_ref[...], tile)


def _combine(x2, y4, gates, g_final, tile, tok_tile0, n_tiles, y_tile0, out_prev=None):
    n, d = x2.shape
    in_specs = [
        pl.BlockSpec((tile, d), lambda i: (i + tok_tile0, 0)),
        pl.BlockSpec((TOP_K, tile, d // 2), lambda i: (0, i + y_tile0, 0)),
        pl.BlockSpec((TOP_K, tile), lambda i: (0, i + tok_tile0)),
        pl.BlockSpec((1, d), lambda i: (0, 0)),
    ]
    args = [x2, y4, gates, g_final]
    aliases = {}
    if out_prev is not None:
        in_specs.append(pl.BlockSpec(memory_space=pl.ANY))
        args.append(out_prev)
        aliases = {len(args) - 1: 0}
    return pl.pallas_call(
        functools.partial(_combine_kernel, tile=tile),
        grid=(n_tiles,),
        in_specs=in_specs,
        out_specs=pl.BlockSpec((tile, d), lambda i: (i + tok_tile0, 0)),
        out_shape=jax.ShapeDtypeStruct((n, d), F32),
        input_output_aliases=aliases,
        compiler_params=_cparams(("arbitrary",)),
        name="combine",
    )(*args)


def _combine_gather_kernel(dest_ref, next_ref, x2_ref, gates_ref, gf_ref, y_hbm, out_ref, buf, sem, *,
                           tile):
    i = pl.program_id(0)
    n = pl.num_programs(0)

    def row_copy(d, slot, k, t):
        return pltpu.make_async_copy(y_hbm.at[pl.ds(d, 1)], buf.at[slot, k, pl.ds(t, 1)], sem.at[slot])

    def issue(rows_ref, slot):
        for t in range(tile):
            for k in range(TOP_K):
                row_copy(rows_ref[k, t], slot, k, t).start(priority=(t * TOP_K + k) % 2)

    def wait(slot):
        for t in range(tile):
            for k in range(TOP_K):
                row_copy(0, slot, k, t).wait()

    @pl.when(i == 0)
    def _():
        issue(dest_ref, 0)

    for slot in range(2):
        @pl.when(i % 2 == slot)
        def _():
            @pl.when(i + 1 < n)
            def _():
                issue(next_ref, 1 - slot)

            wait(slot)
            out_ref[...] = _gated_sum_norm(x2_ref[...], buf[slot], gates_ref[...], gf_ref[...], tile)


def _combine_gather(dest, x2, gates, g_final, y_rows, tile, n_tiles):
    n, d = x2.shape
    rows = lambda nxt: pl.BlockSpec((TOP_K, tile), lambda i: (0, jnp.minimum(i + nxt, n_tiles - 1)),
                                    memory_space=pltpu.SMEM)
    grid_spec = pltpu.PrefetchScalarGridSpec(
        num_scalar_prefetch=0,
        grid=(n_tiles,),
        in_specs=[
            rows(0), rows(1),
            pl.BlockSpec((tile, d), lambda i: (i, 0)),
            pl.BlockSpec((TOP_K, tile), lambda i: (0, i)),
            pl.BlockSpec((1, d), lambda i: (0, 0)),
            pl.BlockSpec(memory_space=pl.ANY),
        ],
        out_specs=pl.BlockSpec((tile, d), lambda i: (i, 0)),
        scratch_shapes=[pltpu.VMEM((2, TOP_K, tile, d // 2), y_rows.dtype),
                        pltpu.SemaphoreType.DMA((2,))],
    )
    return pl.pallas_call(
        functools.partial(_combine_gather_kernel, tile=tile),
        grid_spec=grid_spec,
        out_shape=jax.ShapeDtypeStruct((n, d), F32),
        compiler_params=_cparams(("arbitrary",)),
        name="combine_gather",
    )(dest, dest, x2, gates, g_final, y_rows)


def kernel(x_prompt, x_sample, mem_prompt, cache_mem_k, cache_mem_v, state_conv, g_mix, w_in, g_v,
           w_spatial, b_spatial, w_conv, w_out, g_xattn, g_mem, w_q, w_k, w_v, w_xo, g_moe, w_router,
           b_router, w_gate_up, b_gate_up, w_down, b_down, g_final):
    depth = g_mix.shape[0]
    assert depth == 1, "one layer supported"
    b, s, d = x_prompt.shape
    nb, ns, _ = x_sample.shape
    assert ns == 1
    n_mem = mem_prompt.shape[1]
    a_heads, a_hd = g_v.shape[1], g_v.shape[2]
    a_width = a_heads * a_hd
    b_width = w_conv.shape[2]
    assert a_heads == A_HEADS and 2 * a_hd == CHUNK and w_spatial.shape[2] == CHUNK
    assert s % MIX_TILE == 0 and s % ATT_TILE == 0 and (b * n_mem) % KV_TILE == 0
    assert nb % SAMPLE_ATT_GROUP == 0
    l = 0
    row = lambda a: a.reshape(1, -1)

    w_in_b, w_out_b, w_q_b, w_k_b, w_v_b, w_xo_b = w_in[l], w_out[l], w_q[l], w_k[l], w_v[l], w_xo[l]
    wrt_b = w_router[l].T.astype(BF16)
    br_col = b_router[l].reshape(N_EXPERTS, 1).astype(F32)
    gv_row = row(g_v[l])
    ws = w_spatial[l]
    ws_pairs = jnp.concatenate([ws[0::2], ws[1::2]], axis=2)
    bs_full = jnp.repeat(b_spatial[l].T, a_hd, axis=1)
    w00 = row(jnp.repeat(ws[:, 0, 0], a_hd))
    b0 = row(jnp.repeat(b_spatial[l][:, 0], a_hd))

    mk2, mv2 = _mem_kv(mem_prompt.reshape(b * n_mem, d), row(g_mem[l]), w_k_b, w_v_b)
    x1_p, conv_p = _mixer_prompt(x_prompt, row(g_mix[l]), w_in_b, gv_row, ws_pairs, bs_full,
                                 w_conv[l], w_out_b)
    x2_p, h3_p, tope_p, gates_p = _attn_prompt(
        x1_p, mk2.reshape(b, n_mem, d), mv2.reshape(b, n_mem, d), row(g_xattn[l]), w_q_b, w_xo_b,
        row(g_moe[l]), wrt_b, br_col)

    xs2 = x_sample.reshape(nb, d)
    st = state_conv[l]
    x1_s, v_s, z_s, q_s = _mixer_sample(xs2, st[:, 0], st[:, 1], row(g_mix[l]), w_in_b, gv_row, w00, b0,
                                        w_conv[l], w_out_b, row(g_xattn[l]), w_q_b)
    x_heads, x_hd = cache_mem_k.shape[3], cache_mem_k.shape[4]
    assert x_heads == X_HEADS
    o_s = _attn_sample(q_s.reshape(nb, x_heads, x_hd), cache_mem_k.reshape(nb, n_mem, x_heads, x_hd),
                       cache_mem_v.reshape(nb, n_mem, x_heads, x_hd)).reshape(nb, d)
    x2_s, h3_s, tope_s, gates_s = _tail_sample(x1_s, o_s, w_xo_b, row(g_moe[l]), wrt_b, br_col)

    t_p = b * s
    t_all = t_p + nb
    t_pad = -(-t_all // ROUTE_TILE) * ROUTE_TILE
    tope_all = jnp.concatenate([tope_p, tope_s, jnp.full((TOP_K, t_pad - t_all), -1, I32)], axis=1)
    dest, cnt = _route(tope_all)
    counts = cnt[:, 0]
    padded = (counts + MOE_BLOCK - 1) // MOE_BLOCK * MOE_BLOCK
    pad_ends = jnp.cumsum(padded)
    pad_starts = pad_ends - padded
    n_blocks = -(-(t_all * TOP_K) // MOE_BLOCK) + N_EXPERTS
    n_rows = n_blocks * MOE_BLOCK
    assert t_all % SC_CHUNK == 0 and t_p % SC_CHUNK == 0
    dest3 = dest[:, :t_all].reshape(TOP_K, t_all // SC_CHUNK, SC_CHUNK).transpose(1, 0, 2)
    xs_rows = _dispatch(dest3, h3_p.reshape(t_p, d // 2), h3_s, n_rows)
    x_heads = cache_mem_k.shape[3]
    mem_k_out = mk2.reshape(1, b, n_mem, x_heads, d // x_heads)
    mem_v_out = mv2.reshape(1, b, n_mem, x_heads, d // x_heads)
    xs_rows, mem_k_out, mem_v_out = lax.optimization_barrier((xs_rows, mem_k_out, mem_v_out))
    y_rows = _expert_ffn(pad_starts.astype(I32), (padded // MOE_BLOCK).astype(I32), counts.astype(I32),
                         xs_rows, w_gate_up[l], b_gate_up[l], w_down[l], b_down[l])

    t_g = GATHER_TILES * GATHER_TILE
    assert t_g % SC_CHUNK == 0 and t_g % COMBINE_TILE == 0 and (t_p - t_g) % nb == 0
    y4 = _undispatch(dest3[t_g // SC_CHUNK:], y_rows)
    gf = row(g_final)
    x2_p2 = x2_p.reshape(t_p, d)
    y_p = _combine_gather(dest, x2_p2, gates_p, gf, y_rows, GATHER_TILE, GATHER_TILES)
    y_p = _combine(x2_p2, y4, gates_p, gf, COMBINE_TILE, t_g // COMBINE_TILE,
                   (t_p - t_g) // COMBINE_TILE, 0, out_prev=y_p)
    y_s = _combine(x2_s, y4, gates_s, gf, nb, 0, 1, (t_p - t_g) // nb)

    return (y_p.reshape(b, s, d),
            y_s.reshape(nb, 1, d),
            mem_k_out,
            mem_v_out,
            conv_p.reshape(1, b, CONV_W - 1, b_width),
            jnp.stack([st[:, 1], z_s], axis=1).reshape(1, nb, CONV_W - 1, b_width),
            v_s.reshape(1, nb, 1, a_heads, a_hd))
```

```python
import functools

import jax
import jax.numpy as jnp
from jax import lax
from jax.experimental import pallas as pl
from jax.experimental.pallas import tpu as pltpu
from jax.experimental.pallas import tpu_sc as plsc

F32 = jnp.float32
BF16 = jnp.bfloat16
I32 = jnp.int32
U32 = jnp.uint32

A_HEADS = 8
CHUNK = 128
CONV_W = 3
X_HEADS = 4
N_EXPERTS = 32
TOP_K = 4
SWIGLU_LIMIT = 7.0
SWIGLU_ALPHA = 1.702
EPS = 1e-5

SUBLANES = 8
LANES = 128

MIX_TILE = 1024
ATT_TILE = 1024
KV_TILE = 512
ROUTE_TILE = 512
MOE_BLOCK = 256
COMBINE_TILE = 512
GATHER_TILE = 256
GATHER_TILES = 20
SAMPLE_ATT_GROUP = 8
SC_CHUNK = 64
V7X_VMEM_BYTES = 64 * 1024 * 1024
VMEM_LIMIT = V7X_VMEM_BYTES - 8 * 1024 * 1024


def _cparams(sem=None):
    return pltpu.CompilerParams(dimension_semantics=sem, vmem_limit_bytes=VMEM_LIMIT)


def _rms(x, g):
    r = lax.rsqrt(jnp.mean(x * x, axis=-1, keepdims=True) + EPS)
    return (x * r) * g


def _gelu(x):
    return 0.5 * x * (1.0 + lax.erf(x * 0.7071067811865476))


def _dot(a, b):
    return jnp.dot(a, b, preferred_element_type=F32)


def _dot_nt(a, b):
    return lax.dot_general(a, b, (((1,), (1,)), ((), ())), preferred_element_type=F32)


def _pack_bf16_pairs(x):
    d = x.shape[1]
    r = x.astype(BF16).astype(F32)
    lo = lax.shift_right_logical(pltpu.bitcast(r[:, :d // 2], U32), jnp.uint32(16))
    hi = lax.bitwise_and(pltpu.bitcast(r[:, d // 2:], U32), jnp.uint32(0xFFFF0000))
    return lax.bitwise_or(lo, hi)


def _unpack_bf16_pairs(w):
    lo = pltpu.bitcast(lax.shift_left(w, jnp.uint32(16)), F32)
    hi = pltpu.bitcast(lax.bitwise_and(w, jnp.uint32(0xFFFF0000)), F32)
    return lo, hi


def _memkv_kernel(m_ref, g_ref, wk_ref, wv_ref, k_ref, v_ref, wk_b, wv_b):
    @pl.when(pl.program_id(0) == 0)
    def _():
        wk_b[...] = wk_ref[...].astype(BF16)
        wv_b[...] = wv_ref[...].astype(BF16)

    h = _rms(m_ref[...], g_ref[...]).astype(BF16)
    k_ref[...] = _dot(h, wk_b[...])
    v_ref[...] = _dot(h, wv_b[...])


def _resident(shape, n_grid):
    zeros = (0,) * len(shape)
    index_map = (lambda i: zeros) if n_grid == 1 else (lambda i, j: zeros)
    return pl.BlockSpec(shape, index_map, pipeline_mode=pl.Buffered(1))


def _mem_kv(mem2d, g_mem, wk, wv):
    n, d = mem2d.shape
    row = pl.BlockSpec((KV_TILE, d), lambda i: (i, 0))
    return pl.pallas_call(
        _memkv_kernel,
        grid=(n // KV_TILE,),
        in_specs=[row, _resident((1, d), 1), _resident((d, d), 1), _resident((d, d), 1)],
        out_specs=[row, row],
        out_shape=[jax.ShapeDtypeStruct((n, d), F32)] * 2,
        scratch_shapes=[pltpu.VMEM((d, d), BF16), pltpu.VMEM((d, d), BF16)],
        compiler_params=_cparams(("arbitrary",)),
        name="mem_kv",
    )(mem2d, g_mem, wk, wv)


def _head_rms(v, gv, a_width):
    hd = a_width // A_HEADS
    r_i = lax.broadcasted_iota(I32, (a_width, a_width), 0) // hd
    c_i = lax.broadcasted_iota(I32, (a_width, a_width), 1) // hd
    ones_bd = jnp.where(r_i == c_i, 1.0, 0.0).astype(BF16)
    sq = v * v
    sq_hi = sq.astype(BF16)
    sq_lo = (sq - sq_hi.astype(F32)).astype(BF16)
    gs = _dot(sq_hi, ones_bd) + _dot(sq_lo, ones_bd)
    return (v * lax.rsqrt(gs * (1.0 / hd) + EPS)) * gv


def _mixer_prompt_kernel(x_ref, gmix_ref, win_f32, gv_ref, ws_ref, bs_ref, wc_ref, wout_f32,
                         x1_ref, cs_ref, zbuf, win_ref, wout_ref, *, tile, a_width, b_width):
    j = pl.program_id(1)
    nj = pl.num_programs(1)

    @pl.when((pl.program_id(0) == 0) & (j == 0))
    def _():
        win_ref[...] = win_f32[...].astype(BF16)
        wout_ref[...] = wout_f32[...].astype(BF16)

    x = x_ref[...]
    h = _rms(x, gmix_ref[...]).astype(BF16)
    proj = _dot(h, win_ref[...])
    u = _gelu(proj[:, :a_width])
    v = _head_rms(_gelu(proj[:, a_width:2 * a_width]), gv_ref[...], a_width)
    o = 2 * a_width
    hb = proj[:, o:o + b_width]
    bg = proj[:, o + b_width:o + 2 * b_width]
    cg = proj[:, o + 2 * b_width:o + 3 * b_width]

    lane = lax.broadcasted_iota(I32, (CHUNK, 2 * (a_width // A_HEADS)), 1)
    first = lane < (a_width // A_HEADS)
    t_i = lax.broadcasted_iota(I32, (CHUNK, 2 * CHUNK), 0)
    s_i = lax.broadcasted_iota(I32, (CHUNK, 2 * CHUNK), 1) % CHUNK
    causal = s_i <= t_i
    pair_cols = []
    for p in range(A_HEADS // 2):
        w_pair = jnp.where(causal, ws_ref[p], 0.0).astype(BF16)
        vp = v[:, p * CHUNK:(p + 1) * CHUNK]
        rows = []
        for c in range(tile // CHUNK):
            vc = vp[c * CHUNK:(c + 1) * CHUNK]
            rhs = jnp.concatenate([jnp.where(first, vc, 0.0), jnp.where(first, 0.0, vc)],
                                  axis=0).astype(BF16)
            rows.append(_dot(w_pair, rhs))
        pair_cols.append(jnp.concatenate(rows, axis=0))
    gate = jnp.concatenate(pair_cols, axis=1)
    bias = jnp.concatenate([bs_ref[...]] * (tile // CHUNK), axis=0)
    a_out = u * (gate + bias)

    head = SUBLANES

    @pl.when(j == 0)
    def _():
        zbuf[0:head, :] = jnp.zeros((head, b_width), F32)

    z = cg * hb
    zbuf[head:tile + head, :] = z
    z1 = zbuf[head - 1:tile + head - 1, :]
    z2 = zbuf[head - 2:tile + head - 2, :]
    wc = wc_ref[...]
    conv = z2 * wc[0:1] + z1 * wc[1:2] + z * wc[2:3]
    b_out = bg * conv
    tail = zbuf[tile:tile + head, :]
    zbuf[0:head, :] = tail

    @pl.when(j == nj - 1)
    def _():
        cs_ref[...] = tail[head - (CONV_W - 1):, :]

    y = _dot(a_out.astype(BF16), wout_ref[0:a_width, :]) + _dot(b_out.astype(BF16), wout_ref[a_width:, :])
    x1_ref[...] = x + y


def _mixer_prompt(x, g_mix, w_in, g_v, ws_pairs, bs_full, w_conv, w_out):
    b, s, d = x.shape
    a_width = g_v.shape[1]
    b_width = w_conv.shape[1]
    in_width = w_in.shape[1]
    tile = MIX_TILE
    full = lambda shape: _resident(shape, 2)
    kern = functools.partial(_mixer_prompt_kernel, tile=tile, a_width=a_width, b_width=b_width)
    return pl.pallas_call(
        kern,
        grid=(b, s // tile),
        in_specs=[
            pl.BlockSpec((None, tile, d), lambda i, j: (i, j, 0)),
            full((1, d)), full((d, in_width)), full((1, a_width)),
            full(ws_pairs.shape), full(bs_full.shape), full(w_conv.shape), full(w_out.shape),
        ],
        out_specs=[
            pl.BlockSpec((None, tile, d), lambda i, j: (i, j, 0)),
            pl.BlockSpec((None, CONV_W - 1, b_width), lambda i, j: (i, 0, 0)),
        ],
        out_shape=[jax.ShapeDtypeStruct((b, s, d), F32),
                   jax.ShapeDtypeStruct((b, CONV_W - 1, b_width), F32)],
        scratch_shapes=[pltpu.VMEM((tile + SUBLANES, b_width), F32), pltpu.VMEM((d, in_width), BF16),
                        pltpu.VMEM(w_out.shape, BF16)],
        compiler_params=_cparams(("arbitrary", "arbitrary")),
        name="mixer_prompt",
    )(x, g_mix, w_in, g_v, ws_pairs, bs_full, w_conv, w_out)


def _mixer_sample_kernel(x_ref, s0_ref, s1_ref, gmix_ref, win_ref, gv_ref, w00_ref, b0_ref, wc_ref,
                         wout_ref, gx_ref, wq_ref, x1_ref, v_ref, z_ref, q_ref, *, a_width, b_width):
    x = x_ref[...]
    h = _rms(x, gmix_ref[...]).astype(BF16)
    proj = _dot(h, win_ref[...].astype(BF16))
    u = _gelu(proj[:, :a_width])
    v = _head_rms(_gelu(proj[:, a_width:2 * a_width]), gv_ref[...], a_width)
    v_ref[...] = v
    o = 2 * a_width
    hb = proj[:, o:o + b_width]
    bg = proj[:, o + b_width:o + 2 * b_width]
    cg = proj[:, o + 2 * b_width:o + 3 * b_width]
    a_out = u * (v * w00_ref[...] + b0_ref[...])
    z = cg * hb
    z_ref[...] = z
    wc = wc_ref[...]
    conv = s0_ref[...] * wc[0:1] + s1_ref[...] * wc[1:2] + z * wc[2:3]
    b_out = bg * conv
    y = (_dot(a_out.astype(BF16), wout_ref[0:a_width, :].astype(BF16))
         + _dot(b_out.astype(BF16), wout_ref[a_width:, :].astype(BF16)))
    x1 = x + y
    x1_ref[...] = x1
    q_ref[...] = _dot(_rms(x1, gx_ref[...]).astype(BF16), wq_ref[...].astype(BF16))


def _mixer_sample(x, s0, s1, g_mix, w_in, g_v, w00, b0, w_conv, w_out, g_x, w_q):
    n, d = x.shape
    a_width = g_v.shape[1]
    b_width = w_conv.shape[1]
    kern = functools.partial(_mixer_sample_kernel, a_width=a_width, b_width=b_width)
    return pl.pallas_call(
        kern,
        out_shape=[jax.ShapeDtypeStruct((n, d), F32), jax.ShapeDtypeStruct((n, a_width), F32),
                   jax.ShapeDtypeStruct((n, b_width), F32), jax.ShapeDtypeStruct((n, d), F32)],
        compiler_params=_cparams(),
        name="mixer_sample",
    )(x, s0, s1, g_mix, w_in, g_v, w00, b0, w_conv, w_out, g_x, w_q)


def _router_tail(x2, gmoe, wrt, br, h3_ref, tope_ref, gates_ref):
    h3f = _rms(x2, gmoe)
    h3 = h3f.astype(BF16)
    h3_ref[...] = _pack_bf16_pairs(h3f)

    logits = _dot_nt(wrt, h3) + br
    n_e, n_t = logits.shape
    e_idx = lax.broadcasted_iota(I32, (n_e, n_t), 0).astype(F32)
    tops, idxs = [], []
    for _ in range(TOP_K):
        m = jnp.max(logits, axis=0, keepdims=True)
        idx = jnp.min(jnp.where(logits == m, e_idx, float(n_e)), axis=0, keepdims=True)
        tops.append(m)
        idxs.append(idx)
        logits = jnp.where(e_idx == idx, -jnp.inf, logits)
    top = jnp.concatenate(tops, axis=0)
    ex = jnp.exp(top - top[0:1])
    gates_ref[...] = ex / jnp.sum(ex, axis=0, keepdims=True)
    tope_ref[...] = jnp.concatenate(idxs, axis=0).astype(I32)


def _attn_prompt_kernel(x1_ref, k_ref, v_ref, gx_ref, wq_f32, wxo_f32, gmoe_ref, wrt_ref, br_ref,
                        x2_ref, h3_ref, tope_ref, gates_ref, wq_ref, wxo_ref):
    @pl.when((pl.program_id(0) == 0) & (pl.program_id(1) == 0))
    def _():
        wq_ref[...] = wq_f32[...].astype(BF16)
        wxo_ref[...] = wxo_f32[...].astype(BF16)

    x = x1_ref[...]
    d = x.shape[1]
    hd = d // X_HEADS
    q = _dot(_rms(x, gx_ref[...]).astype(BF16), wq_ref[...]).astype(BF16)
    kb = k_ref[...].astype(BF16)
    vb = v_ref[...].astype(BF16)
    outs = []
    for hh in range(X_HEADS):
        sl = slice(hh * hd, (hh + 1) * hd)
        s = _dot_nt(q[:, sl], kb[:, sl]) * (hd ** -0.5)
        e = jnp.exp(s - jnp.max(s, axis=-1, keepdims=True))
        outs.append(_dot(e.astype(BF16), vb[:, sl]) / jnp.sum(e, axis=-1, keepdims=True))
    o = jnp.concatenate(outs, axis=1).astype(BF16)
    x2 = x + _dot(o, wxo_ref[...])
    x2_ref[...] = x2
    _router_tail(x2, gmoe_ref[...], wrt_ref[...], br_ref[...], h3_ref, tope_ref, gates_ref)


def _attn_prompt(x1, mk, mv, g_x, w_q, w_xo, g_moe, wrt, br):
    b, s, d = x1.shape
    n_mem = mk.shape[1]
    tile = ATT_TILE
    nq = s // tile
    full = lambda shape: _resident(shape, 2)
    tok = pl.BlockSpec((None, tile, d), lambda i, j: (i, j, 0))
    mem = pl.BlockSpec((None, n_mem, d), lambda i, j: (i, 0, 0))
    lanes = pl.BlockSpec((TOP_K, tile), lambda i, j: (0, i * nq + j))
    return pl.pallas_call(
        _attn_prompt_kernel,
        grid=(b, nq),
        in_specs=[tok, mem, mem, full((1, d)), full((d, d)), full((d, d)), full((1, d)),
                  full(wrt.shape), full(br.shape)],
        out_specs=[tok, pl.BlockSpec((None, tile, d // 2), lambda i, j: (i, j, 0)), lanes, lanes],
        out_shape=[jax.ShapeDtypeStruct((b, s, d), F32), jax.ShapeDtypeStruct((b, s, d // 2), U32),
                   jax.ShapeDtypeStruct((TOP_K, b * s), I32), jax.ShapeDtypeStruct((TOP_K, b * s), F32)],
        scratch_shapes=[pltpu.VMEM((d, d), BF16), pltpu.VMEM((d, d), BF16)],
        compiler_params=_cparams(("arbitrary", "arbitrary")),
        name="attn_prompt",
    )(x1, mk, mv, g_x, w_q, w_xo, g_moe, wrt, br)


def _attn_sample_kernel(q_ref, k_ref, v_ref, o_ref, *, group, heads, scale):
    for g in range(group):
        q = q_ref[g] * scale
        k = k_ref[g]
        part = jnp.broadcast_to(jnp.sum(k * q[None], axis=-1, keepdims=True), k.shape)
        s = part + pltpu.roll(part, heads, axis=1)
        e = jnp.exp(s - jnp.max(s, axis=0, keepdims=True))
        o_ref[g] = jnp.sum(e * v_ref[g], axis=0) / jnp.sum(e, axis=0)


def _to_rows8(a, heads, hd):
    lead = a.shape[:-2]
    n = len(lead)
    a = a.reshape(*lead, heads, hd // LANES, LANES)
    a = a.transpose(*range(n), n + 1, n, n + 2)
    return a.reshape(*lead, (hd // LANES) * heads, LANES)


def _from_rows8(a, heads, hd):
    lead = a.shape[:-2]
    n = len(lead)
    a = a.reshape(*lead, hd // LANES, heads, LANES)
    a = a.transpose(*range(n), n + 1, n, n + 2)
    return a.reshape(*lead, heads, hd)


def _attn_sample(q, ck, cv):
    n, heads, hd = q.shape
    assert hd == 2 * LANES and 2 * heads == SUBLANES
    n_mem = ck.shape[1]
    group = SAMPLE_ATT_GROUP
    rows = (hd // LANES) * heads
    qo = pl.BlockSpec((group, rows, LANES), lambda i: (i, 0, 0))
    kv = pl.BlockSpec((group, n_mem, rows, LANES), lambda i: (i, 0, 0, 0))
    o = pl.pallas_call(
        functools.partial(_attn_sample_kernel, group=group, heads=heads, scale=hd ** -0.5),
        grid=(n // group,),
        in_specs=[qo, kv, kv],
        out_specs=qo,
        out_shape=jax.ShapeDtypeStruct((n, rows, LANES), F32),
        compiler_params=_cparams(("arbitrary",)),
        name="attn_sample",
    )(_to_rows8(q, heads, hd), _to_rows8(ck, heads, hd), _to_rows8(cv, heads, hd))
    return _from_rows8(o, heads, hd)


def _tail_sample_kernel(x1_ref, o_ref, wxo_ref, gmoe_ref, wrt_ref, br_ref,
                        x2_ref, h3_ref, tope_ref, gates_ref):
    x2 = x1_ref[...] + _dot(o_ref[...].astype(BF16), wxo_ref[...].astype(BF16))
    x2_ref[...] = x2
    _router_tail(x2, gmoe_ref[...], wrt_ref[...], br_ref[...], h3_ref, tope_ref, gates_ref)


def _tail_sample(x1, o, w_xo, g_moe, wrt, br):
    n, d = x1.shape
    return pl.pallas_call(
        _tail_sample_kernel,
        out_shape=[jax.ShapeDtypeStruct((n, d), F32), jax.ShapeDtypeStruct((n, d // 2), U32),
                   jax.ShapeDtypeStruct((TOP_K, n), I32), jax.ShapeDtypeStruct((TOP_K, n), F32)],
        compiler_params=_cparams(),
        name="tail_sample",
    )(x1, o, w_xo, g_moe, wrt, br)


def _route_kernel(tope_ref, dest_ref, cnt_ref, *, n_blocks, tile, moe_block):
    e_idx = lax.broadcasted_iota(I32, (N_EXPERTS, tile), 0)
    earlier = jnp.where(lax.broadcasted_iota(I32, (tile, tile), 0)
                        < lax.broadcasted_iota(I32, (tile, tile), 1), 1.0, 0.0).astype(BF16)

    def onehot(k, off):
        return e_idx == tope_ref[pl.ds(k, 1), pl.ds(off, tile)]

    def count_body(j, cnt):
        off = pl.multiple_of(j * tile, tile)
        for k in range(TOP_K):
            cnt = cnt + jnp.sum(jnp.where(onehot(k, off), 1.0, 0.0), axis=1, keepdims=True)
        return cnt

    counts = lax.fori_loop(0, n_blocks, count_body, jnp.zeros((N_EXPERTS, 1), F32))
    cnt_ref[...] = jnp.broadcast_to(counts, cnt_ref.shape).astype(I32)

    n_blk = jnp.floor((counts + (moe_block - 1)) * (1.0 / moe_block))
    n_hi = jnp.floor(n_blk * (1.0 / 16.0))
    n_lo = n_blk - 16.0 * n_hi
    below = jnp.where(lax.broadcasted_iota(I32, (N_EXPERTS, N_EXPERTS), 1)
                      < lax.broadcasted_iota(I32, (N_EXPERTS, N_EXPERTS), 0), 1.0, 0.0).astype(BF16)
    wide = lambda c: jnp.broadcast_to(c, (N_EXPERTS, LANES)).astype(BF16)
    start = (16.0 * _dot(below, wide(n_hi)) + _dot(below, wide(n_lo)))[:, 0:1] * float(moe_block)

    def dest_body(j, run):
        off = pl.multiple_of(j * tile, tile)
        for k in range(TOP_K):
            oh = onehot(k, off)
            ohf = jnp.where(oh, 1.0, 0.0)
            before = _dot(ohf.astype(BF16), earlier)
            dest = jnp.sum(jnp.where(oh, before + run, 0.0), axis=0, keepdims=True)
            dest_ref[pl.ds(k, 1), pl.ds(off, tile)] = dest.astype(I32)
            run = run + jnp.sum(ohf, axis=1, keepdims=True)
        return run

    lax.fori_loop(0, n_blocks, dest_body, start)


def _route(tope_pad):
    t_pad = tope_pad.shape[1]
    kern = functools.partial(_route_kernel, n_blocks=t_pad // ROUTE_TILE, tile=ROUTE_TILE,
                             moe_block=MOE_BLOCK)
    return pl.pallas_call(
        kern,
        out_shape=[jax.ShapeDtypeStruct((TOP_K, t_pad), I32),
                   jax.ShapeDtypeStruct((N_EXPERTS, LANES), I32)],
        compiler_params=_cparams(),
        name="route",
    )(tope_pad)


def _dispatch(dest3, h_prompt, h_sample, n_rows):
    w = h_prompt.shape[1]
    n_prompt_chunks = h_prompt.shape[0] // SC_CHUNK
    n_chunks = dest3.shape[0]
    mesh = plsc.VectorSubcoreMesh(core_axis_name="c", subcore_axis_name="s")
    n_workers = mesh.num_cores * mesh.num_subcores

    n_steps = -(-n_chunks // n_workers)

    def body(dest_hbm, hp_hbm, hs_hbm, xs_hbm, idx0, idx1, rows0, rows1, sem):
        wid = lax.axis_index("s") * mesh.num_cores + lax.axis_index("c")
        idx_v, rows_v = (idx0, idx1), (rows0, rows1)

        def chunk(j):
            return wid + j * n_workers

        def loads(j, slot):
            c = chunk(j)
            prompt = (hp_hbm, c)
            sample = (hs_hbm, c - n_prompt_chunks)
            return c, [
                (pltpu.make_async_copy(dest_hbm.at[c], idx_v[slot], sem.at[slot]), None),
                (pltpu.make_async_copy(prompt[0].at[pl.ds(prompt[1] * SC_CHUNK, SC_CHUNK)], rows_v[slot],
                                       sem.at[slot]), c < n_prompt_chunks),
                (pltpu.make_async_copy(sample[0].at[pl.ds(sample[1] * SC_CHUNK, SC_CHUNK)], rows_v[slot],
                                       sem.at[slot]), c >= n_prompt_chunks),
            ]

        def start_loads(j, slot):
            c, copies = loads(j, slot)
            for cp, cond in copies:
                pl.when((c < n_chunks) if cond is None else ((c < n_chunks) & cond))(cp.start)

        def wait_loads(j, slot):
            c, copies = loads(j, slot)
            for cp, cond in copies:
                pl.when((c < n_chunks) if cond is None else ((c < n_chunks) & cond))(cp.wait)

        def scatters(slot):
            return [pltpu.make_async_copy(rows_v[slot], xs_hbm.at[idx_v[slot].at[k]], sem.at[2 + slot])
                    for k in range(TOP_K)]

        start_loads(0, 0)
        for j in range(n_steps):
            slot = j % 2
            wait_loads(j, slot)
            if j >= 1:
                @pl.when(chunk(j - 1) < n_chunks)
                def _():
                    for cp in scatters(1 - slot):
                        cp.wait()
            if j + 1 < n_steps:
                start_loads(j + 1, 1 - slot)

            @pl.when(chunk(j) < n_chunks)
            def _():
                for cp in scatters(slot):
                    cp.start()

        @pl.when(chunk(n_steps - 1) < n_chunks)
        def _():
            for cp in scatters((n_steps - 1) % 2):
                cp.wait()

    return pl.kernel(
        body,
        out_type=jax.ShapeDtypeStruct((n_rows, w), U32),
        mesh=mesh,
        scratch_types=[pltpu.VMEM((TOP_K, SC_CHUNK), I32), pltpu.VMEM((TOP_K, SC_CHUNK), I32),
                       pltpu.VMEM((SC_CHUNK, w), U32), pltpu.VMEM((SC_CHUNK, w), U32),
                       pltpu.SemaphoreType.DMA((4,))],
        name="dispatch_sc",
    )(dest3, h_prompt, h_sample)


def _ffn_kernel(start_ref, nblk_ref, count_ref, b1_ref, b2_ref, xs_hbm, w1_hbm, w2_hbm, y_hbm,
                w1f, w2f, w1b, w2b, xbuf, ybuf, sem_w, sem_x, sem_y, *, blk):
    e = pl.program_id(0)
    n_e = pl.num_programs(0)
    nb = nblk_ref[e]
    base = start_ref[e]
    wslot = e % 2

    def w_copies(expert, slot):
        return (pltpu.make_async_copy(w1_hbm.at[expert], w1f.at[slot], sem_w.at[slot]),
                pltpu.make_async_copy(w2_hbm.at[expert], w2f.at[slot], sem_w.at[slot]))

    def x_copy(row0, slot):
        return pltpu.make_async_copy(xs_hbm.at[pl.ds(pl.multiple_of(row0, blk), blk)], xbuf.at[slot],
                                     sem_x.at[slot])

    def y_copy(row0, slot):
        return pltpu.make_async_copy(ybuf.at[slot], y_hbm.at[pl.ds(pl.multiple_of(row0, blk), blk)],
                                     sem_y.at[slot])

    @pl.when(e == 0)
    def _():
        for cp in w_copies(0, 0):
            cp.start()

        @pl.when(nb > 0)
        def _():
            x_copy(base, 0).start()

    for cp in w_copies(e, wslot):
        cp.wait()

    @pl.when(e + 1 < n_e)
    def _():
        for cp in w_copies(e + 1, 1 - wslot):
            cp.start(priority=1)

    w1b[...] = w1f[wslot].astype(BF16)
    w2b[...] = w2f[wslot].astype(BF16)
    b1 = b1_ref[e]
    b2 = b2_ref[e]

    def block(j, slot):
        x_copy(base, slot).wait()

        @pl.when(j + 1 < nb)
        def _():
            x_copy(base + (j + 1) * blk, 1 - slot).start()

        @pl.when(j >= 2)
        def _():
            y_copy(base, slot).wait()

        w = xbuf[slot]
        row = lax.broadcasted_iota(I32, w.shape, 0)
        w = jnp.where(row < count_ref[e] - j * blk, w, jnp.uint32(0))
        half = w.shape[1]
        d_e = w2b.shape[0]
        lo, hi = _unpack_bf16_pairs(w)
        gu = _dot(lo.astype(BF16), w1b[0:half, :]) + _dot(hi.astype(BF16), w1b[half:, :]) + b1
        gate = jnp.minimum(gu[:, :d_e], SWIGLU_LIMIT)
        up = jnp.clip(gu[:, d_e:], -SWIGLU_LIMIT, SWIGLU_LIMIT)
        glu = gate * jax.nn.sigmoid(gate * SWIGLU_ALPHA)
        act = ((up + 1.0) * glu).astype(BF16)
        ybuf[slot] = _pack_bf16_pairs(_dot(act, w2b[...]) + b2)
        y_copy(base + j * blk, slot).start()

    def pair(j2, carry):
        j = 2 * j2
        block(j, 0)

        @pl.when(j + 1 < nb)
        def _():
            block(j + 1, 1)

        return carry

    lax.fori_loop(0, (nb + 1) // 2, pair, 0)

    @pl.when(nb >= 2)
    def _():
        y_copy(base, 0).wait()
        y_copy(base, 1).wait()

    @pl.when(nb == 1)
    def _():
        y_copy(base, 0).wait()

    nxt = jnp.minimum(e + 1, n_e - 1)

    @pl.when((e + 1 < n_e) & (nblk_ref[nxt] > 0))
    def _():
        x_copy(start_ref[nxt], 0).start()


def _expert_ffn(grp_start, grp_blocks, grp_count, xs, w1, b1, w2, b2):
    n_rows, half = xs.shape
    n_e, d, d2 = w1.shape
    d_e = w2.shape[1]
    blk = MOE_BLOCK
    whole = lambda shape: pl.BlockSpec(shape, lambda e, s, n, c: (0,) * len(shape))
    hbm = pl.BlockSpec(memory_space=pl.ANY)
    grid_spec = pltpu.PrefetchScalarGridSpec(
        num_scalar_prefetch=3,
        grid=(n_e,),
        in_specs=[whole((n_e, 1, d2)), whole((n_e, 1, d)), hbm, hbm, hbm],
        out_specs=hbm,
        scratch_shapes=[pltpu.VMEM((2, d, d2), F32), pltpu.VMEM((2, d_e, d), F32),
                        pltpu.VMEM((d, d2), BF16), pltpu.VMEM((d_e, d), BF16),
                        pltpu.VMEM((2, blk, half), U32), pltpu.VMEM((2, blk, d // 2), U32),
                        pltpu.SemaphoreType.DMA((2,)), pltpu.SemaphoreType.DMA((2,)),
                        pltpu.SemaphoreType.DMA((2,))],
    )
    return pl.pallas_call(
        functools.partial(_ffn_kernel, blk=blk),
        grid_spec=grid_spec,
        out_shape=jax.ShapeDtypeStruct((n_rows, d // 2), U32),
        compiler_params=_cparams(("arbitrary",)),
        name="expert_ffn",
    )(grp_start, grp_blocks, grp_count, b1.reshape(n_e, 1, d2), b2.reshape(n_e, 1, d), xs, w1, w2)


def _undispatch(dest3, y_rows):
    n_chunks = dest3.shape[0]
    w = y_rows.shape[1]
    mesh = plsc.VectorSubcoreMesh(core_axis_name="c", subcore_axis_name="s")
    n_workers = mesh.num_cores * mesh.num_subcores
    n_steps = -(-n_chunks // n_workers)

    def body(dest_hbm, y_hbm, out_hbm, idx0, idx1, rows0, rows1, sem):
        wid = lax.axis_index("s") * mesh.num_cores + lax.axis_index("c")
        idx_v, rows_v = (idx0, idx1), (rows0, rows1)

        def chunk(j):
            return wid + j * n_workers

        def gather(j, k, slot):
            return pltpu.make_async_copy(y_hbm.at[idx_v[j % 2].at[k]], rows_v[slot], sem.at[slot])

        def store(j, k, slot):
            return pltpu.make_async_copy(rows_v[slot], out_hbm.at[k, pl.ds(chunk(j) * SC_CHUNK, SC_CHUNK)],
                                         sem.at[2 + slot])

        units = [(j, k) for j in range(n_steps) for k in range(TOP_K)]

        def valid(j):
            return chunk(j) < n_chunks

        def load_idx(j):
            pl.when(valid(j))(lambda: pltpu.sync_copy(dest_hbm.at[chunk(j)], idx_v[j % 2]))

        load_idx(0)
        pl.when(valid(0))(gather(0, 0, 0).start)
        for u, (j, k) in enumerate(units):
            slot = u % 2
            pl.when(valid(j))(gather(j, k, slot).wait)
            if u + 1 < len(units):
                jn, kn = units[u + 1]
                if u >= 1:
                    jp, kp = units[u - 1]
                    pl.when(valid(jp))(store(jp, kp, 1 - slot).wait)
                if kn == 0:
                    load_idx(jn)
                pl.when(valid(jn))(gather(jn, kn, 1 - slot).start)
            pl.when(valid(j))(store(j, k, slot).start)
        for u in (len(units) - 2, len(units) - 1):
            j, k = units[u]
            pl.when(valid(j))(store(j, k, u % 2).wait)

    return pl.kernel(
        body,
        out_type=jax.ShapeDtypeStruct((TOP_K, n_chunks * SC_CHUNK, w), y_rows.dtype),
        mesh=mesh,
        scratch_types=[pltpu.VMEM((TOP_K, SC_CHUNK), I32), pltpu.VMEM((TOP_K, SC_CHUNK), I32),
                       pltpu.VMEM((SC_CHUNK, w), y_rows.dtype), pltpu.VMEM((SC_CHUNK, w), y_rows.dtype),
                       pltpu.SemaphoreType.DMA((4,))],
        name="undispatch_sc",
    )(dest3, y_rows)


def _gated_sum_norm(x2, y4, gates, g_final, tile):
    g = jnp.concatenate([gates, jnp.zeros((LANES - TOP_K, tile), F32)], axis=0)
    gt = g.T
    half = y4.shape[2]
    moe_lo = jnp.zeros((tile, half), F32)
    moe_hi = jnp.zeros((tile, half), F32)
    for k in range(TOP_K):
        lo, hi = _unpack_bf16_pairs(y4[k])
        moe_lo = moe_lo + gt[:, k:k + 1] * lo
        moe_hi = moe_hi + gt[:, k:k + 1] * hi
    return _rms(x2 + jnp.concatenate([moe_lo, moe_hi], axis=1), g_final)


def _combine_kernel(x2_ref, y4_ref, gates_ref, gf_ref, *out_refs, tile):
    out_ref = out_refs[-1]
    out_ref[...] = _gated_sum_norm(x2_ref[...], y4_ref[...], gates_ref[...], gf_ref[...], tile)


def _combine(x2, y4, gates, g_final, tile, tok_tile0, n_tiles, y_tile0, out_prev=None):
    n, d = x2.shape
    in_specs = [
        pl.BlockSpec((tile, d), lambda i: (i + tok_tile0, 0)),
        pl.BlockSpec((TOP_K, tile, d // 2), lambda i: (0, i + y_tile0, 0)),
        pl.BlockSpec((TOP_K, tile), lambda i: (0, i + tok_tile0)),
        pl.BlockSpec((1, d), lambda i: (0, 0)),
    ]
    args = [x2, y4, gates, g_final]
    aliases = {}
    if out_prev is not None:
        in_specs.append(pl.BlockSpec(memory_space=pl.ANY))
        args.append(out_prev)
        aliases = {len(args) - 1: 0}
    return pl.pallas_call(
        functools.partial(_combine_kernel, tile=tile),
        grid=(n_tiles,),
        in_specs=in_specs,
        out_specs=pl.BlockSpec((tile, d), lambda i: (i + tok_tile0, 0)),
        out_shape=jax.ShapeDtypeStruct((n, d), F32),
        input_output_aliases=aliases,
        compiler_params=_cparams(("arbitrary",)),
        name="combine",
    )(*args)


def _combine_gather_kernel(dest_ref, next_ref, x2_ref, gates_ref, gf_ref, y_hbm, out_ref, buf, sem, *,
                           tile):
    i = pl.program_id(0)
    n = pl.num_programs(0)

    def row_copy(d, slot, k, t):
        return pltpu.make_async_copy(y_hbm.at[pl.ds(d, 1)], buf.at[slot, k, pl.ds(t, 1)], sem.at[slot])

    def issue(rows_ref, slot):
        for t in range(tile):
            for k in range(TOP_K):
                row_copy(rows_ref[k, t], slot, k, t).start(priority=(t * TOP_K + k) % 2)

    def wait(slot):
        for t in range(tile):
            for k in range(TOP_K):
                row_copy(0, slot, k, t).wait()

    @pl.when(i == 0)
    def _():
        issue(dest_ref, 0)

    for slot in range(2):
        @pl.when(i % 2 == slot)
        def _():
            @pl.when(i + 1 < n)
            def _():
                issue(next_ref, 1 - slot)

            wait(slot)
            out_ref[...] = _gated_sum_norm(x2_ref[...], buf[slot], gates_ref[...], gf_ref[...], tile)


def _combine_gather(dest, x2, gates, g_final, y_rows, tile, n_tiles):
    n, d = x2.shape
    rows = lambda nxt: pl.BlockSpec((TOP_K, tile), lambda i: (0, jnp.minimum(i + nxt, n_tiles - 1)),
                                    memory_space=pltpu.SMEM)
    grid_spec = pltpu.PrefetchScalarGridSpec(
        num_scalar_prefetch=0,
        grid=(n_tiles,),
        in_specs=[
            rows(0), rows(1),
            pl.BlockSpec((tile, d), lambda i: (i, 0)),
            pl.BlockSpec((TOP_K, tile), lambda i: (0, i)),
            pl.BlockSpec((1, d), lambda i: (0, 0)),
            pl.BlockSpec(memory_space=pl.ANY),
        ],
        out_specs=pl.BlockSpec((tile, d), lambda i: (i, 0)),
        scratch_shapes=[pltpu.VMEM((2, TOP_K, tile, d // 2), y_rows.dtype),
                        pltpu.SemaphoreType.DMA((2,))],
    )
    return pl.pallas_call(
        functools.partial(_combine_gather_kernel, tile=tile),
        grid_spec=grid_spec,
        out_shape=jax.ShapeDtypeStruct((n, d), F32),
        compiler_params=_cparams(("arbitrary",)),
        name="combine_gather",
    )(dest, dest, x2, gates, g_final, y_rows)


def kernel(x_prompt, x_sample, mem_prompt, cache_mem_k, cache_mem_v, state_conv, g_mix, w_in, g_v,
           w_spatial, b_spatial, w_conv, w_out, g_xattn, g_mem, w_q, w_k, w_v, w_xo, g_moe, w_router,
           b_router, w_gate_up, b_gate_up, w_down, b_down, g_final):
    depth = g_mix.shape[0]
    assert depth == 1, "one layer supported"
    b, s, d = x_prompt.shape
    nb, ns, _ = x_sample.shape
    assert ns == 1
    n_mem = mem_prompt.shape[1]
    a_heads, a_hd = g_v.shape[1], g_v.shape[2]
    a_width = a_heads * a_hd
    b_width = w_conv.shape[2]
    assert a_heads == A_HEADS and 2 * a_hd == CHUNK and w_spatial.shape[2] == CHUNK
    assert s % MIX_TILE == 0 and s % ATT_TILE == 0 and (b * n_mem) % KV_TILE == 0
    assert nb % SAMPLE_ATT_GROUP == 0
    l = 0
    row = lambda a: a.reshape(1, -1)

    w_in_b, w_out_b, w_q_b, w_k_b, w_v_b, w_xo_b = w_in[l], w_out[l], w_q[l], w_k[l], w_v[l], w_xo[l]
    wrt_b = w_router[l].T.astype(BF16)
    br_col = b_router[l].reshape(N_EXPERTS, 1).astype(F32)
    gv_row = row(g_v[l])
    ws = w_spatial[l]
    ws_pairs = jnp.concatenate([ws[0::2], ws[1::2]], axis=2)
    bs_full = jnp.repeat(b_spatial[l].T, a_hd, axis=1)
    w00 = row(jnp.repeat(ws[:, 0, 0], a_hd))
    b0 = row(jnp.repeat(b_spatial[l][:, 0], a_hd))

    mk2, mv2 = _mem_kv(mem_prompt.reshape(b * n_mem, d), row(g_mem[l]), w_k_b, w_v_b)
    x1_p, conv_p = _mixer_prompt(x_prompt, row(g_mix[l]), w_in_b, gv_row, ws_pairs, bs_full,
                                 w_conv[l], w_out_b)
    x2_p, h3_p, tope_p, gates_p = _attn_prompt(
        x1_p, mk2.reshape(b, n_mem, d), mv2.reshape(b, n_mem, d), row(g_xattn[l]), w_q_b, w_xo_b,
        row(g_moe[l]), wrt_b, br_col)

    xs2 = x_sample.reshape(nb, d)
    st = state_conv[l]
    x1_s, v_s, z_s, q_s = _mixer_sample(xs2, st[:, 0], st[:, 1], row(g_mix[l]), w_in_b, gv_row, w00, b0,
                                        w_conv[l], w_out_b, row(g_xattn[l]), w_q_b)
    x_heads, x_hd = cache_mem_k.shape[3], cache_mem_k.shape[4]
    assert x_heads == X_HEADS
    o_s = _attn_sample(q_s.reshape(nb, x_heads, x_hd), cache_mem_k.reshape(nb, n_mem, x_heads, x_hd),
                       cache_mem_v.reshape(nb, n_mem, x_heads, x_hd)).reshape(nb, d)
    x2_s, h3_s, tope_s, gates_s = _tail_sample(x1_s, o_s, w_xo_b, row(g_moe[l]), wrt_b, br_col)

    t_p = b * s
    t_all = t_p + nb
    t_pad = -(-t_all // ROUTE_TILE) * ROUTE_TILE
    tope_all = jnp.concatenate([tope_p, tope_s, jnp.full((TOP_K, t_pad - t_all), -1, I32)], axis=1)
    dest, cnt = _route(tope_all)
    counts = cnt[:, 0]
    padded = (counts + MOE_BLOCK - 1) // MOE_BLOCK * MOE_BLOCK
    pad_ends = jnp.cumsum(padded)
    pad_starts = pad_ends - padded
    n_blocks = -(-(t_all * TOP_K) // MOE_BLOCK) + N_EXPERTS
    n_rows = n_blocks * MOE_BLOCK
    assert t_all % SC_CHUNK == 0 and t_p % SC_CHUNK == 0
    dest3 = dest[:, :t_all].reshape(TOP_K, t_all // SC_CHUNK, SC_CHUNK).transpose(1, 0, 2)
    xs_rows = _dispatch(dest3, h3_p.reshape(t_p, d // 2), h3_s, n_rows)
    x_heads = cache_mem_k.shape[3]
    mem_k_out = mk2.reshape(1, b, n_mem, x_heads, d // x_heads)
    mem_v_out = mv2.reshape(1, b, n_mem, x_heads, d // x_heads)
    xs_rows, mem_k_out, mem_v_out = lax.optimization_barrier((xs_rows, mem_k_out, mem_v_out))
    y_rows = _expert_ffn(pad_starts.astype(I32), (padded // MOE_BLOCK).astype(I32), counts.astype(I32),
                         xs_rows, w_gate_up[l], b_gate_up[l], w_down[l], b_down[l])

    t_g = GATHER_TILES * GATHER_TILE
    assert t_g % SC_CHUNK == 0 and t_g % COMBINE_TILE == 0 and (t_p - t_g) % nb == 0
    y4 = _undispatch(dest3[t_g // SC_CHUNK:], y_rows)
    gf = row(g_final)
    x2_p2 = x2_p.reshape(t_p, d)
    y_p = _combine_gather(dest, x2_p2, gates_p, gf, y_rows, GATHER_TILE, GATHER_TILES)
    y_p = _combine(x2_p2, y4, gates_p, gf, COMBINE_TILE, t_g // COMBINE_TILE,
                   (t_p - t_g) // COMBINE_TILE, 0, out_prev=y_p)
    y_s = _combine(x2_s, y4, gates_s, gf, nb, 0, 1, (t_p - t_g) // nb)

    return (y_p.reshape(b, s, d),
            y_s.reshape(nb, 1, d),
            mem_k_out,
            mem_v_out,
            conv_p.reshape(1, b, CONV_W - 1, b_width),
            jnp.stack([st[:, 1], z_s], axis=1).reshape(1, nb, CONV_W - 1, b_width),
            v_s.reshape(1, nb, 1, a_heads, a_hd))
```
